```python
import math
import jax, jax.numpy as jnp
from jax import lax
import numpy as np

D_MODEL = 2048
BATCH = 16
SEQ = 2048
DEPTH = 1

EPS = 1e-6
GDN_HEADS = 16
GDN_DK = 128
GDN_DV = D_MODEL // GDN_HEADS
GDN_CHUNK = 64
CONV_W = 4
SWA_HEADS = 32
SWA_KV_HEADS = 4
SWA_DH = D_MODEL // SWA_HEADS
SWA_GROUP = SWA_HEADS // SWA_KV_HEADS
WINDOW = 128
SWA_BLOCK = 128
N_GROUPS = 8
EXPERTS_PER_GROUP = 8
N_EXPERTS = N_GROUPS * EXPERTS_PER_GROUP
TOP_K = 2
D_EXPERT = D_MODEL // 4
MOE_BLOCK = 128

GDN_QK_DIM = GDN_HEADS * GDN_DK
GDN_V_DIM = GDN_HEADS * GDN_DV
CONV_DIM = 2 * GDN_QK_DIM + GDN_V_DIM
SWA_Q_DIM = SWA_HEADS * SWA_DH
SWA_KV_DIM = SWA_KV_HEADS * SWA_DH
IN_SIZES = (CONV_DIM, GDN_V_DIM, GDN_HEADS, GDN_HEADS, SWA_Q_DIM, SWA_KV_DIM, SWA_KV_DIM, D_MODEL, D_MODEL)
IN_DIM = sum(IN_SIZES)
IN_OFFSETS = tuple(sum(IN_SIZES[:i + 1]) for i in range(len(IN_SIZES) - 1))

kernel_name = "hybrid_gdn_swa_hmoe_block"


def rms_norm(x, gain):
    x32 = x.astype(jnp.float32)
    y = x32 * lax.rsqrt(jnp.mean(x32 * x32, axis=-1, keepdims=True) + EPS)
    return (y * gain.astype(jnp.float32)).astype(x.dtype)


def causal_depthwise_conv(x, w):
    return lax.conv_general_dilated(
        x, w[:, None, :].astype(x.dtype), window_strides=(1,),
        padding=((CONV_W - 1, 0),), dimension_numbers=("NWC", "WIO", "NWC"),
        feature_group_count=x.shape[-1])


def gated_deltanet(qkv, z, a, b, conv_w, a_log, dt_bias, norm_gain):
    f32 = jnp.float32
    bsz, seq = qkv.shape[:2]
    nc = seq // GDN_CHUNK
    qkv = jax.nn.silu(causal_depthwise_conv(qkv, conv_w)).astype(f32)
    q, k, v = jnp.split(qkv, [GDN_QK_DIM, 2 * GDN_QK_DIM], axis=-1)

    def to_chunks(t, d):
        return t.reshape(bsz, nc, GDN_CHUNK, GDN_HEADS, d).transpose(1, 0, 3, 2, 4)

    def head_chunks(t):
        return t.astype(f32).reshape(bsz, nc, GDN_CHUNK, GDN_HEADS).transpose(1, 0, 3, 2)

    q, k, v = to_chunks(q, GDN_DK), to_chunks(k, GDN_DK), to_chunks(v, GDN_DV)
    q = q * lax.rsqrt(jnp.sum(q * q, -1, keepdims=True) + EPS) * (GDN_DK ** -0.5)
    k = k * lax.rsqrt(jnp.sum(k * k, -1, keepdims=True) + EPS)
    beta = jax.nn.sigmoid(head_chunks(b))
    g = -jnp.exp(a_log.astype(f32))[:, None] * jax.nn.softplus(head_chunks(a) + dt_bias.astype(f32)[:, None])
    gam = jnp.cumsum(g, axis=-1)

    idx = jnp.arange(GDN_CHUNK)
    causal = idx[:, None] >= idx[None, :]
    strict = idx[:, None] > idx[None, :]
    decay = jnp.exp(jnp.where(causal, gam[..., :, None] - gam[..., None, :], -jnp.inf))
    kk = jnp.einsum("nbhid,nbhjd->nbhij", k, k)
    a_mat = jnp.where(strict, kk * decay * beta[..., :, None], 0.0) + jnp.eye(GDN_CHUNK, dtype=f32)
    rhs = jnp.concatenate([v * beta[..., None], k * (beta * jnp.exp(gam))[..., None]], axis=-1)
    sol = lax.linalg.triangular_solve(a_mat, rhs, left_side=True, lower=True)
    u0, kcd = sol[..., :GDN_DV], sol[..., GDN_DV:]
    qk = jnp.einsum("nbhid,nbhjd->nbhij", q, k) * decay
    q_dec = q * jnp.exp(gam)[..., None]
    k_dec = k * jnp.exp(gam[..., -1:] - gam)[..., None]
    chunk_decay = jnp.exp(gam[..., -1])

    def step(state, xs):
        qk_c, qd_c, kd_c, u0_c, kcd_c, dec_c = xs
        u = u0_c - jnp.einsum("bhck,bhkv->bhcv", kcd_c, state)
        o = jnp.einsum("bhck,bhkv->bhcv", qd_c, state) + jnp.einsum("bhij,bhjv->bhiv", qk_c, u)
        state = state * dec_c[..., None, None] + jnp.einsum("bhck,bhcv->bhkv", kd_c, u)
        return state, o

    s0 = jnp.zeros((bsz, GDN_HEADS, GDN_DK, GDN_DV), f32)
    _, o = lax.scan(step, s0, (qk, q_dec, k_dec, u0, kcd, chunk_decay))
    o = o.transpose(1, 0, 3, 2, 4).reshape(bsz, seq, GDN_HEADS, GDN_DV)
    o = o * lax.rsqrt(jnp.mean(o * o, -1, keepdims=True) + EPS) * norm_gain.astype(f32)
    zz = z.astype(f32).reshape(bsz, seq, GDN_HEADS, GDN_DV)
    return (o * jax.nn.silu(zz)).reshape(bsz, seq, GDN_V_DIM).astype(z.dtype)


def sliding_window_attention(q, k, v, sinks):
    f32 = jnp.float32
    bsz, seq = q.shape[:2]
    nb = seq // SWA_BLOCK
    q = q.reshape(bsz, seq, SWA_KV_HEADS, SWA_GROUP, SWA_DH)
    k = k.reshape(bsz, seq, SWA_KV_HEADS, SWA_DH)
    v = v.reshape(bsz, seq, SWA_KV_HEADS, SWA_DH)
    pad = ((0, 0), (SWA_BLOCK, 0), (0, 0), (0, 0))
    k_pad, v_pad = jnp.pad(k, pad), jnp.pad(v, pad)
    scale = SWA_DH ** -0.5
    qi = jnp.arange(SWA_BLOCK)[:, None]
    ki = jnp.arange(2 * SWA_BLOCK)[None, :]
    rel = qi - ki + SWA_BLOCK
    band = (rel >= 0) & (rel < WINDOW)
    sink = sinks.astype(f32).reshape(SWA_KV_HEADS, SWA_GROUP)[:, :, None, None]

    def block(i):
        start = i * SWA_BLOCK
        qb = lax.dynamic_slice_in_dim(q, start, SWA_BLOCK, axis=1)
        kb = lax.dynamic_slice_in_dim(k_pad, start, 2 * SWA_BLOCK, axis=1)
        vb = lax.dynamic_slice_in_dim(v_pad, start, 2 * SWA_BLOCK, axis=1)
        s = jnp.einsum("bqhgd,bkhd->bhgqk", qb, kb).astype(f32) * scale
        mask = band & (start - SWA_BLOCK + ki >= 0)
        s = jnp.where(mask, s, -jnp.inf)
        m = jnp.maximum(jnp.max(s, axis=-1, keepdims=True), sink)
        p = jnp.exp(s - m)
        denom = jnp.sum(p, axis=-1, keepdims=True) + jnp.exp(sink - m)
        return jnp.einsum("bhgqk,bkhd->bqhgd", (p / denom).astype(vb.dtype), vb)

    o = lax.map(block, jnp.arange(nb))
    return o.transpose(1, 0, 2, 3, 4, 5).reshape(bsz, seq, SWA_Q_DIM)


def hybrid_mixer(x, norm_mix, w_in, conv_w, a_log, dt_bias, gdn_norm, swa_sinks, w_out):
    h = rms_norm(x, norm_mix)
    proj = jnp.einsum("bld,de->ble", h, w_in)
    qkv, z, a, b, sq, sk, sv, gate_gdn, gate_swa = jnp.split(proj, IN_OFFSETS, axis=-1)
    y_gdn = gated_deltanet(qkv, z, a, b, conv_w, a_log, dt_bias, gdn_norm)
    y_swa = sliding_window_attention(sq, sk, sv, swa_sinks)
    merged = jax.nn.sigmoid(gate_gdn) * y_gdn + jax.nn.sigmoid(gate_swa) * y_swa
    return x + jnp.einsum("bld,de->ble", merged, w_out)


def hierarchical_moe(h, w_rg, b_rg, w_re, b_re, w_gate, w_up, w_down):
    f32 = jnp.float32
    bsz, seq, d = h.shape
    t = bsz * seq
    xt = h.reshape(t, d)
    x32 = xt.astype(f32)
    lg = x32 @ w_rg.astype(f32) + b_rg.astype(f32)
    p_group = jax.nn.softmax(lg, axis=-1)
    grp = jnp.argmax(lg, axis=-1).astype(jnp.int32)
    p_sel = jnp.take_along_axis(p_group, grp[:, None], axis=-1)
    le = (x32 @ w_re.astype(f32) + b_re.astype(f32)).reshape(t, N_GROUPS, EXPERTS_PER_GROUP)
    le_sel = jnp.take_along_axis(le, grp[:, None, None], axis=1)[:, 0]
    top_v, top_i = lax.top_k(le_sel, TOP_K)
    wts = p_sel * jax.nn.softmax(top_v, axis=-1)
    eidx = grp[:, None] * EXPERTS_PER_GROUP + top_i.astype(jnp.int32)

    n_assign = t * TOP_K
    flat_e = eidx.reshape(n_assign)
    flat_w = wts.reshape(n_assign)
    order = jnp.argsort(flat_e)
    sorted_e = flat_e[order]
    counts = jnp.bincount(flat_e, length=N_EXPERTS)
    padded = (counts + MOE_BLOCK - 1) // MOE_BLOCK * MOE_BLOCK
    pad_end = jnp.cumsum(padded)
    pad_start = pad_end - padded
    seg_start = jnp.cumsum(counts) - counts
    dest = pad_start[sorted_e] + jnp.arange(n_assign) - seg_start[sorted_e]
    n_blocks = (n_assign + N_EXPERTS * (MOE_BLOCK - 1)) // MOE_BLOCK + 1
    n_slots = n_blocks * MOE_BLOCK
    slot_tok = jnp.full((n_slots,), t, jnp.int32).at[dest].set((order // TOP_K).astype(jnp.int32))
    slot_w = jnp.zeros((n_slots,), f32).at[dest].set(flat_w[order])
    blk_e = jnp.minimum(jnp.searchsorted(pad_end, jnp.arange(n_blocks) * MOE_BLOCK, side="right"), N_EXPERTS - 1)
    xt_pad = jnp.concatenate([xt, jnp.zeros((1, d), xt.dtype)], axis=0)

    def expert_block(args):
        tok, e = args
        xb = xt_pad[tok]
        hb = jax.nn.silu(xb @ w_gate[e]) * (xb @ w_up[e])
        return hb @ w_down[e]

    y = lax.map(expert_block, (slot_tok.reshape(n_blocks, MOE_BLOCK), blk_e))
    y = y.reshape(n_slots, d).astype(f32) * slot_w[:, None]
    out = jnp.zeros((t + 1, d), f32).at[slot_tok].add(y)[:t]
    return out.reshape(bsz, seq, d).astype(h.dtype)


def setup_inputs(seed: int = 0) -> dict:
    key = jax.random.key(seed)
    ks = jax.random.split(key, 20)
    f32 = jnp.float32

    def normal(k, shape, scale):
        return jax.random.normal(k, shape, f32) * scale

    dt = jnp.exp(jax.random.uniform(ks[5], (DEPTH, GDN_HEADS), f32, math.log(1e-3), math.log(1e-1)))
    return {
        "x": normal(ks[0], (BATCH, SEQ, D_MODEL), 1.0),
        "norm_mix": 1.0 + normal(ks[1], (DEPTH, D_MODEL), 0.02),
        "w_in": normal(ks[2], (DEPTH, D_MODEL, IN_DIM), D_MODEL ** -0.5),
        "conv_w": normal(ks[3], (DEPTH, CONV_W, CONV_DIM), CONV_W ** -0.5),
        "gdn_a_log": jnp.log(jax.random.uniform(ks[4], (DEPTH, GDN_HEADS), f32, 1.0, 16.0)),
        "gdn_dt_bias": dt + jnp.log(-jnp.expm1(-dt)),
        "gdn_norm": 1.0 + normal(ks[6], (DEPTH, GDN_DV), 0.02),
        "swa_sinks": normal(ks[7], (DEPTH, SWA_HEADS), 1.0),
        "w_out": normal(ks[8], (DEPTH, D_MODEL, D_MODEL), D_MODEL ** -0.5),
        "norm_ffn": 1.0 + normal(ks[9], (DEPTH, D_MODEL), 0.02),
        "w_router_group": normal(ks[10], (DEPTH, D_MODEL, N_GROUPS), D_MODEL ** -0.5),
        "b_router_group": normal(ks[11], (DEPTH, N_GROUPS), 0.01),
        "w_router_expert": normal(ks[12], (DEPTH, D_MODEL, N_EXPERTS), D_MODEL ** -0.5),
        "b_router_expert": normal(ks[13], (DEPTH, N_EXPERTS), 0.01),
        "w_gate": normal(ks[14], (DEPTH, N_EXPERTS, D_MODEL, D_EXPERT), D_MODEL ** -0.5),
        "w_up": normal(ks[15], (DEPTH, N_EXPERTS, D_MODEL, D_EXPERT), D_MODEL ** -0.5),
        "w_down": normal(ks[16], (DEPTH, N_EXPERTS, D_EXPERT, D_MODEL), D_EXPERT ** -0.5),
        "norm_final": 1.0 + normal(ks[17], (D_MODEL,), 0.02),
    }


def reference(x, norm_mix, w_in, conv_w, gdn_a_log, gdn_dt_bias, gdn_norm, swa_sinks, w_out,
              norm_ffn, w_router_group, b_router_group, w_router_expert, b_router_expert,
              w_gate, w_up, w_down, norm_final):
    for l in range(DEPTH):
        x = hybrid_mixer(x, norm_mix[l], w_in[l], conv_w[l], gdn_a_log[l], gdn_dt_bias[l],
                         gdn_norm[l], swa_sinks[l], w_out[l])
        x = x + hierarchical_moe(rms_norm(x, norm_ffn[l]), w_router_group[l], b_router_group[l],
                                 w_router_expert[l], b_router_expert[l], w_gate[l], w_up[l], w_down[l])
    return rms_norm(x, norm_final)
```

```python
import functools

import jax
import jax.numpy as jnp
from jax import lax
from jax.experimental import pallas as pl
from jax.experimental.pallas import tpu as pltpu

F32 = jnp.float32
BF16 = jnp.bfloat16
I32 = jnp.int32
U32 = jnp.uint32

D_MODEL = 2048
BATCH = 16
SEQ = 2048
TOKENS = BATCH * SEQ
EPS = 1e-6

GDN_HEADS = 16
GDN_DK = 128
GDN_DV = 128
GDN_CHUNK = 64
CONV_W = 4
GDN_GROUP = 4
GDN_GROUP_ROWS = GDN_GROUP * GDN_CHUNK
N_CHUNKS = SEQ // GDN_CHUNK
N_GROUPS_SEQ = SEQ // GDN_GROUP_ROWS

SWA_HEADS = 32
SWA_KV_HEADS = 4
SWA_DH = 64
SWA_GROUP = 8
SWA_BLOCK = 128
WINDOW = 128

N_GROUPS = 8
EXPERTS_PER_GROUP = 8
N_EXPERTS = 64
TOP_K = 2
D_EXPERT = 512
MOE_BLOCK = 128
N_ASSIGN = TOKENS * TOP_K
N_BLOCKS = N_ASSIGN // MOE_BLOCK + N_EXPERTS
N_SLOTS = N_BLOCKS * MOE_BLOCK

PROJ_DIM = 7 * D_MODEL + 2 * SWA_KV_HEADS * SWA_DH
LANES = 128

VMEM_LIMIT = 56 * 1024 * 1024


def _cparams(sem, vmem=VMEM_LIMIT):
    return pltpu.CompilerParams(dimension_semantics=sem, vmem_limit_bytes=vmem)


def _sigmoid(x):
    return 1.0 / (1.0 + jnp.exp(-x))


def _silu(x):
    return x * _sigmoid(x)


INPROJ_TM = 1024
INPROJ_TN = 512


def _inproj_body(x_ref, g_ref, w_ref, wab_ref, o_ref, ab_ref, h_ref):
    @pl.when(pl.program_id(1) == 0)
    def _():
        def chunk(i, c):
            r = pl.ds(pl.multiple_of(i * 128, 128), 128)
            x = x_ref[r, :]
            ms = jnp.mean(x * x, axis=-1, keepdims=True)
            h_ref[r, :] = (x * lax.rsqrt(ms + EPS) * g_ref[...]).astype(BF16)
            return c
        lax.fori_loop(0, INPROJ_TM // 128, chunk, 0)
        ab_ref[...] = jnp.dot(h_ref[...], wab_ref[...], preferred_element_type=F32)

    o_ref[...] = jnp.dot(h_ref[...], w_ref[...], preferred_element_type=F32).astype(BF16)


def _inproj(x2d, gain, w_main, w_ab):
    grid = (TOKENS // INPROJ_TM, PROJ_DIM // INPROJ_TN)
    return pl.pallas_call(
        _inproj_body,
        grid=grid,
        in_specs=[
            pl.BlockSpec((INPROJ_TM, D_MODEL), lambda i, j: (i, 0)),
            pl.BlockSpec((1, D_MODEL), lambda i, j: (0, 0)),
            pl.BlockSpec((D_MODEL, INPROJ_TN), lambda i, j: (0, j)),
            pl.BlockSpec((D_MODEL, LANES), lambda i, j: (0, 0)),
        ],
        out_specs=[
            pl.BlockSpec((INPROJ_TM, INPROJ_TN), lambda i, j: (i, j)),
            pl.BlockSpec((INPROJ_TM, LANES), lambda i, j: (i, 0)),
        ],
        out_shape=[
            jax.ShapeDtypeStruct((TOKENS, PROJ_DIM), BF16),
            jax.ShapeDtypeStruct((TOKENS, LANES), F32),
        ],
        scratch_shapes=[pltpu.VMEM((INPROJ_TM, D_MODEL), BF16)],
        compiler_params=_cparams(("parallel", "arbitrary")),
        name="inproj",
    )(x2d, gain, w_main, w_ab)


def _gdn_prep_body(ab_ref, alog_ref, dtb_ref, g1_ref, gt_ref):
    ab = ab_ref[...]
    lane = lax.broadcasted_iota(I32, ab.shape, 1)
    row = lax.broadcasted_iota(I32, ab.shape, 0) % GDN_CHUNK
    xa = ab + dtb_ref[...]
    softplus = jnp.maximum(xa, 0.0) + jnp.log(1.0 + jnp.exp(-jnp.abs(xa)))
    g = jnp.where(lane < GDN_HEADS, -jnp.exp(alog_ref[...]) * softplus, 0.0)
    gam = g
    s = 1
    while s < GDN_CHUNK:
        gam = gam + jnp.where(row >= s, pltpu.roll(gam, s, 0), 0.0)
        s *= 2
    gam3 = gam.reshape(N_CHUNKS, GDN_CHUNK, LANES)
    glast = jnp.broadcast_to(gam3[:, GDN_CHUNK - 1:GDN_CHUNK, :], gam3.shape).reshape(gam.shape)
    beta = _sigmoid(ab)
    h = GDN_HEADS
    out = jnp.where(lane < h, gam, 0.0)
    out = jnp.where((lane >= h) & (lane < 2 * h), beta, out)
    out = jnp.where((lane >= 2 * h) & (lane < 3 * h), pltpu.roll(jnp.exp(gam), 2 * h, 1), out)
    out = jnp.where((lane >= 3 * h) & (lane < 4 * h), pltpu.roll(jnp.exp(glast - gam), 3 * h, 1), out)
    out = jnp.where((lane >= 4 * h) & (lane < 5 * h), pltpu.roll(jnp.exp(glast), 4 * h, 1), out)
    g1_ref[...] = out
    gt_ref[0] = gam.T


def _gdn_prep(ab, alog_pad, dtb_pad):
    return pl.pallas_call(
        _gdn_prep_body,
        grid=(BATCH,),
        in_specs=[
            pl.BlockSpec((SEQ, LANES), lambda b: (b, 0)),
            pl.BlockSpec((1, LANES), lambda b: (0, 0)),
            pl.BlockSpec((1, LANES), lambda b: (0, 0)),
        ],
        out_specs=[
            pl.BlockSpec((SEQ, LANES), lambda b: (b, 0)),
            pl.BlockSpec((1, LANES, SEQ), lambda b: (b, 0, 0)),
        ],
        out_shape=[
            jax.ShapeDtypeStruct((TOKENS, LANES), F32),
            jax.ShapeDtypeStruct((BATCH, LANES, SEQ), F32),
        ],
        compiler_params=_cparams(("parallel",)),
        name="gdn_prep",
    )(ab, alog_pad, dtb_pad)


def _split_bf16(x):
    hi = x.astype(BF16)
    lo = (x - hi.astype(F32)).astype(BF16)
    return hi, lo


def _block_diag_rhs(x):
    t = jnp.concatenate([x] * GDN_GROUP, axis=0)
    rb = lax.broadcasted_iota(I32, t.shape, 0) // GDN_CHUNK
    cb = lax.broadcasted_iota(I32, t.shape, 1) // GDN_CHUNK
    return jnp.where(rb == cb, t, jnp.zeros_like(t))


def _gdn_body(q_ref, k_ref, v_ref, z_ref, g1_ref, gr_ref, cw_q_ref, cw_k_ref, cw_v_ref, gain_ref,
              o_ref,
              qbf, kbf, kdbf, rhsbf, qd, gamb, betab, cdecb, o0s, qts, vs, wps):
    h = pl.program_id(1)
    shape = (SEQ, LANES)
    row = lax.broadcasted_iota(I32, shape, 0)
    lane = lax.broadcasted_iota(I32, shape, 1)

    def conv_silu(x_ref, cw_ref):
        x = x_ref[...].astype(F32)
        cw = cw_ref[...]
        y = x * cw[CONV_W - 1:CONV_W, :]
        for s in range(1, CONV_W):
            xs = jnp.where(row >= s, pltpu.roll(x, s, 0), 0.0)
            y = y + xs * cw[CONV_W - 1 - s:CONV_W - s, :]
        return _silu(y)

    g1 = g1_ref[...]

    def col(off):
        c = jnp.sum(jnp.where(lane == off + h, g1, 0.0), axis=-1, keepdims=True)
        return jnp.broadcast_to(c, shape)

    gamb[...] = col(0)
    beta = col(GDN_HEADS)
    betab[...] = beta
    eg = col(2 * GDN_HEADS)
    kdf = col(3 * GDN_HEADS)
    cdecb[...] = col(4 * GDN_HEADS)

    q = conv_silu(q_ref, cw_q_ref)
    q = q * lax.rsqrt(jnp.sum(q * q, -1, keepdims=True) + EPS) * (GDN_DK ** -0.5)
    qbf[...] = q.astype(BF16)
    qd[...] = q * eg
    k = conv_silu(k_ref, cw_k_ref)
    k = k * lax.rsqrt(jnp.sum(k * k, -1, keepdims=True) + EPS)
    kbf[...] = k.astype(BF16)
    kdbf[...] = (k * kdf).astype(BF16)
    rhsbf[:, LANES:] = (k * (beta * eg)).astype(BF16)
    v = conv_silu(v_ref, cw_v_ref)
    rhsbf[:, :LANES] = (v * beta).astype(BF16)

    gshape = (GDN_CHUNK, GDN_GROUP_ROWS)
    ii = lax.broadcasted_iota(I32, gshape, 0)
    jl = lax.broadcasted_iota(I32, gshape, 1)
    jj = jl % GDN_CHUNK
    jb = jl // GDN_CHUNK
    eye = jnp.where(ii == jj, 1.0, 0.0).astype(F32)
    pr = lax.broadcasted_iota(I32, (2 * GDN_CHUNK, LANES), 0) // GDN_CHUNK
    tl = lax.broadcasted_iota(I32, (LANES, GDN_GROUP_ROWS), 1) // GDN_CHUNK
    nt = (((1,), (1,)), ((), ()))

    def sbs_bcast(ref, base):
        parts = []
        for m in range(GDN_GROUP):
            parts.append(ref[pl.ds(base + m * GDN_CHUNK, GDN_CHUNK), :])
        a = jnp.where(lax.broadcasted_iota(I32, parts[0].shape, 1) < GDN_CHUNK, parts[0], parts[1])
        b = jnp.where(lax.broadcasted_iota(I32, parts[0].shape, 1) < GDN_CHUNK, parts[2], parts[3])
        return jnp.concatenate([a, b], axis=1)

    def group(g, carry):
        base = pl.multiple_of(g * GDN_GROUP_ROWS, GDN_GROUP_ROWS)
        res = []
        for p in range(2):
            rows = pl.ds(base + p * 2 * GDN_CHUNK, 2 * GDN_CHUNK)
            kst = kbf[rows, :]
            qst = qbf[rows, :]
            lhs_k = jnp.concatenate([kst[:GDN_CHUNK], kst[GDN_CHUNK:]], axis=1)
            lhs_q = jnp.concatenate([qst[:GDN_CHUNK], qst[GDN_CHUNK:]], axis=1)
            lhs = jnp.concatenate([lhs_q, lhs_k], axis=0)
            zero = jnp.zeros_like(kst)
            bt = jnp.concatenate([jnp.where(pr == 0, kst, zero), jnp.where(pr == 1, kst, zero)], axis=1)
            res.append(lax.dot_general(lhs, bt, nt, preferred_element_type=F32))
        qk = jnp.concatenate([res[0][:GDN_CHUNK], res[1][:GDN_CHUNK]], axis=1)
        kk = jnp.concatenate([res[0][GDN_CHUNK:], res[1][GDN_CHUNK:]], axis=1)

        gc = sbs_bcast(gamb, base)
        bc = sbs_bcast(betab, base)
        grow = gr_ref[0, 0, pl.ds(g, 1), :]
        decay = jnp.exp(jnp.where(ii >= jj, gc - grow, -jnp.inf))
        a = jnp.where(ii > jj, kk * decay * bc, 0.0)
        qkd = qk * decay

        def prod(lhs_list, y):
            yh, yl = _split_bf16(y)
            bdh = _block_diag_rhs(yh)
            bdl = _block_diag_rhs(yl)
            his, los = zip(*[_split_bf16(x) for x in lhs_list])
            n = len(lhs_list)
            l1 = jnp.concatenate(list(his) + list(los), axis=0)
            r1 = jnp.dot(l1, bdh, preferred_element_type=F32)
            l2 = his[0] if n == 1 else jnp.concatenate(list(his), axis=0)
            r2 = jnp.dot(l2, bdl, preferred_element_type=F32)
            c = GDN_CHUNK
            return [r1[i * c:(i + 1) * c] + r1[(n + i) * c:(n + i + 1) * c] + r2[i * c:(i + 1) * c]
                    for i in range(n)]

        u = eye - a
        (x,) = prod([a], a)
        for lvl in range(1, 6):
            if lvl < 5:
                x2, ux = prod([x, u], x)
                u = u + ux
                x = x2
            else:
                (ux,) = prod([u], x)
                u = u + ux
        t = u

        def block_diag_lhs(x):
            tt = jnp.concatenate([x] * GDN_GROUP, axis=0)
            rb = lax.broadcasted_iota(I32, tt.shape, 0) // GDN_CHUNK
            cb = lax.broadcasted_iota(I32, tt.shape, 1) // GDN_CHUNK
            return jnp.where(rb == cb, tt, jnp.zeros_like(tt))

        rows4 = pl.ds(base, GDN_GROUP_ROWS)
        th, tlo = _split_bf16(t)
        lhs_t = jnp.concatenate([block_diag_lhs(th), block_diag_lhs(tlo)], axis=0)
        so = jnp.dot(lhs_t, rhsbf[rows4, :], preferred_element_type=F32)
        sol = (so[:GDN_GROUP_ROWS] + so[GDN_GROUP_ROWS:]).astype(BF16)

        kdt = kdbf[rows4, :].astype(F32).T.astype(BF16)
        zt = jnp.zeros_like(kdt)
        lhs2 = jnp.concatenate(
            [block_diag_lhs(qkd.astype(BF16))] + [jnp.where(tl == m, kdt, zt) for m in range(GDN_GROUP)],
            axis=0)
        r = jnp.dot(lhs2, sol, preferred_element_type=F32)
        o0s[rows4, :] = r[:GDN_GROUP_ROWS, :LANES]
        qts[rows4, :] = qd[rows4, :] - r[:GDN_GROUP_ROWS, LANES:]
        for m in range(GDN_GROUP):
            blk = r[GDN_GROUP_ROWS + m * LANES:GDN_GROUP_ROWS + (m + 1) * LANES]
            dst = pl.ds(pl.multiple_of((g * GDN_GROUP + m) * LANES, LANES), LANES)
            vs[dst, :] = blk[:, :LANES]
            wps[dst, :] = blk[:, LANES:]
        return carry

    lax.fori_loop(0, N_GROUPS_SEQ, group, 0)

    gain = gain_ref[...]

    def chunk(c, s):
        rows = pl.ds(pl.multiple_of(c * GDN_CHUNK, GDN_CHUNK), GDN_CHUNK)
        srow = pl.ds(pl.multiple_of(c * LANES, LANES), LANES)
        sb = s.astype(BF16)
        lhs = jnp.concatenate([wps[srow, :].astype(BF16), qts[rows, :].astype(BF16)], axis=0)
        r = jnp.dot(lhs, sb, preferred_element_type=F32)
        o = r[LANES:] + o0s[rows, :]
        dec = cdecb[pl.ds(pl.multiple_of(c * GDN_CHUNK, GDN_CHUNK), 1), :]
        s_new = s * dec + vs[srow, :] - r[:LANES]
        y = o * lax.rsqrt(jnp.mean(o * o, -1, keepdims=True) + EPS) * gain
        zz = z_ref[rows, :].astype(F32)
        o_ref[rows, :] = (y * _silu(zz)).astype(BF16)
        return s_new

    lax.fori_loop(0, N_CHUNKS, chunk, jnp.zeros((GDN_DK, GDN_DV), F32))


def _gdn(proj, g1, gr, conv_w, gdn_norm):
    hblk = lambda off: pl.BlockSpec((SEQ, LANES), lambda b, h, off=off: (b, off + h))
    cblk = lambda off: pl.BlockSpec((CONV_W, LANES), lambda b, h, off=off: (0, off + h))
    big = lambda dt: pltpu.VMEM((SEQ, LANES), dt)
    return pl.pallas_call(
        _gdn_body,
        grid=(BATCH, GDN_HEADS),
        in_specs=[
            hblk(0), hblk(GDN_HEADS), hblk(2 * GDN_HEADS), hblk(3 * GDN_HEADS),
            pl.BlockSpec((SEQ, LANES), lambda b, h: (b, 0)),
            pl.BlockSpec((1, 1, N_GROUPS_SEQ, GDN_GROUP_ROWS), lambda b, h: (b, h, 0, 0)),
            cblk(0), cblk(GDN_HEADS), cblk(2 * GDN_HEADS),
            pl.BlockSpec((1, LANES), lambda b, h: (0, 0)),
        ],
        out_specs=pl.BlockSpec((SEQ, LANES), lambda b, h: (b, h)),
        out_shape=jax.ShapeDtypeStruct((TOKENS, D_MODEL), BF16),
        scratch_shapes=[
            big(BF16), big(BF16), big(BF16), pltpu.VMEM((SEQ, 2 * LANES), BF16),
            big(F32), big(F32), big(F32), big(F32), big(F32), big(F32),
            pltpu.VMEM((N_CHUNKS * LANES, LANES), F32), pltpu.VMEM((N_CHUNKS * LANES, LANES), F32),
        ],
        compiler_params=_cparams(("parallel", "parallel")),
        name="gdn",
    )(proj, proj, proj, proj, g1, gr, conv_w, conv_w, conv_w, gdn_norm)


def _swa_body(sink_ref, q_ref, kc_ref, kp_ref, vc_ref, vp_ref, o_ref):
    i = pl.program_id(1)
    kcat = jnp.concatenate([kp_ref[...], kc_ref[...]], axis=0)
    vcat = jnp.concatenate([vp_ref[...], vc_ref[...]], axis=0)
    qi = lax.broadcasted_iota(I32, (SWA_BLOCK, 2 * SWA_BLOCK), 0)
    ki = lax.broadcasted_iota(I32, (SWA_BLOCK, 2 * SWA_BLOCK), 1)
    rel = qi - ki + SWA_BLOCK
    mask = (rel >= 0) & (rel < WINDOW) & ((ki >= SWA_BLOCK) | (i > 0))
    lane_kv = lax.broadcasted_iota(I32, (2 * SWA_BLOCK, LANES), 1)
    lane_q = lax.broadcasted_iota(I32, (SWA_BLOCK, LANES), 1)
    nt = (((1,), (1,)), ((), ()))
    scale = SWA_DH ** -0.5

    for slab in range(SWA_KV_HEADS // 2):
        k2 = kcat[:, slab * LANES:(slab + 1) * LANES].astype(F32)
        v2 = vcat[:, slab * LANES:(slab + 1) * LANES].astype(F32)
        k2r = pltpu.roll(k2, SWA_DH, 1)
        v2r = pltpu.roll(v2, SWA_DH, 1)
        for sub in range(2):
            kvh = slab * 2 + sub
            if sub == 0:
                k_lo = jnp.where(lane_kv < SWA_DH, k2, 0.0)
                k_hi = jnp.where(lane_kv >= SWA_DH, k2r, 0.0)
                v_lo = jnp.where(lane_kv < SWA_DH, v2, 0.0)
                v_hi = jnp.where(lane_kv >= SWA_DH, v2r, 0.0)
            else:
                k_lo = jnp.where(lane_kv < SWA_DH, k2r, 0.0)
                k_hi = jnp.where(lane_kv >= SWA_DH, k2, 0.0)
                v_lo = jnp.where(lane_kv < SWA_DH, v2r, 0.0)
                v_hi = jnp.where(lane_kv >= SWA_DH, v2, 0.0)
            k_lo, k_hi, v_lo, v_hi = (t.astype(BF16) for t in (k_lo, k_hi, v_lo, v_hi))
            for gp in range(SWA_GROUP // 2):
                h0 = kvh * SWA_GROUP + 2 * gp
                qs = q_ref[:, h0 * SWA_DH:(h0 + 2) * SWA_DH]
                acc = None
                inv = None
                for half, (kk, vv) in enumerate(((k_lo, v_lo), (k_hi, v_hi))):
                    sink = sink_ref[h0 + half]
                    s = lax.dot_general(qs, kk, nt, preferred_element_type=F32) * scale
                    s = jnp.where(mask, s, -jnp.inf)
                    m = jnp.maximum(jnp.max(s, axis=-1, keepdims=True), sink)
                    p = jnp.exp(s - m)
                    den = jnp.sum(p, axis=-1, keepdims=True) + jnp.exp(sink - m)
                    pv = jnp.dot(p.astype(BF16), vv, preferred_element_type=F32)
                    acc = pv if acc is None else acc + pv
                    r = 1.0 / den
                    inv = r if inv is None else jnp.where(lane_q < SWA_DH, inv, r)
                o_ref[:, h0 * SWA_DH:(h0 + 2) * SWA_DH] = (acc * inv).astype(BF16)


def _swa(proj, sinks):
    nb = SEQ // SWA_BLOCK
    qcol = 4
    kcol = (7 * D_MODEL) // 256
    vcol = kcol + 1
    cur = lambda col: pl.BlockSpec((SWA_BLOCK, 256), lambda b, i, col=col: (b * nb + i, col))
    prev = lambda col: pl.BlockSpec(
        (SWA_BLOCK, 256), lambda b, i, col=col: (b * nb + jnp.maximum(i - 1, 0), col))
    return pl.pallas_call(
        _swa_body,
        grid=(BATCH, nb),
        in_specs=[
            pl.BlockSpec(memory_space=pltpu.SMEM),
            pl.BlockSpec((SWA_BLOCK, D_MODEL), lambda b, i: (b * nb + i, qcol)),
            cur(kcol), prev(kcol), cur(vcol), prev(vcol),
        ],
        out_specs=pl.BlockSpec((SWA_BLOCK, D_MODEL), lambda b, i: (b * nb + i, 0)),
        out_shape=jax.ShapeDtypeStruct((TOKENS, D_MODEL), BF16),
        compiler_params=_cparams(("parallel", "parallel")),
        name="swa",
    )(sinks, proj, proj, proj, proj, proj)


OUT_TM = 256


def _outproj_body(x_ref, yg_ref, ys_ref, gg_ref, gs_ref, w_ref, nf_ref, wrh_ref, wrl_ref, rb_ref,
                  x1_ref, hp_ref, lg_ref):
    merged = (_sigmoid(gg_ref[...].astype(F32)) * yg_ref[...].astype(F32)
              + _sigmoid(gs_ref[...].astype(F32)) * ys_ref[...].astype(F32))
    x1 = x_ref[...] + jnp.dot(merged.astype(BF16), w_ref[...], preferred_element_type=F32)
    x1_ref[...] = x1
    ms = jnp.mean(x1 * x1, axis=-1, keepdims=True)
    h = x1 * lax.rsqrt(ms + EPS) * nf_ref[...]
    hh, hl = _split_bf16(h)
    lg_ref[...] = (jnp.dot(hh, wrh_ref[...], preferred_element_type=F32)
                   + jnp.dot(hh, wrl_ref[...], preferred_element_type=F32)
                   + jnp.dot(hl, wrh_ref[...], preferred_element_type=F32) + rb_ref[...])
    half = D_MODEL // 2
    hf = hh.astype(F32)
    wa = pltpu.bitcast(hf[:, :half], U32)
    wb = pltpu.bitcast(hf[:, half:], U32)
    hp_ref[...] = (wa & jnp.uint32(0xFFFF0000)) | (wb >> 16)


def _outproj(x2d, y_gdn, y_swa, proj, w_out, norm_ffn, wr_hi, wr_lo, r_bias):
    row = lambda w: pl.BlockSpec((OUT_TM, w), lambda i: (i, 0))
    const = lambda s: pl.BlockSpec(s, lambda i: (0, 0))
    return pl.pallas_call(
        _outproj_body,
        grid=(TOKENS // OUT_TM,),
        in_specs=[
            row(D_MODEL), row(D_MODEL), row(D_MODEL),
            pl.BlockSpec((OUT_TM, D_MODEL), lambda i: (i, 5)),
            pl.BlockSpec((OUT_TM, D_MODEL), lambda i: (i, 6)),
            const((D_MODEL, D_MODEL)), const((1, D_MODEL)),
            const((D_MODEL, LANES)), const((D_MODEL, LANES)), const((1, LANES)),
        ],
        out_specs=[row(D_MODEL), row(D_MODEL // 2), row(LANES)],
        out_shape=[
            jax.ShapeDtypeStruct((TOKENS, D_MODEL), F32),
            jax.ShapeDtypeStruct((TOKENS, D_MODEL // 2), U32),
            jax.ShapeDtypeStruct((TOKENS, LANES), F32),
        ],
        compiler_params=_cparams(("parallel",)),
        name="outproj",
    )(x2d, y_gdn, y_swa, proj, proj, w_out, norm_ffn, wr_hi, wr_lo, r_bias)


ROUTE_TM = 512


def _route_body(lg_ref, tri_ref, idx_ref, wt_ref, cnt_ref, run_ref):
    @pl.when(pl.program_id(0) == 0)
    def _():
        run_ref[...] = jnp.zeros_like(run_ref)

    lg = lg_ref[...]
    lane = lax.broadcasted_iota(I32, lg.shape, 1)
    ninf = -jnp.inf
    big = jnp.int32(LANES)
    is_g = lane < N_GROUPS
    glog = jnp.where(is_g, lg, ninf)
    gmax = jnp.max(glog, axis=-1, keepdims=True)
    gden = jnp.sum(jnp.where(is_g, jnp.exp(lg - gmax), 0.0), axis=-1, keepdims=True)
    p_sel = 1.0 / gden
    grp = jnp.min(jnp.where(glog == gmax, lane, big), axis=-1, keepdims=True)
    emask = (lane >= N_GROUPS) & (lane < N_GROUPS + N_EXPERTS) & (((lane - N_GROUPS) >> 3) == grp)
    el = jnp.where(emask, lg, ninf)
    v1 = jnp.max(el, axis=-1, keepdims=True)
    i1 = jnp.min(jnp.where(el == v1, lane, big), axis=-1, keepdims=True)
    el2 = jnp.where(lane == i1, ninf, el)
    v2 = jnp.max(el2, axis=-1, keepdims=True)
    i2 = jnp.min(jnp.where(el2 == v2, lane, big), axis=-1, keepdims=True)
    e = jnp.exp(v2 - v1)
    w1 = p_sel / (1.0 + e)
    w2 = p_sel * e / (1.0 + e)
    e0 = i1 - N_GROUPS
    e1 = i2 - N_GROUPS

    oh0 = lane == e0
    oh1 = lane == e1
    onehot = jnp.where(oh0 | oh1, 1.0, 0.0)
    prefix = jnp.dot(tri_ref[...], onehot.astype(BF16), preferred_element_type=F32) + run_ref[0:1, :]
    r0 = jnp.sum(jnp.where(oh0, prefix, 0.0), axis=-1, keepdims=True).astype(I32)
    r1 = jnp.sum(jnp.where(oh1, prefix, 0.0), axis=-1, keepdims=True).astype(I32)
    run = run_ref[0:1, :] + jnp.sum(onehot, axis=0, keepdims=True)
    run_ref[...] = jnp.broadcast_to(run, run_ref.shape)
    cnt_ref[...] = jnp.broadcast_to(run, cnt_ref.shape).astype(I32)

    zi = jnp.zeros(lg.shape, I32)
    idx = jnp.where(lane == 0, e0, zi)
    idx = jnp.where(lane == 1, e1, idx)
    idx = jnp.where(lane == 2, r0, idx)
    idx = jnp.where(lane == 3, r1, idx)
    idx_ref[...] = idx
    wt_ref[...] = jnp.where(lane == 0, w1, jnp.where(lane == 1, w2, 0.0))


def _route(logits, tri):
    row = pl.BlockSpec((ROUTE_TM, LANES), lambda i: (i, 0))
    return pl.pallas_call(
        _route_body,
        grid=(TOKENS // ROUTE_TM,),
        in_specs=[row, pl.BlockSpec((ROUTE_TM, ROUTE_TM), lambda i: (0, 0))],
        out_specs=[row, row, pl.BlockSpec((8, LANES), lambda i: (0, 0))],
        out_shape=[
            jax.ShapeDtypeStruct((TOKENS, LANES), I32),
            jax.ShapeDtypeStruct((TOKENS, LANES), F32),
            jax.ShapeDtypeStruct((8, LANES), I32),
        ],
        scratch_shapes=[pltpu.VMEM((8, LANES), F32)],
        compiler_params=_cparams(("arbitrary",)),
        name="route",
    )(logits, tri)


DISP_TM = 256


def _dispatch_body(dest_ref, h_ref, xs_in_ref, xs_ref, sem):
    del xs_in_ref

    def row_copy(r, d):
        return pltpu.make_async_copy(h_ref.at[pl.ds(r, 1), :], xs_ref.at[pl.ds(d, 1), :], sem)

    def issue(r, c):
        for k in range(TOP_K):
            row_copy(r, dest_ref[TOP_K * r + k]).start()
        return c

    lax.fori_loop(0, DISP_TM, issue, 0)

    def drain(r, c):
        for k in range(TOP_K):
            row_copy(0, 0).wait()
        return c

    lax.fori_loop(0, DISP_TM, drain, 0)


def _dispatch(dest_flat, hp, xs_init):
    return pl.pallas_call(
        _dispatch_body,
        grid=(TOKENS // DISP_TM,),
        in_specs=[
            pl.BlockSpec((DISP_TM * TOP_K,), lambda i: (i,), memory_space=pltpu.SMEM),
            pl.BlockSpec((DISP_TM, D_MODEL // 2), lambda i: (i, 0)),
            pl.BlockSpec(memory_space=pl.ANY),
        ],
        out_specs=pl.BlockSpec(memory_space=pl.ANY),
        out_shape=jax.ShapeDtypeStruct((N_SLOTS, D_MODEL // 2), U32),
        scratch_shapes=[pltpu.SemaphoreType.DMA],
        input_output_aliases={2: 0},
        compiler_params=_cparams(("arbitrary",)),
        name="dispatch",
    )(dest_flat, hp, xs_init)


def _experts_body(blk_e_ref, x_ref, wg_ref, wu_ref, wd_ref, y_ref):
    del blk_e_ref
    w = x_ref[...]
    xa = pltpu.bitcast(w & jnp.uint32(0xFFFF0000), F32).astype(BF16)
    xb = pltpu.bitcast(w << 16, F32).astype(BF16)
    x = jnp.concatenate([xa, xb], axis=1)
    g = jnp.dot(x, wg_ref[0], preferred_element_type=F32)
    u = jnp.dot(x, wu_ref[0], preferred_element_type=F32)
    hb = (_silu(g) * u).astype(BF16)
    y_ref[...] = jnp.dot(hb, wd_ref[0], preferred_element_type=F32)


def _experts(blk_e, xs, w_gate, w_up, w_down):
    grid_spec = pltpu.PrefetchScalarGridSpec(
        num_scalar_prefetch=1,
        grid=(N_BLOCKS,),
        in_specs=[
            pl.BlockSpec((MOE_BLOCK, D_MODEL // 2), lambda j, be: (j, 0)),
            pl.BlockSpec((1, D_MODEL, D_EXPERT), lambda j, be: (be[j], 0, 0)),
            pl.BlockSpec((1, D_MODEL, D_EXPERT), lambda j, be: (be[j], 0, 0)),
            pl.BlockSpec((1, D_EXPERT, D_MODEL), lambda j, be: (be[j], 0, 0)),
        ],
        out_specs=pl.BlockSpec((MOE_BLOCK, D_MODEL), lambda j, be: (j, 0)),
    )
    return pl.pallas_call(
        _experts_body,
        grid_spec=grid_spec,
        out_shape=jax.ShapeDtypeStruct((N_SLOTS, D_MODEL), F32),
        compiler_params=_cparams(("arbitrary",)),
        name="experts",
    )(blk_e, xs, w_gate, w_up, w_down)


COMB_TM = 256


def _combine_body(dest_ref, wt_ref, x1_ref, gain_ref, ys_ref, o_ref, buf, sem):
    def row_copy(r, k, d):
        return pltpu.make_async_copy(ys_ref.at[pl.ds(d, 1), :], buf.at[k, pl.ds(r, 1), :], sem)

    def issue(r, c):
        for k in range(TOP_K):
            row_copy(r, k, dest_ref[TOP_K * r + k]).start()
        return c

    lax.fori_loop(0, COMB_TM, issue, 0)

    def drain(r, c):
        for k in range(TOP_K):
            row_copy(0, k, 0).wait()
        return c

    lax.fori_loop(0, COMB_TM, drain, 0)

    wt = wt_ref[...]
    y = x1_ref[...] + wt[:, 0:1] * buf[0] + wt[:, 1:2] * buf[1]
    ms = jnp.mean(y * y, axis=-1, keepdims=True)
    o_ref[...] = y * lax.rsqrt(ms + EPS) * gain_ref[...]


def _combine(dest_flat, wts, x1, norm_final, ys):
    return pl.pallas_call(
        _combine_body,
        grid=(TOKENS // COMB_TM,),
        in_specs=[
            pl.BlockSpec((COMB_TM * TOP_K,), lambda i: (i,), memory_space=pltpu.SMEM),
            pl.BlockSpec((COMB_TM, LANES), lambda i: (i, 0)),
            pl.BlockSpec((COMB_TM, D_MODEL), lambda i: (i, 0)),
            pl.BlockSpec((1, D_MODEL), lambda i: (0, 0)),
            pl.BlockSpec(memory_space=pl.ANY),
        ],
        out_specs=pl.BlockSpec((COMB_TM, D_MODEL), lambda i: (i, 0)),
        out_shape=jax.ShapeDtypeStruct((TOKENS, D_MODEL), F32),
        scratch_shapes=[pltpu.VMEM((TOP_K, COMB_TM, D_MODEL), F32), pltpu.SemaphoreType.DMA],
        compiler_params=_cparams(("arbitrary",)),
        name="combine",
    )(dest_flat, wts, x1, norm_final, ys)


def _pad_lanes(v):
    v = v.reshape(1, -1).astype(F32)
    return jnp.pad(v, ((0, 0), (0, LANES - v.shape[1])))


def kernel(x, norm_mix, w_in, conv_w, gdn_a_log, gdn_dt_bias, gdn_norm, swa_sinks, w_out, norm_ffn,
           w_router_group, b_router_group, w_router_expert, b_router_expert, w_gate, w_up, w_down,
           norm_final):
    l = 0
    x2d = x.reshape(TOKENS, D_MODEL)
    w = w_in[l]
    o_z = 3 * D_MODEL
    o_a = o_z + D_MODEL
    o_sq = o_a + 2 * GDN_HEADS
    o_sk = o_sq + D_MODEL
    o_sv = o_sk + SWA_KV_HEADS * SWA_DH
    o_gg = o_sv + SWA_KV_HEADS * SWA_DH
    o_gs = o_gg + D_MODEL
    w_main = jnp.concatenate(
        [w[:, :o_a], w[:, o_sq:o_sk], w[:, o_gg:o_gs], w[:, o_gs:], w[:, o_sk:o_sv], w[:, o_sv:o_gg]],
        axis=1).astype(BF16)
    w_ab = jnp.pad(w[:, o_a:o_sq], ((0, 0), (0, LANES - 2 * GDN_HEADS))).astype(BF16)

    proj, ab = _inproj(x2d, norm_mix[l].reshape(1, D_MODEL), w_main, w_ab)

    g1, gt = _gdn_prep(ab, _pad_lanes(gdn_a_log[l]), _pad_lanes(gdn_dt_bias[l]))
    gr = gt[:, :GDN_HEADS, :].reshape(BATCH, GDN_HEADS, N_GROUPS_SEQ, GDN_GROUP_ROWS)
    y_gdn = _gdn(proj, g1, gr, conv_w[l], gdn_norm[l].reshape(1, GDN_DV))

    y_swa = _swa(proj, swa_sinks[l].astype(F32))

    w_r = jnp.concatenate([w_router_group[l], w_router_expert[l]], axis=1).astype(F32)
    w_r = jnp.pad(w_r, ((0, 0), (0, LANES - w_r.shape[1])))
    wr_hi = w_r.astype(BF16)
    wr_lo = (w_r - wr_hi.astype(F32)).astype(BF16)
    r_bias = _pad_lanes(jnp.concatenate([b_router_group[l], b_router_expert[l]]))
    x1, hp, logits = _outproj(x2d, y_gdn, y_swa, proj, w_out[l].astype(BF16),
                              norm_ffn[l].reshape(1, D_MODEL), wr_hi, wr_lo, r_bias)

    tri = (lax.broadcasted_iota(I32, (ROUTE_TM, ROUTE_TM), 1)
           < lax.broadcasted_iota(I32, (ROUTE_TM, ROUTE_TM), 0)).astype(BF16)
    idx, wts, cnt = _route(logits, tri)

    counts = cnt[0, :N_EXPERTS]
    padded = (counts + MOE_BLOCK - 1) // MOE_BLOCK * MOE_BLOCK
    pad_end = jnp.cumsum(padded)
    pad_start = pad_end - padded
    dest = pad_start[idx[:, :TOP_K]] + idx[:, TOP_K:2 * TOP_K]
    dest_flat = dest.reshape(N_ASSIGN).astype(I32)
    blk_e = jnp.minimum(
        jnp.searchsorted(pad_end, jnp.arange(N_BLOCKS, dtype=I32) * MOE_BLOCK, side="right"),
        N_EXPERTS - 1).astype(I32)

    xs = _dispatch(dest_flat, hp, jnp.zeros((N_SLOTS, D_MODEL // 2), U32))
    ys = _experts(blk_e, xs, w_gate[l].astype(BF16), w_up[l].astype(BF16), w_down[l].astype(BF16))
    out = _combine(dest_flat, wts, x1, norm_final.reshape(1, D_MODEL), ys)
    return out.reshape(BATCH, SEQ, D_MODEL)
```

```python
import functools

import jax
import jax.numpy as jnp
from jax import lax
from jax.experimental import pallas as pl
from jax.experimental.pallas import tpu as pltpu

F32 = jnp.float32
BF16 = jnp.bfloat16
I32 = jnp.int32
U32 = jnp.uint32

D_MODEL = 2048
BATCH = 16
SEQ = 2048
TOKENS = BATCH * SEQ
EPS = 1e-6

GDN_HEADS = 16
GDN_DK = 128
GDN_DV = 128
GDN_CHUNK = 64
CONV_W = 4
GDN_GROUP = 4
GDN_GROUP_ROWS = GDN_GROUP * GDN_CHUNK
N_CHUNKS = SEQ // GDN_CHUNK
N_GROUPS_SEQ = SEQ // GDN_GROUP_ROWS

SWA_HEADS = 32
SWA_KV_HEADS = 4
SWA_DH = 64
SWA_GROUP = 8
SWA_BLOCK = 128
WINDOW = 128

N_GROUPS = 8
EXPERTS_PER_GROUP = 8
N_EXPERTS = 64
TOP_K = 2
D_EXPERT = 512
MOE_BLOCK = 128
N_ASSIGN = TOKENS * TOP_K
N_BLOCKS = N_ASSIGN // MOE_BLOCK + N_EXPERTS
N_SLOTS = N_BLOCKS * MOE_BLOCK

PROJ_DIM = 7 * D_MODEL + 2 * SWA_KV_HEADS * SWA_DH
LANES = 128

VMEM_LIMIT = 56 * 1024 * 1024


def _cparams(sem, vmem=VMEM_LIMIT):
    return pltpu.CompilerParams(dimension_semantics=sem, vmem_limit_bytes=vmem)


def _sigmoid(x):
    return 0.5 * jnp.tanh(0.5 * x) + 0.5


def _silu(x):
    h = 0.5 * x
    return h * jnp.tanh(h) + h


INPROJ_TM = 1024
INPROJ_TN = 512


def _inproj_body(x_ref, g_ref, w_ref, wab_ref, o_ref, ab_ref, h_ref):
    @pl.when(pl.program_id(1) == 0)
    def _():
        def chunk(i, c):
            r = pl.ds(pl.multiple_of(i * 128, 128), 128)
            x = x_ref[r, :]
            ms = jnp.mean(x * x, axis=-1, keepdims=True)
            h_ref[r, :] = (x * lax.rsqrt(ms + EPS) * g_ref[...]).astype(BF16)
            return c
        lax.fori_loop(0, INPROJ_TM // 128, chunk, 0)
        ab_ref[...] = jnp.dot(h_ref[...], wab_ref[...], preferred_element_type=F32)

    o_ref[...] = jnp.dot(h_ref[...], w_ref[...], preferred_element_type=F32).astype(BF16)


def _inproj(x2d, gain, w_main, w_ab):
    grid = (TOKENS // INPROJ_TM, PROJ_DIM // INPROJ_TN)
    return pl.pallas_call(
        _inproj_body,
        grid=grid,
        in_specs=[
            pl.BlockSpec((INPROJ_TM, D_MODEL), lambda i, j: (i, 0)),
            pl.BlockSpec((1, D_MODEL), lambda i, j: (0, 0)),
            pl.BlockSpec((D_MODEL, INPROJ_TN), lambda i, j: (0, j)),
            pl.BlockSpec((D_MODEL, LANES), lambda i, j: (0, 0)),
        ],
        out_specs=[
            pl.BlockSpec((INPROJ_TM, INPROJ_TN), lambda i, j: (i, j)),
            pl.BlockSpec((INPROJ_TM, LANES), lambda i, j: (i, 0)),
        ],
        out_shape=[
            jax.ShapeDtypeStruct((TOKENS, PROJ_DIM), BF16),
            jax.ShapeDtypeStruct((TOKENS, LANES), F32),
        ],
        scratch_shapes=[pltpu.VMEM((INPROJ_TM, D_MODEL), BF16)],
        compiler_params=_cparams(("parallel", "arbitrary")),
        name="inproj",
    )(x2d, gain, w_main, w_ab)


def _gdn_prep_body(ab_ref, alog_ref, dtb_ref, g1_ref, gt_ref):
    ab = ab_ref[...]
    lane = lax.broadcasted_iota(I32, ab.shape, 1)
    row = lax.broadcasted_iota(I32, ab.shape, 0) % GDN_CHUNK
    xa = ab + dtb_ref[...]
    softplus = jnp.maximum(xa, 0.0) + jnp.log(1.0 + jnp.exp(-jnp.abs(xa)))
    g = jnp.where(lane < GDN_HEADS, -jnp.exp(alog_ref[...]) * softplus, 0.0)
    gam = g
    s = 1
    while s < GDN_CHUNK:
        gam = gam + jnp.where(row >= s, pltpu.roll(gam, s, 0), 0.0)
        s *= 2
    g1_ref[...] = jnp.where(lane < GDN_HEADS, gam, _sigmoid(ab))
    gt_ref[0] = gam.T


def _gdn_prep(ab, alog_pad, dtb_pad):
    return pl.pallas_call(
        _gdn_prep_body,
        grid=(BATCH,),
        in_specs=[
            pl.BlockSpec((SEQ, LANES), lambda b: (b, 0)),
            pl.BlockSpec((1, LANES), lambda b: (0, 0)),
            pl.BlockSpec((1, LANES), lambda b: (0, 0)),
        ],
        out_specs=[
            pl.BlockSpec((SEQ, LANES), lambda b: (b, 0)),
            pl.BlockSpec((1, LANES, SEQ), lambda b: (b, 0, 0)),
        ],
        out_shape=[
            jax.ShapeDtypeStruct((TOKENS, LANES), F32),
            jax.ShapeDtypeStruct((BATCH, LANES, SEQ), F32),
        ],
        compiler_params=_cparams(("parallel",)),
        name="gdn_prep",
    )(ab, alog_pad, dtb_pad)


GDN_HB = 2
GDN_PAIR = 2
N_PAIRS = N_GROUPS_SEQ // GDN_PAIR
PAIR_CHUNKS = GDN_PAIR * GDN_GROUP


def _split_bf16(x):
    hi = x.astype(BF16)
    lo = (x - hi.astype(F32)).astype(BF16)
    return hi, lo


def _block_diag(x):
    t = jnp.concatenate([x] * GDN_GROUP, axis=0)
    rb = lax.broadcasted_iota(I32, t.shape, 0) // GDN_CHUNK
    cb = lax.broadcasted_iota(I32, t.shape, 1) // GDN_CHUNK
    return jnp.where(rb == cb, t, jnp.zeros_like(t))


def _sbs_product(lhs_list, y):
    yh, yl = _split_bf16(y)
    bdh = _block_diag(yh)
    bdl = _block_diag(yl)
    his, los = zip(*[_split_bf16(x) for x in lhs_list])
    n = len(lhs_list)
    l1 = jnp.concatenate(list(his) + list(los), axis=0)
    r1 = jnp.dot(l1, bdh, preferred_element_type=F32)
    l2 = his[0] if n == 1 else jnp.concatenate(list(his), axis=0)
    r2 = jnp.dot(l2, bdl, preferred_element_type=F32)
    c = GDN_CHUNK
    return [r1[i * c:(i + 1) * c] + r1[(n + i) * c:(n + i + 1) * c] + r2[i * c:(i + 1) * c]
            for i in range(n)]


def _gdn_body(q_ref, k_ref, v_ref, z_ref, g1_ref, gr_ref, cw_q_ref, cw_k_ref, cw_v_ref, gain_ref,
              o_ref,
              qbf, kbf, kdbf, rhsbf, qd, gamb, betab, o0s, qts, vs, wps):
    hg = pl.program_id(1)
    shape = (SEQ, LANES)
    lane = lax.broadcasted_iota(I32, shape, 1)
    row8 = lax.broadcasted_iota(I32, (8, LANES), 0)

    def conv_silu(x_ref, cw_ref, hs):
        x = x_ref[:, hs].astype(F32)
        cw = cw_ref[:, hs]
        y = x * cw[CONV_W - 1:CONV_W, :]
        for s in range(1, CONV_W):
            xs = pltpu.roll(x, s, 0)
            xs = jnp.concatenate([jnp.where(row8 >= s, xs[:8], 0.0), xs[8:]], axis=0)
            y = y + xs * cw[CONV_W - 1 - s:CONV_W - s, :]
        return _silu(y)

    g1 = g1_ref[...]
    for hd in range(GDN_HB):
        hs = slice(hd * LANES, (hd + 1) * LANES)
        head = hg * GDN_HB + hd

        def col(off):
            c = jnp.sum(jnp.where(lane == off + head, g1, 0.0), axis=-1, keepdims=True)
            return jnp.broadcast_to(c, shape)

        gam = col(0)
        beta = col(GDN_HEADS)
        gamb[hd] = gam
        betab[hd] = beta
        gam3 = gam.reshape(N_CHUNKS, GDN_CHUNK, LANES)
        glast = jnp.broadcast_to(gam3[:, GDN_CHUNK - 1:GDN_CHUNK, :], gam3.shape).reshape(shape)
        eg = jnp.exp(gam)
        kdf = jnp.exp(glast - gam)

        q = conv_silu(q_ref, cw_q_ref, hs)
        q = q * (lax.rsqrt(jnp.sum(q * q, -1, keepdims=True) + EPS) * (GDN_DK ** -0.5))
        qbf[hd] = q.astype(BF16)
        qd[hd] = q * eg
        k = conv_silu(k_ref, cw_k_ref, hs)
        k = k * lax.rsqrt(jnp.sum(k * k, -1, keepdims=True) + EPS)
        kbf[hd] = k.astype(BF16)
        kdbf[hd] = (k * kdf).astype(BF16)
        rhsbf[hd, :, LANES:] = (k * (beta * eg)).astype(BF16)
        v = conv_silu(v_ref, cw_v_ref, hs)
        rhsbf[hd, :, :LANES] = (v * beta).astype(BF16)

    gshape = (GDN_CHUNK, GDN_GROUP_ROWS)
    ii = lax.broadcasted_iota(I32, gshape, 0)
    jj = lax.broadcasted_iota(I32, gshape, 1) % GDN_CHUNK
    eye = jnp.where(ii == jj, 1.0, 0.0).astype(F32)
    pr = lax.broadcasted_iota(I32, (2 * GDN_CHUNK, LANES), 0) // GDN_CHUNK
    tl = lax.broadcasted_iota(I32, (LANES, GDN_GROUP_ROWS), 1) // GDN_CHUNK
    lane_c = lax.broadcasted_iota(I32, (GDN_CHUNK, LANES), 1)
    nt = (((1,), (1,)), ((), ()))

    def sbs_bcast(ref, hd, base):
        parts = [ref[hd, pl.ds(base + m * GDN_CHUNK, GDN_CHUNK), :] for m in range(GDN_GROUP)]
        a = jnp.where(lane_c < GDN_CHUNK, parts[0], parts[1])
        b = jnp.where(lane_c < GDN_CHUNK, parts[2], parts[3])
        return jnp.concatenate([a, b], axis=1)

    def phase1(hd, g):
        base = pl.multiple_of(g * GDN_GROUP_ROWS, GDN_GROUP_ROWS)
        res = []
        for p in range(2):
            rows = pl.ds(base + p * 2 * GDN_CHUNK, 2 * GDN_CHUNK)
            kst = kbf[hd, rows, :]
            qst = qbf[hd, rows, :]
            lhs_k = jnp.concatenate([kst[:GDN_CHUNK], kst[GDN_CHUNK:]], axis=1)
            lhs_q = jnp.concatenate([qst[:GDN_CHUNK], qst[GDN_CHUNK:]], axis=1)
            lhs = jnp.concatenate([lhs_q, lhs_k], axis=0)
            zero = jnp.zeros_like(kst)
            bt = jnp.concatenate([jnp.where(pr == 0, kst, zero), jnp.where(pr == 1, kst, zero)], axis=1)
            res.append(lax.dot_general(lhs, bt, nt, preferred_element_type=F32))
        yield
        qk = jnp.concatenate([res[0][:GDN_CHUNK], res[1][:GDN_CHUNK]], axis=1)
        kk = jnp.concatenate([res[0][GDN_CHUNK:], res[1][GDN_CHUNK:]], axis=1)

        gc = sbs_bcast(gamb, hd, base)
        bc = sbs_bcast(betab, hd, base)
        grow = gr_ref[0, hd, pl.ds(g, 1), :]
        decay = jnp.exp(jnp.where(ii >= jj, gc - grow, -jnp.inf))
        a = jnp.where(ii > jj, kk * decay * bc, 0.0)
        qkd = qk * decay

        u = eye - a
        (x,) = _sbs_product([a], a)
        yield
        for lvl in range(1, 6):
            if lvl < 5:
                x2, ux = _sbs_product([x, u], x)
                u = u + ux
                x = x2
            else:
                (ux,) = _sbs_product([u], x)
                u = u + ux
            yield

        rows4 = pl.ds(base, GDN_GROUP_ROWS)
        th, tlo = _split_bf16(u)
        lhs_t = jnp.concatenate([_block_diag(th), _block_diag(tlo)], axis=0)
        so = jnp.dot(lhs_t, rhsbf[hd, rows4, :], preferred_element_type=F32)
        yield
        sol = (so[:GDN_GROUP_ROWS] + so[GDN_GROUP_ROWS:]).astype(BF16)

        kdt = kdbf[hd, rows4, :].astype(F32).T.astype(BF16)
        zt = jnp.zeros_like(kdt)
        lhs2 = jnp.concatenate(
            [_block_diag(qkd.astype(BF16))] + [jnp.where(tl == m, kdt, zt) for m in range(GDN_GROUP)],
            axis=0)
        r = jnp.dot(lhs2, sol, preferred_element_type=F32)
        o0s[hd, rows4, :] = r[:GDN_GROUP_ROWS, :LANES]
        qts[hd, rows4, :] = qd[hd, rows4, :] - r[:GDN_GROUP_ROWS, LANES:]
        for m in range(GDN_GROUP):
            blk = r[GDN_GROUP_ROWS + m * LANES:GDN_GROUP_ROWS + (m + 1) * LANES]
            dst = pl.ds(pl.multiple_of((g * GDN_GROUP + m) * LANES, LANES), LANES)
            vs[hd, dst, :] = blk[:, :LANES]
            wps[hd, dst, :] = blk[:, LANES:]

    gain = gain_ref[...]

    def phase2(hd, c, s):
        rows = pl.ds(pl.multiple_of(c * GDN_CHUNK, GDN_CHUNK), GDN_CHUNK)
        srow = pl.ds(pl.multiple_of(c * LANES, LANES), LANES)
        lhs = jnp.concatenate([wps[hd, srow, :].astype(BF16), qts[hd, rows, :].astype(BF16)], axis=0)
        r = jnp.dot(lhs, s.astype(BF16), preferred_element_type=F32)
        o = r[LANES:] + o0s[hd, rows, :]
        dec = jnp.exp(gamb[hd, pl.ds(c * GDN_CHUNK + GDN_CHUNK - 1, 1), :])
        s_new = s * dec + vs[hd, srow, :] - r[:LANES]
        y = o * lax.rsqrt(jnp.mean(o * o, -1, keepdims=True) + EPS) * gain
        zz = z_ref[rows, hd * LANES:(hd + 1) * LANES].astype(F32)
        o_ref[rows, hd * LANES:(hd + 1) * LANES] = (y * _silu(zz)).astype(BF16)
        return s_new

    def phase2_chain(hd, pair, states):
        s = states[hd]
        for j in range(PAIR_CHUNKS):
            s = phase2(hd, pair * PAIR_CHUNKS + j, s)
            yield
        states[hd] = s

    def run_interleaved(chains):
        live = list(chains)
        while live:
            nxt = []
            for c in live:
                try:
                    next(c)
                    nxt.append(c)
                except StopIteration:
                    pass
            live = nxt

    def phase1_chains(pair):
        return [phase1(hd, pair * GDN_PAIR + gg) for gg in range(GDN_PAIR) for hd in range(GDN_HB)]

    run_interleaved(phase1_chains(0))

    def body(pair, states):
        states = list(states)
        run_interleaved([phase2_chain(hd, pair - 1, states) for hd in range(GDN_HB)] + phase1_chains(pair))
        return tuple(states)

    s0 = tuple(jnp.zeros((GDN_DK, GDN_DV), F32) for _ in range(GDN_HB))
    states = list(lax.fori_loop(1, N_PAIRS, body, s0))
    run_interleaved([phase2_chain(hd, N_PAIRS - 1, states) for hd in range(GDN_HB)])


def _gdn(proj, g1, gr, conv_w, gdn_norm):
    w = GDN_HB * LANES
    per_row = D_MODEL // w
    hblk = lambda off: pl.BlockSpec((SEQ, w), lambda b, h, off=off: (b, off * per_row + h))
    cblk = lambda off: pl.BlockSpec((CONV_W, w), lambda b, h, off=off: (0, off * per_row + h))
    big = lambda dt: pltpu.VMEM((GDN_HB, SEQ, LANES), dt)
    return pl.pallas_call(
        _gdn_body,
        grid=(BATCH, GDN_HEADS // GDN_HB),
        in_specs=[
            hblk(0), hblk(1), hblk(2), hblk(3),
            pl.BlockSpec((SEQ, LANES), lambda b, h: (b, 0)),
            pl.BlockSpec((1, GDN_HB, N_GROUPS_SEQ, GDN_GROUP_ROWS), lambda b, h: (b, h, 0, 0)),
            cblk(0), cblk(1), cblk(2),
            pl.BlockSpec((1, LANES), lambda b, h: (0, 0)),
        ],
        out_specs=pl.BlockSpec((SEQ, w), lambda b, h: (b, h)),
        out_shape=jax.ShapeDtypeStruct((TOKENS, D_MODEL), BF16),
        scratch_shapes=[
            big(BF16), big(BF16), big(BF16), pltpu.VMEM((GDN_HB, SEQ, 2 * LANES), BF16),
            big(F32), big(F32), big(F32), big(F32), big(F32),
            pltpu.VMEM((GDN_HB, N_CHUNKS * LANES, LANES), F32),
            pltpu.VMEM((GDN_HB, N_CHUNKS * LANES, LANES), F32),
        ],
        compiler_params=_cparams(("parallel", "parallel")),
        name="gdn",
    )(proj, proj, proj, proj, g1, gr, conv_w, conv_w, conv_w, gdn_norm)


SWA_PIPE = 3


def _swa_body(sink_ref, q_ref, kc_ref, kp_ref, vc_ref, vp_ref, o_ref):
    i = pl.program_id(1)
    kcat = jnp.concatenate([kp_ref[...], kc_ref[...]], axis=0)
    vcat = jnp.concatenate([vp_ref[...], vc_ref[...]], axis=0)
    qi = lax.broadcasted_iota(I32, (SWA_BLOCK, 2 * SWA_BLOCK), 0)
    ki = lax.broadcasted_iota(I32, (SWA_BLOCK, 2 * SWA_BLOCK), 1)
    rel = qi - ki + SWA_BLOCK
    mask = (rel >= 0) & (rel < WINDOW) & ((ki >= SWA_BLOCK) | (i > 0))
    lane_kv = lax.broadcasted_iota(I32, (2 * SWA_BLOCK, LANES), 1)
    lane_q = lax.broadcasted_iota(I32, (SWA_BLOCK, LANES), 1)
    nt = (((1,), (1,)), ((), ()))
    scale = SWA_DH ** -0.5

    def head_pair(h0, ks, vs):
        qs = q_ref[:, h0 * SWA_DH:(h0 + 2) * SWA_DH]
        scores = [lax.dot_general(qs, kk, nt, preferred_element_type=F32) for kk in ks]
        for _ in range(SWA_PIPE):
            yield
        acc = None
        inv = None
        for half in range(2):
            sink = sink_ref[h0 + half]
            s = jnp.where(mask, scores[half] * scale, -jnp.inf)
            m = jnp.maximum(jnp.max(s, axis=-1, keepdims=True), sink)
            p = jnp.exp(s - m)
            den = jnp.sum(p, axis=-1, keepdims=True) + jnp.exp(sink - m)
            pv = jnp.dot(p.astype(BF16), vs[half], preferred_element_type=F32)
            acc = pv if acc is None else acc + pv
            r = 1.0 / den
            inv = r if inv is None else jnp.where(lane_q < SWA_DH, inv, r)
        yield
        o_ref[:, h0 * SWA_DH:(h0 + 2) * SWA_DH] = (acc * inv).astype(BF16)

    chains = []
    for slab in range(SWA_KV_HEADS // 2):
        k2 = kcat[:, slab * LANES:(slab + 1) * LANES].astype(F32)
        v2 = vcat[:, slab * LANES:(slab + 1) * LANES].astype(F32)
        k2r = pltpu.roll(k2, SWA_DH, 1)
        v2r = pltpu.roll(v2, SWA_DH, 1)
        for sub in range(2):
            kvh = slab * 2 + sub
            if sub == 0:
                k_lo = jnp.where(lane_kv < SWA_DH, k2, 0.0)
                k_hi = jnp.where(lane_kv >= SWA_DH, k2r, 0.0)
                v_lo = jnp.where(lane_kv < SWA_DH, v2, 0.0)
                v_hi = jnp.where(lane_kv >= SWA_DH, v2r, 0.0)
            else:
                k_lo = jnp.where(lane_kv < SWA_DH, k2r, 0.0)
                k_hi = jnp.where(lane_kv >= SWA_DH, k2, 0.0)
                v_lo = jnp.where(lane_kv < SWA_DH, v2r, 0.0)
                v_hi = jnp.where(lane_kv >= SWA_DH, v2, 0.0)
            k_lo, k_hi, v_lo, v_hi = (t.astype(BF16) for t in (k_lo, k_hi, v_lo, v_hi))
            for gp in range(SWA_GROUP // 2):
                h0 = kvh * SWA_GROUP + 2 * gp
                chains.append(head_pair(h0, (k_lo, k_hi), (v_lo, v_hi)))

    live = []
    pending = list(chains)
    while pending or live:
        if pending:
            live.append(pending.pop(0))
        nxt = []
        for c in live:
            try:
                next(c)
                nxt.append(c)
            except StopIteration:
                pass
        live = nxt


def _swa(proj, sinks):
    nb = SEQ // SWA_BLOCK
    qcol = 4
    kcol = (7 * D_MODEL) // 256
    vcol = kcol + 1
    cur = lambda col: pl.BlockSpec((SWA_BLOCK, 256), lambda b, i, col=col: (b * nb + i, col))
    prev = lambda col: pl.BlockSpec(
        (SWA_BLOCK, 256), lambda b, i, col=col: (b * nb + jnp.maximum(i - 1, 0), col))
    return pl.pallas_call(
        _swa_body,
        grid=(BATCH, nb),
        in_specs=[
            pl.BlockSpec(memory_space=pltpu.SMEM),
            pl.BlockSpec((SWA_BLOCK, D_MODEL), lambda b, i: (b * nb + i, qcol)),
            cur(kcol), prev(kcol), cur(vcol), prev(vcol),
        ],
        out_specs=pl.BlockSpec((SWA_BLOCK, D_MODEL), lambda b, i: (b * nb + i, 0)),
        out_shape=jax.ShapeDtypeStruct((TOKENS, D_MODEL), BF16),
        compiler_params=_cparams(("parallel", "parallel")),
        name="swa",
    )(sinks, proj, proj, proj, proj, proj)


OUT_TM = 256


def _outproj_body(x_ref, yg_ref, ys_ref, gg_ref, gs_ref, w_ref, nf_ref, wrh_ref, wrl_ref, rb_ref,
                  x1_ref, hp_ref, lg_ref):
    merged = (_sigmoid(gg_ref[...].astype(F32)) * yg_ref[...].astype(F32)
              + _sigmoid(gs_ref[...].astype(F32)) * ys_ref[...].astype(F32))
    x1 = x_ref[...] + jnp.dot(merged.astype(BF16), w_ref[...], preferred_element_type=F32)
    x1_ref[...] = x1
    ms = jnp.mean(x1 * x1, axis=-1, keepdims=True)
    h = x1 * lax.rsqrt(ms + EPS) * nf_ref[...]
    hh, hl = _split_bf16(h)
    lg_ref[...] = (jnp.dot(hh, wrh_ref[...], preferred_element_type=F32)
                   + jnp.dot(hh, wrl_ref[...], preferred_element_type=F32)
                   + jnp.dot(hl, wrh_ref[...], preferred_element_type=F32) + rb_ref[...])
    half = D_MODEL // 2
    hf = hh.astype(F32)
    wa = pltpu.bitcast(hf[:, :half], U32)
    wb = pltpu.bitcast(hf[:, half:], U32)
    hp_ref[...] = (wa & jnp.uint32(0xFFFF0000)) | (wb >> 16)


def _outproj(x2d, y_gdn, y_swa, proj, w_out, norm_ffn, wr_hi, wr_lo, r_bias):
    row = lambda w: pl.BlockSpec((OUT_TM, w), lambda i: (i, 0))
    const = lambda s: pl.BlockSpec(s, lambda i: (0, 0))
    return pl.pallas_call(
        _outproj_body,
        grid=(TOKENS // OUT_TM,),
        in_specs=[
            row(D_MODEL), row(D_MODEL), row(D_MODEL),
            pl.BlockSpec((OUT_TM, D_MODEL), lambda i: (i, 5)),
            pl.BlockSpec((OUT_TM, D_MODEL), lambda i: (i, 6)),
            const((D_MODEL, D_MODEL)), const((1, D_MODEL)),
            const((D_MODEL, LANES)), const((D_MODEL, LANES)), const((1, LANES)),
        ],
        out_specs=[row(D_MODEL), row(D_MODEL // 2), row(LANES)],
        out_shape=[
            jax.ShapeDtypeStruct((TOKENS, D_MODEL), F32),
            jax.ShapeDtypeStruct((TOKENS, D_MODEL // 2), U32),
            jax.ShapeDtypeStruct((TOKENS, LANES), F32),
        ],
        compiler_params=_cparams(("parallel",)),
        name="outproj",
    )(x2d, y_gdn, y_swa, proj, proj, w_out, norm_ffn, wr_hi, wr_lo, r_bias)


ROUTE_TM = 512


def _route_body(lg_ref, tri_ref, idx_ref, wt_ref, cnt_ref, run_ref):
    @pl.when(pl.program_id(0) == 0)
    def _():
        run_ref[...] = jnp.zeros_like(run_ref)

    lg = lg_ref[...]
    lane = lax.broadcasted_iota(I32, lg.shape, 1)
    ninf = -jnp.inf
    big = jnp.int32(LANES)
    is_g = lane < N_GROUPS
    glog = jnp.where(is_g, lg, ninf)
    gmax = jnp.max(glog, axis=-1, keepdims=True)
    gden = jnp.sum(jnp.where(is_g, jnp.exp(lg - gmax), 0.0), axis=-1, keepdims=True)
    p_sel = 1.0 / gden
    grp = jnp.min(jnp.where(glog == gmax, lane, big), axis=-1, keepdims=True)
    emask = (lane >= N_GROUPS) & (lane < N_GROUPS + N_EXPERTS) & (((lane - N_GROUPS) >> 3) == grp)
    el = jnp.where(emask, lg, ninf)
    v1 = jnp.max(el, axis=-1, keepdims=True)
    i1 = jnp.min(jnp.where(el == v1, lane, big), axis=-1, keepdims=True)
    el2 = jnp.where(lane == i1, ninf, el)
    v2 = jnp.max(el2, axis=-1, keepdims=True)
    i2 = jnp.min(jnp.where(el2 == v2, lane, big), axis=-1, keepdims=True)
    e = jnp.exp(v2 - v1)
    w1 = p_sel / (1.0 + e)
    w2 = p_sel * e / (1.0 + e)
    e0 = i1 - N_GROUPS
    e1 = i2 - N_GROUPS

    oh0 = lane == e0
    oh1 = lane == e1
    onehot = jnp.where(oh0 | oh1, 1.0, 0.0)
    prefix = jnp.dot(tri_ref[...], onehot.astype(BF16), preferred_element_type=F32) + run_ref[0:1, :]
    r0 = jnp.sum(jnp.where(oh0, prefix, 0.0), axis=-1, keepdims=True).astype(I32)
    r1 = jnp.sum(jnp.where(oh1, prefix, 0.0), axis=-1, keepdims=True).astype(I32)
    run = run_ref[0:1, :] + jnp.sum(onehot, axis=0, keepdims=True)
    run_ref[...] = jnp.broadcast_to(run, run_ref.shape)
    cnt_ref[...] = jnp.broadcast_to(run, cnt_ref.shape).astype(I32)

    zi = jnp.zeros(lg.shape, I32)
    idx = jnp.where(lane == 0, e0, zi)
    idx = jnp.where(lane == 1, e1, idx)
    idx = jnp.where(lane == 2, r0, idx)
    idx = jnp.where(lane == 3, r1, idx)
    idx_ref[...] = idx
    wt_ref[...] = jnp.where(lane == 0, w1, jnp.where(lane == 1, w2, 0.0))


def _route(logits, tri):
    row = pl.BlockSpec((ROUTE_TM, LANES), lambda i: (i, 0))
    return pl.pallas_call(
        _route_body,
        grid=(TOKENS // ROUTE_TM,),
        in_specs=[row, pl.BlockSpec((ROUTE_TM, ROUTE_TM), lambda i: (0, 0))],
        out_specs=[row, row, pl.BlockSpec((8, LANES), lambda i: (0, 0))],
        out_shape=[
            jax.ShapeDtypeStruct((TOKENS, LANES), I32),
            jax.ShapeDtypeStruct((TOKENS, LANES), F32),
            jax.ShapeDtypeStruct((8, LANES), I32),
        ],
        scratch_shapes=[pltpu.VMEM((8, LANES), F32)],
        compiler_params=_cparams(("arbitrary",)),
        name="route",
    )(logits, tri)


DISP_TM = 256


def _dispatch_body(dest_ref, h_ref, xs_in_ref, xs_ref, sem):
    del xs_in_ref

    def row_copy(r, d):
        return pltpu.make_async_copy(h_ref.at[pl.ds(r, 1), :], xs_ref.at[pl.ds(d, 1), :], sem)

    def issue(r, c):
        for k in range(TOP_K):
            row_copy(r, dest_ref[TOP_K * r + k]).start()
        return c

    lax.fori_loop(0, DISP_TM, issue, 0, unroll=4)
    for k in range(TOP_K):
        pltpu.make_async_copy(h_ref, xs_ref.at[pl.ds(0, DISP_TM), :], sem).wait()


def _dispatch(dest_flat, hp, xs_init):
    return pl.pallas_call(
        _dispatch_body,
        grid=(TOKENS // DISP_TM,),
        in_specs=[
            pl.BlockSpec((DISP_TM * TOP_K,), lambda i: (i,), memory_space=pltpu.SMEM),
            pl.BlockSpec((DISP_TM, D_MODEL // 2), lambda i: (i, 0)),
            pl.BlockSpec(memory_space=pl.ANY),
        ],
        out_specs=pl.BlockSpec(memory_space=pl.ANY),
        out_shape=jax.ShapeDtypeStruct((N_SLOTS, D_MODEL // 2), U32),
        scratch_shapes=[pltpu.SemaphoreType.DMA],
        input_output_aliases={2: 0},
        compiler_params=_cparams(("arbitrary",)),
        name="dispatch",
    )(dest_flat, hp, xs_init)


def _experts_body(blk_e_ref, x_ref, wg_ref, wu_ref, wd_ref, y_ref, wg_bf, wu_bf, wd_bf):
    j = pl.program_id(0)
    new_expert = (j == 0) | (blk_e_ref[j] != blk_e_ref[jnp.maximum(j - 1, 0)])

    @pl.when(new_expert)
    def _():
        wg_bf[...] = wg_ref[0].astype(BF16)
        wu_bf[...] = wu_ref[0].astype(BF16)
        wd_bf[...] = wd_ref[0].astype(BF16)

    w = x_ref[...]
    xa = pltpu.bitcast(w & jnp.uint32(0xFFFF0000), F32).astype(BF16)
    xb = pltpu.bitcast(w << 16, F32).astype(BF16)
    x = jnp.concatenate([xa, xb], axis=1)
    g = jnp.dot(x, wg_bf[...], preferred_element_type=F32)
    u = jnp.dot(x, wu_bf[...], preferred_element_type=F32)
    hb = (_silu(g) * u).astype(BF16)
    y_ref[...] = jnp.dot(hb, wd_bf[...], preferred_element_type=F32)


def _experts(blk_e, xs, w_gate, w_up, w_down):
    grid_spec = pltpu.PrefetchScalarGridSpec(
        num_scalar_prefetch=1,
        grid=(N_BLOCKS,),
        in_specs=[
            pl.BlockSpec((MOE_BLOCK, D_MODEL // 2), lambda j, be: (j, 0)),
            pl.BlockSpec((1, D_MODEL, D_EXPERT), lambda j, be: (be[j], 0, 0)),
            pl.BlockSpec((1, D_MODEL, D_EXPERT), lambda j, be: (be[j], 0, 0)),
            pl.BlockSpec((1, D_EXPERT, D_MODEL), lambda j, be: (be[j], 0, 0)),
        ],
        out_specs=pl.BlockSpec((MOE_BLOCK, D_MODEL), lambda j, be: (j, 0)),
        scratch_shapes=[
            pltpu.VMEM((D_MODEL, D_EXPERT), BF16),
            pltpu.VMEM((D_MODEL, D_EXPERT), BF16),
            pltpu.VMEM((D_EXPERT, D_MODEL), BF16),
        ],
    )
    return pl.pallas_call(
        _experts_body,
        grid_spec=grid_spec,
        out_shape=jax.ShapeDtypeStruct((N_SLOTS, D_MODEL), F32),
        compiler_params=_cparams(("arbitrary",)),
        name="experts",
    )(blk_e, xs, w_gate, w_up, w_down)


COMB_TM = 256


def _combine_body(dest_ref, wt_ref, x1_ref, gain_ref, ys_ref, o_ref, buf, sem):
    def row_copy(r, k, d):
        return pltpu.make_async_copy(ys_ref.at[pl.ds(d, 1), :], buf.at[k, pl.ds(r, 1), :], sem)

    def issue(r, c):
        for k in range(TOP_K):
            row_copy(r, k, dest_ref[TOP_K * r + k]).start()
        return c

    lax.fori_loop(0, COMB_TM, issue, 0, unroll=4)
    for k in range(TOP_K):
        pltpu.make_async_copy(ys_ref.at[pl.ds(0, COMB_TM), :], buf.at[k], sem).wait()

    wt = wt_ref[...]
    y = x1_ref[...] + wt[:, 0:1] * buf[0] + wt[:, 1:2] * buf[1]
    ms = jnp.mean(y * y, axis=-1, keepdims=True)
    o_ref[...] = y * lax.rsqrt(ms + EPS) * gain_ref[...]


def _combine(dest_flat, wts, x1, norm_final, ys):
    return pl.pallas_call(
        _combine_body,
        grid=(TOKENS // COMB_TM,),
        in_specs=[
            pl.BlockSpec((COMB_TM * TOP_K,), lambda i: (i,), memory_space=pltpu.SMEM),
            pl.BlockSpec((COMB_TM, LANES), lambda i: (i, 0)),
            pl.BlockSpec((COMB_TM, D_MODEL), lambda i: (i, 0)),
            pl.BlockSpec((1, D_MODEL), lambda i: (0, 0)),
            pl.BlockSpec(memory_space=pl.ANY),
        ],
        out_specs=pl.BlockSpec((COMB_TM, D_MODEL), lambda i: (i, 0)),
        out_shape=jax.ShapeDtypeStruct((TOKENS, D_MODEL), F32),
        scratch_shapes=[pltpu.VMEM((TOP_K, COMB_TM, D_MODEL), F32), pltpu.SemaphoreType.DMA],
        compiler_params=_cparams(("arbitrary",)),
        name="combine",
    )(dest_flat, wts, x1, norm_final, ys)


def _pad_lanes(v):
    v = v.reshape(1, -1).astype(F32)
    return jnp.pad(v, ((0, 0), (0, LANES - v.shape[1])))


def kernel(x, norm_mix, w_in, conv_w, gdn_a_log, gdn_dt_bias, gdn_norm, swa_sinks, w_out, norm_ffn,
           w_router_group, b_router_group, w_router_expert, b_router_expert, w_gate, w_up, w_down,
           norm_final):
    l = 0
    x2d = x.reshape(TOKENS, D_MODEL)
    w = w_in[l]
    o_z = 3 * D_MODEL
    o_a = o_z + D_MODEL
    o_sq = o_a + 2 * GDN_HEADS
    o_sk = o_sq + D_MODEL
    o_sv = o_sk + SWA_KV_HEADS * SWA_DH
    o_gg = o_sv + SWA_KV_HEADS * SWA_DH
    o_gs = o_gg + D_MODEL
    w_main = jnp.concatenate(
        [w[:, :o_a], w[:, o_sq:o_sk], w[:, o_gg:o_gs], w[:, o_gs:], w[:, o_sk:o_sv], w[:, o_sv:o_gg]],
        axis=1).astype(BF16)
    w_ab = jnp.pad(w[:, o_a:o_sq], ((0, 0), (0, LANES - 2 * GDN_HEADS))).astype(BF16)

    proj, ab = _inproj(x2d, norm_mix[l].reshape(1, D_MODEL), w_main, w_ab)

    g1, gt = _gdn_prep(ab, _pad_lanes(gdn_a_log[l]), _pad_lanes(gdn_dt_bias[l]))
    gr = gt[:, :GDN_HEADS, :].reshape(BATCH, GDN_HEADS, N_GROUPS_SEQ, GDN_GROUP_ROWS)
    y_gdn = _gdn(proj, g1, gr, conv_w[l], gdn_norm[l].reshape(1, GDN_DV))

    y_swa = _swa(proj, swa_sinks[l].astype(F32))

    w_r = jnp.concatenate([w_router_group[l], w_router_expert[l]], axis=1).astype(F32)
    w_r = jnp.pad(w_r, ((0, 0), (0, LANES - w_r.shape[1])))
    wr_hi = w_r.astype(BF16)
    wr_lo = (w_r - wr_hi.astype(F32)).astype(BF16)
    r_bias = _pad_lanes(jnp.concatenate([b_router_group[l], b_router_expert[l]]))
    x1, hp, logits = _outproj(x2d, y_gdn, y_swa, proj, w_out[l].astype(BF16),
                              norm_ffn[l].reshape(1, D_MODEL), wr_hi, wr_lo, r_bias)

    tri = (lax.broadcasted_iota(I32, (ROUTE_TM, ROUTE_TM), 1)
           < lax.broadcasted_iota(I32, (ROUTE_TM, ROUTE_TM), 0)).astype(BF16)
    idx, wts, cnt = _route(logits, tri)

    counts = cnt[0, :N_EXPERTS]
    padded = (counts + MOE_BLOCK - 1) // MOE_BLOCK * MOE_BLOCK
    pad_end = jnp.cumsum(padded)
    pad_start = pad_end - padded
    dest = pad_start[idx[:, :TOP_K]] + idx[:, TOP_K:2 * TOP_K]
    dest_flat = dest.reshape(N_ASSIGN).astype(I32)
    blk_pos = jnp.arange(N_BLOCKS, dtype=I32) * MOE_BLOCK
    blk_e = jnp.minimum(jnp.sum((pad_end[None, :] <= blk_pos[:, None]).astype(I32), axis=1), N_EXPERTS - 1)

    xs = _dispatch(dest_flat, hp, jnp.zeros((N_SLOTS, D_MODEL // 2), U32))
    ys = _experts(blk_e, xs, w_gate[l], w_up[l], w_down[l])
    out = _combine(dest_flat, wts, x1, norm_final.reshape(1, D_MODEL), ys)
    return out.reshape(BATCH, SEQ, D_MODEL)
```

```python
import functools

import jax
import jax.numpy as jnp
from jax import lax
from jax.experimental import pallas as pl
from jax.experimental.pallas import tpu as pltpu

F32 = jnp.float32
BF16 = jnp.bfloat16
I32 = jnp.int32
U32 = jnp.uint32

D_MODEL = 2048
BATCH = 16
SEQ = 2048
TOKENS = BATCH * SEQ
EPS = 1e-6

GDN_HEADS = 16
GDN_DK = 128
GDN_DV = 128
GDN_CHUNK = 64
CONV_W = 4
GDN_GROUP = 4
GDN_GROUP_ROWS = GDN_GROUP * GDN_CHUNK
N_CHUNKS = SEQ // GDN_CHUNK
N_GROUPS_SEQ = SEQ // GDN_GROUP_ROWS

SWA_HEADS = 32
SWA_KV_HEADS = 4
SWA_DH = 64
SWA_GROUP = 8
SWA_BLOCK = 128
WINDOW = 128

N_GROUPS = 8
EXPERTS_PER_GROUP = 8
N_EXPERTS = 64
TOP_K = 2
D_EXPERT = 512
MOE_BLOCK = 128
N_ASSIGN = TOKENS * TOP_K
N_BLOCKS = N_ASSIGN // MOE_BLOCK + N_EXPERTS
N_SLOTS = N_BLOCKS * MOE_BLOCK

PROJ_DIM = 7 * D_MODEL + 2 * SWA_KV_HEADS * SWA_DH
LANES = 128

VMEM_LIMIT = 56 * 1024 * 1024


def _cparams(sem, vmem=VMEM_LIMIT):
    return pltpu.CompilerParams(dimension_semantics=sem, vmem_limit_bytes=vmem)


def _sigmoid(x):
    return 0.5 * jnp.tanh(0.5 * x) + 0.5


def _silu(x):
    h = 0.5 * x
    return h * jnp.tanh(h) + h


INPROJ_TM = 2048
INPROJ_TN = 512
INPROJ_VMEM = 60 * 1024 * 1024


def _inproj_body(x_ref, g_ref, w_ref, wab_ref, o_ref, ab_ref, h_ref):
    @pl.when(pl.program_id(1) == 0)
    def _():
        def chunk(i, c):
            r = pl.ds(pl.multiple_of(i * 128, 128), 128)
            x = x_ref[r, :]
            ms = jnp.mean(x * x, axis=-1, keepdims=True)
            h_ref[r, :] = (x * lax.rsqrt(ms + EPS) * g_ref[...]).astype(BF16)
            return c
        lax.fori_loop(0, INPROJ_TM // 128, chunk, 0)
        ab_ref[...] = jnp.dot(h_ref[...], wab_ref[...], preferred_element_type=F32)

    o_ref[...] = jnp.dot(h_ref[...], w_ref[...], preferred_element_type=F32).astype(BF16)


def _inproj(x2d, gain, w_main, w_ab):
    grid = (TOKENS // INPROJ_TM, PROJ_DIM // INPROJ_TN)
    return pl.pallas_call(
        _inproj_body,
        grid=grid,
        in_specs=[
            pl.BlockSpec((INPROJ_TM, D_MODEL), lambda i, j: (i, 0)),
            pl.BlockSpec((1, D_MODEL), lambda i, j: (0, 0)),
            pl.BlockSpec((D_MODEL, INPROJ_TN), lambda i, j: (0, j)),
            pl.BlockSpec((D_MODEL, LANES), lambda i, j: (0, 0)),
        ],
        out_specs=[
            pl.BlockSpec((INPROJ_TM, INPROJ_TN), lambda i, j: (i, j)),
            pl.BlockSpec((INPROJ_TM, LANES), lambda i, j: (i, 0)),
        ],
        out_shape=[
            jax.ShapeDtypeStruct((TOKENS, PROJ_DIM), BF16),
            jax.ShapeDtypeStruct((TOKENS, LANES), F32),
        ],
        scratch_shapes=[pltpu.VMEM((INPROJ_TM, D_MODEL), BF16)],
        compiler_params=_cparams(("parallel", "arbitrary"), vmem=INPROJ_VMEM),
        name="inproj",
    )(x2d, gain, w_main, w_ab)


def _gdn_prep_body(ab_ref, alog_ref, dtb_ref, g1_ref, gt_ref):
    ab = ab_ref[...]
    lane = lax.broadcasted_iota(I32, ab.shape, 1)
    row = lax.broadcasted_iota(I32, ab.shape, 0) % GDN_CHUNK
    xa = ab + dtb_ref[...]
    softplus = jnp.maximum(xa, 0.0) + jnp.log(1.0 + jnp.exp(-jnp.abs(xa)))
    g = jnp.where(lane < GDN_HEADS, -jnp.exp(alog_ref[...]) * softplus, 0.0)
    gam = g
    s = 1
    while s < GDN_CHUNK:
        gam = gam + jnp.where(row >= s, pltpu.roll(gam, s, 0), 0.0)
        s *= 2
    g1_ref[...] = jnp.where(lane < GDN_HEADS, gam, _sigmoid(ab))
    gt_ref[0] = gam.T


def _gdn_prep(ab, alog_pad, dtb_pad):
    return pl.pallas_call(
        _gdn_prep_body,
        grid=(BATCH,),
        in_specs=[
            pl.BlockSpec((SEQ, LANES), lambda b: (b, 0)),
            pl.BlockSpec((1, LANES), lambda b: (0, 0)),
            pl.BlockSpec((1, LANES), lambda b: (0, 0)),
        ],
        out_specs=[
            pl.BlockSpec((SEQ, LANES), lambda b: (b, 0)),
            pl.BlockSpec((1, LANES, SEQ), lambda b: (b, 0, 0)),
        ],
        out_shape=[
            jax.ShapeDtypeStruct((TOKENS, LANES), F32),
            jax.ShapeDtypeStruct((BATCH, LANES, SEQ), F32),
        ],
        compiler_params=_cparams(("parallel",)),
        name="gdn_prep",
    )(ab, alog_pad, dtb_pad)


GDN_HB = 2
GDN_PAIR = 2
N_PAIRS = N_GROUPS_SEQ // GDN_PAIR
PAIR_CHUNKS = GDN_PAIR * GDN_GROUP
CONV_PAD = 8


def _split_bf16(x):
    hi = x.astype(BF16)
    lo = (x - hi.astype(F32)).astype(BF16)
    return hi, lo


def _block_diag(x):
    t = jnp.concatenate([x] * GDN_GROUP, axis=0)
    rb = lax.broadcasted_iota(I32, t.shape, 0) // GDN_CHUNK
    cb = lax.broadcasted_iota(I32, t.shape, 1) // GDN_CHUNK
    return jnp.where(rb == cb, t, jnp.zeros_like(t))


def _sbs_product(lhs_list, y):
    yh, yl = _split_bf16(y)
    bdh = _block_diag(yh)
    bdl = _block_diag(yl)
    his, los = zip(*[_split_bf16(x) for x in lhs_list])
    n = len(lhs_list)
    l1 = jnp.concatenate(list(his) + list(los), axis=0)
    r1 = jnp.dot(l1, bdh, preferred_element_type=F32)
    l2 = his[0] if n == 1 else jnp.concatenate(list(his), axis=0)
    r2 = jnp.dot(l2, bdl, preferred_element_type=F32)
    c = GDN_CHUNK
    return [r1[i * c:(i + 1) * c] + r1[(n + i) * c:(n + i + 1) * c] + r2[i * c:(i + 1) * c]
            for i in range(n)]


def _gdn_body(q_ref, k_ref, v_ref, z_ref, g1_ref, gr_ref, cw_q_ref, cw_k_ref, cw_v_ref, gain_ref,
              o_ref,
              qbf, kbf, kdbf, rhsbf, qd, gamb, betab, o0s, qts, vs, wps, xpad):
    hg = pl.program_id(1)
    shape = (SEQ, LANES)
    lane = lax.broadcasted_iota(I32, shape, 1)

    xpad[pl.ds(0, CONV_PAD), :] = jnp.zeros((CONV_PAD, LANES), F32)

    def conv_silu(x_ref, cw_ref, hs):
        xpad[pl.ds(CONV_PAD, SEQ), :] = x_ref[:, hs].astype(F32)
        cw = cw_ref[:, hs]
        y = None
        for s in range(CONV_W):
            t = xpad[pl.ds(CONV_PAD - s, SEQ), :] * cw[CONV_W - 1 - s:CONV_W - s, :]
            y = t if y is None else y + t
        return _silu(y)

    g1 = g1_ref[...]
    for hd in range(GDN_HB):
        hs = slice(hd * LANES, (hd + 1) * LANES)
        head = hg * GDN_HB + hd

        def col(off):
            c = jnp.sum(jnp.where(lane == off + head, g1, 0.0), axis=-1, keepdims=True)
            return jnp.broadcast_to(c, shape)

        gam = col(0)
        beta = col(GDN_HEADS)
        gamb[hd] = gam
        betab[hd] = beta
        gam3 = gam.reshape(N_CHUNKS, GDN_CHUNK, LANES)
        glast = jnp.broadcast_to(gam3[:, GDN_CHUNK - 1:GDN_CHUNK, :], gam3.shape).reshape(shape)
        eg = jnp.exp(gam)
        kdf = jnp.exp(glast - gam)

        q = conv_silu(q_ref, cw_q_ref, hs)
        q = q * (lax.rsqrt(jnp.sum(q * q, -1, keepdims=True) + EPS) * (GDN_DK ** -0.5))
        qbf[hd] = q.astype(BF16)
        qd[hd] = q * eg
        k = conv_silu(k_ref, cw_k_ref, hs)
        k = k * lax.rsqrt(jnp.sum(k * k, -1, keepdims=True) + EPS)
        kbf[hd] = k.astype(BF16)
        kdbf[hd] = (k * kdf).astype(BF16)
        rhsbf[hd, :, LANES:] = (k * (beta * eg)).astype(BF16)
        v = conv_silu(v_ref, cw_v_ref, hs)
        rhsbf[hd, :, :LANES] = (v * beta).astype(BF16)

    gshape = (GDN_CHUNK, GDN_GROUP_ROWS)
    ii = lax.broadcasted_iota(I32, gshape, 0)
    jj = lax.broadcasted_iota(I32, gshape, 1) % GDN_CHUNK
    eye = jnp.where(ii == jj, 1.0, 0.0).astype(F32)
    pr = lax.broadcasted_iota(I32, (2 * GDN_CHUNK, LANES), 0) // GDN_CHUNK
    tl = lax.broadcasted_iota(I32, (LANES, GDN_GROUP_ROWS), 1) // GDN_CHUNK
    lane_c = lax.broadcasted_iota(I32, (GDN_CHUNK, LANES), 1)
    nt = (((1,), (1,)), ((), ()))

    def sbs_bcast(ref, hd, base):
        parts = [ref[hd, pl.ds(base + m * GDN_CHUNK, GDN_CHUNK), :] for m in range(GDN_GROUP)]
        a = jnp.where(lane_c < GDN_CHUNK, parts[0], parts[1])
        b = jnp.where(lane_c < GDN_CHUNK, parts[2], parts[3])
        return jnp.concatenate([a, b], axis=1)

    def phase1(hd, g):
        base = pl.multiple_of(g * GDN_GROUP_ROWS, GDN_GROUP_ROWS)
        res = []
        for p in range(2):
            rows = pl.ds(base + p * 2 * GDN_CHUNK, 2 * GDN_CHUNK)
            kst = kbf[hd, rows, :]
            qst = qbf[hd, rows, :]
            lhs_k = jnp.concatenate([kst[:GDN_CHUNK], kst[GDN_CHUNK:]], axis=1)
            lhs_q = jnp.concatenate([qst[:GDN_CHUNK], qst[GDN_CHUNK:]], axis=1)
            lhs = jnp.concatenate([lhs_q, lhs_k], axis=0)
            zero = jnp.zeros_like(kst)
            bt = jnp.concatenate([jnp.where(pr == 0, kst, zero), jnp.where(pr == 1, kst, zero)], axis=1)
            res.append(lax.dot_general(lhs, bt, nt, preferred_element_type=F32))
        yield
        qk = jnp.concatenate([res[0][:GDN_CHUNK], res[1][:GDN_CHUNK]], axis=1)
        kk = jnp.concatenate([res[0][GDN_CHUNK:], res[1][GDN_CHUNK:]], axis=1)

        gc = sbs_bcast(gamb, hd, base)
        bc = sbs_bcast(betab, hd, base)
        grow = gr_ref[0, hd, pl.ds(g, 1), :]
        decay = jnp.exp(jnp.where(ii >= jj, gc - grow, -jnp.inf))
        a = jnp.where(ii > jj, kk * decay * bc, 0.0)
        qkd = qk * decay

        u = eye - a
        (x,) = _sbs_product([a], a)
        yield
        for lvl in range(1, 6):
            if lvl < 5:
                x2, ux = _sbs_product([x, u], x)
                u = u + ux
                x = x2
            else:
                (ux,) = _sbs_product([u], x)
                u = u + ux
            yield

        rows4 = pl.ds(base, GDN_GROUP_ROWS)
        th, tlo = _split_bf16(u)
        lhs_t = jnp.concatenate([_block_diag(th), _block_diag(tlo)], axis=0)
        so = jnp.dot(lhs_t, rhsbf[hd, rows4, :], preferred_element_type=F32)
        yield
        sol = (so[:GDN_GROUP_ROWS] + so[GDN_GROUP_ROWS:]).astype(BF16)

        kdt = kdbf[hd, rows4, :].astype(F32).T.astype(BF16)
        zt = jnp.zeros_like(kdt)
        lhs2 = jnp.concatenate(
            [_block_diag(qkd.astype(BF16))] + [jnp.where(tl == m, kdt, zt) for m in range(GDN_GROUP)],
            axis=0)
        r = jnp.dot(lhs2, sol, preferred_element_type=F32)
        o0s[hd, rows4, :] = r[:GDN_GROUP_ROWS, :LANES]
        qts[hd, rows4, :] = qd[hd, rows4, :] - r[:GDN_GROUP_ROWS, LANES:]
        for m in range(GDN_GROUP):
            blk = r[GDN_GROUP_ROWS + m * LANES:GDN_GROUP_ROWS + (m + 1) * LANES]
            dst = pl.ds(pl.multiple_of((g * GDN_GROUP + m) * LANES, LANES), LANES)
            vs[hd, dst, :] = blk[:, :LANES]
            wps[hd, dst, :] = blk[:, LANES:]

    gain = gain_ref[...]

    def phase2(hd, c, s):
        rows = pl.ds(pl.multiple_of(c * GDN_CHUNK, GDN_CHUNK), GDN_CHUNK)
        srow = pl.ds(pl.multiple_of(c * LANES, LANES), LANES)
        lhs = jnp.concatenate([wps[hd, srow, :].astype(BF16), qts[hd, rows, :].astype(BF16)], axis=0)
        r = jnp.dot(lhs, s.astype(BF16), preferred_element_type=F32)
        o = r[LANES:] + o0s[hd, rows, :]
        dec = jnp.exp(gamb[hd, pl.ds(c * GDN_CHUNK + GDN_CHUNK - 1, 1), :])
        s_new = s * dec + vs[hd, srow, :] - r[:LANES]
        y = o * lax.rsqrt(jnp.mean(o * o, -1, keepdims=True) + EPS) * gain
        zz = z_ref[rows, hd * LANES:(hd + 1) * LANES].astype(F32)
        o_ref[rows, hd * LANES:(hd + 1) * LANES] = (y * _silu(zz)).astype(BF16)
        return s_new

    def phase2_chain(hd, pair, states):
        s = states[hd]
        for j in range(PAIR_CHUNKS):
            s = phase2(hd, pair * PAIR_CHUNKS + j, s)
            yield
        states[hd] = s

    def run_interleaved(chains):
        live = list(chains)
        while live:
            nxt = []
            for c in live:
                try:
                    next(c)
                    nxt.append(c)
                except StopIteration:
                    pass
            live = nxt

    def phase1_chains(pair):
        return [phase1(hd, pair * GDN_PAIR + gg) for gg in range(GDN_PAIR) for hd in range(GDN_HB)]

    run_interleaved(phase1_chains(0))

    def body(pair, states):
        states = list(states)
        run_interleaved([phase2_chain(hd, pair - 1, states) for hd in range(GDN_HB)] + phase1_chains(pair))
        return tuple(states)

    s0 = tuple(jnp.zeros((GDN_DK, GDN_DV), F32) for _ in range(GDN_HB))
    states = list(lax.fori_loop(1, N_PAIRS, body, s0))
    run_interleaved([phase2_chain(hd, N_PAIRS - 1, states) for hd in range(GDN_HB)])


def _gdn(proj, g1, gr, conv_w, gdn_norm):
    w = GDN_HB * LANES
    per_row = D_MODEL // w
    hblk = lambda off: pl.BlockSpec((SEQ, w), lambda b, h, off=off: (b, off * per_row + h))
    cblk = lambda off: pl.BlockSpec((CONV_W, w), lambda b, h, off=off: (0, off * per_row + h))
    big = lambda dt: pltpu.VMEM((GDN_HB, SEQ, LANES), dt)
    return pl.pallas_call(
        _gdn_body,
        grid=(BATCH, GDN_HEADS // GDN_HB),
        in_specs=[
            hblk(0), hblk(1), hblk(2), hblk(3),
            pl.BlockSpec((SEQ, LANES), lambda b, h: (b, 0)),
            pl.BlockSpec((1, GDN_HB, N_GROUPS_SEQ, GDN_GROUP_ROWS), lambda b, h: (b, h, 0, 0)),
            cblk(0), cblk(1), cblk(2),
            pl.BlockSpec((1, LANES), lambda b, h: (0, 0)),
        ],
        out_specs=pl.BlockSpec((SEQ, w), lambda b, h: (b, h)),
        out_shape=jax.ShapeDtypeStruct((TOKENS, D_MODEL), BF16),
        scratch_shapes=[
            big(BF16), big(BF16), big(BF16), pltpu.VMEM((GDN_HB, SEQ, 2 * LANES), BF16),
            big(F32), big(F32), big(F32), big(F32), big(F32),
            pltpu.VMEM((GDN_HB, N_CHUNKS * LANES, LANES), F32),
            pltpu.VMEM((GDN_HB, N_CHUNKS * LANES, LANES), F32),
            pltpu.VMEM((CONV_PAD + SEQ, LANES), F32),
        ],
        compiler_params=_cparams(("parallel", "parallel")),
        name="gdn",
    )(proj, proj, proj, proj, g1, gr, conv_w, conv_w, conv_w, gdn_norm)


SWA_PIPE = 3


def _swa_body(sink_ref, q_ref, kc_ref, kp_ref, vc_ref, vp_ref, o_ref):
    i = pl.program_id(1)
    kcat = jnp.concatenate([kp_ref[...], kc_ref[...]], axis=0)
    vcat = jnp.concatenate([vp_ref[...], vc_ref[...]], axis=0)
    qi = lax.broadcasted_iota(I32, (SWA_BLOCK, 2 * SWA_BLOCK), 0)
    ki = lax.broadcasted_iota(I32, (SWA_BLOCK, 2 * SWA_BLOCK), 1)
    rel = qi - ki + SWA_BLOCK
    mask = (rel >= 0) & (rel < WINDOW) & ((ki >= SWA_BLOCK) | (i > 0))
    lane_kv = lax.broadcasted_iota(I32, (2 * SWA_BLOCK, LANES), 1)
    lane_q = lax.broadcasted_iota(I32, (SWA_BLOCK, LANES), 1)
    nt = (((1,), (1,)), ((), ()))
    scale = SWA_DH ** -0.5

    def head_pair(h0, ks, vs):
        qs = q_ref[:, h0 * SWA_DH:(h0 + 2) * SWA_DH]
        scores = [lax.dot_general(qs, kk, nt, preferred_element_type=F32) for kk in ks]
        for _ in range(SWA_PIPE):
            yield
        acc = None
        inv = None
        for half in range(2):
            sink = sink_ref[h0 + half]
            s = jnp.where(mask, scores[half] * scale, -jnp.inf)
            m = jnp.maximum(jnp.max(s, axis=-1, keepdims=True), sink)
            p = jnp.exp(s - m)
            den = jnp.sum(p, axis=-1, keepdims=True) + jnp.exp(sink - m)
            pv = jnp.dot(p.astype(BF16), vs[half], preferred_element_type=F32)
            acc = pv if acc is None else acc + pv
            r = 1.0 / den
            inv = r if inv is None else jnp.where(lane_q < SWA_DH, inv, r)
        yield
        o_ref[:, h0 * SWA_DH:(h0 + 2) * SWA_DH] = (acc * inv).astype(BF16)

    chains = []
    for slab in range(SWA_KV_HEADS // 2):
        k2 = kcat[:, slab * LANES:(slab + 1) * LANES].astype(F32)
        v2 = vcat[:, slab * LANES:(slab + 1) * LANES].astype(F32)
        k2r = pltpu.roll(k2, SWA_DH, 1)
        v2r = pltpu.roll(v2, SWA_DH, 1)
        for sub in range(2):
            kvh = slab * 2 + sub
            if sub == 0:
                k_lo = jnp.where(lane_kv < SWA_DH, k2, 0.0)
                k_hi = jnp.where(lane_kv >= SWA_DH, k2r, 0.0)
                v_lo = jnp.where(lane_kv < SWA_DH, v2, 0.0)
                v_hi = jnp.where(lane_kv >= SWA_DH, v2r, 0.0)
            else:
                k_lo = jnp.where(lane_kv < SWA_DH, k2r, 0.0)
                k_hi = jnp.where(lane_kv >= SWA_DH, k2, 0.0)
                v_lo = jnp.where(lane_kv < SWA_DH, v2r, 0.0)
                v_hi = jnp.where(lane_kv >= SWA_DH, v2, 0.0)
            k_lo, k_hi, v_lo, v_hi = (t.astype(BF16) for t in (k_lo, k_hi, v_lo, v_hi))
            for gp in range(SWA_GROUP // 2):
                h0 = kvh * SWA_GROUP + 2 * gp
                chains.append(head_pair(h0, (k_lo, k_hi), (v_lo, v_hi)))

    live = []
    pending = list(chains)
    while pending or live:
        if pending:
            live.append(pending.pop(0))
        nxt = []
        for c in live:
            try:
                next(c)
                nxt.append(c)
            except StopIteration:
                pass
        live = nxt


def _swa(proj, sinks):
    nb = SEQ // SWA_BLOCK
    qcol = 4
    kcol = (7 * D_MODEL) // 256
    vcol = kcol + 1
    cur = lambda col: pl.BlockSpec((SWA_BLOCK, 256), lambda b, i, col=col: (b * nb + i, col))
    prev = lambda col: pl.BlockSpec(
        (SWA_BLOCK, 256), lambda b, i, col=col: (b * nb + jnp.maximum(i - 1, 0), col))
    return pl.pallas_call(
        _swa_body,
        grid=(BATCH, nb),
        in_specs=[
            pl.BlockSpec(memory_space=pltpu.SMEM),
            pl.BlockSpec((SWA_BLOCK, D_MODEL), lambda b, i: (b * nb + i, qcol)),
            cur(kcol), prev(kcol), cur(vcol), prev(vcol),
        ],
        out_specs=pl.BlockSpec((SWA_BLOCK, D_MODEL), lambda b, i: (b * nb + i, 0)),
        out_shape=jax.ShapeDtypeStruct((TOKENS, D_MODEL), BF16),
        compiler_params=_cparams(("parallel", "parallel")),
        name="swa",
    )(sinks, proj, proj, proj, proj, proj)


OUT_TM = 256


def _outproj_body(x_ref, yg_ref, ys_ref, gg_ref, gs_ref, w_ref, nf_ref, wrh_ref, wrl_ref, rb_ref,
                  x1_ref, hp_ref, lg_ref):
    merged = (_sigmoid(gg_ref[...].astype(F32)) * yg_ref[...].astype(F32)
              + _sigmoid(gs_ref[...].astype(F32)) * ys_ref[...].astype(F32))
    x1 = x_ref[...] + jnp.dot(merged.astype(BF16), w_ref[...], preferred_element_type=F32)
    x1_ref[...] = x1
    ms = jnp.mean(x1 * x1, axis=-1, keepdims=True)
    h = x1 * lax.rsqrt(ms + EPS) * nf_ref[...]
    hh, hl = _split_bf16(h)
    lg_ref[...] = (jnp.dot(hh, wrh_ref[...], preferred_element_type=F32)
                   + jnp.dot(hh, wrl_ref[...], preferred_element_type=F32)
                   + jnp.dot(hl, wrh_ref[...], preferred_element_type=F32) + rb_ref[...])
    half = D_MODEL // 2
    hf = hh.astype(F32)
    wa = pltpu.bitcast(hf[:, :half], U32)
    wb = pltpu.bitcast(hf[:, half:], U32)
    hp_ref[...] = (wa & jnp.uint32(0xFFFF0000)) | (wb >> 16)


def _outproj(x2d, y_gdn, y_swa, proj, w_out, norm_ffn, wr_hi, wr_lo, r_bias):
    row = lambda w: pl.BlockSpec((OUT_TM, w), lambda i: (i, 0))
    const = lambda s: pl.BlockSpec(s, lambda i: (0, 0))
    return pl.pallas_call(
        _outproj_body,
        grid=(TOKENS // OUT_TM,),
        in_specs=[
            row(D_MODEL), row(D_MODEL), row(D_MODEL),
            pl.BlockSpec((OUT_TM, D_MODEL), lambda i: (i, 5)),
            pl.BlockSpec((OUT_TM, D_MODEL), lambda i: (i, 6)),
            const((D_MODEL, D_MODEL)), const((1, D_MODEL)),
            const((D_MODEL, LANES)), const((D_MODEL, LANES)), const((1, LANES)),
        ],
        out_specs=[row(D_MODEL), row(D_MODEL // 2), row(LANES)],
        out_shape=[
            jax.ShapeDtypeStruct((TOKENS, D_MODEL), F32),
            jax.ShapeDtypeStruct((TOKENS, D_MODEL // 2), U32),
            jax.ShapeDtypeStruct((TOKENS, LANES), F32),
        ],
        compiler_params=_cparams(("parallel",)),
        name="outproj",
    )(x2d, y_gdn, y_swa, proj, proj, w_out, norm_ffn, wr_hi, wr_lo, r_bias)


ROUTE_TM = 512


def _route_body(lg_ref, tri_ref, idx_ref, wt_ref, cnt_ref, run_ref):
    @pl.when(pl.program_id(0) == 0)
    def _():
        run_ref[...] = jnp.zeros_like(run_ref)

    lg = lg_ref[...]
    lane = lax.broadcasted_iota(I32, lg.shape, 1)
    ninf = -jnp.inf
    big = jnp.int32(LANES)
    is_g = lane < N_GROUPS
    glog = jnp.where(is_g, lg, ninf)
    gmax = jnp.max(glog, axis=-1, keepdims=True)
    gden = jnp.sum(jnp.where(is_g, jnp.exp(lg - gmax), 0.0), axis=-1, keepdims=True)
    p_sel = 1.0 / gden
    grp = jnp.min(jnp.where(glog == gmax, lane, big), axis=-1, keepdims=True)
    emask = (lane >= N_GROUPS) & (lane < N_GROUPS + N_EXPERTS) & (((lane - N_GROUPS) >> 3) == grp)
    el = jnp.where(emask, lg, ninf)
    v1 = jnp.max(el, axis=-1, keepdims=True)
    i1 = jnp.min(jnp.where(el == v1, lane, big), axis=-1, keepdims=True)
    el2 = jnp.where(lane == i1, ninf, el)
    v2 = jnp.max(el2, axis=-1, keepdims=True)
    i2 = jnp.min(jnp.where(el2 == v2, lane, big), axis=-1, keepdims=True)
    e = jnp.exp(v2 - v1)
    w1 = p_sel / (1.0 + e)
    w2 = p_sel * e / (1.0 + e)
    e0 = i1 - N_GROUPS
    e1 = i2 - N_GROUPS

    oh0 = lane == e0
    oh1 = lane == e1
    onehot = jnp.where(oh0 | oh1, 1.0, 0.0)
    prefix = jnp.dot(tri_ref[...], onehot.astype(BF16), preferred_element_type=F32) + run_ref[0:1, :]
    r0 = jnp.sum(jnp.where(oh0, prefix, 0.0), axis=-1, keepdims=True).astype(I32)
    r1 = jnp.sum(jnp.where(oh1, prefix, 0.0), axis=-1, keepdims=True).astype(I32)
    run = run_ref[0:1, :] + jnp.sum(onehot, axis=0, keepdims=True)
    run_ref[...] = jnp.broadcast_to(run, run_ref.shape)
    cnt_ref[...] = jnp.broadcast_to(run, cnt_ref.shape).astype(I32)

    zi = jnp.zeros(lg.shape, I32)
    idx = jnp.where(lane == 0, e0, zi)
    idx = jnp.where(lane == 1, e1, idx)
    idx = jnp.where(lane == 2, r0, idx)
    idx = jnp.where(lane == 3, r1, idx)
    idx_ref[...] = idx
    wt_ref[...] = jnp.where(lane == 0, w1, jnp.where(lane == 1, w2, 0.0))


def _route(logits, tri):
    row = pl.BlockSpec((ROUTE_TM, LANES), lambda i: (i, 0))
    return pl.pallas_call(
        _route_body,
        grid=(TOKENS // ROUTE_TM,),
        in_specs=[row, pl.BlockSpec((ROUTE_TM, ROUTE_TM), lambda i: (0, 0))],
        out_specs=[row, row, pl.BlockSpec((8, LANES), lambda i: (0, 0))],
        out_shape=[
            jax.ShapeDtypeStruct((TOKENS, LANES), I32),
            jax.ShapeDtypeStruct((TOKENS, LANES), F32),
            jax.ShapeDtypeStruct((8, LANES), I32),
        ],
        scratch_shapes=[pltpu.VMEM((8, LANES), F32)],
        compiler_params=_cparams(("arbitrary",)),
        name="route",
    )(logits, tri)


DISP_TM = 256


def _dispatch_body(meta_ref, dest_ref, h_ref, xs_ref, zbuf, sem, zsem):
    @pl.when(pl.program_id(0) == 0)
    def _():
        zbuf[...] = jnp.zeros_like(zbuf)

        def zero_block(row0):
            return pltpu.make_async_copy(zbuf, xs_ref.at[pl.ds(row0, MOE_BLOCK), :], zsem)

        def per_expert(e, n):
            has = meta_ref[N_EXPERTS + e] > 0

            @pl.when(has)
            def _():
                zero_block(pl.multiple_of(meta_ref[e] - MOE_BLOCK, MOE_BLOCK)).start()

            return n + has.astype(I32)

        n_last = lax.fori_loop(0, N_EXPERTS, per_expert, jnp.int32(0))
        used = meta_ref[2 * N_EXPERTS]

        def tail(j, c):
            zero_block(pl.multiple_of(j * MOE_BLOCK, MOE_BLOCK)).start()
            return c

        lax.fori_loop(used, N_BLOCKS, tail, 0)

        def drain(i, c):
            zero_block(0).wait()
            return c

        lax.fori_loop(0, n_last + (N_BLOCKS - used), drain, 0)

    def row_copy(r, d):
        return pltpu.make_async_copy(h_ref.at[pl.ds(r, 1), :], xs_ref.at[pl.ds(d, 1), :], sem)

    def issue(r, c):
        for k in range(TOP_K):
            row_copy(r, dest_ref[TOP_K * r + k]).start()
        return c

    lax.fori_loop(0, DISP_TM, issue, 0, unroll=4)
    for k in range(TOP_K):
        pltpu.make_async_copy(h_ref, xs_ref.at[pl.ds(0, DISP_TM), :], sem).wait()


def _dispatch(meta, dest_flat, hp):
    grid_spec = pltpu.PrefetchScalarGridSpec(
        num_scalar_prefetch=1,
        grid=(TOKENS // DISP_TM,),
        in_specs=[
            pl.BlockSpec((DISP_TM * TOP_K,), lambda i, m: (i,), memory_space=pltpu.SMEM),
            pl.BlockSpec((DISP_TM, D_MODEL // 2), lambda i, m: (i, 0)),
        ],
        out_specs=pl.BlockSpec(memory_space=pl.ANY),
        scratch_shapes=[
            pltpu.VMEM((MOE_BLOCK, D_MODEL // 2), U32),
            pltpu.SemaphoreType.DMA,
            pltpu.SemaphoreType.DMA,
        ],
    )
    return pl.pallas_call(
        _dispatch_body,
        grid_spec=grid_spec,
        out_shape=jax.ShapeDtypeStruct((N_SLOTS, D_MODEL // 2), U32),
        compiler_params=_cparams(("arbitrary",)),
        name="dispatch",
    )(meta, dest_flat, hp)


def _slots_body(idx_ref, start_ref, dest_ref):
    idx = idx_ref[...]
    lane = lax.broadcasted_iota(I32, idx.shape, 1)
    start = start_ref[...]
    out = jnp.zeros(idx.shape, I32)
    for k in range(TOP_K):
        e = idx[:, k:k + 1]
        base = jnp.sum(jnp.where(lane == e, start, 0.0), axis=-1, keepdims=True).astype(I32)
        out = jnp.where(lane == k, base + idx[:, TOP_K + k:TOP_K + k + 1], out)
    dest_ref[...] = out


def _slots(idx, pad_start_row):
    row = pl.BlockSpec((ROUTE_TM, LANES), lambda i: (i, 0))
    return pl.pallas_call(
        _slots_body,
        grid=(TOKENS // ROUTE_TM,),
        in_specs=[row, pl.BlockSpec((1, LANES), lambda i: (0, 0))],
        out_specs=row,
        out_shape=jax.ShapeDtypeStruct((TOKENS, LANES), I32),
        compiler_params=_cparams(("parallel",)),
        name="slots",
    )(idx, pad_start_row)


def _experts_body(blk_e_ref, x_ref, wg_ref, wu_ref, wd_ref, y_ref, wg_bf, wu_bf, wd_bf):
    j = pl.program_id(0)
    new_expert = (j == 0) | (blk_e_ref[j] != blk_e_ref[jnp.maximum(j - 1, 0)])

    @pl.when(new_expert)
    def _():
        wg_bf[...] = wg_ref[0].astype(BF16)
        wu_bf[...] = wu_ref[0].astype(BF16)
        wd_bf[...] = wd_ref[0].astype(BF16)

    w = x_ref[...]
    xa = pltpu.bitcast(w & jnp.uint32(0xFFFF0000), F32).astype(BF16)
    xb = pltpu.bitcast(w << 16, F32).astype(BF16)
    x = jnp.concatenate([xa, xb], axis=1)
    g = jnp.dot(x, wg_bf[...], preferred_element_type=F32)
    u = jnp.dot(x, wu_bf[...], preferred_element_type=F32)
    hb = (_silu(g) * u).astype(BF16)
    y_ref[...] = jnp.dot(hb, wd_bf[...], preferred_element_type=F32)


def _experts(blk_e, xs, w_gate, w_up, w_down):
    grid_spec = pltpu.PrefetchScalarGridSpec(
        num_scalar_prefetch=1,
        grid=(N_BLOCKS,),
        in_specs=[
            pl.BlockSpec((MOE_BLOCK, D_MODEL // 2), lambda j, be: (j, 0)),
            pl.BlockSpec((1, D_MODEL, D_EXPERT), lambda j, be: (be[j], 0, 0)),
            pl.BlockSpec((1, D_MODEL, D_EXPERT), lambda j, be: (be[j], 0, 0)),
            pl.BlockSpec((1, D_EXPERT, D_MODEL), lambda j, be: (be[j], 0, 0)),
        ],
        out_specs=pl.BlockSpec((MOE_BLOCK, D_MODEL), lambda j, be: (j, 0)),
        scratch_shapes=[
            pltpu.VMEM((D_MODEL, D_EXPERT), BF16),
            pltpu.VMEM((D_MODEL, D_EXPERT), BF16),
            pltpu.VMEM((D_EXPERT, D_MODEL), BF16),
        ],
    )
    return pl.pallas_call(
        _experts_body,
        grid_spec=grid_spec,
        out_shape=jax.ShapeDtypeStruct((N_SLOTS, D_MODEL), F32),
        compiler_params=_cparams(("arbitrary",)),
        name="experts",
    )(blk_e, xs, w_gate, w_up, w_down)


COMB_TM = 256


def _combine_body(dest_ref, wt_ref, x1_ref, gain_ref, ys_ref, o_ref, buf, sem):
    def row_copy(r, k, d):
        return pltpu.make_async_copy(ys_ref.at[pl.ds(d, 1), :], buf.at[k, pl.ds(r, 1), :], sem)

    def issue(r, c):
        for k in range(TOP_K):
            row_copy(r, k, dest_ref[TOP_K * r + k]).start()
        return c

    lax.fori_loop(0, COMB_TM, issue, 0, unroll=4)
    for k in range(TOP_K):
        pltpu.make_async_copy(ys_ref.at[pl.ds(0, COMB_TM), :], buf.at[k], sem).wait()

    wt = wt_ref[...]
    y = x1_ref[...] + wt[:, 0:1] * buf[0] + wt[:, 1:2] * buf[1]
    ms = jnp.mean(y * y, axis=-1, keepdims=True)
    o_ref[...] = y * lax.rsqrt(ms + EPS) * gain_ref[...]


def _combine(dest_flat, wts, x1, norm_final, ys):
    return pl.pallas_call(
        _combine_body,
        grid=(TOKENS // COMB_TM,),
        in_specs=[
            pl.BlockSpec((COMB_TM * TOP_K,), lambda i: (i,), memory_space=pltpu.SMEM),
            pl.BlockSpec((COMB_TM, LANES), lambda i: (i, 0)),
            pl.BlockSpec((COMB_TM, D_MODEL), lambda i: (i, 0)),
            pl.BlockSpec((1, D_MODEL), lambda i: (0, 0)),
            pl.BlockSpec(memory_space=pl.ANY),
        ],
        out_specs=pl.BlockSpec((COMB_TM, D_MODEL), lambda i: (i, 0)),
        out_shape=jax.ShapeDtypeStruct((TOKENS, D_MODEL), F32),
        scratch_shapes=[pltpu.VMEM((TOP_K, COMB_TM, D_MODEL), F32), pltpu.SemaphoreType.DMA],
        compiler_params=_cparams(("arbitrary",)),
        name="combine",
    )(dest_flat, wts, x1, norm_final, ys)


def _pad_lanes(v):
    v = v.reshape(1, -1).astype(F32)
    return jnp.pad(v, ((0, 0), (0, LANES - v.shape[1])))


def kernel(x, norm_mix, w_in, conv_w, gdn_a_log, gdn_dt_bias, gdn_norm, swa_sinks, w_out, norm_ffn,
           w_router_group, b_router_group, w_router_expert, b_router_expert, w_gate, w_up, w_down,
           norm_final):
    l = 0
    x2d = x.reshape(TOKENS, D_MODEL)
    w = w_in[l]
    o_z = 3 * D_MODEL
    o_a = o_z + D_MODEL
    o_sq = o_a + 2 * GDN_HEADS
    o_sk = o_sq + D_MODEL
    o_sv = o_sk + SWA_KV_HEADS * SWA_DH
    o_gg = o_sv + SWA_KV_HEADS * SWA_DH
    o_gs = o_gg + D_MODEL
    w_main = jnp.concatenate(
        [w[:, :o_a], w[:, o_sq:o_sk], w[:, o_gg:o_gs], w[:, o_gs:], w[:, o_sk:o_sv], w[:, o_sv:o_gg]],
        axis=1).astype(BF16)
    w_ab = jnp.pad(w[:, o_a:o_sq], ((0, 0), (0, LANES - 2 * GDN_HEADS))).astype(BF16)

    proj, ab = _inproj(x2d, norm_mix[l].reshape(1, D_MODEL), w_main, w_ab)

    g1, gt = _gdn_prep(ab, _pad_lanes(gdn_a_log[l]), _pad_lanes(gdn_dt_bias[l]))
    gr = gt[:, :GDN_HEADS, :].reshape(BATCH, GDN_HEADS, N_GROUPS_SEQ, GDN_GROUP_ROWS)
    y_gdn = _gdn(proj, g1, gr, conv_w[l], gdn_norm[l].reshape(1, GDN_DV))

    y_swa = _swa(proj, swa_sinks[l].astype(F32))

    w_r = jnp.concatenate([w_router_group[l], w_router_expert[l]], axis=1).astype(F32)
    w_r = jnp.pad(w_r, ((0, 0), (0, LANES - w_r.shape[1])))
    wr_hi = w_r.astype(BF16)
    wr_lo = (w_r - wr_hi.astype(F32)).astype(BF16)
    r_bias = _pad_lanes(jnp.concatenate([b_router_group[l], b_router_expert[l]]))
    x1, hp, logits = _outproj(x2d, y_gdn, y_swa, proj, w_out[l].astype(BF16),
                              norm_ffn[l].reshape(1, D_MODEL), wr_hi, wr_lo, r_bias)

    tri = (lax.broadcasted_iota(I32, (ROUTE_TM, ROUTE_TM), 1)
           < lax.broadcasted_iota(I32, (ROUTE_TM, ROUTE_TM), 0)).astype(BF16)
    idx, wts, cnt = _route(logits, tri)

    counts = cnt[0, :N_EXPERTS]
    padded = (counts + MOE_BLOCK - 1) // MOE_BLOCK * MOE_BLOCK
    pad_end = jnp.cumsum(padded)
    pad_start = pad_end - padded
    dest = _slots(idx, _pad_lanes(pad_start))
    dest_flat = dest[:, :TOP_K].reshape(N_ASSIGN)
    blk_pos = jnp.arange(N_BLOCKS, dtype=I32) * MOE_BLOCK
    blk_e = jnp.minimum(jnp.sum((pad_end[None, :] <= blk_pos[:, None]).astype(I32), axis=1), N_EXPERTS - 1)
    meta = jnp.concatenate([pad_end, padded, pad_end[-1:] // MOE_BLOCK]).astype(I32)

    xs = _dispatch(meta, dest_flat, hp)
    ys = _experts(blk_e, xs, w_gate[l], w_up[l], w_down[l])
    out = _combine(dest_flat, wts, x1, norm_final.reshape(1, D_MODEL), ys)
    return out.reshape(BATCH, SEQ, D_MODEL)
```

```python
import functools

import jax
import jax.numpy as jnp
from jax import lax
from jax.experimental import pallas as pl
from jax.experimental.pallas import tpu as pltpu

F32 = jnp.float32
BF16 = jnp.bfloat16
I32 = jnp.int32
U32 = jnp.uint32

D_MODEL = 2048
BATCH = 16
SEQ = 2048
TOKENS = BATCH * SEQ
EPS = 1e-6

GDN_HEADS = 16
GDN_DK = 128
GDN_DV = 128
GDN_CHUNK = 64
CONV_W = 4
GDN_GROUP = 4
GDN_GROUP_ROWS = GDN_GROUP * GDN_CHUNK
N_CHUNKS = SEQ // GDN_CHUNK
N_GROUPS_SEQ = SEQ // GDN_GROUP_ROWS

SWA_HEADS = 32
SWA_KV_HEADS = 4
SWA_DH = 64
SWA_GROUP = 8
SWA_BLOCK = 128
WINDOW = 128

N_GROUPS = 8
EXPERTS_PER_GROUP = 8
N_EXPERTS = 64
TOP_K = 2
D_EXPERT = 512
MOE_BLOCK = 128
N_ASSIGN = TOKENS * TOP_K
N_BLOCKS = N_ASSIGN // MOE_BLOCK + N_EXPERTS
N_SLOTS = N_BLOCKS * MOE_BLOCK

PROJ_DIM = 7 * D_MODEL + 2 * SWA_KV_HEADS * SWA_DH
LANES = 128

VMEM_LIMIT = 56 * 1024 * 1024


def _cparams(sem, vmem=VMEM_LIMIT):
    return pltpu.CompilerParams(dimension_semantics=sem, vmem_limit_bytes=vmem)


def _sigmoid(x):
    return 0.5 * jnp.tanh(0.5 * x) + 0.5


def _silu(x):
    h = 0.5 * x
    return h * jnp.tanh(h) + h


INPROJ_TM = 2048
INPROJ_TN = 512
INPROJ_VMEM = 60 * 1024 * 1024


def _inproj_body(x_ref, g_ref, w_ref, wab_ref, o_ref, ab_ref, h_ref):
    @pl.when(pl.program_id(1) == 0)
    def _():
        def chunk(i, c):
            r = pl.ds(pl.multiple_of(i * 128, 128), 128)
            x = x_ref[r, :]
            ms = jnp.mean(x * x, axis=-1, keepdims=True)
            h_ref[r, :] = (x * lax.rsqrt(ms + EPS) * g_ref[...]).astype(BF16)
            return c
        lax.fori_loop(0, INPROJ_TM // 128, chunk, 0)
        ab_ref[...] = jnp.dot(h_ref[...], wab_ref[...], preferred_element_type=F32)

    o_ref[...] = jnp.dot(h_ref[...], w_ref[...], preferred_element_type=F32).astype(BF16)


def _inproj(x2d, gain, w_main, w_ab):
    grid = (TOKENS // INPROJ_TM, PROJ_DIM // INPROJ_TN)
    return pl.pallas_call(
        _inproj_body,
        grid=grid,
        in_specs=[
            pl.BlockSpec((INPROJ_TM, D_MODEL), lambda i, j: (i, 0)),
            pl.BlockSpec((1, D_MODEL), lambda i, j: (0, 0)),
            pl.BlockSpec((D_MODEL, INPROJ_TN), lambda i, j: (0, j)),
            pl.BlockSpec((D_MODEL, LANES), lambda i, j: (0, 0)),
        ],
        out_specs=[
            pl.BlockSpec((INPROJ_TM, INPROJ_TN), lambda i, j: (i, j)),
            pl.BlockSpec((INPROJ_TM, LANES), lambda i, j: (i, 0)),
        ],
        out_shape=[
            jax.ShapeDtypeStruct((TOKENS, PROJ_DIM), BF16),
            jax.ShapeDtypeStruct((TOKENS, LANES), F32),
        ],
        scratch_shapes=[pltpu.VMEM((INPROJ_TM, D_MODEL), BF16)],
        compiler_params=_cparams(("parallel", "arbitrary"), vmem=INPROJ_VMEM),
        name="inproj",
    )(x2d, gain, w_main, w_ab)


def _gdn_prep_body(ab_ref, alog_ref, dtb_ref, g1_ref, gt_ref):
    ab = ab_ref[...]
    lane = lax.broadcasted_iota(I32, ab.shape, 1)
    row = lax.broadcasted_iota(I32, ab.shape, 0) % GDN_CHUNK
    xa = ab + dtb_ref[...]
    softplus = jnp.maximum(xa, 0.0) + jnp.log(1.0 + jnp.exp(-jnp.abs(xa)))
    g = jnp.where(lane < GDN_HEADS, -jnp.exp(alog_ref[...]) * softplus, 0.0)
    gam = g
    s = 1
    while s < GDN_CHUNK:
        gam = gam + jnp.where(row >= s, pltpu.roll(gam, s, 0), 0.0)
        s *= 2
    g1_ref[...] = jnp.where(lane < GDN_HEADS, gam, _sigmoid(ab))
    gt_ref[0] = gam.T


def _gdn_prep(ab, alog_pad, dtb_pad):
    return pl.pallas_call(
        _gdn_prep_body,
        grid=(BATCH,),
        in_specs=[
            pl.BlockSpec((SEQ, LANES), lambda b: (b, 0)),
            pl.BlockSpec((1, LANES), lambda b: (0, 0)),
            pl.BlockSpec((1, LANES), lambda b: (0, 0)),
        ],
        out_specs=[
            pl.BlockSpec((SEQ, LANES), lambda b: (b, 0)),
            pl.BlockSpec((1, LANES, SEQ), lambda b: (b, 0, 0)),
        ],
        out_shape=[
            jax.ShapeDtypeStruct((TOKENS, LANES), F32),
            jax.ShapeDtypeStruct((BATCH, LANES, SEQ), F32),
        ],
        compiler_params=_cparams(("parallel",)),
        name="gdn_prep",
    )(ab, alog_pad, dtb_pad)


GDN_HB = 2
GDN_PAIR = 2
N_PAIRS = N_GROUPS_SEQ // GDN_PAIR
PAIR_CHUNKS = GDN_PAIR * GDN_GROUP
CONV_PAD = 8


def _split_bf16(x):
    hi = x.astype(BF16)
    lo = (x - hi.astype(F32)).astype(BF16)
    return hi, lo


def _block_diag(x):
    t = jnp.concatenate([x] * GDN_GROUP, axis=0)
    rb = lax.broadcasted_iota(I32, t.shape, 0) // GDN_CHUNK
    cb = lax.broadcasted_iota(I32, t.shape, 1) // GDN_CHUNK
    return jnp.where(rb == cb, t, jnp.zeros_like(t))


def _sbs_product(lhs_list, y):
    yh, yl = _split_bf16(y)
    bdh = _block_diag(yh)
    bdl = _block_diag(yl)
    his, los = zip(*[_split_bf16(x) for x in lhs_list])
    n = len(lhs_list)
    l1 = jnp.concatenate(list(his) + list(los), axis=0)
    r1 = jnp.dot(l1, bdh, preferred_element_type=F32)
    l2 = his[0] if n == 1 else jnp.concatenate(list(his), axis=0)
    r2 = jnp.dot(l2, bdl, preferred_element_type=F32)
    c = GDN_CHUNK
    return [r1[i * c:(i + 1) * c] + r1[(n + i) * c:(n + i + 1) * c] + r2[i * c:(i + 1) * c]
            for i in range(n)]


def _gdn_body(q_ref, k_ref, v_ref, z_ref, g1_ref, gr_ref, cw_q_ref, cw_k_ref, cw_v_ref, gain_ref,
              o_ref,
              qbf, kbf, kdbf, rhsbf, qd, gamb, betab, o0s, qts, vs, wps, xpad):
    hg = pl.program_id(1)
    shape = (SEQ, LANES)
    lane = lax.broadcasted_iota(I32, shape, 1)

    xpad[pl.ds(0, CONV_PAD), :] = jnp.zeros((CONV_PAD, LANES), F32)

    def conv_silu(x_ref, cw_ref, hs):
        xpad[pl.ds(CONV_PAD, SEQ), :] = x_ref[:, hs].astype(F32)
        cw = cw_ref[:, hs]
        y = None
        for s in range(CONV_W):
            t = xpad[pl.ds(CONV_PAD - s, SEQ), :] * cw[CONV_W - 1 - s:CONV_W - s, :]
            y = t if y is None else y + t
        return _silu(y)

    g1 = g1_ref[...]
    for hd in range(GDN_HB):
        hs = slice(hd * LANES, (hd + 1) * LANES)
        head = hg * GDN_HB + hd

        def col(off):
            c = jnp.sum(jnp.where(lane == off + head, g1, 0.0), axis=-1, keepdims=True)
            return jnp.broadcast_to(c, shape)

        gam = col(0)
        beta = col(GDN_HEADS)
        gamb[hd] = gam
        betab[hd] = beta
        gam3 = gam.reshape(N_CHUNKS, GDN_CHUNK, LANES)
        glast = jnp.broadcast_to(gam3[:, GDN_CHUNK - 1:GDN_CHUNK, :], gam3.shape).reshape(shape)
        eg = jnp.exp(gam)
        kdf = jnp.exp(glast - gam)

        q = conv_silu(q_ref, cw_q_ref, hs)
        q = q * (lax.rsqrt(jnp.sum(q * q, -1, keepdims=True) + EPS) * (GDN_DK ** -0.5))
        qbf[hd] = q.astype(BF16)
        qd[hd] = q * eg
        k = conv_silu(k_ref, cw_k_ref, hs)
        k = k * lax.rsqrt(jnp.sum(k * k, -1, keepdims=True) + EPS)
        kbf[hd] = k.astype(BF16)
        kdbf[hd] = (k * kdf).astype(BF16)
        rhsbf[hd, :, LANES:] = (k * (beta * eg)).astype(BF16)
        v = conv_silu(v_ref, cw_v_ref, hs)
        rhsbf[hd, :, :LANES] = (v * beta).astype(BF16)

    gshape = (GDN_CHUNK, GDN_GROUP_ROWS)
    ii = lax.broadcasted_iota(I32, gshape, 0)
    jj = lax.broadcasted_iota(I32, gshape, 1) % GDN_CHUNK
    eye = jnp.where(ii == jj, 1.0, 0.0).astype(F32)
    pr = lax.broadcasted_iota(I32, (2 * GDN_CHUNK, LANES), 0) // GDN_CHUNK
    tl = lax.broadcasted_iota(I32, (LANES, GDN_GROUP_ROWS), 1) // GDN_CHUNK
    lane_c = lax.broadcasted_iota(I32, (GDN_CHUNK, LANES), 1)
    nt = (((1,), (1,)), ((), ()))

    def sbs_bcast(ref, hd, base):
        parts = [ref[hd, pl.ds(base + m * GDN_CHUNK, GDN_CHUNK), :] for m in range(GDN_GROUP)]
        a = jnp.where(lane_c < GDN_CHUNK, parts[0], parts[1])
        b = jnp.where(lane_c < GDN_CHUNK, parts[2], parts[3])
        return jnp.concatenate([a, b], axis=1)

    def phase1(hd, g):
        base = pl.multiple_of(g * GDN_GROUP_ROWS, GDN_GROUP_ROWS)
        res = []
        for p in range(2):
            rows = pl.ds(base + p * 2 * GDN_CHUNK, 2 * GDN_CHUNK)
            kst = kbf[hd, rows, :]
            qst = qbf[hd, rows, :]
            lhs_k = jnp.concatenate([kst[:GDN_CHUNK], kst[GDN_CHUNK:]], axis=1)
            lhs_q = jnp.concatenate([qst[:GDN_CHUNK], qst[GDN_CHUNK:]], axis=1)
            lhs = jnp.concatenate([lhs_q, lhs_k], axis=0)
            zero = jnp.zeros_like(kst)
            bt = jnp.concatenate([jnp.where(pr == 0, kst, zero), jnp.where(pr == 1, kst, zero)], axis=1)
            res.append(lax.dot_general(lhs, bt, nt, preferred_element_type=F32))
        yield
        qk = jnp.concatenate([res[0][:GDN_CHUNK], res[1][:GDN_CHUNK]], axis=1)
        kk = jnp.concatenate([res[0][GDN_CHUNK:], res[1][GDN_CHUNK:]], axis=1)

        gc = sbs_bcast(gamb, hd, base)
        bc = sbs_bcast(betab, hd, base)
        grow = gr_ref[0, hd, pl.ds(g, 1), :]
        decay = jnp.exp(jnp.where(ii >= jj, gc - grow, -jnp.inf))
        a = jnp.where(ii > jj, kk * decay * bc, 0.0)
        qkd = qk * decay

        u = eye - a
        (x,) = _sbs_product([a], a)
        yield
        for lvl in range(1, 6):
            if lvl < 5:
                x2, ux = _sbs_product([x, u], x)
                u = u + ux
                x = x2
            else:
                (ux,) = _sbs_product([u], x)
                u = u + ux
            yield

        rows4 = pl.ds(base, GDN_GROUP_ROWS)
        th, tlo = _split_bf16(u)
        lhs_t = jnp.concatenate([_block_diag(th), _block_diag(tlo)], axis=0)
        so = jnp.dot(lhs_t, rhsbf[hd, rows4, :], preferred_element_type=F32)
        yield
        sol = (so[:GDN_GROUP_ROWS] + so[GDN_GROUP_ROWS:]).astype(BF16)

        kdt = kdbf[hd, rows4, :].astype(F32).T.astype(BF16)
        zt = jnp.zeros_like(kdt)
        lhs2 = jnp.concatenate(
            [_block_diag(qkd.astype(BF16))] + [jnp.where(tl == m, kdt, zt) for m in range(GDN_GROUP)],
            axis=0)
        r = jnp.dot(lhs2, sol, preferred_element_type=F32)
        o0s[hd, rows4, :] = r[:GDN_GROUP_ROWS, :LANES]
        qts[hd, rows4, :] = qd[hd, rows4, :] - r[:GDN_GROUP_ROWS, LANES:]
        for m in range(GDN_GROUP):
            blk = r[GDN_GROUP_ROWS + m * LANES:GDN_GROUP_ROWS + (m + 1) * LANES]
            dst = pl.ds(pl.multiple_of((g * GDN_GROUP + m) * LANES, LANES), LANES)
            vs[hd, dst, :] = blk[:, :LANES]
            wps[hd, dst, :] = blk[:, LANES:]

    gain = gain_ref[...]

    def phase2(hd, c, s):
        rows = pl.ds(pl.multiple_of(c * GDN_CHUNK, GDN_CHUNK), GDN_CHUNK)
        srow = pl.ds(pl.multiple_of(c * LANES, LANES), LANES)
        lhs = jnp.concatenate([wps[hd, srow, :].astype(BF16), qts[hd, rows, :].astype(BF16)], axis=0)
        r = jnp.dot(lhs, s.astype(BF16), preferred_element_type=F32)
        o = r[LANES:] + o0s[hd, rows, :]
        dec = jnp.exp(gamb[hd, pl.ds(c * GDN_CHUNK + GDN_CHUNK - 1, 1), :])
        s_new = s * dec + vs[hd, srow, :] - r[:LANES]
        y = o * lax.rsqrt(jnp.mean(o * o, -1, keepdims=True) + EPS) * gain
        zz = z_ref[rows, hd * LANES:(hd + 1) * LANES].astype(F32)
        o_ref[rows, hd * LANES:(hd + 1) * LANES] = (y * _silu(zz)).astype(BF16)
        return s_new

    def phase2_chain(hd, pair, states):
        s = states[hd]
        for j in range(PAIR_CHUNKS):
            s = phase2(hd, pair * PAIR_CHUNKS + j, s)
            yield
        states[hd] = s

    def run_interleaved(chains):
        live = list(chains)
        while live:
            nxt = []
            for c in live:
                try:
                    next(c)
                    nxt.append(c)
                except StopIteration:
                    pass
            live = nxt

    def phase1_chains(pair):
        return [phase1(hd, pair * GDN_PAIR + gg) for gg in range(GDN_PAIR) for hd in range(GDN_HB)]

    run_interleaved(phase1_chains(0))

    def body(pair, states):
        states = list(states)
        run_interleaved([phase2_chain(hd, pair - 1, states) for hd in range(GDN_HB)] + phase1_chains(pair))
        return tuple(states)

    s0 = tuple(jnp.zeros((GDN_DK, GDN_DV), F32) for _ in range(GDN_HB))
    states = list(lax.fori_loop(1, N_PAIRS, body, s0))
    run_interleaved([phase2_chain(hd, N_PAIRS - 1, states) for hd in range(GDN_HB)])


def _gdn(proj, g1, gr, conv_w, gdn_norm):
    w = GDN_HB * LANES
    per_row = D_MODEL // w
    hblk = lambda off: pl.BlockSpec((SEQ, w), lambda b, h, off=off: (b, off * per_row + h))
    cblk = lambda off: pl.BlockSpec((CONV_W, w), lambda b, h, off=off: (0, off * per_row + h))
    big = lambda dt: pltpu.VMEM((GDN_HB, SEQ, LANES), dt)
    return pl.pallas_call(
        _gdn_body,
        grid=(BATCH, GDN_HEADS // GDN_HB),
        in_specs=[
            hblk(0), hblk(1), hblk(2), hblk(3),
            pl.BlockSpec((SEQ, LANES), lambda b, h: (b, 0)),
            pl.BlockSpec((1, GDN_HB, N_GROUPS_SEQ, GDN_GROUP_ROWS), lambda b, h: (b, h, 0, 0)),
            cblk(0), cblk(1), cblk(2),
            pl.BlockSpec((1, LANES), lambda b, h: (0, 0)),
        ],
        out_specs=pl.BlockSpec((SEQ, w), lambda b, h: (b, h)),
        out_shape=jax.ShapeDtypeStruct((TOKENS, D_MODEL), BF16),
        scratch_shapes=[
            big(BF16), big(BF16), big(BF16), pltpu.VMEM((GDN_HB, SEQ, 2 * LANES), BF16),
            big(F32), big(F32), big(F32), big(F32), big(F32),
            pltpu.VMEM((GDN_HB, N_CHUNKS * LANES, LANES), F32),
            pltpu.VMEM((GDN_HB, N_CHUNKS * LANES, LANES), F32),
            pltpu.VMEM((CONV_PAD + SEQ, LANES), F32),
        ],
        compiler_params=_cparams(("parallel", "parallel")),
        name="gdn",
    )(proj, proj, proj, proj, g1, gr, conv_w, conv_w, conv_w, gdn_norm)


SWA_PIPE = 3
assert WINDOW == SWA_BLOCK


def _swa_body(sink_ref, q_ref, kc_ref, kp_ref, vc_ref, vp_ref, o_ref):
    i = pl.program_id(1)
    kcat = jnp.concatenate([kp_ref[...], kc_ref[...]], axis=0)
    vcat = jnp.concatenate([vp_ref[...], vc_ref[...]], axis=0)
    qi = lax.broadcasted_iota(I32, (SWA_BLOCK, SWA_BLOCK), 0)
    ki = lax.broadcasted_iota(I32, (SWA_BLOCK, SWA_BLOCK), 1)
    take_cur = ki <= qi
    valid = take_cur | (i > 0)
    lane_kv = lax.broadcasted_iota(I32, (2 * SWA_BLOCK, LANES), 1)
    lane_q = lax.broadcasted_iota(I32, (SWA_BLOCK, LANES), 1)
    nt = (((1,), (1,)), ((), ()))
    scale = SWA_DH ** -0.5

    def head_pair(h0, ks, vs):
        qs = q_ref[:, h0 * SWA_DH:(h0 + 2) * SWA_DH]
        scores = [lax.dot_general(qs, kk, nt, preferred_element_type=F32) for kk in ks]
        for _ in range(SWA_PIPE):
            yield
        acc = None
        inv = None
        for half in range(2):
            sink = sink_ref[h0 + half]
            sc = scores[half]
            s = jnp.where(take_cur, sc[:, SWA_BLOCK:], sc[:, :SWA_BLOCK]) * scale
            s = jnp.where(valid, s, -jnp.inf)
            m = jnp.maximum(jnp.max(s, axis=-1, keepdims=True), sink)
            p = jnp.exp(s - m)
            den = jnp.sum(p, axis=-1, keepdims=True) + jnp.exp(sink - m)
            p2 = jnp.concatenate([jnp.where(take_cur, 0.0, p), jnp.where(take_cur, p, 0.0)], axis=1)
            pv = jnp.dot(p2.astype(BF16), vs[half], preferred_element_type=F32)
            acc = pv if acc is None else acc + pv
            r = 1.0 / den
            inv = r if inv is None else jnp.where(lane_q < SWA_DH, inv, r)
        yield
        o_ref[:, h0 * SWA_DH:(h0 + 2) * SWA_DH] = (acc * inv).astype(BF16)

    chains = []
    for slab in range(SWA_KV_HEADS // 2):
        k2 = kcat[:, slab * LANES:(slab + 1) * LANES].astype(F32)
        v2 = vcat[:, slab * LANES:(slab + 1) * LANES].astype(F32)
        k2r = pltpu.roll(k2, SWA_DH, 1)
        v2r = pltpu.roll(v2, SWA_DH, 1)
        for sub in range(2):
            kvh = slab * 2 + sub
            if sub == 0:
                k_lo = jnp.where(lane_kv < SWA_DH, k2, 0.0)
                k_hi = jnp.where(lane_kv >= SWA_DH, k2r, 0.0)
                v_lo = jnp.where(lane_kv < SWA_DH, v2, 0.0)
                v_hi = jnp.where(lane_kv >= SWA_DH, v2r, 0.0)
            else:
                k_lo = jnp.where(lane_kv < SWA_DH, k2r, 0.0)
                k_hi = jnp.where(lane_kv >= SWA_DH, k2, 0.0)
                v_lo = jnp.where(lane_kv < SWA_DH, v2r, 0.0)
                v_hi = jnp.where(lane_kv >= SWA_DH, v2, 0.0)
            k_lo, k_hi, v_lo, v_hi = (t.astype(BF16) for t in (k_lo, k_hi, v_lo, v_hi))
            for gp in range(SWA_GROUP // 2):
                h0 = kvh * SWA_GROUP + 2 * gp
                chains.append(head_pair(h0, (k_lo, k_hi), (v_lo, v_hi)))

    live = []
    pending = list(chains)
    while pending or live:
        if pending:
            live.append(pending.pop(0))
        nxt = []
        for c in live:
            try:
                next(c)
                nxt.append(c)
            except StopIteration:
                pass
        live = nxt


def _swa(proj, sinks):
    nb = SEQ // SWA_BLOCK
    qcol = 4
    kcol = (7 * D_MODEL) // 256
    vcol = kcol + 1
    cur = lambda col: pl.BlockSpec((SWA_BLOCK, 256), lambda b, i, col=col: (b * nb + i, col))
    prev = lambda col: pl.BlockSpec(
        (SWA_BLOCK, 256), lambda b, i, col=col: (b * nb + jnp.maximum(i - 1, 0), col))
    return pl.pallas_call(
        _swa_body,
        grid=(BATCH, nb),
        in_specs=[
            pl.BlockSpec(memory_space=pltpu.SMEM),
            pl.BlockSpec((SWA_BLOCK, D_MODEL), lambda b, i: (b * nb + i, qcol)),
            cur(kcol), prev(kcol), cur(vcol), prev(vcol),
        ],
        out_specs=pl.BlockSpec((SWA_BLOCK, D_MODEL), lambda b, i: (b * nb + i, 0)),
        out_shape=jax.ShapeDtypeStruct((TOKENS, D_MODEL), BF16),
        compiler_params=_cparams(("parallel", "parallel")),
        name="swa",
    )(sinks, proj, proj, proj, proj, proj)


OUT_TM = 512
OUT_SUB = 256


def _outproj_body(x_ref, yg_ref, ys_ref, gg_ref, gs_ref, w_ref, nf_ref, wr_ref, rb_ref,
                  x1_ref, hp_ref, lg_ref):
    def sub_tile(t):
        rows = pl.ds(t * OUT_SUB, OUT_SUB)
        merged = (_sigmoid(gg_ref[rows, :].astype(F32)) * yg_ref[rows, :].astype(F32)
                  + _sigmoid(gs_ref[rows, :].astype(F32)) * ys_ref[rows, :].astype(F32))
        acc = jnp.dot(merged.astype(BF16), w_ref[...], preferred_element_type=F32)
        yield
        x1 = x_ref[rows, :] + acc
        x1_ref[rows, :] = x1
        ms = jnp.mean(x1 * x1, axis=-1, keepdims=True)
        h = x1 * lax.rsqrt(ms + EPS) * nf_ref[...]
        hh = h.astype(BF16)
        r = jnp.dot(hh, wr_ref[...], preferred_element_type=F32)
        lg_ref[rows, :] = r[:, :LANES] + r[:, LANES:] + rb_ref[...]
        half = D_MODEL // 2
        hf = hh.astype(F32)
        wa = pltpu.bitcast(hf[:, :half], U32)
        wb = pltpu.bitcast(hf[:, half:], U32)
        hp_ref[rows, :] = (wa & jnp.uint32(0xFFFF0000)) | (wb >> 16)

    tiles = [sub_tile(t) for t in range(OUT_TM // OUT_SUB)]
    for _ in range(2):
        for g in tiles:
            next(g, None)


def _outproj(x2d, y_gdn, y_swa, proj, w_out, norm_ffn, wr2, r_bias):
    row = lambda w: pl.BlockSpec((OUT_TM, w), lambda i: (i, 0))
    const = lambda s: pl.BlockSpec(s, lambda i: (0, 0))
    return pl.pallas_call(
        _outproj_body,
        grid=(TOKENS // OUT_TM,),
        in_specs=[
            row(D_MODEL), row(D_MODEL), row(D_MODEL),
            pl.BlockSpec((OUT_TM, D_MODEL), lambda i: (i, 5)),
            pl.BlockSpec((OUT_TM, D_MODEL), lambda i: (i, 6)),
            pl.BlockSpec((D_MODEL, D_MODEL), lambda i: (0, 0), pipeline_mode=pl.Buffered(1)),
            const((1, D_MODEL)),
            const((D_MODEL, 2 * LANES)), const((1, LANES)),
        ],
        out_specs=[row(D_MODEL), row(D_MODEL // 2), row(LANES)],
        out_shape=[
            jax.ShapeDtypeStruct((TOKENS, D_MODEL), F32),
            jax.ShapeDtypeStruct((TOKENS, D_MODEL // 2), U32),
            jax.ShapeDtypeStruct((TOKENS, LANES), F32),
        ],
        compiler_params=_cparams(("parallel",)),
        name="outproj",
    )(x2d, y_gdn, y_swa, proj, proj, w_out, norm_ffn, wr2, r_bias)


ROUTE_TM = 1024


def _route_body(lg_ref, tri_ref, idx_ref, wt_ref, cnt_ref, run_ref):
    @pl.when(pl.program_id(0) == 0)
    def _():
        run_ref[...] = jnp.zeros_like(run_ref)

    lg = lg_ref[...]
    lane = lax.broadcasted_iota(I32, lg.shape, 1)
    ninf = -jnp.inf
    big = jnp.int32(LANES)
    is_g = lane < N_GROUPS
    glog = jnp.where(is_g, lg, ninf)
    gmax = jnp.max(glog, axis=-1, keepdims=True)
    gden = jnp.sum(jnp.where(is_g, jnp.exp(lg - gmax), 0.0), axis=-1, keepdims=True)
    p_sel = 1.0 / gden
    grp = jnp.min(jnp.where(glog == gmax, lane, big), axis=-1, keepdims=True)
    emask = (lane >= N_GROUPS) & (lane < N_GROUPS + N_EXPERTS) & (((lane - N_GROUPS) >> 3) == grp)
    el = jnp.where(emask, lg, ninf)
    v1 = jnp.max(el, axis=-1, keepdims=True)
    i1 = jnp.min(jnp.where(el == v1, lane, big), axis=-1, keepdims=True)
    el2 = jnp.where(lane == i1, ninf, el)
    v2 = jnp.max(el2, axis=-1, keepdims=True)
    i2 = jnp.min(jnp.where(el2 == v2, lane, big), axis=-1, keepdims=True)
    e = jnp.exp(v2 - v1)
    w1 = p_sel / (1.0 + e)
    w2 = p_sel * e / (1.0 + e)
    e0 = i1 - N_GROUPS
    e1 = i2 - N_GROUPS

    oh0 = lane == e0
    oh1 = lane == e1
    onehot = jnp.where(oh0 | oh1, 1.0, 0.0)
    prefix = jnp.dot(tri_ref[...], onehot.astype(BF16), preferred_element_type=F32) + run_ref[0:1, :]
    r0 = jnp.sum(jnp.where(oh0, prefix, 0.0), axis=-1, keepdims=True).astype(I32)
    r1 = jnp.sum(jnp.where(oh1, prefix, 0.0), axis=-1, keepdims=True).astype(I32)
    run = run_ref[0:1, :] + jnp.sum(onehot, axis=0, keepdims=True)
    run_ref[...] = jnp.broadcast_to(run, run_ref.shape)
    cnt_ref[...] = jnp.broadcast_to(run, cnt_ref.shape).astype(I32)

    zi = jnp.zeros(lg.shape, I32)
    idx = jnp.where(lane == 0, e0, zi)
    idx = jnp.where(lane == 1, e1, idx)
    idx = jnp.where(lane == 2, r0, idx)
    idx = jnp.where(lane == 3, r1, idx)
    idx_ref[...] = idx
    wt_ref[...] = jnp.where(lane == 0, w1, jnp.where(lane == 1, w2, 0.0))


def _route(logits, tri):
    row = pl.BlockSpec((ROUTE_TM, LANES), lambda i: (i, 0))
    return pl.pallas_call(
        _route_body,
        grid=(TOKENS // ROUTE_TM,),
        in_specs=[row, pl.BlockSpec((ROUTE_TM, ROUTE_TM), lambda i: (0, 0))],
        out_specs=[row, row, pl.BlockSpec((8, LANES), lambda i: (0, 0))],
        out_shape=[
            jax.ShapeDtypeStruct((TOKENS, LANES), I32),
            jax.ShapeDtypeStruct((TOKENS, LANES), F32),
            jax.ShapeDtypeStruct((8, LANES), I32),
        ],
        scratch_shapes=[pltpu.VMEM((8, LANES), F32)],
        compiler_params=_cparams(("arbitrary",)),
        name="route",
    )(logits, tri)


DISP_TM = 256


def _dispatch_body(meta_ref, dest_ref, h_ref, xs_ref, zbuf, sem, zsem):
    @pl.when(pl.program_id(0) == 0)
    def _():
        zbuf[...] = jnp.zeros_like(zbuf)

        def zero_block(row0):
            return pltpu.make_async_copy(zbuf, xs_ref.at[pl.ds(row0, MOE_BLOCK), :], zsem)

        def per_expert(e, n):
            has = meta_ref[N_EXPERTS + e] > 0

            @pl.when(has)
            def _():
                zero_block(pl.multiple_of(meta_ref[e] - MOE_BLOCK, MOE_BLOCK)).start()

            return n + has.astype(I32)

        n_last = lax.fori_loop(0, N_EXPERTS, per_expert, jnp.int32(0))
        used = meta_ref[2 * N_EXPERTS]

        def tail(j, c):
            zero_block(pl.multiple_of(j * MOE_BLOCK, MOE_BLOCK)).start()
            return c

        lax.fori_loop(used, N_BLOCKS, tail, 0)

        def drain(i, c):
            zero_block(0).wait()
            return c

        lax.fori_loop(0, n_last + (N_BLOCKS - used), drain, 0)

    def row_copy(r, d):
        return pltpu.make_async_copy(h_ref.at[pl.ds(r, 1), :], xs_ref.at[pl.ds(d, 1), :], sem)

    def issue(r, c):
        for k in range(TOP_K):
            row_copy(r, dest_ref[TOP_K * r + k]).start()
        return c

    lax.fori_loop(0, DISP_TM, issue, 0, unroll=4)
    for k in range(TOP_K):
        pltpu.make_async_copy(h_ref, xs_ref.at[pl.ds(0, DISP_TM), :], sem).wait()


def _dispatch(meta, dest_flat, hp):
    grid_spec = pltpu.PrefetchScalarGridSpec(
        num_scalar_prefetch=1,
        grid=(TOKENS // DISP_TM,),
        in_specs=[
            pl.BlockSpec((DISP_TM * TOP_K,), lambda i, m: (i,), memory_space=pltpu.SMEM),
            pl.BlockSpec((DISP_TM, D_MODEL // 2), lambda i, m: (i, 0)),
        ],
        out_specs=pl.BlockSpec(memory_space=pl.ANY),
        scratch_shapes=[
            pltpu.VMEM((MOE_BLOCK, D_MODEL // 2), U32),
            pltpu.SemaphoreType.DMA,
            pltpu.SemaphoreType.DMA,
        ],
    )
    return pl.pallas_call(
        _dispatch_body,
        grid_spec=grid_spec,
        out_shape=jax.ShapeDtypeStruct((N_SLOTS, D_MODEL // 2), U32),
        compiler_params=_cparams(("arbitrary",)),
        name="dispatch",
    )(meta, dest_flat, hp)


def _slots_body(idx_ref, start_ref, dest_ref):
    idx = idx_ref[...]
    lane = lax.broadcasted_iota(I32, idx.shape, 1)
    start = start_ref[...]
    out = jnp.zeros(idx.shape, I32)
    for k in range(TOP_K):
        e = idx[:, k:k + 1]
        base = jnp.sum(jnp.where(lane == e, start, 0.0), axis=-1, keepdims=True).astype(I32)
        out = jnp.where(lane == k, base + idx[:, TOP_K + k:TOP_K + k + 1], out)
    dest_ref[...] = out


SLOTS_TM = 2048


def _slots(idx, pad_start_row):
    row = pl.BlockSpec((SLOTS_TM, LANES), lambda i: (i, 0))
    return pl.pallas_call(
        _slots_body,
        grid=(TOKENS // SLOTS_TM,),
        in_specs=[row, pl.BlockSpec((1, LANES), lambda i: (0, 0))],
        out_specs=row,
        out_shape=jax.ShapeDtypeStruct((TOKENS, LANES), I32),
        compiler_params=_cparams(("parallel",)),
        name="slots",
    )(idx, pad_start_row)


def _experts_body(sched_ref, x_ref, wg_hbm, wu_hbm, wd_hbm, y_ref,
                  wg_st, wu_st, wd_st, wg_bf, wu_bf, wd_bf, sem):
    j = pl.program_id(0)
    e = sched_ref[j]
    slot = sched_ref[N_BLOCKS + j]
    nxt = sched_ref[2 * N_BLOCKS + j]
    used = sched_ref[3 * N_BLOCKS]
    new_expert = (j == 0) | (e != sched_ref[jnp.maximum(j - 1, 0)])

    def weight_copies(expert, s):
        return (pltpu.make_async_copy(wg_hbm.at[expert], wg_st.at[s], sem.at[s, 0]),
                pltpu.make_async_copy(wu_hbm.at[expert], wu_st.at[s], sem.at[s, 1]),
                pltpu.make_async_copy(wd_hbm.at[expert], wd_st.at[s], sem.at[s, 2]))

    @pl.when(j == 0)
    def _():
        for c in weight_copies(e, slot):
            c.start()

    @pl.when(new_expert)
    def _():
        for c in weight_copies(e, slot):
            c.wait()

        @pl.when(nxt >= 0)
        def _():
            for c in weight_copies(nxt, 1 - slot):
                c.start()

        wg_bf[...] = wg_st[slot].astype(BF16)
        wu_bf[...] = wu_st[slot].astype(BF16)
        wd_bf[...] = wd_st[slot].astype(BF16)

    @pl.when(j < used)
    def _():
        w = x_ref[...]
        xa = pltpu.bitcast(w & jnp.uint32(0xFFFF0000), F32).astype(BF16)
        xb = pltpu.bitcast(w << 16, F32).astype(BF16)
        x = jnp.concatenate([xa, xb], axis=1)
        g = jnp.dot(x, wg_bf[...], preferred_element_type=F32)
        u = jnp.dot(x, wu_bf[...], preferred_element_type=F32)
        hb = (_silu(g) * u).astype(BF16)
        y_ref[...] = jnp.dot(hb, wd_bf[...], preferred_element_type=F32)

    @pl.when(j >= used)
    def _():
        y_ref[...] = jnp.zeros_like(y_ref)


def _experts(sched, xs, w_gate, w_up, w_down):
    grid_spec = pltpu.PrefetchScalarGridSpec(
        num_scalar_prefetch=1,
        grid=(N_BLOCKS,),
        in_specs=[
            pl.BlockSpec((MOE_BLOCK, D_MODEL // 2), lambda j, s: (j, 0)),
            pl.BlockSpec(memory_space=pl.ANY),
            pl.BlockSpec(memory_space=pl.ANY),
            pl.BlockSpec(memory_space=pl.ANY),
        ],
        out_specs=pl.BlockSpec((MOE_BLOCK, D_MODEL), lambda j, s: (j, 0)),
        scratch_shapes=[
            pltpu.VMEM((2, D_MODEL, D_EXPERT), F32),
            pltpu.VMEM((2, D_MODEL, D_EXPERT), F32),
            pltpu.VMEM((2, D_EXPERT, D_MODEL), F32),
            pltpu.VMEM((D_MODEL, D_EXPERT), BF16),
            pltpu.VMEM((D_MODEL, D_EXPERT), BF16),
            pltpu.VMEM((D_EXPERT, D_MODEL), BF16),
            pltpu.SemaphoreType.DMA((2, 3)),
        ],
    )
    return pl.pallas_call(
        _experts_body,
        grid_spec=grid_spec,
        out_shape=jax.ShapeDtypeStruct((N_SLOTS, D_MODEL), F32),
        compiler_params=_cparams(("arbitrary",)),
        name="experts",
    )(sched, xs, w_gate, w_up, w_down)


COMB_TM = 256


def _combine_body(dest_ref, wt_ref, x1_ref, gain_ref, ys_ref, o_ref, buf, sem):
    def row_copy(r, k, d):
        return pltpu.make_async_copy(ys_ref.at[pl.ds(d, 1), :], buf.at[k, pl.ds(r, 1), :], sem)

    def issue(r, c):
        for k in range(TOP_K):
            row_copy(r, k, dest_ref[TOP_K * r + k]).start()
        return c

    lax.fori_loop(0, COMB_TM, issue, 0, unroll=4)
    for k in range(TOP_K):
        pltpu.make_async_copy(ys_ref.at[pl.ds(0, COMB_TM), :], buf.at[k], sem).wait()

    wt = wt_ref[...]
    y = x1_ref[...] + wt[:, 0:1] * buf[0] + wt[:, 1:2] * buf[1]
    ms = jnp.mean(y * y, axis=-1, keepdims=True)
    o_ref[...] = y * lax.rsqrt(ms + EPS) * gain_ref[...]


def _combine(dest_flat, wts, x1, norm_final, ys):
    return pl.pallas_call(
        _combine_body,
        grid=(TOKENS // COMB_TM,),
        in_specs=[
            pl.BlockSpec((COMB_TM * TOP_K,), lambda i: (i,), memory_space=pltpu.SMEM),
            pl.BlockSpec((COMB_TM, LANES), lambda i: (i, 0)),
            pl.BlockSpec((COMB_TM, D_MODEL), lambda i: (i, 0)),
            pl.BlockSpec((1, D_MODEL), lambda i: (0, 0)),
            pl.BlockSpec(memory_space=pl.ANY),
        ],
        out_specs=pl.BlockSpec((COMB_TM, D_MODEL), lambda i: (i, 0)),
        out_shape=jax.ShapeDtypeStruct((TOKENS, D_MODEL), F32),
        scratch_shapes=[pltpu.VMEM((TOP_K, COMB_TM, D_MODEL), F32), pltpu.SemaphoreType.DMA],
        compiler_params=_cparams(("arbitrary",)),
        name="combine",
    )(dest_flat, wts, x1, norm_final, ys)


def _pad_lanes(v):
    v = v.reshape(1, -1).astype(F32)
    return jnp.pad(v, ((0, 0), (0, LANES - v.shape[1])))


def kernel(x, norm_mix, w_in, conv_w, gdn_a_log, gdn_dt_bias, gdn_norm, swa_sinks, w_out, norm_ffn,
           w_router_group, b_router_group, w_router_expert, b_router_expert, w_gate, w_up, w_down,
           norm_final):
    l = 0
    x2d = x.reshape(TOKENS, D_MODEL)
    w = w_in[l]
    o_z = 3 * D_MODEL
    o_a = o_z + D_MODEL
    o_sq = o_a + 2 * GDN_HEADS
    o_sk = o_sq + D_MODEL
    o_sv = o_sk + SWA_KV_HEADS * SWA_DH
    o_gg = o_sv + SWA_KV_HEADS * SWA_DH
    o_gs = o_gg + D_MODEL
    w_main = jnp.concatenate(
        [w[:, :o_a], w[:, o_sq:o_sk], w[:, o_gg:o_gs], w[:, o_gs:], w[:, o_sk:o_sv], w[:, o_sv:o_gg]],
        axis=1).astype(BF16)
    w_ab = jnp.pad(w[:, o_a:o_sq], ((0, 0), (0, LANES - 2 * GDN_HEADS))).astype(BF16)

    proj, ab = _inproj(x2d, norm_mix[l].reshape(1, D_MODEL), w_main, w_ab)

    g1, gt = _gdn_prep(ab, _pad_lanes(gdn_a_log[l]), _pad_lanes(gdn_dt_bias[l]))
    gr = gt[:, :GDN_HEADS, :].reshape(BATCH, GDN_HEADS, N_GROUPS_SEQ, GDN_GROUP_ROWS)
    y_gdn = _gdn(proj, g1, gr, conv_w[l], gdn_norm[l].reshape(1, GDN_DV))

    y_swa = _swa(proj, swa_sinks[l].astype(F32))

    w_r = jnp.concatenate([w_router_group[l], w_router_expert[l]], axis=1).astype(F32)
    w_r = jnp.pad(w_r, ((0, 0), (0, LANES - w_r.shape[1])))
    wr_hi = w_r.astype(BF16)
    wr_lo = (w_r - wr_hi.astype(F32)).astype(BF16)
    r_bias = _pad_lanes(jnp.concatenate([b_router_group[l], b_router_expert[l]]))
    x1, hp, logits = _outproj(x2d, y_gdn, y_swa, proj, w_out[l].astype(BF16),
                              norm_ffn[l].reshape(1, D_MODEL), jnp.concatenate([wr_hi, wr_lo], axis=1), r_bias)

    tri = (lax.broadcasted_iota(I32, (ROUTE_TM, ROUTE_TM), 1)
           < lax.broadcasted_iota(I32, (ROUTE_TM, ROUTE_TM), 0)).astype(BF16)
    idx, wts, cnt = _route(logits, tri)

    counts = cnt[0, :N_EXPERTS]
    padded = (counts + MOE_BLOCK - 1) // MOE_BLOCK * MOE_BLOCK
    pad_end = jnp.cumsum(padded)
    pad_start = pad_end - padded
    dest = _slots(idx, _pad_lanes(pad_start))
    dest_flat = dest[:, :TOP_K].reshape(N_ASSIGN)
    blk_pos = jnp.arange(N_BLOCKS, dtype=I32) * MOE_BLOCK
    blk_e = jnp.minimum(jnp.sum((pad_end[None, :] <= blk_pos[:, None]).astype(I32), axis=1), N_EXPERTS - 1)
    used = pad_end[-1:] // MOE_BLOCK
    meta = jnp.concatenate([pad_end, padded, used]).astype(I32)
    is_new = jnp.concatenate([jnp.ones((1,), I32), (blk_e[1:] != blk_e[:-1]).astype(I32)])
    ordinal = jnp.cumsum(is_new) - 1
    nxt_pos = jnp.sum((ordinal[None, :] <= ordinal[:, None]).astype(I32), axis=1)
    nxt = jnp.where(nxt_pos < N_BLOCKS, blk_e[jnp.minimum(nxt_pos, N_BLOCKS - 1)], -1)
    sched = jnp.concatenate([blk_e, ordinal & 1, nxt, used]).astype(I32)

    xs = _dispatch(meta, dest_flat, hp)
    ys = _experts(sched, xs, w_gate[l], w_up[l], w_down[l])
    out = _combine(dest_flat, wts, x1, norm_final.reshape(1, D_MODEL), ys)
    return out.reshape(BATCH, SEQ, D_MODEL)
```

```python
import functools

import jax
import jax.numpy as jnp
from jax import lax
from jax.experimental import pallas as pl
from jax.experimental.pallas import tpu as pltpu

F32 = jnp.float32
BF16 = jnp.bfloat16
I32 = jnp.int32
U32 = jnp.uint32

D_MODEL = 2048
BATCH = 16
SEQ = 2048
TOKENS = BATCH * SEQ
EPS = 1e-6

GDN_HEADS = 16
GDN_DK = 128
GDN_DV = 128
GDN_CHUNK = 64
CONV_W = 4
GDN_GROUP = 4
GDN_GROUP_ROWS = GDN_GROUP * GDN_CHUNK
N_CHUNKS = SEQ // GDN_CHUNK
N_GROUPS_SEQ = SEQ // GDN_GROUP_ROWS

SWA_HEADS = 32
SWA_KV_HEADS = 4
SWA_DH = 64
SWA_GROUP = 8
SWA_BLOCK = 128
WINDOW = 128

N_GROUPS = 8
EXPERTS_PER_GROUP = 8
N_EXPERTS = 64
TOP_K = 2
D_EXPERT = 512
MOE_BLOCK = 128
N_ASSIGN = TOKENS * TOP_K
N_BLOCKS = N_ASSIGN // MOE_BLOCK + N_EXPERTS
N_SLOTS = N_BLOCKS * MOE_BLOCK

PROJ_DIM = 7 * D_MODEL + 2 * SWA_KV_HEADS * SWA_DH
LANES = 128

VMEM_LIMIT = 56 * 1024 * 1024


def _cparams(sem, vmem=VMEM_LIMIT):
    return pltpu.CompilerParams(dimension_semantics=sem, vmem_limit_bytes=vmem)


def _sigmoid(x):
    return 0.5 * jnp.tanh(0.5 * x) + 0.5


def _silu(x):
    h = 0.5 * x
    return h * jnp.tanh(h) + h


INPROJ_TM = 2048
INPROJ_TN = 512
INPROJ_VMEM = 60 * 1024 * 1024


def _inproj_body(x_ref, g_ref, w_ref, wab_ref, o_ref, ab_ref, h_ref):
    @pl.when(pl.program_id(1) == 0)
    def _():
        def chunk(i, c):
            r = pl.ds(pl.multiple_of(i * 128, 128), 128)
            x = x_ref[r, :]
            ms = jnp.mean(x * x, axis=-1, keepdims=True)
            h_ref[r, :] = (x * lax.rsqrt(ms + EPS) * g_ref[...]).astype(BF16)
            return c
        lax.fori_loop(0, INPROJ_TM // 128, chunk, 0)
        ab_ref[...] = jnp.dot(h_ref[...], wab_ref[...], preferred_element_type=F32)

    o_ref[...] = jnp.dot(h_ref[...], w_ref[...], preferred_element_type=F32).astype(BF16)


def _inproj(x2d, gain, w_main, w_ab):
    grid = (TOKENS // INPROJ_TM, PROJ_DIM // INPROJ_TN)
    return pl.pallas_call(
        _inproj_body,
        grid=grid,
        in_specs=[
            pl.BlockSpec((INPROJ_TM, D_MODEL), lambda i, j: (i, 0)),
            pl.BlockSpec((1, D_MODEL), lambda i, j: (0, 0)),
            pl.BlockSpec((D_MODEL, INPROJ_TN), lambda i, j: (0, j)),
            pl.BlockSpec((D_MODEL, LANES), lambda i, j: (0, 0)),
        ],
        out_specs=[
            pl.BlockSpec((INPROJ_TM, INPROJ_TN), lambda i, j: (i, j)),
            pl.BlockSpec((INPROJ_TM, LANES), lambda i, j: (i, 0)),
        ],
        out_shape=[
            jax.ShapeDtypeStruct((TOKENS, PROJ_DIM), BF16),
            jax.ShapeDtypeStruct((TOKENS, LANES), F32),
        ],
        scratch_shapes=[pltpu.VMEM((INPROJ_TM, D_MODEL), BF16)],
        compiler_params=_cparams(("parallel", "arbitrary"), vmem=INPROJ_VMEM),
        name="inproj",
    )(x2d, gain, w_main, w_ab)


def _gdn_prep_body(ab_ref, alog_ref, dtb_ref, g1_ref, gt_ref):
    ab = ab_ref[...]
    lane = lax.broadcasted_iota(I32, ab.shape, 1)
    row = lax.broadcasted_iota(I32, ab.shape, 0) % GDN_CHUNK
    xa = ab + dtb_ref[...]
    softplus = jnp.maximum(xa, 0.0) + jnp.log(1.0 + jnp.exp(-jnp.abs(xa)))
    g = jnp.where(lane < GDN_HEADS, -jnp.exp(alog_ref[...]) * softplus, 0.0)
    gam = g
    s = 1
    while s < GDN_CHUNK:
        gam = gam + jnp.where(row >= s, pltpu.roll(gam, s, 0), 0.0)
        s *= 2
    g1_ref[...] = jnp.where(lane < GDN_HEADS, gam, _sigmoid(ab))
    gt_ref[0] = gam.T


def _gdn_prep(ab, alog_pad, dtb_pad):
    return pl.pallas_call(
        _gdn_prep_body,
        grid=(BATCH,),
        in_specs=[
            pl.BlockSpec((SEQ, LANES), lambda b: (b, 0)),
            pl.BlockSpec((1, LANES), lambda b: (0, 0)),
            pl.BlockSpec((1, LANES), lambda b: (0, 0)),
        ],
        out_specs=[
            pl.BlockSpec((SEQ, LANES), lambda b: (b, 0)),
            pl.BlockSpec((1, LANES, SEQ), lambda b: (b, 0, 0)),
        ],
        out_shape=[
            jax.ShapeDtypeStruct((TOKENS, LANES), F32),
            jax.ShapeDtypeStruct((BATCH, LANES, SEQ), F32),
        ],
        compiler_params=_cparams(("parallel",)),
        name="gdn_prep",
    )(ab, alog_pad, dtb_pad)


GDN_HB = 2
GDN_PAIR = 2
N_PAIRS = N_GROUPS_SEQ // GDN_PAIR
PAIR_CHUNKS = GDN_PAIR * GDN_GROUP
CONV_PAD = 8


def _block_diag(x):
    t = jnp.concatenate([x] * GDN_GROUP, axis=0)
    rb = lax.broadcasted_iota(I32, t.shape, 0) // GDN_CHUNK
    cb = lax.broadcasted_iota(I32, t.shape, 1) // GDN_CHUNK
    return jnp.where(rb == cb, t, jnp.zeros_like(t))


def _sbs_product(lhs_list, y):
    bd = _block_diag(y.astype(BF16))
    xs = [x.astype(BF16) for x in lhs_list]
    lhs = xs[0] if len(xs) == 1 else jnp.concatenate(xs, axis=0)
    r = jnp.dot(lhs, bd, preferred_element_type=F32)
    c = GDN_CHUNK
    return [r[i * c:(i + 1) * c] for i in range(len(xs))]


def _gdn_body(q_ref, k_ref, v_ref, z_ref, g1_ref, gr_ref, cw_q_ref, cw_k_ref, cw_v_ref, gain_ref,
              o_ref,
              qbf, kbf, kdbf, rhsbf, qd, gamb, betab, o0s, qts, vs, wps, xpad):
    hg = pl.program_id(1)
    shape = (SEQ, LANES)
    lane = lax.broadcasted_iota(I32, shape, 1)

    xpad[pl.ds(0, CONV_PAD), :] = jnp.zeros((CONV_PAD, LANES), F32)

    def conv_silu(x_ref, cw_ref, hs):
        xpad[pl.ds(CONV_PAD, SEQ), :] = x_ref[:, hs].astype(F32)
        cw = cw_ref[:, hs]
        y = None
        for s in range(CONV_W):
            t = xpad[pl.ds(CONV_PAD - s, SEQ), :] * cw[CONV_W - 1 - s:CONV_W - s, :]
            y = t if y is None else y + t
        return _silu(y)

    g1 = g1_ref[...]
    for hd in range(GDN_HB):
        hs = slice(hd * LANES, (hd + 1) * LANES)
        head = hg * GDN_HB + hd

        def col(off):
            c = jnp.sum(jnp.where(lane == off + head, g1, 0.0), axis=-1, keepdims=True)
            return jnp.broadcast_to(c, shape)

        gam = col(0)
        beta = col(GDN_HEADS)
        gamb[hd] = gam
        betab[hd] = beta
        gam3 = gam.reshape(N_CHUNKS, GDN_CHUNK, LANES)
        glast = jnp.broadcast_to(gam3[:, GDN_CHUNK - 1:GDN_CHUNK, :], gam3.shape).reshape(shape)
        eg = jnp.exp(gam)
        kdf = jnp.exp(glast - gam)

        q = conv_silu(q_ref, cw_q_ref, hs)
        q = q * (lax.rsqrt(jnp.sum(q * q, -1, keepdims=True) + EPS) * (GDN_DK ** -0.5))
        qbf[hd] = q.astype(BF16)
        qd[hd] = q * eg
        k = conv_silu(k_ref, cw_k_ref, hs)
        k = k * lax.rsqrt(jnp.sum(k * k, -1, keepdims=True) + EPS)
        kbf[hd] = k.astype(BF16)
        kdbf[hd] = (k * kdf).astype(BF16)
        rhsbf[hd, :, LANES:] = (k * (beta * eg)).astype(BF16)
        v = conv_silu(v_ref, cw_v_ref, hs)
        rhsbf[hd, :, :LANES] = (v * beta).astype(BF16)

    gshape = (GDN_CHUNK, GDN_GROUP_ROWS)
    ii = lax.broadcasted_iota(I32, gshape, 0)
    jj = lax.broadcasted_iota(I32, gshape, 1) % GDN_CHUNK
    eye = jnp.where(ii == jj, 1.0, 0.0).astype(F32)
    pr = lax.broadcasted_iota(I32, (2 * GDN_CHUNK, LANES), 0) // GDN_CHUNK
    tl = lax.broadcasted_iota(I32, (LANES, GDN_GROUP_ROWS), 1) // GDN_CHUNK
    lane_c = lax.broadcasted_iota(I32, (GDN_CHUNK, LANES), 1)
    nt = (((1,), (1,)), ((), ()))

    def sbs_bcast(ref, hd, base):
        parts = [ref[hd, pl.ds(base + m * GDN_CHUNK, GDN_CHUNK), :] for m in range(GDN_GROUP)]
        a = jnp.where(lane_c < GDN_CHUNK, parts[0], parts[1])
        b = jnp.where(lane_c < GDN_CHUNK, parts[2], parts[3])
        return jnp.concatenate([a, b], axis=1)

    def phase1(hd, g):
        base = pl.multiple_of(g * GDN_GROUP_ROWS, GDN_GROUP_ROWS)
        res = []
        for p in range(2):
            rows = pl.ds(base + p * 2 * GDN_CHUNK, 2 * GDN_CHUNK)
            kst = kbf[hd, rows, :]
            qst = qbf[hd, rows, :]
            lhs_k = jnp.concatenate([kst[:GDN_CHUNK], kst[GDN_CHUNK:]], axis=1)
            lhs_q = jnp.concatenate([qst[:GDN_CHUNK], qst[GDN_CHUNK:]], axis=1)
            lhs = jnp.concatenate([lhs_q, lhs_k], axis=0)
            zero = jnp.zeros_like(kst)
            bt = jnp.concatenate([jnp.where(pr == 0, kst, zero), jnp.where(pr == 1, kst, zero)], axis=1)
            res.append(lax.dot_general(lhs, bt, nt, preferred_element_type=F32))
        yield
        qk = jnp.concatenate([res[0][:GDN_CHUNK], res[1][:GDN_CHUNK]], axis=1)
        kk = jnp.concatenate([res[0][GDN_CHUNK:], res[1][GDN_CHUNK:]], axis=1)

        gc = sbs_bcast(gamb, hd, base)
        bc = sbs_bcast(betab, hd, base)
        grow = gr_ref[0, hd, pl.ds(g, 1), :]
        decay = jnp.exp(jnp.where(ii >= jj, gc - grow, -jnp.inf))
        a = jnp.where(ii > jj, kk * decay * bc, 0.0)
        qkd = qk * decay

        u = eye - a
        (x,) = _sbs_product([a], a)
        yield
        for lvl in range(1, 6):
            if lvl < 5:
                x2, ux = _sbs_product([x, u], x)
                u = u + ux
                x = x2
            else:
                (ux,) = _sbs_product([u], x)
                u = u + ux
            yield

        rows4 = pl.ds(base, GDN_GROUP_ROWS)
        so = jnp.dot(_block_diag(u.astype(BF16)), rhsbf[hd, rows4, :], preferred_element_type=F32)
        yield
        sol = so.astype(BF16)

        kdt = kdbf[hd, rows4, :].astype(F32).T.astype(BF16)
        zt = jnp.zeros_like(kdt)
        lhs2 = jnp.concatenate(
            [_block_diag(qkd.astype(BF16))] + [jnp.where(tl == m, kdt, zt) for m in range(GDN_GROUP)],
            axis=0)
        r = jnp.dot(lhs2, sol, preferred_element_type=F32)
        o0s[hd, rows4, :] = r[:GDN_GROUP_ROWS, :LANES]
        qts[hd, rows4, :] = qd[hd, rows4, :] - r[:GDN_GROUP_ROWS, LANES:]
        for m in range(GDN_GROUP):
            blk = r[GDN_GROUP_ROWS + m * LANES:GDN_GROUP_ROWS + (m + 1) * LANES]
            dst = pl.ds(pl.multiple_of((g * GDN_GROUP + m) * LANES, LANES), LANES)
            vs[hd, dst, :] = blk[:, :LANES]
            wps[hd, dst, :] = blk[:, LANES:]

    gain = gain_ref[...]

    def phase2(hd, c, s):
        rows = pl.ds(pl.multiple_of(c * GDN_CHUNK, GDN_CHUNK), GDN_CHUNK)
        srow = pl.ds(pl.multiple_of(c * LANES, LANES), LANES)
        lhs = jnp.concatenate([wps[hd, srow, :].astype(BF16), qts[hd, rows, :].astype(BF16)], axis=0)
        r = jnp.dot(lhs, s.astype(BF16), preferred_element_type=F32)
        o = r[LANES:] + o0s[hd, rows, :]
        dec = jnp.exp(gamb[hd, pl.ds(c * GDN_CHUNK + GDN_CHUNK - 1, 1), :])
        s_new = s * dec + vs[hd, srow, :] - r[:LANES]
        y = o * lax.rsqrt(jnp.mean(o * o, -1, keepdims=True) + EPS) * gain
        zz = z_ref[rows, hd * LANES:(hd + 1) * LANES].astype(F32)
        o_ref[rows, hd * LANES:(hd + 1) * LANES] = (y * _silu(zz)).astype(BF16)
        return s_new

    def phase2_chain(hd, pair, states):
        s = states[hd]
        for j in range(PAIR_CHUNKS):
            s = phase2(hd, pair * PAIR_CHUNKS + j, s)
            yield
        states[hd] = s

    def run_interleaved(chains):
        live = list(chains)
        while live:
            nxt = []
            for c in live:
                try:
                    next(c)
                    nxt.append(c)
                except StopIteration:
                    pass
            live = nxt

    def phase1_chains(pair):
        return [phase1(hd, pair * GDN_PAIR + gg) for gg in range(GDN_PAIR) for hd in range(GDN_HB)]

    run_interleaved(phase1_chains(0))

    def body(pair, states):
        states = list(states)
        run_interleaved([phase2_chain(hd, pair - 1, states) for hd in range(GDN_HB)] + phase1_chains(pair))
        return tuple(states)

    s0 = tuple(jnp.zeros((GDN_DK, GDN_DV), F32) for _ in range(GDN_HB))
    states = list(lax.fori_loop(1, N_PAIRS, body, s0))
    run_interleaved([phase2_chain(hd, N_PAIRS - 1, states) for hd in range(GDN_HB)])


def _gdn(proj, g1, gr, conv_w, gdn_norm):
    w = GDN_HB * LANES
    per_row = D_MODEL // w
    hblk = lambda off: pl.BlockSpec((SEQ, w), lambda b, h, off=off: (b, off * per_row + h))
    cblk = lambda off: pl.BlockSpec((CONV_W, w), lambda b, h, off=off: (0, off * per_row + h))
    big = lambda dt: pltpu.VMEM((GDN_HB, SEQ, LANES), dt)
    return pl.pallas_call(
        _gdn_body,
        grid=(BATCH, GDN_HEADS // GDN_HB),
        in_specs=[
            hblk(0), hblk(1), hblk(2), hblk(3),
            pl.BlockSpec((SEQ, LANES), lambda b, h: (b, 0)),
            pl.BlockSpec((1, GDN_HB, N_GROUPS_SEQ, GDN_GROUP_ROWS), lambda b, h: (b, h, 0, 0)),
            cblk(0), cblk(1), cblk(2),
            pl.BlockSpec((1, LANES), lambda b, h: (0, 0)),
        ],
        out_specs=pl.BlockSpec((SEQ, w), lambda b, h: (b, h)),
        out_shape=jax.ShapeDtypeStruct((TOKENS, D_MODEL), BF16),
        scratch_shapes=[
            big(BF16), big(BF16), big(BF16), pltpu.VMEM((GDN_HB, SEQ, 2 * LANES), BF16),
            big(F32), big(F32), big(F32), big(F32), big(F32),
            pltpu.VMEM((GDN_HB, N_CHUNKS * LANES, LANES), F32),
            pltpu.VMEM((GDN_HB, N_CHUNKS * LANES, LANES), F32),
            pltpu.VMEM((CONV_PAD + SEQ, LANES), F32),
        ],
        compiler_params=_cparams(("parallel", "parallel")),
        name="gdn",
    )(proj, proj, proj, proj, g1, gr, conv_w, conv_w, conv_w, gdn_norm)


SWA_PIPE = 3
assert WINDOW == SWA_BLOCK


def _swa_body(sink_ref, q_ref, kc_ref, kp_ref, vc_ref, vp_ref, o_ref):
    i = pl.program_id(1)
    kcat = jnp.concatenate([kp_ref[...], kc_ref[...]], axis=0)
    vcat = jnp.concatenate([vp_ref[...], vc_ref[...]], axis=0)
    qi = lax.broadcasted_iota(I32, (SWA_BLOCK, SWA_BLOCK), 0)
    ki = lax.broadcasted_iota(I32, (SWA_BLOCK, SWA_BLOCK), 1)
    take_cur = ki <= qi
    valid = take_cur | (i > 0)
    lane_kv = lax.broadcasted_iota(I32, (2 * SWA_BLOCK, LANES), 1)
    lane_q = lax.broadcasted_iota(I32, (SWA_BLOCK, LANES), 1)
    nt = (((1,), (1,)), ((), ()))
    scale = SWA_DH ** -0.5

    def head_pair(h0, ks, vs):
        qs = q_ref[:, h0 * SWA_DH:(h0 + 2) * SWA_DH]
        scores = [lax.dot_general(qs, kk, nt, preferred_element_type=F32) for kk in ks]
        for _ in range(SWA_PIPE):
            yield
        acc = None
        inv = None
        for half in range(2):
            sink = sink_ref[h0 + half]
            sc = scores[half]
            s = jnp.where(take_cur, sc[:, SWA_BLOCK:], sc[:, :SWA_BLOCK]) * scale
            s = jnp.where(valid, s, -jnp.inf)
            m = jnp.maximum(jnp.max(s, axis=-1, keepdims=True), sink)
            p = jnp.exp(s - m)
            den = jnp.sum(p, axis=-1, keepdims=True) + jnp.exp(sink - m)
            p2 = jnp.concatenate([jnp.where(take_cur, 0.0, p), jnp.where(take_cur, p, 0.0)], axis=1)
            pv = jnp.dot(p2.astype(BF16), vs[half], preferred_element_type=F32)
            acc = pv if acc is None else acc + pv
            r = 1.0 / den
            inv = r if inv is None else jnp.where(lane_q < SWA_DH, inv, r)
        yield
        o_ref[:, h0 * SWA_DH:(h0 + 2) * SWA_DH] = (acc * inv).astype(BF16)

    chains = []
    for slab in range(SWA_KV_HEADS // 2):
        k2 = kcat[:, slab * LANES:(slab + 1) * LANES].astype(F32)
        v2 = vcat[:, slab * LANES:(slab + 1) * LANES].astype(F32)
        k2r = pltpu.roll(k2, SWA_DH, 1)
        v2r = pltpu.roll(v2, SWA_DH, 1)
        for sub in range(2):
            kvh = slab * 2 + sub
            if sub == 0:
                k_lo = jnp.where(lane_kv < SWA_DH, k2, 0.0)
                k_hi = jnp.where(lane_kv >= SWA_DH, k2r, 0.0)
                v_lo = jnp.where(lane_kv < SWA_DH, v2, 0.0)
                v_hi = jnp.where(lane_kv >= SWA_DH, v2r, 0.0)
            else:
                k_lo = jnp.where(lane_kv < SWA_DH, k2r, 0.0)
                k_hi = jnp.where(lane_kv >= SWA_DH, k2, 0.0)
                v_lo = jnp.where(lane_kv < SWA_DH, v2r, 0.0)
                v_hi = jnp.where(lane_kv >= SWA_DH, v2, 0.0)
            k_lo, k_hi, v_lo, v_hi = (t.astype(BF16) for t in (k_lo, k_hi, v_lo, v_hi))
            for gp in range(SWA_GROUP // 2):
                h0 = kvh * SWA_GROUP + 2 * gp
                chains.append(head_pair(h0, (k_lo, k_hi), (v_lo, v_hi)))

    live = []
    pending = list(chains)
    while pending or live:
        if pending:
            live.append(pending.pop(0))
        nxt = []
        for c in live:
            try:
                next(c)
                nxt.append(c)
            except StopIteration:
                pass
        live = nxt


def _swa(proj, sinks):
    nb = SEQ // SWA_BLOCK
    qcol = 4
    kcol = (7 * D_MODEL) // 256
    vcol = kcol + 1
    cur = lambda col: pl.BlockSpec((SWA_BLOCK, 256), lambda b, i, col=col: (b * nb + i, col))
    prev = lambda col: pl.BlockSpec(
        (SWA_BLOCK, 256), lambda b, i, col=col: (b * nb + jnp.maximum(i - 1, 0), col))
    return pl.pallas_call(
        _swa_body,
        grid=(BATCH, nb),
        in_specs=[
            pl.BlockSpec(memory_space=pltpu.SMEM),
            pl.BlockSpec((SWA_BLOCK, D_MODEL), lambda b, i: (b * nb + i, qcol)),
            cur(kcol), prev(kcol), cur(vcol), prev(vcol),
        ],
        out_specs=pl.BlockSpec((SWA_BLOCK, D_MODEL), lambda b, i: (b * nb + i, 0)),
        out_shape=jax.ShapeDtypeStruct((TOKENS, D_MODEL), BF16),
        compiler_params=_cparams(("parallel", "parallel")),
        name="swa",
    )(sinks, proj, proj, proj, proj, proj)


OUT_TM = 512
OUT_SUB = 256


def _outproj_body(x_ref, yg_ref, ys_ref, gg_ref, gs_ref, w_ref, nf_ref, wr_ref, rb_ref,
                  x1_ref, hp_ref, lg_ref):
    def sub_tile(t):
        rows = pl.ds(t * OUT_SUB, OUT_SUB)
        merged = (_sigmoid(gg_ref[rows, :].astype(F32)) * yg_ref[rows, :].astype(F32)
                  + _sigmoid(gs_ref[rows, :].astype(F32)) * ys_ref[rows, :].astype(F32))
        acc = jnp.dot(merged.astype(BF16), w_ref[...], preferred_element_type=F32)
        yield
        x1 = x_ref[rows, :] + acc
        x1_ref[rows, :] = x1
        ms = jnp.mean(x1 * x1, axis=-1, keepdims=True)
        h = x1 * lax.rsqrt(ms + EPS) * nf_ref[...]
        hh = h.astype(BF16)
        r = jnp.dot(hh, wr_ref[...], preferred_element_type=F32)
        lg_ref[rows, :] = r[:, :LANES] + r[:, LANES:] + rb_ref[...]
        half = D_MODEL // 2
        hf = hh.astype(F32)
        wa = pltpu.bitcast(hf[:, :half], U32)
        wb = pltpu.bitcast(hf[:, half:], U32)
        hp_ref[rows, :] = (wa & jnp.uint32(0xFFFF0000)) | (wb >> 16)

    tiles = [sub_tile(t) for t in range(OUT_TM // OUT_SUB)]
    for _ in range(2):
        for g in tiles:
            next(g, None)


def _outproj(x2d, y_gdn, y_swa, proj, w_out, norm_ffn, wr2, r_bias):
    row = lambda w: pl.BlockSpec((OUT_TM, w), lambda i: (i, 0))
    const = lambda s: pl.BlockSpec(s, lambda i: (0, 0))
    return pl.pallas_call(
        _outproj_body,
        grid=(TOKENS // OUT_TM,),
        in_specs=[
            row(D_MODEL), row(D_MODEL), row(D_MODEL),
            pl.BlockSpec((OUT_TM, D_MODEL), lambda i: (i, 5)),
            pl.BlockSpec((OUT_TM, D_MODEL), lambda i: (i, 6)),
            pl.BlockSpec((D_MODEL, D_MODEL), lambda i: (0, 0), pipeline_mode=pl.Buffered(1)),
            const((1, D_MODEL)),
            const((D_MODEL, 2 * LANES)), const((1, LANES)),
        ],
        out_specs=[row(D_MODEL), row(D_MODEL // 2), row(LANES)],
        out_shape=[
            jax.ShapeDtypeStruct((TOKENS, D_MODEL), F32),
            jax.ShapeDtypeStruct((TOKENS, D_MODEL // 2), U32),
            jax.ShapeDtypeStruct((TOKENS, LANES), F32),
        ],
        compiler_params=_cparams(("parallel",)),
        name="outproj",
    )(x2d, y_gdn, y_swa, proj, proj, w_out, norm_ffn, wr2, r_bias)


ROUTE_TM = 1024


def _route_body(lg_ref, tri_ref, idx_ref, wt_ref, cnt_ref, run_ref):
    @pl.when(pl.program_id(0) == 0)
    def _():
        run_ref[...] = jnp.zeros_like(run_ref)

    lg = lg_ref[...]
    lane = lax.broadcasted_iota(I32, lg.shape, 1)
    ninf = -jnp.inf
    big = jnp.int32(LANES)
    is_g = lane < N_GROUPS
    glog = jnp.where(is_g, lg, ninf)
    gmax = jnp.max(glog, axis=-1, keepdims=True)
    gden = jnp.sum(jnp.where(is_g, jnp.exp(lg - gmax), 0.0), axis=-1, keepdims=True)
    p_sel = 1.0 / gden
    grp = jnp.min(jnp.where(glog == gmax, lane, big), axis=-1, keepdims=True)
    emask = (lane >= N_GROUPS) & (lane < N_GROUPS + N_EXPERTS) & (((lane - N_GROUPS) >> 3) == grp)
    el = jnp.where(emask, lg, ninf)
    v1 = jnp.max(el, axis=-1, keepdims=True)
    i1 = jnp.min(jnp.where(el == v1, lane, big), axis=-1, keepdims=True)
    el2 = jnp.where(lane == i1, ninf, el)
    v2 = jnp.max(el2, axis=-1, keepdims=True)
    i2 = jnp.min(jnp.where(el2 == v2, lane, big), axis=-1, keepdims=True)
    e = jnp.exp(v2 - v1)
    w1 = p_sel / (1.0 + e)
    w2 = p_sel * e / (1.0 + e)
    e0 = i1 - N_GROUPS
    e1 = i2 - N_GROUPS

    oh0 = lane == e0
    oh1 = lane == e1
    onehot = jnp.where(oh0 | oh1, 1.0, 0.0)
    prefix = jnp.dot(tri_ref[...], onehot.astype(BF16), preferred_element_type=F32) + run_ref[0:1, :]
    r0 = jnp.sum(jnp.where(oh0, prefix, 0.0), axis=-1, keepdims=True).astype(I32)
    r1 = jnp.sum(jnp.where(oh1, prefix, 0.0), axis=-1, keepdims=True).astype(I32)
    run = run_ref[0:1, :] + jnp.sum(onehot, axis=0, keepdims=True)
    run_ref[...] = jnp.broadcast_to(run, run_ref.shape)
    cnt_ref[...] = jnp.broadcast_to(run, cnt_ref.shape).astype(I32)

    zi = jnp.zeros(lg.shape, I32)
    idx = jnp.where(lane == 0, e0, zi)
    idx = jnp.where(lane == 1, e1, idx)
    idx = jnp.where(lane == 2, r0, idx)
    idx = jnp.where(lane == 3, r1, idx)
    idx_ref[...] = idx
    wt_ref[...] = jnp.where(lane == 0, w1, jnp.where(lane == 1, w2, 0.0))


def _route(logits, tri):
    row = pl.BlockSpec((ROUTE_TM, LANES), lambda i: (i, 0))
    return pl.pallas_call(
        _route_body,
        grid=(TOKENS // ROUTE_TM,),
        in_specs=[row, pl.BlockSpec((ROUTE_TM, ROUTE_TM), lambda i: (0, 0))],
        out_specs=[row, row, pl.BlockSpec((8, LANES), lambda i: (0, 0))],
        out_shape=[
            jax.ShapeDtypeStruct((TOKENS, LANES), I32),
            jax.ShapeDtypeStruct((TOKENS, LANES), F32),
            jax.ShapeDtypeStruct((8, LANES), I32),
        ],
        scratch_shapes=[pltpu.VMEM((8, LANES), F32)],
        compiler_params=_cparams(("arbitrary",)),
        name="route",
    )(logits, tri)


DISP_TM = 256


def _dispatch_body(meta_ref, dest_ref, h_ref, xs_ref, zbuf, sem, zsem):
    @pl.when(pl.program_id(0) == 0)
    def _():
        zbuf[...] = jnp.zeros_like(zbuf)

        def zero_block(row0):
            return pltpu.make_async_copy(zbuf, xs_ref.at[pl.ds(row0, MOE_BLOCK), :], zsem)

        def per_expert(e, n):
            has = meta_ref[N_EXPERTS + e] > 0

            @pl.when(has)
            def _():
                zero_block(pl.multiple_of(meta_ref[e] - MOE_BLOCK, MOE_BLOCK)).start()

            return n + has.astype(I32)

        n_last = lax.fori_loop(0, N_EXPERTS, per_expert, jnp.int32(0))
        used = meta_ref[2 * N_EXPERTS]

        def tail(j, c):
            zero_block(pl.multiple_of(j * MOE_BLOCK, MOE_BLOCK)).start()
            return c

        lax.fori_loop(used, N_BLOCKS, tail, 0)

        def drain(i, c):
            zero_block(0).wait()
            return c

        lax.fori_loop(0, n_last + (N_BLOCKS - used), drain, 0)

    def row_copy(r, d):
        return pltpu.make_async_copy(h_ref.at[pl.ds(r, 1), :], xs_ref.at[pl.ds(d, 1), :], sem)

    def issue(r, c):
        for k in range(TOP_K):
            row_copy(r, dest_ref[TOP_K * r + k]).start()
        return c

    lax.fori_loop(0, DISP_TM, issue, 0, unroll=4)
    for k in range(TOP_K):
        pltpu.make_async_copy(h_ref, xs_ref.at[pl.ds(0, DISP_TM), :], sem).wait()


def _dispatch(meta, dest_flat, hp):
    grid_spec = pltpu.PrefetchScalarGridSpec(
        num_scalar_prefetch=1,
        grid=(TOKENS // DISP_TM,),
        in_specs=[
            pl.BlockSpec((DISP_TM * TOP_K,), lambda i, m: (i,), memory_space=pltpu.SMEM),
            pl.BlockSpec((DISP_TM, D_MODEL // 2), lambda i, m: (i, 0)),
        ],
        out_specs=pl.BlockSpec(memory_space=pl.ANY),
        scratch_shapes=[
            pltpu.VMEM((MOE_BLOCK, D_MODEL // 2), U32),
            pltpu.SemaphoreType.DMA,
            pltpu.SemaphoreType.DMA,
        ],
    )
    return pl.pallas_call(
        _dispatch_body,
        grid_spec=grid_spec,
        out_shape=jax.ShapeDtypeStruct((N_SLOTS, D_MODEL // 2), U32),
        compiler_params=_cparams(("arbitrary",)),
        name="dispatch",
    )(meta, dest_flat, hp)


def _slots_body(idx_ref, start_ref, dest_ref):
    idx = idx_ref[...]
    lane = lax.broadcasted_iota(I32, idx.shape, 1)
    start = start_ref[...]
    out = jnp.zeros(idx.shape, I32)
    for k in range(TOP_K):
        e = idx[:, k:k + 1]
        base = jnp.sum(jnp.where(lane == e, start, 0.0), axis=-1, keepdims=True).astype(I32)
        out = jnp.where(lane == k, base + idx[:, TOP_K + k:TOP_K + k + 1], out)
    dest_ref[...] = out


SLOTS_TM = 2048


def _slots(idx, pad_start_row):
    row = pl.BlockSpec((SLOTS_TM, LANES), lambda i: (i, 0))
    return pl.pallas_call(
        _slots_body,
        grid=(TOKENS // SLOTS_TM,),
        in_specs=[row, pl.BlockSpec((1, LANES), lambda i: (0, 0))],
        out_specs=row,
        out_shape=jax.ShapeDtypeStruct((TOKENS, LANES), I32),
        compiler_params=_cparams(("parallel",)),
        name="slots",
    )(idx, pad_start_row)


def _experts_body(sched_ref, x_ref, wg_hbm, wu_hbm, wd_hbm, y_ref,
                  wg_st, wu_st, wd_st, wg_bf, wu_bf, wd_bf, sem):
    j = pl.program_id(0)
    e = sched_ref[j]
    slot = sched_ref[N_BLOCKS + j]
    nxt = sched_ref[2 * N_BLOCKS + j]
    used = sched_ref[3 * N_BLOCKS]
    new_expert = (j == 0) | (e != sched_ref[jnp.maximum(j - 1, 0)])

    def weight_copies(expert, s):
        return (pltpu.make_async_copy(wg_hbm.at[expert], wg_st.at[s], sem.at[s, 0]),
                pltpu.make_async_copy(wu_hbm.at[expert], wu_st.at[s], sem.at[s, 1]),
                pltpu.make_async_copy(wd_hbm.at[expert], wd_st.at[s], sem.at[s, 2]))

    @pl.when(j == 0)
    def _():
        for c in weight_copies(e, slot):
            c.start(priority=1)

    @pl.when(new_expert)
    def _():
        for c in weight_copies(e, slot):
            c.wait()

        @pl.when(nxt >= 0)
        def _():
            for c in weight_copies(nxt, 1 - slot):
                c.start(priority=1)

        wg_bf[...] = wg_st[slot].astype(BF16)
        wu_bf[...] = wu_st[slot].astype(BF16)
        wd_bf[...] = wd_st[slot].astype(BF16)

    @pl.when(j < used)
    def _():
        w = x_ref[...]
        xa = pltpu.bitcast(w & jnp.uint32(0xFFFF0000), F32).astype(BF16)
        xb = pltpu.bitcast(w << 16, F32).astype(BF16)
        x = jnp.concatenate([xa, xb], axis=1)
        g = jnp.dot(x, wg_bf[...], preferred_element_type=F32)
        u = jnp.dot(x, wu_bf[...], preferred_element_type=F32)
        hb = (_silu(g) * u).astype(BF16)
        y_ref[...] = jnp.dot(hb, wd_bf[...], preferred_element_type=F32)

    @pl.when(j >= used)
    def _():
        y_ref[...] = jnp.zeros_like(y_ref)


def _experts(sched, xs, w_gate, w_up, w_down):
    grid_spec = pltpu.PrefetchScalarGridSpec(
        num_scalar_prefetch=1,
        grid=(N_BLOCKS,),
        in_specs=[
            pl.BlockSpec((MOE_BLOCK, D_MODEL // 2), lambda j, s: (j, 0)),
            pl.BlockSpec(memory_space=pl.ANY),
            pl.BlockSpec(memory_space=pl.ANY),
            pl.BlockSpec(memory_space=pl.ANY),
        ],
        out_specs=pl.BlockSpec((MOE_BLOCK, D_MODEL), lambda j, s: (j, 0)),
        scratch_shapes=[
            pltpu.VMEM((2, D_MODEL, D_EXPERT), F32),
            pltpu.VMEM((2, D_MODEL, D_EXPERT), F32),
            pltpu.VMEM((2, D_EXPERT, D_MODEL), F32),
            pltpu.VMEM((D_MODEL, D_EXPERT), BF16),
            pltpu.VMEM((D_MODEL, D_EXPERT), BF16),
            pltpu.VMEM((D_EXPERT, D_MODEL), BF16),
            pltpu.SemaphoreType.DMA((2, 3)),
        ],
    )
    return pl.pallas_call(
        _experts_body,
        grid_spec=grid_spec,
        out_shape=jax.ShapeDtypeStruct((N_SLOTS, D_MODEL), F32),
        compiler_params=_cparams(("arbitrary",)),
        name="experts",
    )(sched, xs, w_gate, w_up, w_down)


COMB_TM = 256


def _combine_body(dest_ref, wt_ref, x1_ref, gain_ref, ys_ref, o_ref, buf, sem):
    def row_copy(r, k, d):
        return pltpu.make_async_copy(ys_ref.at[pl.ds(d, 1), :], buf.at[k, pl.ds(r, 1), :], sem)

    def issue(r, c):
        for k in range(TOP_K):
            row_copy(r, k, dest_ref[TOP_K * r + k]).start()
        return c

    lax.fori_loop(0, COMB_TM, issue, 0, unroll=4)
    for k in range(TOP_K):
        pltpu.make_async_copy(ys_ref.at[pl.ds(0, COMB_TM), :], buf.at[k], sem).wait()

    wt = wt_ref[...]
    y = x1_ref[...] + wt[:, 0:1] * buf[0] + wt[:, 1:2] * buf[1]
    ms = jnp.mean(y * y, axis=-1, keepdims=True)
    o_ref[...] = y * lax.rsqrt(ms + EPS) * gain_ref[...]


def _combine(dest_flat, wts, x1, norm_final, ys):
    return pl.pallas_call(
        _combine_body,
        grid=(TOKENS // COMB_TM,),
        in_specs=[
            pl.BlockSpec((COMB_TM * TOP_K,), lambda i: (i,), memory_space=pltpu.SMEM),
            pl.BlockSpec((COMB_TM, LANES), lambda i: (i, 0)),
            pl.BlockSpec((COMB_TM, D_MODEL), lambda i: (i, 0)),
            pl.BlockSpec((1, D_MODEL), lambda i: (0, 0)),
            pl.BlockSpec(memory_space=pl.ANY),
        ],
        out_specs=pl.BlockSpec((COMB_TM, D_MODEL), lambda i: (i, 0)),
        out_shape=jax.ShapeDtypeStruct((TOKENS, D_MODEL), F32),
        scratch_shapes=[pltpu.VMEM((TOP_K, COMB_TM, D_MODEL), F32), pltpu.SemaphoreType.DMA],
        compiler_params=_cparams(("arbitrary",)),
        name="combine",
    )(dest_flat, wts, x1, norm_final, ys)


def _pad_lanes(v):
    v = v.reshape(1, -1).astype(F32)
    return jnp.pad(v, ((0, 0), (0, LANES - v.shape[1])))


def kernel(x, norm_mix, w_in, conv_w, gdn_a_log, gdn_dt_bias, gdn_norm, swa_sinks, w_out, norm_ffn,
           w_router_group, b_router_group, w_router_expert, b_router_expert, w_gate, w_up, w_down,
           norm_final):
    l = 0
    x2d = x.reshape(TOKENS, D_MODEL)
    w = w_in[l]
    o_z = 3 * D_MODEL
    o_a = o_z + D_MODEL
    o_sq = o_a + 2 * GDN_HEADS
    o_sk = o_sq + D_MODEL
    o_sv = o_sk + SWA_KV_HEADS * SWA_DH
    o_gg = o_sv + SWA_KV_HEADS * SWA_DH
    o_gs = o_gg + D_MODEL
    w_main = jnp.concatenate(
        [w[:, :o_a], w[:, o_sq:o_sk], w[:, o_gg:o_gs], w[:, o_gs:], w[:, o_sk:o_sv], w[:, o_sv:o_gg]],
        axis=1).astype(BF16)
    w_ab = jnp.pad(w[:, o_a:o_sq], ((0, 0), (0, LANES - 2 * GDN_HEADS))).astype(BF16)

    proj, ab = _inproj(x2d, norm_mix[l].reshape(1, D_MODEL), w_main, w_ab)

    g1, gt = _gdn_prep(ab, _pad_lanes(gdn_a_log[l]), _pad_lanes(gdn_dt_bias[l]))
    gr = gt[:, :GDN_HEADS, :].reshape(BATCH, GDN_HEADS, N_GROUPS_SEQ, GDN_GROUP_ROWS)
    y_gdn = _gdn(proj, g1, gr, conv_w[l], gdn_norm[l].reshape(1, GDN_DV))

    y_swa = _swa(proj, swa_sinks[l].astype(F32))

    w_r = jnp.concatenate([w_router_group[l], w_router_expert[l]], axis=1).astype(F32)
    w_r = jnp.pad(w_r, ((0, 0), (0, LANES - w_r.shape[1])))
    wr_hi = w_r.astype(BF16)
    wr_lo = (w_r - wr_hi.astype(F32)).astype(BF16)
    r_bias = _pad_lanes(jnp.concatenate([b_router_group[l], b_router_expert[l]]))
    x1, hp, logits = _outproj(x2d, y_gdn, y_swa, proj, w_out[l].astype(BF16),
                              norm_ffn[l].reshape(1, D_MODEL), jnp.concatenate([wr_hi, wr_lo], axis=1), r_bias)

    tri = (lax.broadcasted_iota(I32, (ROUTE_TM, ROUTE_TM), 1)
           < lax.broadcasted_iota(I32, (ROUTE_TM, ROUTE_TM), 0)).astype(BF16)
    idx, wts, cnt = _route(logits, tri)

    counts = cnt[0, :N_EXPERTS]
    padded = (counts + MOE_BLOCK - 1) // MOE_BLOCK * MOE_BLOCK
    pad_end = jnp.cumsum(padded)
    pad_start = pad_end - padded
    dest = _slots(idx, _pad_lanes(pad_start))
    dest_flat = dest[:, :TOP_K].reshape(N_ASSIGN)
    blk_pos = jnp.arange(N_BLOCKS, dtype=I32) * MOE_BLOCK
    blk_e = jnp.minimum(jnp.sum((pad_end[None, :] <= blk_pos[:, None]).astype(I32), axis=1), N_EXPERTS - 1)
    used = pad_end[-1:] // MOE_BLOCK
    meta = jnp.concatenate([pad_end, padded, used]).astype(I32)
    is_new = jnp.concatenate([jnp.ones((1,), I32), (blk_e[1:] != blk_e[:-1]).astype(I32)])
    ordinal = jnp.cumsum(is_new) - 1
    nxt_pos = jnp.sum((ordinal[None, :] <= ordinal[:, None]).astype(I32), axis=1)
    nxt = jnp.where(nxt_pos < N_BLOCKS, blk_e[jnp.minimum(nxt_pos, N_BLOCKS - 1)], -1)
    sched = jnp.concatenate([blk_e, ordinal & 1, nxt, used]).astype(I32)

    xs = _dispatch(meta, dest_flat, hp)
    ys = _experts(sched, xs, w_gate[l], w_up[l], w_down[l])
    out = _combine(dest_flat, wts, x1, norm_final.reshape(1, D_MODEL), ys)
    return out.reshape(BATCH, SEQ, D_MODEL)
```

```python
import functools

import jax
import jax.numpy as jnp
from jax import lax
from jax.experimental import pallas as pl
from jax.experimental.pallas import tpu as pltpu

F32 = jnp.float32
BF16 = jnp.bfloat16
I32 = jnp.int32
U32 = jnp.uint32

D_MODEL = 2048
BATCH = 16
SEQ = 2048
TOKENS = BATCH * SEQ
EPS = 1e-6

GDN_HEADS = 16
GDN_DK = 128
GDN_DV = 128
GDN_CHUNK = 64
CONV_W = 4
GDN_GROUP = 4
GDN_GROUP_ROWS = GDN_GROUP * GDN_CHUNK
N_CHUNKS = SEQ // GDN_CHUNK
N_GROUPS_SEQ = SEQ // GDN_GROUP_ROWS

SWA_HEADS = 32
SWA_KV_HEADS = 4
SWA_DH = 64
SWA_GROUP = 8
SWA_BLOCK = 128
WINDOW = 128

N_GROUPS = 8
EXPERTS_PER_GROUP = 8
N_EXPERTS = 64
TOP_K = 2
D_EXPERT = 512
MOE_BLOCK = 128
N_ASSIGN = TOKENS * TOP_K
N_BLOCKS = N_ASSIGN // MOE_BLOCK + N_EXPERTS
N_SLOTS = N_BLOCKS * MOE_BLOCK

PROJ_DIM = 7 * D_MODEL + 2 * SWA_KV_HEADS * SWA_DH
LANES = 128

VMEM_LIMIT = 56 * 1024 * 1024


def _cparams(sem, vmem=VMEM_LIMIT):
    return pltpu.CompilerParams(dimension_semantics=sem, vmem_limit_bytes=vmem)


def _sigmoid(x):
    return 0.5 * jnp.tanh(0.5 * x) + 0.5


def _silu(x):
    h = 0.5 * x
    return h * jnp.tanh(h) + h


INPROJ_TM = 2048
INPROJ_TN = 512
INPROJ_VMEM = 60 * 1024 * 1024


INPROJ_RC = 256
QKV_TILES = 3 * D_MODEL // INPROJ_TN
QK_TILES = 2 * D_MODEL // INPROJ_TN
Q_TILES = D_MODEL // INPROJ_TN
assert INPROJ_TM == SEQ


def _inproj_body(x_ref, g_ref, w_ref, wab_ref, cw_ref, o_ref, ab_ref, h_ref, cpad):
    j = pl.program_id(1)

    @pl.when(j == 0)
    def _():
        def chunk(i, c):
            r = pl.ds(pl.multiple_of(i * 128, 128), 128)
            x = x_ref[r, :]
            ms = jnp.mean(x * x, axis=-1, keepdims=True)
            h_ref[r, :] = (x * lax.rsqrt(ms + EPS) * g_ref[...]).astype(BF16)
            return c
        lax.fori_loop(0, INPROJ_TM // 128, chunk, 0)
        ab_ref[...] = jnp.dot(h_ref[...], wab_ref[...], preferred_element_type=F32)

    @pl.when(j >= QKV_TILES)
    def _():
        o_ref[...] = jnp.dot(h_ref[...], w_ref[...], preferred_element_type=F32).astype(BF16)

    @pl.when(j < QKV_TILES)
    def _():
        cw = cw_ref[...]
        is_qk = j < QK_TILES
        qscale = jnp.where(j < Q_TILES, GDN_DK ** -0.5, 1.0).astype(F32)
        n_chunks = INPROJ_TM // INPROJ_RC
        res = [None] * n_chunks

        def matmul(c):
            rows = pl.ds(c * INPROJ_RC, INPROJ_RC)
            res[c] = jnp.dot(h_ref[rows, :], w_ref[...], preferred_element_type=F32)

        def epilogue(c):
            buf = cpad.at[c % 2]
            buf[pl.ds(0, 8), :] = jnp.zeros((8, INPROJ_TN), F32) if c == 0 else res[c - 1][INPROJ_RC - 8:]
            buf[pl.ds(8, INPROJ_RC), :] = res[c]
            y = None
            for s in range(CONV_W):
                t = buf[pl.ds(8 - s, INPROJ_RC), :] * cw[CONV_W - 1 - s:CONV_W - s, :]
                y = t if y is None else y + t
            y = _silu(y)
            parts = []
            for g in range(INPROJ_TN // LANES):
                yg = y[:, g * LANES:(g + 1) * LANES]
                inv = lax.rsqrt(jnp.sum(yg * yg, -1, keepdims=True) + EPS) * qscale
                parts.append(yg * jnp.where(is_qk, inv, 1.0))
            o_ref[pl.ds(c * INPROJ_RC, INPROJ_RC), :] = jnp.concatenate(parts, axis=1).astype(BF16)

        matmul(0)
        for c in range(n_chunks):
            if c + 1 < n_chunks:
                matmul(c + 1)
            epilogue(c)


def _inproj(x2d, gain, w_main, w_ab, conv_w):
    grid = (TOKENS // INPROJ_TM, PROJ_DIM // INPROJ_TN)
    return pl.pallas_call(
        _inproj_body,
        grid=grid,
        in_specs=[
            pl.BlockSpec((INPROJ_TM, D_MODEL), lambda i, j: (i, 0)),
            pl.BlockSpec((1, D_MODEL), lambda i, j: (0, 0)),
            pl.BlockSpec((D_MODEL, INPROJ_TN), lambda i, j: (0, j)),
            pl.BlockSpec((D_MODEL, LANES), lambda i, j: (0, 0)),
            pl.BlockSpec((CONV_W, INPROJ_TN), lambda i, j: (0, jnp.minimum(j, QKV_TILES - 1))),
        ],
        out_specs=[
            pl.BlockSpec((INPROJ_TM, INPROJ_TN), lambda i, j: (i, j)),
            pl.BlockSpec((INPROJ_TM, LANES), lambda i, j: (i, 0)),
        ],
        out_shape=[
            jax.ShapeDtypeStruct((TOKENS, PROJ_DIM), BF16),
            jax.ShapeDtypeStruct((TOKENS, LANES), F32),
        ],
        scratch_shapes=[pltpu.VMEM((INPROJ_TM, D_MODEL), BF16),
                        pltpu.VMEM((2, 8 + INPROJ_RC, INPROJ_TN), F32)],
        compiler_params=_cparams(("parallel", "arbitrary"), vmem=INPROJ_VMEM),
        name="inproj",
    )(x2d, gain, w_main, w_ab, conv_w)


def _gdn_prep_body(ab_ref, alog_ref, dtb_ref, g1_ref, gt_ref):
    ab = ab_ref[...]
    lane = lax.broadcasted_iota(I32, ab.shape, 1)
    row = lax.broadcasted_iota(I32, ab.shape, 0) % GDN_CHUNK
    xa = ab + dtb_ref[...]
    softplus = jnp.maximum(xa, 0.0) + jnp.log(1.0 + jnp.exp(-jnp.abs(xa)))
    g = jnp.where(lane < GDN_HEADS, -jnp.exp(alog_ref[...]) * softplus, 0.0)
    gam = g
    s = 1
    while s < GDN_CHUNK:
        gam = gam + jnp.where(row >= s, pltpu.roll(gam, s, 0), 0.0)
        s *= 2
    g1_ref[...] = jnp.where(lane < GDN_HEADS, gam, _sigmoid(ab))
    gt_ref[0] = gam.T


def _gdn_prep(ab, alog_pad, dtb_pad):
    return pl.pallas_call(
        _gdn_prep_body,
        grid=(BATCH,),
        in_specs=[
            pl.BlockSpec((SEQ, LANES), lambda b: (b, 0)),
            pl.BlockSpec((1, LANES), lambda b: (0, 0)),
            pl.BlockSpec((1, LANES), lambda b: (0, 0)),
        ],
        out_specs=[
            pl.BlockSpec((SEQ, LANES), lambda b: (b, 0)),
            pl.BlockSpec((1, LANES, SEQ), lambda b: (b, 0, 0)),
        ],
        out_shape=[
            jax.ShapeDtypeStruct((TOKENS, LANES), F32),
            jax.ShapeDtypeStruct((BATCH, LANES, SEQ), F32),
        ],
        compiler_params=_cparams(("parallel",)),
        name="gdn_prep",
    )(ab, alog_pad, dtb_pad)


GDN_HB = 2
GDN_PAIR = 2
N_PAIRS = N_GROUPS_SEQ // GDN_PAIR
PAIR_CHUNKS = GDN_PAIR * GDN_GROUP
CONV_PAD = 8


def _block_diag(x):
    t = jnp.concatenate([x] * GDN_GROUP, axis=0)
    rb = lax.broadcasted_iota(I32, t.shape, 0) // GDN_CHUNK
    cb = lax.broadcasted_iota(I32, t.shape, 1) // GDN_CHUNK
    return jnp.where(rb == cb, t, jnp.zeros_like(t))


def _sbs_product(lhs_list, y):
    bd = _block_diag(y.astype(BF16))
    xs = [x.astype(BF16) for x in lhs_list]
    lhs = xs[0] if len(xs) == 1 else jnp.concatenate(xs, axis=0)
    r = jnp.dot(lhs, bd, preferred_element_type=F32)
    c = GDN_CHUNK
    return [r[i * c:(i + 1) * c] for i in range(len(xs))]


def _gdn_body(q_ref, k_ref, v_ref, z_ref, g1_ref, gr_ref, gain_ref,
              o_ref,
              kdbf, rhsbf, qd, gamb, betab, o0s, qts, vs, wps):
    hg = pl.program_id(1)
    shape = (SEQ, LANES)
    lane = lax.broadcasted_iota(I32, shape, 1)

    g1 = g1_ref[...]
    for hd in range(GDN_HB):
        hs = slice(hd * LANES, (hd + 1) * LANES)
        head = hg * GDN_HB + hd

        def col(off):
            c = jnp.sum(jnp.where(lane == off + head, g1, 0.0), axis=-1, keepdims=True)
            return jnp.broadcast_to(c, shape)

        gam = col(0)
        beta = col(GDN_HEADS)
        gamb[hd] = gam
        betab[hd] = beta
        gam3 = gam.reshape(N_CHUNKS, GDN_CHUNK, LANES)
        glast = jnp.broadcast_to(gam3[:, GDN_CHUNK - 1:GDN_CHUNK, :], gam3.shape).reshape(shape)
        eg = jnp.exp(gam)
        kdf = jnp.exp(glast - gam)

        qd[hd] = q_ref[:, hs].astype(F32) * eg
        k = k_ref[:, hs].astype(F32)
        kdbf[hd] = (k * kdf).astype(BF16)
        rhsbf[hd, :, LANES:] = (k * (beta * eg)).astype(BF16)
        rhsbf[hd, :, :LANES] = (v_ref[:, hs].astype(F32) * beta).astype(BF16)

    gshape = (GDN_CHUNK, GDN_GROUP_ROWS)
    ii = lax.broadcasted_iota(I32, gshape, 0)
    jj = lax.broadcasted_iota(I32, gshape, 1) % GDN_CHUNK
    eye = jnp.where(ii == jj, 1.0, 0.0).astype(F32)
    pr = lax.broadcasted_iota(I32, (2 * GDN_CHUNK, LANES), 0) // GDN_CHUNK
    tl = lax.broadcasted_iota(I32, (LANES, GDN_GROUP_ROWS), 1) // GDN_CHUNK
    lane_c = lax.broadcasted_iota(I32, (GDN_CHUNK, LANES), 1)
    nt = (((1,), (1,)), ((), ()))

    def sbs_bcast(ref, hd, base):
        parts = [ref[hd, pl.ds(base + m * GDN_CHUNK, GDN_CHUNK), :] for m in range(GDN_GROUP)]
        a = jnp.where(lane_c < GDN_CHUNK, parts[0], parts[1])
        b = jnp.where(lane_c < GDN_CHUNK, parts[2], parts[3])
        return jnp.concatenate([a, b], axis=1)

    def phase1(hd, g):
        base = pl.multiple_of(g * GDN_GROUP_ROWS, GDN_GROUP_ROWS)
        res = []
        for p in range(2):
            rows = pl.ds(base + p * 2 * GDN_CHUNK, 2 * GDN_CHUNK)
            kst = k_ref[rows, hd * LANES:(hd + 1) * LANES]
            qst = q_ref[rows, hd * LANES:(hd + 1) * LANES]
            lhs_k = jnp.concatenate([kst[:GDN_CHUNK], kst[GDN_CHUNK:]], axis=1)
            lhs_q = jnp.concatenate([qst[:GDN_CHUNK], qst[GDN_CHUNK:]], axis=1)
            lhs = jnp.concatenate([lhs_q, lhs_k], axis=0)
            zero = jnp.zeros_like(kst)
            bt = jnp.concatenate([jnp.where(pr == 0, kst, zero), jnp.where(pr == 1, kst, zero)], axis=1)
            res.append(lax.dot_general(lhs, bt, nt, preferred_element_type=F32))
        yield
        qk = jnp.concatenate([res[0][:GDN_CHUNK], res[1][:GDN_CHUNK]], axis=1)
        kk = jnp.concatenate([res[0][GDN_CHUNK:], res[1][GDN_CHUNK:]], axis=1)

        gc = sbs_bcast(gamb, hd, base)
        bc = sbs_bcast(betab, hd, base)
        grow = gr_ref[0, hd, pl.ds(g, 1), :]
        decay = jnp.exp(jnp.where(ii >= jj, gc - grow, -jnp.inf))
        a = jnp.where(ii > jj, kk * decay * bc, 0.0)
        qkd = qk * decay

        u = eye - a
        (x,) = _sbs_product([a], a)
        yield
        for lvl in range(1, 6):
            if lvl < 5:
                x2, ux = _sbs_product([x, u], x)
                u = u + ux
                x = x2
            else:
                (ux,) = _sbs_product([u], x)
                u = u + ux
            yield

        rows4 = pl.ds(base, GDN_GROUP_ROWS)
        so = jnp.dot(_block_diag(u.astype(BF16)), rhsbf[hd, rows4, :], preferred_element_type=F32)
        yield
        sol = so.astype(BF16)

        kdt = kdbf[hd, rows4, :].astype(F32).T.astype(BF16)
        zt = jnp.zeros_like(kdt)
        lhs2 = jnp.concatenate(
            [_block_diag(qkd.astype(BF16))] + [jnp.where(tl == m, kdt, zt) for m in range(GDN_GROUP)],
            axis=0)
        r = jnp.dot(lhs2, sol, preferred_element_type=F32)
        o0s[hd, rows4, :] = r[:GDN_GROUP_ROWS, :LANES]
        qts[hd, rows4, :] = qd[hd, rows4, :] - r[:GDN_GROUP_ROWS, LANES:]
        for m in range(GDN_GROUP):
            blk = r[GDN_GROUP_ROWS + m * LANES:GDN_GROUP_ROWS + (m + 1) * LANES]
            dst = pl.ds(pl.multiple_of((g * GDN_GROUP + m) * LANES, LANES), LANES)
            vs[hd, dst, :] = blk[:, :LANES]
            wps[hd, dst, :] = blk[:, LANES:]

    gain = gain_ref[...]

    def phase2(hd, c, s):
        rows = pl.ds(pl.multiple_of(c * GDN_CHUNK, GDN_CHUNK), GDN_CHUNK)
        srow = pl.ds(pl.multiple_of(c * LANES, LANES), LANES)
        lhs = jnp.concatenate([wps[hd, srow, :].astype(BF16), qts[hd, rows, :].astype(BF16)], axis=0)
        r = jnp.dot(lhs, s.astype(BF16), preferred_element_type=F32)
        o = r[LANES:] + o0s[hd, rows, :]
        dec = jnp.exp(gamb[hd, pl.ds(c * GDN_CHUNK + GDN_CHUNK - 1, 1), :])
        s_new = s * dec + vs[hd, srow, :] - r[:LANES]
        y = o * lax.rsqrt(jnp.mean(o * o, -1, keepdims=True) + EPS) * gain
        zz = z_ref[rows, hd * LANES:(hd + 1) * LANES].astype(F32)
        o_ref[rows, hd * LANES:(hd + 1) * LANES] = (y * _silu(zz)).astype(BF16)
        return s_new

    def phase2_chain(hd, pair, states):
        s = states[hd]
        for j in range(PAIR_CHUNKS):
            s = phase2(hd, pair * PAIR_CHUNKS + j, s)
            yield
        states[hd] = s

    def run_interleaved(chains):
        live = list(chains)
        while live:
            nxt = []
            for c in live:
                try:
                    next(c)
                    nxt.append(c)
                except StopIteration:
                    pass
            live = nxt

    def phase1_chains(pair):
        return [phase1(hd, pair * GDN_PAIR + gg) for gg in range(GDN_PAIR) for hd in range(GDN_HB)]

    run_interleaved(phase1_chains(0))

    def body(pair, states):
        states = list(states)
        run_interleaved([phase2_chain(hd, pair - 1, states) for hd in range(GDN_HB)] + phase1_chains(pair))
        return tuple(states)

    s0 = tuple(jnp.zeros((GDN_DK, GDN_DV), F32) for _ in range(GDN_HB))
    states = list(lax.fori_loop(1, N_PAIRS, body, s0))
    run_interleaved([phase2_chain(hd, N_PAIRS - 1, states) for hd in range(GDN_HB)])


def _gdn(proj, g1, gr, gdn_norm):
    w = GDN_HB * LANES
    per_row = D_MODEL // w
    hblk = lambda off: pl.BlockSpec((SEQ, w), lambda b, h, off=off: (b, off * per_row + h))
    big = lambda dt: pltpu.VMEM((GDN_HB, SEQ, LANES), dt)
    return pl.pallas_call(
        _gdn_body,
        grid=(BATCH, GDN_HEADS // GDN_HB),
        in_specs=[
            hblk(0), hblk(1), hblk(2), hblk(3),
            pl.BlockSpec((SEQ, LANES), lambda b, h: (b, 0)),
            pl.BlockSpec((1, GDN_HB, N_GROUPS_SEQ, GDN_GROUP_ROWS), lambda b, h: (b, h, 0, 0)),
            pl.BlockSpec((1, LANES), lambda b, h: (0, 0)),
        ],
        out_specs=pl.BlockSpec((SEQ, w), lambda b, h: (b, h)),
        out_shape=jax.ShapeDtypeStruct((TOKENS, D_MODEL), BF16),
        scratch_shapes=[
            big(BF16), pltpu.VMEM((GDN_HB, SEQ, 2 * LANES), BF16),
            big(F32), big(F32), big(F32), big(F32), big(F32),
            pltpu.VMEM((GDN_HB, N_CHUNKS * LANES, LANES), F32),
            pltpu.VMEM((GDN_HB, N_CHUNKS * LANES, LANES), F32),
        ],
        compiler_params=_cparams(("parallel", "parallel")),
        name="gdn",
    )(proj, proj, proj, proj, g1, gr, gdn_norm)


SWA_PIPE = 3
assert WINDOW == SWA_BLOCK


def _swa_body(sink_ref, q_ref, kc_ref, kp_ref, vc_ref, vp_ref, o_ref):
    i = pl.program_id(1)
    kcat = jnp.concatenate([kp_ref[...], kc_ref[...]], axis=0)
    vcat = jnp.concatenate([vp_ref[...], vc_ref[...]], axis=0)
    qi = lax.broadcasted_iota(I32, (SWA_BLOCK, SWA_BLOCK), 0)
    ki = lax.broadcasted_iota(I32, (SWA_BLOCK, SWA_BLOCK), 1)
    take_cur = ki <= qi
    valid = take_cur | (i > 0)
    lane_kv = lax.broadcasted_iota(I32, (2 * SWA_BLOCK, LANES), 1)
    lane_q = lax.broadcasted_iota(I32, (SWA_BLOCK, LANES), 1)
    nt = (((1,), (1,)), ((), ()))
    scale = SWA_DH ** -0.5

    def head_pair(h0, ks, vs):
        qs = q_ref[:, h0 * SWA_DH:(h0 + 2) * SWA_DH]
        scores = [lax.dot_general(qs, kk, nt, preferred_element_type=F32) for kk in ks]
        for _ in range(SWA_PIPE):
            yield
        acc = None
        inv = None
        for half in range(2):
            sink = sink_ref[h0 + half]
            sc = scores[half]
            s = jnp.where(take_cur, sc[:, SWA_BLOCK:], sc[:, :SWA_BLOCK]) * scale
            s = jnp.where(valid, s, -jnp.inf)
            m = jnp.maximum(jnp.max(s, axis=-1, keepdims=True), sink)
            p = jnp.exp(s - m)
            den = jnp.sum(p, axis=-1, keepdims=True) + jnp.exp(sink - m)
            p2 = jnp.concatenate([jnp.where(take_cur, 0.0, p), jnp.where(take_cur, p, 0.0)], axis=1)
            pv = jnp.dot(p2.astype(BF16), vs[half], preferred_element_type=F32)
            acc = pv if acc is None else acc + pv
            r = 1.0 / den
            inv = r if inv is None else jnp.where(lane_q < SWA_DH, inv, r)
        yield
        o_ref[:, h0 * SWA_DH:(h0 + 2) * SWA_DH] = (acc * inv).astype(BF16)

    chains = []
    for slab in range(SWA_KV_HEADS // 2):
        k2 = kcat[:, slab * LANES:(slab + 1) * LANES].astype(F32)
        v2 = vcat[:, slab * LANES:(slab + 1) * LANES].astype(F32)
        k2r = pltpu.roll(k2, SWA_DH, 1)
        v2r = pltpu.roll(v2, SWA_DH, 1)
        for sub in range(2):
            kvh = slab * 2 + sub
            if sub == 0:
                k_lo = jnp.where(lane_kv < SWA_DH, k2, 0.0)
                k_hi = jnp.where(lane_kv >= SWA_DH, k2r, 0.0)
                v_lo = jnp.where(lane_kv < SWA_DH, v2, 0.0)
                v_hi = jnp.where(lane_kv >= SWA_DH, v2r, 0.0)
            else:
                k_lo = jnp.where(lane_kv < SWA_DH, k2r, 0.0)
                k_hi = jnp.where(lane_kv >= SWA_DH, k2, 0.0)
                v_lo = jnp.where(lane_kv < SWA_DH, v2r, 0.0)
                v_hi = jnp.where(lane_kv >= SWA_DH, v2, 0.0)
            k_lo, k_hi, v_lo, v_hi = (t.astype(BF16) for t in (k_lo, k_hi, v_lo, v_hi))
            for gp in range(SWA_GROUP // 2):
                h0 = kvh * SWA_GROUP + 2 * gp
                chains.append(head_pair(h0, (k_lo, k_hi), (v_lo, v_hi)))

    live = []
    pending = list(chains)
    while pending or live:
        if pending:
            live.append(pending.pop(0))
        nxt = []
        for c in live:
            try:
                next(c)
                nxt.append(c)
            except StopIteration:
                pass
        live = nxt


def _swa(proj, sinks):
    nb = SEQ // SWA_BLOCK
    qcol = 4
    kcol = (7 * D_MODEL) // 256
    vcol = kcol + 1
    cur = lambda col: pl.BlockSpec((SWA_BLOCK, 256), lambda b, i, col=col: (b * nb + i, col))
    prev = lambda col: pl.BlockSpec(
        (SWA_BLOCK, 256), lambda b, i, col=col: (b * nb + jnp.maximum(i - 1, 0), col))
    return pl.pallas_call(
        _swa_body,
        grid=(BATCH, nb),
        in_specs=[
            pl.BlockSpec(memory_space=pltpu.SMEM),
            pl.BlockSpec((SWA_BLOCK, D_MODEL), lambda b, i: (b * nb + i, qcol)),
            cur(kcol), prev(kcol), cur(vcol), prev(vcol),
        ],
        out_specs=pl.BlockSpec((SWA_BLOCK, D_MODEL), lambda b, i: (b * nb + i, 0)),
        out_shape=jax.ShapeDtypeStruct((TOKENS, D_MODEL), BF16),
        compiler_params=_cparams(("parallel", "parallel")),
        name="swa",
    )(sinks, proj, proj, proj, proj, proj)


OUT_TM = 512
OUT_SUB = 256


def _outproj_body(x_ref, yg_ref, ys_ref, gg_ref, gs_ref, w_ref, nf_ref, wr_ref, rb_ref,
                  x1_ref, hp_ref, lg_ref):
    def sub_tile(t):
        rows = pl.ds(t * OUT_SUB, OUT_SUB)
        merged = (_sigmoid(gg_ref[rows, :].astype(F32)) * yg_ref[rows, :].astype(F32)
                  + _sigmoid(gs_ref[rows, :].astype(F32)) * ys_ref[rows, :].astype(F32))
        acc = jnp.dot(merged.astype(BF16), w_ref[...], preferred_element_type=F32)
        yield
        x1 = x_ref[rows, :] + acc
        x1_ref[rows, :] = x1
        ms = jnp.mean(x1 * x1, axis=-1, keepdims=True)
        h = x1 * lax.rsqrt(ms + EPS) * nf_ref[...]
        hh = h.astype(BF16)
        r = jnp.dot(hh, wr_ref[...], preferred_element_type=F32)
        lg_ref[rows, :] = r[:, :LANES] + r[:, LANES:] + rb_ref[...]
        half = D_MODEL // 2
        hf = hh.astype(F32)
        wa = pltpu.bitcast(hf[:, :half], U32)
        wb = pltpu.bitcast(hf[:, half:], U32)
        hp_ref[rows, :] = (wa & jnp.uint32(0xFFFF0000)) | (wb >> 16)

    tiles = [sub_tile(t) for t in range(OUT_TM // OUT_SUB)]
    for _ in range(2):
        for g in tiles:
            next(g, None)


def _outproj(x2d, y_gdn, y_swa, proj, w_out, norm_ffn, wr2, r_bias):
    row = lambda w: pl.BlockSpec((OUT_TM, w), lambda i: (i, 0))
    const = lambda s: pl.BlockSpec(s, lambda i: (0, 0))
    return pl.pallas_call(
        _outproj_body,
        grid=(TOKENS // OUT_TM,),
        in_specs=[
            row(D_MODEL), row(D_MODEL), row(D_MODEL),
            pl.BlockSpec((OUT_TM, D_MODEL), lambda i: (i, 5)),
            pl.BlockSpec((OUT_TM, D_MODEL), lambda i: (i, 6)),
            pl.BlockSpec((D_MODEL, D_MODEL), lambda i: (0, 0), pipeline_mode=pl.Buffered(1)),
            const((1, D_MODEL)),
            const((D_MODEL, 2 * LANES)), const((1, LANES)),
        ],
        out_specs=[row(D_MODEL), row(D_MODEL // 2), row(LANES)],
        out_shape=[
            jax.ShapeDtypeStruct((TOKENS, D_MODEL), F32),
            jax.ShapeDtypeStruct((TOKENS, D_MODEL // 2), U32),
            jax.ShapeDtypeStruct((TOKENS, LANES), F32),
        ],
        compiler_params=_cparams(("parallel",)),
        name="outproj",
    )(x2d, y_gdn, y_swa, proj, proj, w_out, norm_ffn, wr2, r_bias)


ROUTE_TM = 1024


def _route_body(lg_ref, tri_ref, idx_ref, wt_ref, cnt_ref, run_ref):
    @pl.when(pl.program_id(0) == 0)
    def _():
        run_ref[...] = jnp.zeros_like(run_ref)

    lg = lg_ref[...]
    lane = lax.broadcasted_iota(I32, lg.shape, 1)
    ninf = -jnp.inf
    big = jnp.int32(LANES)
    is_g = lane < N_GROUPS
    glog = jnp.where(is_g, lg, ninf)
    gmax = jnp.max(glog, axis=-1, keepdims=True)
    gden = jnp.sum(jnp.where(is_g, jnp.exp(lg - gmax), 0.0), axis=-1, keepdims=True)
    p_sel = 1.0 / gden
    grp = jnp.min(jnp.where(glog == gmax, lane, big), axis=-1, keepdims=True)
    emask = (lane >= N_GROUPS) & (lane < N_GROUPS + N_EXPERTS) & (((lane - N_GROUPS) >> 3) == grp)
    el = jnp.where(emask, lg, ninf)
    v1 = jnp.max(el, axis=-1, keepdims=True)
    i1 = jnp.min(jnp.where(el == v1, lane, big), axis=-1, keepdims=True)
    el2 = jnp.where(lane == i1, ninf, el)
    v2 = jnp.max(el2, axis=-1, keepdims=True)
    i2 = jnp.min(jnp.where(el2 == v2, lane, big), axis=-1, keepdims=True)
    e = jnp.exp(v2 - v1)
    w1 = p_sel / (1.0 + e)
    w2 = p_sel * e / (1.0 + e)
    e0 = i1 - N_GROUPS
    e1 = i2 - N_GROUPS

    oh0 = lane == e0
    oh1 = lane == e1
    onehot = jnp.where(oh0 | oh1, 1.0, 0.0)
    prefix = jnp.dot(tri_ref[...], onehot.astype(BF16), preferred_element_type=F32) + run_ref[0:1, :]
    r0 = jnp.sum(jnp.where(oh0, prefix, 0.0), axis=-1, keepdims=True).astype(I32)
    r1 = jnp.sum(jnp.where(oh1, prefix, 0.0), axis=-1, keepdims=True).astype(I32)
    run = run_ref[0:1, :] + jnp.sum(onehot, axis=0, keepdims=True)
    run_ref[...] = jnp.broadcast_to(run, run_ref.shape)
    cnt_ref[...] = jnp.broadcast_to(run, cnt_ref.shape).astype(I32)

    zi = jnp.zeros(lg.shape, I32)
    idx = jnp.where(lane == 0, e0, zi)
    idx = jnp.where(lane == 1, e1, idx)
    idx = jnp.where(lane == 2, r0, idx)
    idx = jnp.where(lane == 3, r1, idx)
    idx_ref[...] = idx
    wt_ref[...] = jnp.where(lane == 0, w1, jnp.where(lane == 1, w2, 0.0))


def _route(logits, tri):
    row = pl.BlockSpec((ROUTE_TM, LANES), lambda i: (i, 0))
    return pl.pallas_call(
        _route_body,
        grid=(TOKENS // ROUTE_TM,),
        in_specs=[row, pl.BlockSpec((ROUTE_TM, ROUTE_TM), lambda i: (0, 0))],
        out_specs=[row, row, pl.BlockSpec((8, LANES), lambda i: (0, 0))],
        out_shape=[
            jax.ShapeDtypeStruct((TOKENS, LANES), I32),
            jax.ShapeDtypeStruct((TOKENS, LANES), F32),
            jax.ShapeDtypeStruct((8, LANES), I32),
        ],
        scratch_shapes=[pltpu.VMEM((8, LANES), F32)],
        compiler_params=_cparams(("arbitrary",)),
        name="route",
    )(logits, tri)


DISP_TM = 256


def _dispatch_body(meta_ref, dest_ref, h_ref, xs_ref, zbuf, sem, zsem):
    @pl.when(pl.program_id(0) == 0)
    def _():
        zbuf[...] = jnp.zeros_like(zbuf)

        def zero_block(row0):
            return pltpu.make_async_copy(zbuf, xs_ref.at[pl.ds(row0, MOE_BLOCK), :], zsem)

        def per_expert(e, n):
            has = meta_ref[N_EXPERTS + e] > 0

            @pl.when(has)
            def _():
                zero_block(pl.multiple_of(meta_ref[e] - MOE_BLOCK, MOE_BLOCK)).start()

            return n + has.astype(I32)

        n_last = lax.fori_loop(0, N_EXPERTS, per_expert, jnp.int32(0))
        used = meta_ref[2 * N_EXPERTS]

        def tail(j, c):
            zero_block(pl.multiple_of(j * MOE_BLOCK, MOE_BLOCK)).start()
            return c

        lax.fori_loop(used, N_BLOCKS, tail, 0)

        def drain(i, c):
            zero_block(0).wait()
            return c

        lax.fori_loop(0, n_last + (N_BLOCKS - used), drain, 0)

    def row_copy(r, d):
        return pltpu.make_async_copy(h_ref.at[pl.ds(r, 1), :], xs_ref.at[pl.ds(d, 1), :], sem)

    def issue(r, c):
        for k in range(TOP_K):
            row_copy(r, dest_ref[TOP_K * r + k]).start()
        return c

    lax.fori_loop(0, DISP_TM, issue, 0, unroll=4)
    for k in range(TOP_K):
        pltpu.make_async_copy(h_ref, xs_ref.at[pl.ds(0, DISP_TM), :], sem).wait()


def _dispatch(meta, dest_flat, hp):
    grid_spec = pltpu.PrefetchScalarGridSpec(
        num_scalar_prefetch=1,
        grid=(TOKENS // DISP_TM,),
        in_specs=[
            pl.BlockSpec((DISP_TM * TOP_K,), lambda i, m: (i,), memory_space=pltpu.SMEM),
            pl.BlockSpec((DISP_TM, D_MODEL // 2), lambda i, m: (i, 0)),
        ],
        out_specs=pl.BlockSpec(memory_space=pl.ANY),
        scratch_shapes=[
            pltpu.VMEM((MOE_BLOCK, D_MODEL // 2), U32),
            pltpu.SemaphoreType.DMA,
            pltpu.SemaphoreType.DMA,
        ],
    )
    return pl.pallas_call(
        _dispatch_body,
        grid_spec=grid_spec,
        out_shape=jax.ShapeDtypeStruct((N_SLOTS, D_MODEL // 2), U32),
        compiler_params=_cparams(("arbitrary",)),
        name="dispatch",
    )(meta, dest_flat, hp)


def _slots_body(idx_ref, start_ref, dest_ref):
    idx = idx_ref[...]
    lane = lax.broadcasted_iota(I32, idx.shape, 1)
    start = start_ref[...]
    out = jnp.zeros(idx.shape, I32)
    for k in range(TOP_K):
        e = idx[:, k:k + 1]
        base = jnp.sum(jnp.where(lane == e, start, 0.0), axis=-1, keepdims=True).astype(I32)
        out = jnp.where(lane == k, base + idx[:, TOP_K + k:TOP_K + k + 1], out)
    dest_ref[...] = out


SLOTS_TM = 2048


def _slots(idx, pad_start_row):
    row = pl.BlockSpec((SLOTS_TM, LANES), lambda i: (i, 0))
    return pl.pallas_call(
        _slots_body,
        grid=(TOKENS // SLOTS_TM,),
        in_specs=[row, pl.BlockSpec((1, LANES), lambda i: (0, 0))],
        out_specs=row,
        out_shape=jax.ShapeDtypeStruct((TOKENS, LANES), I32),
        compiler_params=_cparams(("parallel",)),
        name="slots",
    )(idx, pad_start_row)


def _experts_body(sched_ref, x_ref, wg_hbm, wu_hbm, wd_hbm, y_ref,
                  wg_st, wu_st, wd_st, wg_bf, wu_bf, wd_bf, sem):
    j = pl.program_id(0)
    e = sched_ref[j]
    slot = sched_ref[N_BLOCKS + j]
    nxt = sched_ref[2 * N_BLOCKS + j]
    used = sched_ref[3 * N_BLOCKS]
    new_expert = (j == 0) | (e != sched_ref[jnp.maximum(j - 1, 0)])

    def weight_copies(expert, s):
        return (pltpu.make_async_copy(wg_hbm.at[expert], wg_st.at[s], sem.at[s, 0]),
                pltpu.make_async_copy(wu_hbm.at[expert], wu_st.at[s], sem.at[s, 1]),
                pltpu.make_async_copy(wd_hbm.at[expert], wd_st.at[s], sem.at[s, 2]))

    @pl.when(j == 0)
    def _():
        for c in weight_copies(e, slot):
            c.start(priority=1)

    @pl.when(new_expert)
    def _():
        for c in weight_copies(e, slot):
            c.wait()

        @pl.when(nxt >= 0)
        def _():
            for c in weight_copies(nxt, 1 - slot):
                c.start(priority=1)

        wg_bf[...] = wg_st[slot].astype(BF16)
        wu_bf[...] = wu_st[slot].astype(BF16)
        wd_bf[...] = wd_st[slot].astype(BF16)

    @pl.when(j < used)
    def _():
        w = x_ref[...]
        xa = pltpu.bitcast(w & jnp.uint32(0xFFFF0000), F32).astype(BF16)
        xb = pltpu.bitcast(w << 16, F32).astype(BF16)
        x = jnp.concatenate([xa, xb], axis=1)
        g = jnp.dot(x, wg_bf[...], preferred_element_type=F32)
        u = jnp.dot(x, wu_bf[...], preferred_element_type=F32)
        hb = (_silu(g) * u).astype(BF16)
        y_ref[...] = jnp.dot(hb, wd_bf[...], preferred_element_type=F32)

    @pl.when(j >= used)
    def _():
        y_ref[...] = jnp.zeros_like(y_ref)


def _experts(sched, xs, w_gate, w_up, w_down):
    grid_spec = pltpu.PrefetchScalarGridSpec(
        num_scalar_prefetch=1,
        grid=(N_BLOCKS,),
        in_specs=[
            pl.BlockSpec((MOE_BLOCK, D_MODEL // 2), lambda j, s: (j, 0)),
            pl.BlockSpec(memory_space=pl.ANY),
            pl.BlockSpec(memory_space=pl.ANY),
            pl.BlockSpec(memory_space=pl.ANY),
        ],
        out_specs=pl.BlockSpec((MOE_BLOCK, D_MODEL), lambda j, s: (j, 0)),
        scratch_shapes=[
            pltpu.VMEM((2, D_MODEL, D_EXPERT), F32),
            pltpu.VMEM((2, D_MODEL, D_EXPERT), F32),
            pltpu.VMEM((2, D_EXPERT, D_MODEL), F32),
            pltpu.VMEM((D_MODEL, D_EXPERT), BF16),
            pltpu.VMEM((D_MODEL, D_EXPERT), BF16),
            pltpu.VMEM((D_EXPERT, D_MODEL), BF16),
            pltpu.SemaphoreType.DMA((2, 3)),
        ],
    )
    return pl.pallas_call(
        _experts_body,
        grid_spec=grid_spec,
        out_shape=jax.ShapeDtypeStruct((N_SLOTS, D_MODEL), F32),
        compiler_params=_cparams(("arbitrary",)),
        name="experts",
    )(sched, xs, w_gate, w_up, w_down)


COMB_TM = 256


def _combine_body(dest_ref, wt_ref, x1_ref, gain_ref, ys_ref, o_ref, buf, sem):
    def row_copy(r, k, d):
        return pltpu.make_async_copy(ys_ref.at[pl.ds(d, 1), :], buf.at[k, pl.ds(r, 1), :], sem)

    def issue(r, c):
        for k in range(TOP_K):
            row_copy(r, k, dest_ref[TOP_K * r + k]).start()
        return c

    lax.fori_loop(0, COMB_TM, issue, 0, unroll=4)
    for k in range(TOP_K):
        pltpu.make_async_copy(ys_ref.at[pl.ds(0, COMB_TM), :], buf.at[k], sem).wait()

    wt = wt_ref[...]
    y = x1_ref[...] + wt[:, 0:1] * buf[0] + wt[:, 1:2] * buf[1]
    ms = jnp.mean(y * y, axis=-1, keepdims=True)
    o_ref[...] = y * lax.rsqrt(ms + EPS) * gain_ref[...]


def _combine(dest_flat, wts, x1, norm_final, ys):
    return pl.pallas_call(
        _combine_body,
        grid=(TOKENS // COMB_TM,),
        in_specs=[
            pl.BlockSpec((COMB_TM * TOP_K,), lambda i: (i,), memory_space=pltpu.SMEM),
            pl.BlockSpec((COMB_TM, LANES), lambda i: (i, 0)),
            pl.BlockSpec((COMB_TM, D_MODEL), lambda i: (i, 0)),
            pl.BlockSpec((1, D_MODEL), lambda i: (0, 0)),
            pl.BlockSpec(memory_space=pl.ANY),
        ],
        out_specs=pl.BlockSpec((COMB_TM, D_MODEL), lambda i: (i, 0)),
        out_shape=jax.ShapeDtypeStruct((TOKENS, D_MODEL), F32),
        scratch_shapes=[pltpu.VMEM((TOP_K, COMB_TM, D_MODEL), F32), pltpu.SemaphoreType.DMA],
        compiler_params=_cparams(("arbitrary",)),
        name="combine",
    )(dest_flat, wts, x1, norm_final, ys)


def _pad_lanes(v):
    v = v.reshape(1, -1).astype(F32)
    return jnp.pad(v, ((0, 0), (0, LANES - v.shape[1])))


def kernel(x, norm_mix, w_in, conv_w, gdn_a_log, gdn_dt_bias, gdn_norm, swa_sinks, w_out, norm_ffn,
           w_router_group, b_router_group, w_router_expert, b_router_expert, w_gate, w_up, w_down,
           norm_final):
    l = 0
    x2d = x.reshape(TOKENS, D_MODEL)
    w = w_in[l]
    o_z = 3 * D_MODEL
    o_a = o_z + D_MODEL
    o_sq = o_a + 2 * GDN_HEADS
    o_sk = o_sq + D_MODEL
    o_sv = o_sk + SWA_KV_HEADS * SWA_DH
    o_gg = o_sv + SWA_KV_HEADS * SWA_DH
    o_gs = o_gg + D_MODEL
    w_main = jnp.concatenate(
        [w[:, :o_a], w[:, o_sq:o_sk], w[:, o_gg:o_gs], w[:, o_gs:], w[:, o_sk:o_sv], w[:, o_sv:o_gg]],
        axis=1).astype(BF16)
    w_ab = jnp.pad(w[:, o_a:o_sq], ((0, 0), (0, LANES - 2 * GDN_HEADS))).astype(BF16)

    proj, ab = _inproj(x2d, norm_mix[l].reshape(1, D_MODEL), w_main, w_ab, conv_w[l])

    g1, gt = _gdn_prep(ab, _pad_lanes(gdn_a_log[l]), _pad_lanes(gdn_dt_bias[l]))
    gr = gt[:, :GDN_HEADS, :].reshape(BATCH, GDN_HEADS, N_GROUPS_SEQ, GDN_GROUP_ROWS)
    y_gdn = _gdn(proj, g1, gr, gdn_norm[l].reshape(1, GDN_DV))

    y_swa = _swa(proj, swa_sinks[l].astype(F32))

    w_r = jnp.concatenate([w_router_group[l], w_router_expert[l]], axis=1).astype(F32)
    w_r = jnp.pad(w_r, ((0, 0), (0, LANES - w_r.shape[1])))
    wr_hi = w_r.astype(BF16)
    wr_lo = (w_r - wr_hi.astype(F32)).astype(BF16)
    r_bias = _pad_lanes(jnp.concatenate([b_router_group[l], b_router_expert[l]]))
    x1, hp, logits = _outproj(x2d, y_gdn, y_swa, proj, w_out[l].astype(BF16),
                              norm_ffn[l].reshape(1, D_MODEL), jnp.concatenate([wr_hi, wr_lo], axis=1), r_bias)

    tri = (lax.broadcasted_iota(I32, (ROUTE_TM, ROUTE_TM), 1)
           < lax.broadcasted_iota(I32, (ROUTE_TM, ROUTE_TM), 0)).astype(BF16)
    idx, wts, cnt = _route(logits, tri)

    counts = cnt[0, :N_EXPERTS]
    padded = (counts + MOE_BLOCK - 1) // MOE_BLOCK * MOE_BLOCK
    pad_end = jnp.cumsum(padded)
    pad_start = pad_end - padded
    dest = _slots(idx, _pad_lanes(pad_start))
    dest_flat = dest[:, :TOP_K].reshape(N_ASSIGN)
    blk_pos = jnp.arange(N_BLOCKS, dtype=I32) * MOE_BLOCK
    blk_e = jnp.minimum(jnp.sum((pad_end[None, :] <= blk_pos[:, None]).astype(I32), axis=1), N_EXPERTS - 1)
    used = pad_end[-1:] // MOE_BLOCK
    meta = jnp.concatenate([pad_end, padded, used]).astype(I32)
    is_new = jnp.concatenate([jnp.ones((1,), I32), (blk_e[1:] != blk_e[:-1]).astype(I32)])
    ordinal = jnp.cumsum(is_new) - 1
    nxt_pos = jnp.sum((ordinal[None, :] <= ordinal[:, None]).astype(I32), axis=1)
    nxt = jnp.where(nxt_pos < N_BLOCKS, blk_e[jnp.minimum(nxt_pos, N_BLOCKS - 1)], -1)
    sched = jnp.concatenate([blk_e, ordinal & 1, nxt, used]).astype(I32)

    xs = _dispatch(meta, dest_flat, hp)
    ys = _experts(sched, xs, w_gate[l], w_up[l], w_down[l])
    out = _combine(dest_flat, wts, x1, norm_final.reshape(1, D_MODEL), ys)
    return out.reshape(BATCH, SEQ, D_MODEL)
```

```python
import functools

import jax
import jax.numpy as jnp
from jax import lax
from jax.experimental import pallas as pl
from jax.experimental.pallas import tpu as pltpu

F32 = jnp.float32
BF16 = jnp.bfloat16
I32 = jnp.int32
U32 = jnp.uint32

D_MODEL = 2048
BATCH = 16
SEQ = 2048
TOKENS = BATCH * SEQ
EPS = 1e-6

GDN_HEADS = 16
GDN_DK = 128
GDN_DV = 128
GDN_CHUNK = 64
CONV_W = 4
GDN_GROUP = 4
GDN_GROUP_ROWS = GDN_GROUP * GDN_CHUNK
N_CHUNKS = SEQ // GDN_CHUNK
N_GROUPS_SEQ = SEQ // GDN_GROUP_ROWS

SWA_HEADS = 32
SWA_KV_HEADS = 4
SWA_DH = 64
SWA_GROUP = 8
SWA_BLOCK = 128
WINDOW = 128

N_GROUPS = 8
EXPERTS_PER_GROUP = 8
N_EXPERTS = 64
TOP_K = 2
D_EXPERT = 512
MOE_BLOCK = 128
N_ASSIGN = TOKENS * TOP_K
N_BLOCKS = N_ASSIGN // MOE_BLOCK + N_EXPERTS
N_SLOTS = N_BLOCKS * MOE_BLOCK

PROJ_DIM = 7 * D_MODEL + 2 * SWA_KV_HEADS * SWA_DH
LANES = 128

VMEM_LIMIT = 56 * 1024 * 1024


def _cparams(sem, vmem=VMEM_LIMIT):
    return pltpu.CompilerParams(dimension_semantics=sem, vmem_limit_bytes=vmem)


def _sigmoid(x):
    return 0.5 * jnp.tanh(0.5 * x) + 0.5


def _silu(x):
    h = 0.5 * x
    return h * jnp.tanh(h) + h


INPROJ_TM = 2048
INPROJ_TN = 512
INPROJ_VMEM = 60 * 1024 * 1024


INPROJ_RC = 256
QKV_TILES = 3 * D_MODEL // INPROJ_TN
QK_TILES = 2 * D_MODEL // INPROJ_TN
Q_TILES = D_MODEL // INPROJ_TN
assert INPROJ_TM == SEQ


def _inproj_body(x_ref, g_ref, w_ref, wab_ref, cw_ref, o_ref, ab_ref, h_ref, cpad):
    j = pl.program_id(1)

    @pl.when(j == 0)
    def _():
        def chunk(i, c):
            r = pl.ds(pl.multiple_of(i * 128, 128), 128)
            x = x_ref[r, :]
            ms = jnp.mean(x * x, axis=-1, keepdims=True)
            h_ref[r, :] = (x * lax.rsqrt(ms + EPS) * g_ref[...]).astype(BF16)
            return c
        lax.fori_loop(0, INPROJ_TM // 128, chunk, 0)
        ab_ref[...] = jnp.dot(h_ref[...], wab_ref[...], preferred_element_type=F32)

    @pl.when(j >= QKV_TILES)
    def _():
        o_ref[...] = jnp.dot(h_ref[...], w_ref[...], preferred_element_type=F32).astype(BF16)

    @pl.when(j < QKV_TILES)
    def _():
        cw = cw_ref[...]
        is_qk = j < QK_TILES
        qscale = jnp.where(j < Q_TILES, GDN_DK ** -0.5, 1.0).astype(F32)
        n_chunks = INPROJ_TM // INPROJ_RC
        res = [None] * n_chunks

        def matmul(c):
            rows = pl.ds(c * INPROJ_RC, INPROJ_RC)
            res[c] = jnp.dot(h_ref[rows, :], w_ref[...], preferred_element_type=F32)

        def epilogue(c):
            buf = cpad.at[c % 2]
            buf[pl.ds(0, 8), :] = jnp.zeros((8, INPROJ_TN), F32) if c == 0 else res[c - 1][INPROJ_RC - 8:]
            buf[pl.ds(8, INPROJ_RC), :] = res[c]
            y = None
            for s in range(CONV_W):
                t = buf[pl.ds(8 - s, INPROJ_RC), :] * cw[CONV_W - 1 - s:CONV_W - s, :]
                y = t if y is None else y + t
            y = _silu(y)
            parts = []
            for g in range(INPROJ_TN // LANES):
                yg = y[:, g * LANES:(g + 1) * LANES]
                inv = lax.rsqrt(jnp.sum(yg * yg, -1, keepdims=True) + EPS) * qscale
                parts.append(yg * jnp.where(is_qk, inv, 1.0))
            o_ref[pl.ds(c * INPROJ_RC, INPROJ_RC), :] = jnp.concatenate(parts, axis=1).astype(BF16)

        matmul(0)
        for c in range(n_chunks):
            if c + 1 < n_chunks:
                matmul(c + 1)
            epilogue(c)


def _inproj(x2d, gain, w_main, w_ab, conv_w):
    grid = (TOKENS // INPROJ_TM, PROJ_DIM // INPROJ_TN)
    return pl.pallas_call(
        _inproj_body,
        grid=grid,
        in_specs=[
            pl.BlockSpec((INPROJ_TM, D_MODEL), lambda i, j: (i, 0)),
            pl.BlockSpec((1, D_MODEL), lambda i, j: (0, 0)),
            pl.BlockSpec((D_MODEL, INPROJ_TN), lambda i, j: (0, j)),
            pl.BlockSpec((D_MODEL, LANES), lambda i, j: (0, 0)),
            pl.BlockSpec((CONV_W, INPROJ_TN), lambda i, j: (0, jnp.minimum(j, QKV_TILES - 1))),
        ],
        out_specs=[
            pl.BlockSpec((INPROJ_TM, INPROJ_TN), lambda i, j: (i, j)),
            pl.BlockSpec((INPROJ_TM, LANES), lambda i, j: (i, 0)),
        ],
        out_shape=[
            jax.ShapeDtypeStruct((TOKENS, PROJ_DIM), BF16),
            jax.ShapeDtypeStruct((TOKENS, LANES), F32),
        ],
        scratch_shapes=[pltpu.VMEM((INPROJ_TM, D_MODEL), BF16),
                        pltpu.VMEM((2, 8 + INPROJ_RC, INPROJ_TN), F32)],
        compiler_params=_cparams(("parallel", "arbitrary"), vmem=INPROJ_VMEM),
        name="inproj",
    )(x2d, gain, w_main, w_ab, conv_w)


def _gdn_prep_body(ab_ref, alog_ref, dtb_ref, g1_ref, gt_ref):
    ab = ab_ref[...]
    lane = lax.broadcasted_iota(I32, ab.shape, 1)
    row = lax.broadcasted_iota(I32, ab.shape, 0) % GDN_CHUNK
    xa = ab + dtb_ref[...]
    softplus = jnp.maximum(xa, 0.0) + jnp.log(1.0 + jnp.exp(-jnp.abs(xa)))
    g = jnp.where(lane < GDN_HEADS, -jnp.exp(alog_ref[...]) * softplus, 0.0)
    gam = g
    s = 1
    while s < GDN_CHUNK:
        gam = gam + jnp.where(row >= s, pltpu.roll(gam, s, 0), 0.0)
        s *= 2
    g1_ref[...] = jnp.where(lane < GDN_HEADS, gam, _sigmoid(ab))
    gt_ref[0] = gam.T


def _gdn_prep(ab, alog_pad, dtb_pad):
    return pl.pallas_call(
        _gdn_prep_body,
        grid=(BATCH,),
        in_specs=[
            pl.BlockSpec((SEQ, LANES), lambda b: (b, 0)),
            pl.BlockSpec((1, LANES), lambda b: (0, 0)),
            pl.BlockSpec((1, LANES), lambda b: (0, 0)),
        ],
        out_specs=[
            pl.BlockSpec((SEQ, LANES), lambda b: (b, 0)),
            pl.BlockSpec((1, LANES, SEQ), lambda b: (b, 0, 0)),
        ],
        out_shape=[
            jax.ShapeDtypeStruct((TOKENS, LANES), F32),
            jax.ShapeDtypeStruct((BATCH, LANES, SEQ), F32),
        ],
        compiler_params=_cparams(("parallel",)),
        name="gdn_prep",
    )(ab, alog_pad, dtb_pad)


GDN_HB = 4
GDN_PAIR = 2
N_PAIRS = N_GROUPS_SEQ // GDN_PAIR
PAIR_CHUNKS = GDN_PAIR * GDN_GROUP
PAIR_ROWS = GDN_PAIR * GDN_GROUP_ROWS


def _block_diag(x):
    t = jnp.concatenate([x] * GDN_GROUP, axis=0)
    rb = lax.broadcasted_iota(I32, t.shape, 0) // GDN_CHUNK
    cb = lax.broadcasted_iota(I32, t.shape, 1) // GDN_CHUNK
    return jnp.where(rb == cb, t, jnp.zeros_like(t))


def _sbs_product(lhs_list, y):
    bd = _block_diag(y.astype(BF16))
    xs = [x.astype(BF16) for x in lhs_list]
    lhs = xs[0] if len(xs) == 1 else jnp.concatenate(xs, axis=0)
    r = jnp.dot(lhs, bd, preferred_element_type=F32)
    c = GDN_CHUNK
    return [r[i * c:(i + 1) * c] for i in range(len(xs))]


def _gdn_body(q_ref, k_ref, v_ref, z_ref, g1_ref, gr_ref, gain_ref,
              o_ref,
              kdbf, rhsbf, qd, gamb, betab, o0s, qts, vs, wps):
    hg = pl.program_id(1)
    shape = (SEQ, LANES)
    lane = lax.broadcasted_iota(I32, shape, 1)

    g1 = g1_ref[...]
    for hd in range(GDN_HB):
        hs = slice(hd * LANES, (hd + 1) * LANES)
        head = hg * GDN_HB + hd

        def col(off):
            c = jnp.sum(jnp.where(lane == off + head, g1, 0.0), axis=-1, keepdims=True)
            return jnp.broadcast_to(c, shape)

        gam = col(0)
        beta = col(GDN_HEADS)
        gamb[hd] = gam
        betab[hd] = beta.astype(BF16)
        gam3 = gam.reshape(N_CHUNKS, GDN_CHUNK, LANES)
        glast = jnp.broadcast_to(gam3[:, GDN_CHUNK - 1:GDN_CHUNK, :], gam3.shape).reshape(shape)
        eg = jnp.exp(gam)
        kdf = jnp.exp(glast - gam)

        qd[hd] = (q_ref[:, hs].astype(F32) * eg).astype(BF16)
        k = k_ref[:, hs].astype(F32)
        kdbf[hd] = (k * kdf).astype(BF16)
        rhsbf[hd, :, LANES:] = (k * (beta * eg)).astype(BF16)
        rhsbf[hd, :, :LANES] = (v_ref[:, hs].astype(F32) * beta).astype(BF16)

    gshape = (GDN_CHUNK, GDN_GROUP_ROWS)
    ii = lax.broadcasted_iota(I32, gshape, 0)
    jj = lax.broadcasted_iota(I32, gshape, 1) % GDN_CHUNK
    eye = jnp.where(ii == jj, 1.0, 0.0).astype(F32)
    pr = lax.broadcasted_iota(I32, (2 * GDN_CHUNK, LANES), 0) // GDN_CHUNK
    tl = lax.broadcasted_iota(I32, (LANES, GDN_GROUP_ROWS), 1) // GDN_CHUNK
    lane_c = lax.broadcasted_iota(I32, (GDN_CHUNK, LANES), 1)
    nt = (((1,), (1,)), ((), ()))

    def sbs_bcast(ref, hd, base):
        parts = [ref[hd, pl.ds(base + m * GDN_CHUNK, GDN_CHUNK), :] for m in range(GDN_GROUP)]
        a = jnp.where(lane_c < GDN_CHUNK, parts[0], parts[1])
        b = jnp.where(lane_c < GDN_CHUNK, parts[2], parts[3])
        return jnp.concatenate([a, b], axis=1)

    def phase1(hd, pair, gg):
        g = pair * GDN_PAIR + gg
        base = pl.multiple_of(g * GDN_GROUP_ROWS, GDN_GROUP_ROWS)
        slot = lax.rem(pair, 2)
        res = []
        for p in range(2):
            rows = pl.ds(base + p * 2 * GDN_CHUNK, 2 * GDN_CHUNK)
            kst = k_ref[rows, hd * LANES:(hd + 1) * LANES]
            qst = q_ref[rows, hd * LANES:(hd + 1) * LANES]
            lhs_k = jnp.concatenate([kst[:GDN_CHUNK], kst[GDN_CHUNK:]], axis=1)
            lhs_q = jnp.concatenate([qst[:GDN_CHUNK], qst[GDN_CHUNK:]], axis=1)
            lhs = jnp.concatenate([lhs_q, lhs_k], axis=0)
            zero = jnp.zeros_like(kst)
            bt = jnp.concatenate([jnp.where(pr == 0, kst, zero), jnp.where(pr == 1, kst, zero)], axis=1)
            res.append(lax.dot_general(lhs, bt, nt, preferred_element_type=F32))
        yield
        qk = jnp.concatenate([res[0][:GDN_CHUNK], res[1][:GDN_CHUNK]], axis=1)
        kk = jnp.concatenate([res[0][GDN_CHUNK:], res[1][GDN_CHUNK:]], axis=1)

        gc = sbs_bcast(gamb, hd, base)
        bc = sbs_bcast(betab, hd, base).astype(F32)
        grow = gr_ref[0, hd, pl.ds(g, 1), :]
        decay = jnp.exp(jnp.where(ii >= jj, gc - grow, -jnp.inf))
        a = jnp.where(ii > jj, kk * decay * bc, 0.0)
        qkd = qk * decay

        u = eye - a
        (x,) = _sbs_product([a], a)
        yield
        for lvl in range(1, 6):
            if lvl < 5:
                x2, ux = _sbs_product([x, u], x)
                u = u + ux
                x = x2
            else:
                (ux,) = _sbs_product([u], x)
                u = u + ux
            yield

        rows4 = pl.ds(base, GDN_GROUP_ROWS)
        so = jnp.dot(_block_diag(u.astype(BF16)), rhsbf[hd, rows4, :], preferred_element_type=F32)
        yield
        sol = so.astype(BF16)

        kdt = kdbf[hd, rows4, :].astype(F32).T.astype(BF16)
        zt = jnp.zeros_like(kdt)
        lhs2 = jnp.concatenate(
            [_block_diag(qkd.astype(BF16))] + [jnp.where(tl == m, kdt, zt) for m in range(GDN_GROUP)],
            axis=0)
        r = jnp.dot(lhs2, sol, preferred_element_type=F32)
        srows = pl.ds(pl.multiple_of(slot * PAIR_ROWS + gg * GDN_GROUP_ROWS, GDN_GROUP_ROWS), GDN_GROUP_ROWS)
        o0s[hd, srows, :] = r[:GDN_GROUP_ROWS, :LANES]
        qts[hd, srows, :] = qd[hd, rows4, :].astype(F32) - r[:GDN_GROUP_ROWS, LANES:]
        for m in range(GDN_GROUP):
            blk = r[GDN_GROUP_ROWS + m * LANES:GDN_GROUP_ROWS + (m + 1) * LANES]
            dst = pl.ds(pl.multiple_of((slot * PAIR_CHUNKS + gg * GDN_GROUP + m) * LANES, LANES), LANES)
            vs[hd, dst, :] = blk[:, :LANES]
            wps[hd, dst, :] = blk[:, LANES:]

    gain = gain_ref[...]

    def phase2(hd, pair, j, s):
        c = pair * PAIR_CHUNKS + j
        local = lax.rem(pair, 2) * PAIR_CHUNKS + j
        rows = pl.ds(pl.multiple_of(c * GDN_CHUNK, GDN_CHUNK), GDN_CHUNK)
        lrows = pl.ds(pl.multiple_of(local * GDN_CHUNK, GDN_CHUNK), GDN_CHUNK)
        srow = pl.ds(pl.multiple_of(local * LANES, LANES), LANES)
        lhs = jnp.concatenate([wps[hd, srow, :].astype(BF16), qts[hd, lrows, :].astype(BF16)], axis=0)
        r = jnp.dot(lhs, s.astype(BF16), preferred_element_type=F32)
        o = r[LANES:] + o0s[hd, lrows, :]
        dec = jnp.exp(gamb[hd, pl.ds(c * GDN_CHUNK + GDN_CHUNK - 1, 1), :])
        s_new = s * dec + vs[hd, srow, :] - r[:LANES]
        y = o * lax.rsqrt(jnp.mean(o * o, -1, keepdims=True) + EPS) * gain
        zz = z_ref[rows, hd * LANES:(hd + 1) * LANES].astype(F32)
        o_ref[rows, hd * LANES:(hd + 1) * LANES] = (y * _silu(zz)).astype(BF16)
        return s_new

    def phase2_chain(hd, pair, states):
        s = states[hd]
        for j in range(PAIR_CHUNKS):
            s = phase2(hd, pair, j, s)
            yield
        states[hd] = s

    def run_interleaved(chains):
        live = list(chains)
        while live:
            nxt = []
            for c in live:
                try:
                    next(c)
                    nxt.append(c)
                except StopIteration:
                    pass
            live = nxt

    def phase1_chains(pair):
        return [phase1(hd, pair, gg) for gg in range(GDN_PAIR) for hd in range(GDN_HB)]

    run_interleaved(phase1_chains(0))

    def body(pair, states):
        states = list(states)
        run_interleaved([phase2_chain(hd, pair - 1, states) for hd in range(GDN_HB)] + phase1_chains(pair))
        return tuple(states)

    s0 = tuple(jnp.zeros((GDN_DK, GDN_DV), F32) for _ in range(GDN_HB))
    states = list(lax.fori_loop(1, N_PAIRS, body, s0))
    run_interleaved([phase2_chain(hd, N_PAIRS - 1, states) for hd in range(GDN_HB)])


def _gdn(proj, g1, gr, gdn_norm):
    w = GDN_HB * LANES
    per_row = D_MODEL // w
    hblk = lambda off: pl.BlockSpec((SEQ, w), lambda b, h, off=off: (b, off * per_row + h))
    big = lambda dt: pltpu.VMEM((GDN_HB, SEQ, LANES), dt)
    return pl.pallas_call(
        _gdn_body,
        grid=(BATCH, GDN_HEADS // GDN_HB),
        in_specs=[
            hblk(0), hblk(1), hblk(2), hblk(3),
            pl.BlockSpec((SEQ, LANES), lambda b, h: (b, 0)),
            pl.BlockSpec((1, GDN_HB, N_GROUPS_SEQ, GDN_GROUP_ROWS), lambda b, h: (b, h, 0, 0)),
            pl.BlockSpec((1, LANES), lambda b, h: (0, 0)),
        ],
        out_specs=pl.BlockSpec((SEQ, w), lambda b, h: (b, h)),
        out_shape=jax.ShapeDtypeStruct((TOKENS, D_MODEL), BF16),
        scratch_shapes=[
            big(BF16), pltpu.VMEM((GDN_HB, SEQ, 2 * LANES), BF16),
            big(BF16), big(F32), big(BF16),
            pltpu.VMEM((GDN_HB, 2 * PAIR_ROWS, LANES), F32),
            pltpu.VMEM((GDN_HB, 2 * PAIR_ROWS, LANES), F32),
            pltpu.VMEM((GDN_HB, 2 * PAIR_CHUNKS * LANES, LANES), F32),
            pltpu.VMEM((GDN_HB, 2 * PAIR_CHUNKS * LANES, LANES), F32),
        ],
        compiler_params=_cparams(("parallel", "parallel")),
        name="gdn",
    )(proj, proj, proj, proj, g1, gr, gdn_norm)


SWA_PIPE = 3
assert WINDOW == SWA_BLOCK


def _swa_body(sink_ref, q_ref, kc_ref, kp_ref, vc_ref, vp_ref, o_ref):
    i = pl.program_id(1)
    kcat = jnp.concatenate([kp_ref[...], kc_ref[...]], axis=0)
    vcat = jnp.concatenate([vp_ref[...], vc_ref[...]], axis=0)
    qi = lax.broadcasted_iota(I32, (SWA_BLOCK, SWA_BLOCK), 0)
    ki = lax.broadcasted_iota(I32, (SWA_BLOCK, SWA_BLOCK), 1)
    take_cur = ki <= qi
    valid = take_cur | (i > 0)
    lane_kv = lax.broadcasted_iota(I32, (2 * SWA_BLOCK, LANES), 1)
    lane_q = lax.broadcasted_iota(I32, (SWA_BLOCK, LANES), 1)
    nt = (((1,), (1,)), ((), ()))
    scale = SWA_DH ** -0.5

    def head_pair(h0, ks, vs):
        qs = q_ref[:, h0 * SWA_DH:(h0 + 2) * SWA_DH]
        scores = [lax.dot_general(qs, kk, nt, preferred_element_type=F32) for kk in ks]
        for _ in range(SWA_PIPE):
            yield
        acc = None
        inv = None
        for half in range(2):
            sink = sink_ref[h0 + half]
            sc = scores[half]
            s = jnp.where(take_cur, sc[:, SWA_BLOCK:], sc[:, :SWA_BLOCK]) * scale
            s = jnp.where(valid, s, -jnp.inf)
            m = jnp.maximum(jnp.max(s, axis=-1, keepdims=True), sink)
            p = jnp.exp(s - m)
            den = jnp.sum(p, axis=-1, keepdims=True) + jnp.exp(sink - m)
            p2 = jnp.concatenate([jnp.where(take_cur, 0.0, p), jnp.where(take_cur, p, 0.0)], axis=1)
            pv = jnp.dot(p2.astype(BF16), vs[half], preferred_element_type=F32)
            acc = pv if acc is None else acc + pv
            r = 1.0 / den
            inv = r if inv is None else jnp.where(lane_q < SWA_DH, inv, r)
        yield
        o_ref[:, h0 * SWA_DH:(h0 + 2) * SWA_DH] = (acc * inv).astype(BF16)

    chains = []
    for slab in range(SWA_KV_HEADS // 2):
        k2 = kcat[:, slab * LANES:(slab + 1) * LANES].astype(F32)
        v2 = vcat[:, slab * LANES:(slab + 1) * LANES].astype(F32)
        k2r = pltpu.roll(k2, SWA_DH, 1)
        v2r = pltpu.roll(v2, SWA_DH, 1)
        for sub in range(2):
            kvh = slab * 2 + sub
            if sub == 0:
                k_lo = jnp.where(lane_kv < SWA_DH, k2, 0.0)
                k_hi = jnp.where(lane_kv >= SWA_DH, k2r, 0.0)
                v_lo = jnp.where(lane_kv < SWA_DH, v2, 0.0)
                v_hi = jnp.where(lane_kv >= SWA_DH, v2r, 0.0)
            else:
                k_lo = jnp.where(lane_kv < SWA_DH, k2r, 0.0)
                k_hi = jnp.where(lane_kv >= SWA_DH, k2, 0.0)
                v_lo = jnp.where(lane_kv < SWA_DH, v2r, 0.0)
                v_hi = jnp.where(lane_kv >= SWA_DH, v2, 0.0)
            k_lo, k_hi, v_lo, v_hi = (t.astype(BF16) for t in (k_lo, k_hi, v_lo, v_hi))
            for gp in range(SWA_GROUP // 2):
                h0 = kvh * SWA_GROUP + 2 * gp
                chains.append(head_pair(h0, (k_lo, k_hi), (v_lo, v_hi)))

    live = []
    pending = list(chains)
    while pending or live:
        if pending:
            live.append(pending.pop(0))
        nxt = []
        for c in live:
            try:
                next(c)
                nxt.append(c)
            except StopIteration:
                pass
        live = nxt


def _swa(proj, sinks):
    nb = SEQ // SWA_BLOCK
    qcol = 4
    kcol = (7 * D_MODEL) // 256
    vcol = kcol + 1
    cur = lambda col: pl.BlockSpec((SWA_BLOCK, 256), lambda b, i, col=col: (b * nb + i, col))
    prev = lambda col: pl.BlockSpec(
        (SWA_BLOCK, 256), lambda b, i, col=col: (b * nb + jnp.maximum(i - 1, 0), col))
    return pl.pallas_call(
        _swa_body,
        grid=(BATCH, nb),
        in_specs=[
            pl.BlockSpec(memory_space=pltpu.SMEM),
            pl.BlockSpec((SWA_BLOCK, D_MODEL), lambda b, i: (b * nb + i, qcol)),
            cur(kcol), prev(kcol), cur(vcol), prev(vcol),
        ],
        out_specs=pl.BlockSpec((SWA_BLOCK, D_MODEL), lambda b, i: (b * nb + i, 0)),
        out_shape=jax.ShapeDtypeStruct((TOKENS, D_MODEL), BF16),
        compiler_params=_cparams(("parallel", "parallel")),
        name="swa",
    )(sinks, proj, proj, proj, proj, proj)


OUT_TM = 512
OUT_SUB = 256


def _outproj_body(x_ref, yg_ref, ys_ref, gg_ref, gs_ref, w_ref, nf_ref, wr_ref, rb_ref,
                  x1_ref, hp_ref, lg_ref):
    def sub_tile(t):
        rows = pl.ds(t * OUT_SUB, OUT_SUB)
        merged = (_sigmoid(gg_ref[rows, :].astype(F32)) * yg_ref[rows, :].astype(F32)
                  + _sigmoid(gs_ref[rows, :].astype(F32)) * ys_ref[rows, :].astype(F32))
        acc = jnp.dot(merged.astype(BF16), w_ref[...], preferred_element_type=F32)
        yield
        x1 = x_ref[rows, :] + acc
        x1_ref[rows, :] = x1
        ms = jnp.mean(x1 * x1, axis=-1, keepdims=True)
        h = x1 * lax.rsqrt(ms + EPS) * nf_ref[...]
        hh = h.astype(BF16)
        r = jnp.dot(hh, wr_ref[...], preferred_element_type=F32)
        lg_ref[rows, :] = r[:, :LANES] + r[:, LANES:] + rb_ref[...]
        half = D_MODEL // 2
        hf = hh.astype(F32)
        wa = pltpu.bitcast(hf[:, :half], U32)
        wb = pltpu.bitcast(hf[:, half:], U32)
        hp_ref[rows, :] = (wa & jnp.uint32(0xFFFF0000)) | (wb >> 16)

    tiles = [sub_tile(t) for t in range(OUT_TM // OUT_SUB)]
    for _ in range(2):
        for g in tiles:
            next(g, None)


def _outproj(x2d, y_gdn, y_swa, proj, w_out, norm_ffn, wr2, r_bias):
    row = lambda w: pl.BlockSpec((OUT_TM, w), lambda i: (i, 0))
    const = lambda s: pl.BlockSpec(s, lambda i: (0, 0))
    return pl.pallas_call(
        _outproj_body,
        grid=(TOKENS // OUT_TM,),
        in_specs=[
            row(D_MODEL), row(D_MODEL), row(D_MODEL),
            pl.BlockSpec((OUT_TM, D_MODEL), lambda i: (i, 5)),
            pl.BlockSpec((OUT_TM, D_MODEL), lambda i: (i, 6)),
            pl.BlockSpec((D_MODEL, D_MODEL), lambda i: (0, 0), pipeline_mode=pl.Buffered(1)),
            const((1, D_MODEL)),
            const((D_MODEL, 2 * LANES)), const((1, LANES)),
        ],
        out_specs=[row(D_MODEL), row(D_MODEL // 2), row(LANES)],
        out_shape=[
            jax.ShapeDtypeStruct((TOKENS, D_MODEL), F32),
            jax.ShapeDtypeStruct((TOKENS, D_MODEL // 2), U32),
            jax.ShapeDtypeStruct((TOKENS, LANES), F32),
        ],
        compiler_params=_cparams(("parallel",)),
        name="outproj",
    )(x2d, y_gdn, y_swa, proj, proj, w_out, norm_ffn, wr2, r_bias)


ROUTE_TM = 1024


def _route_body(lg_ref, tri_ref, idx_ref, wt_ref, cnt_ref, run_ref):
    @pl.when(pl.program_id(0) == 0)
    def _():
        run_ref[...] = jnp.zeros_like(run_ref)

    lg = lg_ref[...]
    lane = lax.broadcasted_iota(I32, lg.shape, 1)
    ninf = -jnp.inf
    big = jnp.int32(LANES)
    is_g = lane < N_GROUPS
    glog = jnp.where(is_g, lg, ninf)
    gmax = jnp.max(glog, axis=-1, keepdims=True)
    gden = jnp.sum(jnp.where(is_g, jnp.exp(lg - gmax), 0.0), axis=-1, keepdims=True)
    p_sel = 1.0 / gden
    grp = jnp.min(jnp.where(glog == gmax, lane, big), axis=-1, keepdims=True)
    emask = (lane >= N_GROUPS) & (lane < N_GROUPS + N_EXPERTS) & (((lane - N_GROUPS) >> 3) == grp)
    el = jnp.where(emask, lg, ninf)
    v1 = jnp.max(el, axis=-1, keepdims=True)
    i1 = jnp.min(jnp.where(el == v1, lane, big), axis=-1, keepdims=True)
    el2 = jnp.where(lane == i1, ninf, el)
    v2 = jnp.max(el2, axis=-1, keepdims=True)
    i2 = jnp.min(jnp.where(el2 == v2, lane, big), axis=-1, keepdims=True)
    e = jnp.exp(v2 - v1)
    w1 = p_sel / (1.0 + e)
    w2 = p_sel * e / (1.0 + e)
    e0 = i1 - N_GROUPS
    e1 = i2 - N_GROUPS

    oh0 = lane == e0
    oh1 = lane == e1
    onehot = jnp.where(oh0 | oh1, 1.0, 0.0)
    prefix = jnp.dot(tri_ref[...], onehot.astype(BF16), preferred_element_type=F32) + run_ref[0:1, :]
    r0 = jnp.sum(jnp.where(oh0, prefix, 0.0), axis=-1, keepdims=True).astype(I32)
    r1 = jnp.sum(jnp.where(oh1, prefix, 0.0), axis=-1, keepdims=True).astype(I32)
    run = run_ref[0:1, :] + jnp.sum(onehot, axis=0, keepdims=True)
    run_ref[...] = jnp.broadcast_to(run, run_ref.shape)
    cnt_ref[...] = jnp.broadcast_to(run, cnt_ref.shape).astype(I32)

    zi = jnp.zeros(lg.shape, I32)
    idx = jnp.where(lane == 0, e0, zi)
    idx = jnp.where(lane == 1, e1, idx)
    idx = jnp.where(lane == 2, r0, idx)
    idx = jnp.where(lane == 3, r1, idx)
    idx_ref[...] = idx
    wt_ref[...] = jnp.where(lane == 0, w1, jnp.where(lane == 1, w2, 0.0))


def _route(logits, tri):
    row = pl.BlockSpec((ROUTE_TM, LANES), lambda i: (i, 0))
    return pl.pallas_call(
        _route_body,
        grid=(TOKENS // ROUTE_TM,),
        in_specs=[row, pl.BlockSpec((ROUTE_TM, ROUTE_TM), lambda i: (0, 0))],
        out_specs=[row, row, pl.BlockSpec((8, LANES), lambda i: (0, 0))],
        out_shape=[
            jax.ShapeDtypeStruct((TOKENS, LANES), I32),
            jax.ShapeDtypeStruct((TOKENS, LANES), F32),
            jax.ShapeDtypeStruct((8, LANES), I32),
        ],
        scratch_shapes=[pltpu.VMEM((8, LANES), F32)],
        compiler_params=_cparams(("arbitrary",)),
        name="route",
    )(logits, tri)


DISP_TM = 256


def _dispatch_body(meta_ref, dest_ref, h_ref, xs_ref, zbuf, sem, zsem):
    @pl.when(pl.program_id(0) == 0)
    def _():
        zbuf[...] = jnp.zeros_like(zbuf)

        def zero_block(row0):
            return pltpu.make_async_copy(zbuf, xs_ref.at[pl.ds(row0, MOE_BLOCK), :], zsem)

        def per_expert(e, n):
            has = meta_ref[N_EXPERTS + e] > 0

            @pl.when(has)
            def _():
                zero_block(pl.multiple_of(meta_ref[e] - MOE_BLOCK, MOE_BLOCK)).start()

            return n + has.astype(I32)

        n_last = lax.fori_loop(0, N_EXPERTS, per_expert, jnp.int32(0))
        used = meta_ref[2 * N_EXPERTS]

        def tail(j, c):
            zero_block(pl.multiple_of(j * MOE_BLOCK, MOE_BLOCK)).start()
            return c

        lax.fori_loop(used, N_BLOCKS, tail, 0)

        def drain(i, c):
            zero_block(0).wait()
            return c

        lax.fori_loop(0, n_last + (N_BLOCKS - used), drain, 0)

    def row_copy(r, d):
        return pltpu.make_async_copy(h_ref.at[pl.ds(r, 1), :], xs_ref.at[pl.ds(d, 1), :], sem)

    def issue(r, c):
        for k in range(TOP_K):
            row_copy(r, dest_ref[TOP_K * r + k]).start()
        return c

    lax.fori_loop(0, DISP_TM, issue, 0, unroll=4)
    for k in range(TOP_K):
        pltpu.make_async_copy(h_ref, xs_ref.at[pl.ds(0, DISP_TM), :], sem).wait()


def _dispatch(meta, dest_flat, hp):
    grid_spec = pltpu.PrefetchScalarGridSpec(
        num_scalar_prefetch=1,
        grid=(TOKENS // DISP_TM,),
        in_specs=[
            pl.BlockSpec((DISP_TM * TOP_K,), lambda i, m: (i,), memory_space=pltpu.SMEM),
            pl.BlockSpec((DISP_TM, D_MODEL // 2), lambda i, m: (i, 0)),
        ],
        out_specs=pl.BlockSpec(memory_space=pl.ANY),
        scratch_shapes=[
            pltpu.VMEM((MOE_BLOCK, D_MODEL // 2), U32),
            pltpu.SemaphoreType.DMA,
            pltpu.SemaphoreType.DMA,
        ],
    )
    return pl.pallas_call(
        _dispatch_body,
        grid_spec=grid_spec,
        out_shape=jax.ShapeDtypeStruct((N_SLOTS, D_MODEL // 2), U32),
        compiler_params=_cparams(("arbitrary",)),
        name="dispatch",
    )(meta, dest_flat, hp)


def _slots_body(idx_ref, start_ref, dest_ref):
    idx = idx_ref[...]
    lane = lax.broadcasted_iota(I32, idx.shape, 1)
    start = start_ref[...]
    out = jnp.zeros(idx.shape, I32)
    for k in range(TOP_K):
        e = idx[:, k:k + 1]
        base = jnp.sum(jnp.where(lane == e, start, 0.0), axis=-1, keepdims=True).astype(I32)
        out = jnp.where(lane == k, base + idx[:, TOP_K + k:TOP_K + k + 1], out)
    dest_ref[...] = out


SLOTS_TM = 2048


def _slots(idx, pad_start_row):
    row = pl.BlockSpec((SLOTS_TM, LANES), lambda i: (i, 0))
    return pl.pallas_call(
        _slots_body,
        grid=(TOKENS // SLOTS_TM,),
        in_specs=[row, pl.BlockSpec((1, LANES), lambda i: (0, 0))],
        out_specs=row,
        out_shape=jax.ShapeDtypeStruct((TOKENS, LANES), I32),
        compiler_params=_cparams(("parallel",)),
        name="slots",
    )(idx, pad_start_row)


def _experts_body(sched_ref, x_ref, wg_hbm, wu_hbm, wd_hbm, y_ref,
                  wg_st, wu_st, wd_st, wg_bf, wu_bf, wd_bf, sem):
    j = pl.program_id(0)
    e = sched_ref[j]
    slot = sched_ref[N_BLOCKS + j]
    nxt = sched_ref[2 * N_BLOCKS + j]
    used = sched_ref[3 * N_BLOCKS]
    new_expert = (j == 0) | (e != sched_ref[jnp.maximum(j - 1, 0)])

    def weight_copies(expert, s):
        return (pltpu.make_async_copy(wg_hbm.at[expert], wg_st.at[s], sem.at[s, 0]),
                pltpu.make_async_copy(wu_hbm.at[expert], wu_st.at[s], sem.at[s, 1]),
                pltpu.make_async_copy(wd_hbm.at[expert], wd_st.at[s], sem.at[s, 2]))

    @pl.when(j == 0)
    def _():
        for c in weight_copies(e, slot):
            c.start(priority=1)

    @pl.when(new_expert)
    def _():
        for c in weight_copies(e, slot):
            c.wait()

        @pl.when(nxt >= 0)
        def _():
            for c in weight_copies(nxt, 1 - slot):
                c.start(priority=1)

        wg_bf[...] = wg_st[slot].astype(BF16)
        wu_bf[...] = wu_st[slot].astype(BF16)
        wd_bf[...] = wd_st[slot].astype(BF16)

    @pl.when(j < used)
    def _():
        w = x_ref[...]
        xa = pltpu.bitcast(w & jnp.uint32(0xFFFF0000), F32).astype(BF16)
        xb = pltpu.bitcast(w << 16, F32).astype(BF16)
        x = jnp.concatenate([xa, xb], axis=1)
        g = jnp.dot(x, wg_bf[...], preferred_element_type=F32)
        u = jnp.dot(x, wu_bf[...], preferred_element_type=F32)
        hb = (_silu(g) * u).astype(BF16)
        y_ref[...] = jnp.dot(hb, wd_bf[...], preferred_element_type=F32)

    @pl.when(j >= used)
    def _():
        y_ref[...] = jnp.zeros_like(y_ref)


def _experts(sched, xs, w_gate, w_up, w_down):
    grid_spec = pltpu.PrefetchScalarGridSpec(
        num_scalar_prefetch=1,
        grid=(N_BLOCKS,),
        in_specs=[
            pl.BlockSpec((MOE_BLOCK, D_MODEL // 2), lambda j, s: (j, 0)),
            pl.BlockSpec(memory_space=pl.ANY),
            pl.BlockSpec(memory_space=pl.ANY),
            pl.BlockSpec(memory_space=pl.ANY),
        ],
        out_specs=pl.BlockSpec((MOE_BLOCK, D_MODEL), lambda j, s: (j, 0)),
        scratch_shapes=[
            pltpu.VMEM((2, D_MODEL, D_EXPERT), F32),
            pltpu.VMEM((2, D_MODEL, D_EXPERT), F32),
            pltpu.VMEM((2, D_EXPERT, D_MODEL), F32),
            pltpu.VMEM((D_MODEL, D_EXPERT), BF16),
            pltpu.VMEM((D_MODEL, D_EXPERT), BF16),
            pltpu.VMEM((D_EXPERT, D_MODEL), BF16),
            pltpu.SemaphoreType.DMA((2, 3)),
        ],
    )
    return pl.pallas_call(
        _experts_body,
        grid_spec=grid_spec,
        out_shape=jax.ShapeDtypeStruct((N_SLOTS, D_MODEL), F32),
        compiler_params=_cparams(("arbitrary",)),
        name="experts",
    )(sched, xs, w_gate, w_up, w_down)


COMB_TM = 256


def _combine_body(dest_ref, wt_ref, x1_ref, gain_ref, ys_ref, o_ref, buf, sem):
    def row_copy(r, k, d):
        return pltpu.make_async_copy(ys_ref.at[pl.ds(d, 1), :], buf.at[k, pl.ds(r, 1), :], sem)

    def issue(r, c):
        for k in range(TOP_K):
            row_copy(r, k, dest_ref[TOP_K * r + k]).start()
        return c

    lax.fori_loop(0, COMB_TM, issue, 0, unroll=4)
    for k in range(TOP_K):
        pltpu.make_async_copy(ys_ref.at[pl.ds(0, COMB_TM), :], buf.at[k], sem).wait()

    wt = wt_ref[...]
    y = x1_ref[...] + wt[:, 0:1] * buf[0] + wt[:, 1:2] * buf[1]
    ms = jnp.mean(y * y, axis=-1, keepdims=True)
    o_ref[...] = y * lax.rsqrt(ms + EPS) * gain_ref[...]


def _combine(dest_flat, wts, x1, norm_final, ys):
    return pl.pallas_call(
        _combine_body,
        grid=(TOKENS // COMB_TM,),
        in_specs=[
            pl.BlockSpec((COMB_TM * TOP_K,), lambda i: (i,), memory_space=pltpu.SMEM),
            pl.BlockSpec((COMB_TM, LANES), lambda i: (i, 0)),
            pl.BlockSpec((COMB_TM, D_MODEL), lambda i: (i, 0)),
            pl.BlockSpec((1, D_MODEL), lambda i: (0, 0)),
            pl.BlockSpec(memory_space=pl.ANY),
        ],
        out_specs=pl.BlockSpec((COMB_TM, D_MODEL), lambda i: (i, 0)),
        out_shape=jax.ShapeDtypeStruct((TOKENS, D_MODEL), F32),
        scratch_shapes=[pltpu.VMEM((TOP_K, COMB_TM, D_MODEL), F32), pltpu.SemaphoreType.DMA],
        compiler_params=_cparams(("arbitrary",)),
        name="combine",
    )(dest_flat, wts, x1, norm_final, ys)


def _pad_lanes(v):
    v = v.reshape(1, -1).astype(F32)
    return jnp.pad(v, ((0, 0), (0, LANES - v.shape[1])))


def kernel(x, norm_mix, w_in, conv_w, gdn_a_log, gdn_dt_bias, gdn_norm, swa_sinks, w_out, norm_ffn,
           w_router_group, b_router_group, w_router_expert, b_router_expert, w_gate, w_up, w_down,
           norm_final):
    l = 0
    x2d = x.reshape(TOKENS, D_MODEL)
    w = w_in[l]
    o_z = 3 * D_MODEL
    o_a = o_z + D_MODEL
    o_sq = o_a + 2 * GDN_HEADS
    o_sk = o_sq + D_MODEL
    o_sv = o_sk + SWA_KV_HEADS * SWA_DH
    o_gg = o_sv + SWA_KV_HEADS * SWA_DH
    o_gs = o_gg + D_MODEL
    w_main = jnp.concatenate(
        [w[:, :o_a], w[:, o_sq:o_sk], w[:, o_gg:o_gs], w[:, o_gs:], w[:, o_sk:o_sv], w[:, o_sv:o_gg]],
        axis=1).astype(BF16)
    w_ab = jnp.pad(w[:, o_a:o_sq], ((0, 0), (0, LANES - 2 * GDN_HEADS))).astype(BF16)

    proj, ab = _inproj(x2d, norm_mix[l].reshape(1, D_MODEL), w_main, w_ab, conv_w[l])

    g1, gt = _gdn_prep(ab, _pad_lanes(gdn_a_log[l]), _pad_lanes(gdn_dt_bias[l]))
    gr = gt[:, :GDN_HEADS, :].reshape(BATCH, GDN_HEADS, N_GROUPS_SEQ, GDN_GROUP_ROWS)
    y_gdn = _gdn(proj, g1, gr, gdn_norm[l].reshape(1, GDN_DV))

    y_swa = _swa(proj, swa_sinks[l].astype(F32))

    w_r = jnp.concatenate([w_router_group[l], w_router_expert[l]], axis=1).astype(F32)
    w_r = jnp.pad(w_r, ((0, 0), (0, LANES - w_r.shape[1])))
    wr_hi = w_r.astype(BF16)
    wr_lo = (w_r - wr_hi.astype(F32)).astype(BF16)
    r_bias = _pad_lanes(jnp.concatenate([b_router_group[l], b_router_expert[l]]))
    x1, hp, logits = _outproj(x2d, y_gdn, y_swa, proj, w_out[l].astype(BF16),
                              norm_ffn[l].reshape(1, D_MODEL), jnp.concatenate([wr_hi, wr_lo], axis=1), r_bias)

    tri = (lax.broadcasted_iota(I32, (ROUTE_TM, ROUTE_TM), 1)
           < lax.broadcasted_iota(I32, (ROUTE_TM, ROUTE_TM), 0)).astype(BF16)
    idx, wts, cnt = _route(logits, tri)

    counts = cnt[0, :N_EXPERTS]
    padded = (counts + MOE_BLOCK - 1) // MOE_BLOCK * MOE_BLOCK
    pad_end = jnp.cumsum(padded)
    pad_start = pad_end - padded
    dest = _slots(idx, _pad_lanes(pad_start))
    dest_flat = dest[:, :TOP_K].reshape(N_ASSIGN)
    blk_pos = jnp.arange(N_BLOCKS, dtype=I32) * MOE_BLOCK
    blk_e = jnp.minimum(jnp.sum((pad_end[None, :] <= blk_pos[:, None]).astype(I32), axis=1), N_EXPERTS - 1)
    used = pad_end[-1:] // MOE_BLOCK
    meta = jnp.concatenate([pad_end, padded, used]).astype(I32)
    is_new = jnp.concatenate([jnp.ones((1,), I32), (blk_e[1:] != blk_e[:-1]).astype(I32)])
    ordinal = jnp.cumsum(is_new) - 1
    nxt_pos = jnp.sum((ordinal[None, :] <= ordinal[:, None]).astype(I32), axis=1)
    nxt = jnp.where(nxt_pos < N_BLOCKS, blk_e[jnp.minimum(nxt_pos, N_BLOCKS - 1)], -1)
    sched = jnp.concatenate([blk_e, ordinal & 1, nxt, used]).astype(I32)

    xs = _dispatch(meta, dest_flat, hp)
    ys = _experts(sched, xs, w_gate[l], w_up[l], w_down[l])
    out = _combine(dest_flat, wts, x1, norm_final.reshape(1, D_MODEL), ys)
    return out.reshape(BATCH, SEQ, D_MODEL)
```

```python
import functools

import jax
import jax.numpy as jnp
from jax import lax
from jax.experimental import pallas as pl
from jax.experimental.pallas import tpu as pltpu

F32 = jnp.float32
BF16 = jnp.bfloat16
I32 = jnp.int32
U32 = jnp.uint32

D_MODEL = 2048
BATCH = 16
SEQ = 2048
TOKENS = BATCH * SEQ
EPS = 1e-6

GDN_HEADS = 16
GDN_DK = 128
GDN_DV = 128
GDN_CHUNK = 64
CONV_W = 4
GDN_GROUP = 4
GDN_GROUP_ROWS = GDN_GROUP * GDN_CHUNK
N_CHUNKS = SEQ // GDN_CHUNK
N_GROUPS_SEQ = SEQ // GDN_GROUP_ROWS

SWA_HEADS = 32
SWA_KV_HEADS = 4
SWA_DH = 64
SWA_GROUP = 8
SWA_BLOCK = 128
WINDOW = 128

N_GROUPS = 8
EXPERTS_PER_GROUP = 8
N_EXPERTS = 64
TOP_K = 2
D_EXPERT = 512
MOE_BLOCK = 128
N_ASSIGN = TOKENS * TOP_K
N_BLOCKS = N_ASSIGN // MOE_BLOCK + N_EXPERTS
N_SLOTS = N_BLOCKS * MOE_BLOCK

PROJ_DIM = 7 * D_MODEL + 2 * SWA_KV_HEADS * SWA_DH
LANES = 128

VMEM_LIMIT = 56 * 1024 * 1024


def _cparams(sem, vmem=VMEM_LIMIT):
    return pltpu.CompilerParams(dimension_semantics=sem, vmem_limit_bytes=vmem)


def _sigmoid(x):
    return 0.5 * jnp.tanh(0.5 * x) + 0.5


def _silu(x):
    h = 0.5 * x
    return h * jnp.tanh(h) + h


INPROJ_TM = 2048
INPROJ_TN = 512
INPROJ_VMEM = 60 * 1024 * 1024


INPROJ_RC = 256
INPROJ_RING = 3
QKV_TILES = 3 * D_MODEL // INPROJ_TN
QK_TILES = 2 * D_MODEL // INPROJ_TN
Q_TILES = D_MODEL // INPROJ_TN
assert INPROJ_TM == SEQ


def _inproj_body(x_ref, g_ref, w_ref, wab_ref, cw_ref, o_ref, ab_ref, h_ref, cpad):
    j = pl.program_id(1)

    @pl.when(j == 0)
    def _():
        def chunk(i, c):
            r = pl.ds(pl.multiple_of(i * 128, 128), 128)
            x = x_ref[r, :]
            ms = jnp.mean(x * x, axis=-1, keepdims=True)
            h_ref[r, :] = (x * lax.rsqrt(ms + EPS) * g_ref[...]).astype(BF16)
            return c
        lax.fori_loop(0, INPROJ_TM // 128, chunk, 0)
        ab_ref[...] = jnp.dot(h_ref[...], wab_ref[...], preferred_element_type=F32)

    @pl.when(j >= QKV_TILES)
    def _():
        o_ref[...] = jnp.dot(h_ref[...], w_ref[...], preferred_element_type=F32).astype(BF16)

    @pl.when(j < QKV_TILES)
    def _():
        cw = cw_ref[...]
        is_qk = j < QK_TILES
        qscale = jnp.where(j < Q_TILES, GDN_DK ** -0.5, 1.0).astype(F32)
        n_chunks = INPROJ_TM // INPROJ_RC

        def matmul(c):
            rows = pl.ds(c * INPROJ_RC, INPROJ_RC)
            cpad[c % INPROJ_RING, pl.ds(8, INPROJ_RC), :] = jnp.dot(
                h_ref[rows, :], w_ref[...], preferred_element_type=F32)

        def epilogue(c):
            buf = cpad.at[c % INPROJ_RING]
            if c == 0:
                buf[pl.ds(0, 8), :] = jnp.zeros((8, INPROJ_TN), F32)
            else:
                buf[pl.ds(0, 8), :] = cpad[(c - 1) % INPROJ_RING, pl.ds(INPROJ_RC, 8), :]
            y = None
            for s in range(CONV_W):
                t = buf[pl.ds(8 - s, INPROJ_RC), :] * cw[CONV_W - 1 - s:CONV_W - s, :]
                y = t if y is None else y + t
            y = _silu(y)
            parts = []
            for g in range(INPROJ_TN // LANES):
                yg = y[:, g * LANES:(g + 1) * LANES]
                inv = lax.rsqrt(jnp.sum(yg * yg, -1, keepdims=True) + EPS) * qscale
                parts.append(yg * jnp.where(is_qk, inv, 1.0))
            o_ref[pl.ds(c * INPROJ_RC, INPROJ_RC), :] = jnp.concatenate(parts, axis=1).astype(BF16)

        matmul(0)
        for c in range(n_chunks):
            if c + 1 < n_chunks:
                matmul(c + 1)
            epilogue(c)


def _inproj(x2d, gain, w_main, w_ab, conv_w):
    grid = (TOKENS // INPROJ_TM, PROJ_DIM // INPROJ_TN)
    return pl.pallas_call(
        _inproj_body,
        grid=grid,
        in_specs=[
            pl.BlockSpec((INPROJ_TM, D_MODEL), lambda i, j: (i, 0)),
            pl.BlockSpec((1, D_MODEL), lambda i, j: (0, 0)),
            pl.BlockSpec((D_MODEL, INPROJ_TN), lambda i, j: (0, j)),
            pl.BlockSpec((D_MODEL, LANES), lambda i, j: (0, 0)),
            pl.BlockSpec((CONV_W, INPROJ_TN), lambda i, j: (0, jnp.minimum(j, QKV_TILES - 1))),
        ],
        out_specs=[
            pl.BlockSpec((INPROJ_TM, INPROJ_TN), lambda i, j: (i, j)),
            pl.BlockSpec((INPROJ_TM, LANES), lambda i, j: (i, 0)),
        ],
        out_shape=[
            jax.ShapeDtypeStruct((TOKENS, PROJ_DIM), BF16),
            jax.ShapeDtypeStruct((TOKENS, LANES), F32),
        ],
        scratch_shapes=[pltpu.VMEM((INPROJ_TM, D_MODEL), BF16),
                        pltpu.VMEM((INPROJ_RING, 8 + INPROJ_RC, INPROJ_TN), F32)],
        compiler_params=_cparams(("parallel", "arbitrary"), vmem=INPROJ_VMEM),
        name="inproj",
    )(x2d, gain, w_main, w_ab, conv_w)


def _gdn_prep_body(ab_ref, alog_ref, dtb_ref, g1_ref, gt_ref):
    ab = ab_ref[...]
    lane = lax.broadcasted_iota(I32, ab.shape, 1)
    row = lax.broadcasted_iota(I32, ab.shape, 0) % GDN_CHUNK
    xa = ab + dtb_ref[...]
    softplus = jnp.maximum(xa, 0.0) + jnp.log(1.0 + jnp.exp(-jnp.abs(xa)))
    g = jnp.where(lane < GDN_HEADS, -jnp.exp(alog_ref[...]) * softplus, 0.0)
    gam = g
    s = 1
    while s < GDN_CHUNK:
        gam = gam + jnp.where(row >= s, pltpu.roll(gam, s, 0), 0.0)
        s *= 2
    g1_ref[...] = jnp.where(lane < GDN_HEADS, gam, _sigmoid(ab))
    gt_ref[0] = gam.T


def _gdn_prep(ab, alog_pad, dtb_pad):
    return pl.pallas_call(
        _gdn_prep_body,
        grid=(BATCH,),
        in_specs=[
            pl.BlockSpec((SEQ, LANES), lambda b: (b, 0)),
            pl.BlockSpec((1, LANES), lambda b: (0, 0)),
            pl.BlockSpec((1, LANES), lambda b: (0, 0)),
        ],
        out_specs=[
            pl.BlockSpec((SEQ, LANES), lambda b: (b, 0)),
            pl.BlockSpec((1, LANES, SEQ), lambda b: (b, 0, 0)),
        ],
        out_shape=[
            jax.ShapeDtypeStruct((TOKENS, LANES), F32),
            jax.ShapeDtypeStruct((BATCH, LANES, SEQ), F32),
        ],
        compiler_params=_cparams(("parallel",)),
        name="gdn_prep",
    )(ab, alog_pad, dtb_pad)


GDN_HB = 4
GDN_PAIR = 2
N_PAIRS = N_GROUPS_SEQ // GDN_PAIR
PAIR_CHUNKS = GDN_PAIR * GDN_GROUP
PAIR_ROWS = GDN_PAIR * GDN_GROUP_ROWS


def _block_diag(x):
    t = jnp.concatenate([x] * GDN_GROUP, axis=0)
    rb = lax.broadcasted_iota(I32, t.shape, 0) // GDN_CHUNK
    cb = lax.broadcasted_iota(I32, t.shape, 1) // GDN_CHUNK
    return jnp.where(rb == cb, t, jnp.zeros_like(t))


def _sbs_product(lhs_list, y):
    bd = _block_diag(y.astype(BF16))
    xs = [x.astype(BF16) for x in lhs_list]
    lhs = xs[0] if len(xs) == 1 else jnp.concatenate(xs, axis=0)
    r = jnp.dot(lhs, bd, preferred_element_type=F32)
    c = GDN_CHUNK
    return [r[i * c:(i + 1) * c] for i in range(len(xs))]


def _gdn_body(q_ref, k_ref, v_ref, z_ref, g1_ref, gr_ref, gain_ref,
              o_ref,
              kdbf, rhsbf, qd, gamb, betab, o0s, qts, vs, wps):
    hg = pl.program_id(1)
    shape = (SEQ, LANES)
    lane = lax.broadcasted_iota(I32, shape, 1)

    g1 = g1_ref[...]
    for hd in range(GDN_HB):
        hs = slice(hd * LANES, (hd + 1) * LANES)
        head = hg * GDN_HB + hd

        def col(off):
            c = jnp.sum(jnp.where(lane == off + head, g1, 0.0), axis=-1, keepdims=True)
            return jnp.broadcast_to(c, shape)

        gam = col(0)
        beta = col(GDN_HEADS)
        gamb[hd] = gam
        betab[hd] = beta.astype(BF16)
        gam3 = gam.reshape(N_CHUNKS, GDN_CHUNK, LANES)
        glast = jnp.broadcast_to(gam3[:, GDN_CHUNK - 1:GDN_CHUNK, :], gam3.shape).reshape(shape)
        eg = jnp.exp(gam)
        kdf = jnp.exp(glast - gam)

        qd[hd] = (q_ref[:, hs].astype(F32) * eg).astype(BF16)
        k = k_ref[:, hs].astype(F32)
        kdbf[hd] = (k * kdf).astype(BF16)
        rhsbf[hd, :, LANES:] = (k * (beta * eg)).astype(BF16)
        rhsbf[hd, :, :LANES] = (v_ref[:, hs].astype(F32) * beta).astype(BF16)

    gshape = (GDN_CHUNK, GDN_GROUP_ROWS)
    ii = lax.broadcasted_iota(I32, gshape, 0)
    jj = lax.broadcasted_iota(I32, gshape, 1) % GDN_CHUNK
    eye = jnp.where(ii == jj, 1.0, 0.0).astype(F32)
    pr = lax.broadcasted_iota(I32, (2 * GDN_CHUNK, LANES), 0) // GDN_CHUNK
    tl = lax.broadcasted_iota(I32, (LANES, GDN_GROUP_ROWS), 1) // GDN_CHUNK
    lane_c = lax.broadcasted_iota(I32, (GDN_CHUNK, LANES), 1)
    nt = (((1,), (1,)), ((), ()))

    def sbs_bcast(ref, hd, base):
        parts = [ref[hd, pl.ds(base + m * GDN_CHUNK, GDN_CHUNK), :] for m in range(GDN_GROUP)]
        a = jnp.where(lane_c < GDN_CHUNK, parts[0], parts[1])
        b = jnp.where(lane_c < GDN_CHUNK, parts[2], parts[3])
        return jnp.concatenate([a, b], axis=1)

    def phase1(hd, pair, gg):
        g = pair * GDN_PAIR + gg
        base = pl.multiple_of(g * GDN_GROUP_ROWS, GDN_GROUP_ROWS)
        slot = lax.rem(pair, 2)
        res = []
        for p in range(2):
            rows = pl.ds(base + p * 2 * GDN_CHUNK, 2 * GDN_CHUNK)
            kst = k_ref[rows, hd * LANES:(hd + 1) * LANES]
            qst = q_ref[rows, hd * LANES:(hd + 1) * LANES]
            lhs_k = jnp.concatenate([kst[:GDN_CHUNK], kst[GDN_CHUNK:]], axis=1)
            lhs_q = jnp.concatenate([qst[:GDN_CHUNK], qst[GDN_CHUNK:]], axis=1)
            lhs = jnp.concatenate([lhs_q, lhs_k], axis=0)
            zero = jnp.zeros_like(kst)
            bt = jnp.concatenate([jnp.where(pr == 0, kst, zero), jnp.where(pr == 1, kst, zero)], axis=1)
            res.append(lax.dot_general(lhs, bt, nt, preferred_element_type=F32))
        yield
        qk = jnp.concatenate([res[0][:GDN_CHUNK], res[1][:GDN_CHUNK]], axis=1)
        kk = jnp.concatenate([res[0][GDN_CHUNK:], res[1][GDN_CHUNK:]], axis=1)

        gc = sbs_bcast(gamb, hd, base)
        bc = sbs_bcast(betab, hd, base).astype(F32)
        grow = gr_ref[0, hd, pl.ds(g, 1), :]
        decay = jnp.exp(jnp.where(ii >= jj, gc - grow, -jnp.inf))
        a = jnp.where(ii > jj, kk * decay * bc, 0.0)
        qkd = qk * decay

        u = eye - a
        (x,) = _sbs_product([a], a)
        yield
        for lvl in range(1, 6):
            if lvl < 5:
                x2, ux = _sbs_product([x, u], x)
                u = u + ux
                x = x2
            else:
                (ux,) = _sbs_product([u], x)
                u = u + ux
            yield

        rows4 = pl.ds(base, GDN_GROUP_ROWS)
        so = jnp.dot(_block_diag(u.astype(BF16)), rhsbf[hd, rows4, :], preferred_element_type=F32)
        yield
        sol = so.astype(BF16)

        kdt = kdbf[hd, rows4, :].astype(F32).T.astype(BF16)
        zt = jnp.zeros_like(kdt)
        lhs2 = jnp.concatenate(
            [_block_diag(qkd.astype(BF16))] + [jnp.where(tl == m, kdt, zt) for m in range(GDN_GROUP)],
            axis=0)
        r = jnp.dot(lhs2, sol, preferred_element_type=F32)
        srows = pl.ds(pl.multiple_of(slot * PAIR_ROWS + gg * GDN_GROUP_ROWS, GDN_GROUP_ROWS), GDN_GROUP_ROWS)
        o0s[hd, srows, :] = r[:GDN_GROUP_ROWS, :LANES]
        qts[hd, srows, :] = qd[hd, rows4, :].astype(F32) - r[:GDN_GROUP_ROWS, LANES:]
        for m in range(GDN_GROUP):
            blk = r[GDN_GROUP_ROWS + m * LANES:GDN_GROUP_ROWS + (m + 1) * LANES]
            dst = pl.ds(pl.multiple_of((slot * PAIR_CHUNKS + gg * GDN_GROUP + m) * LANES, LANES), LANES)
            vs[hd, dst, :] = blk[:, :LANES]
            wps[hd, dst, :] = blk[:, LANES:]

    gain = gain_ref[...]

    def phase2(hd, pair, j, s):
        c = pair * PAIR_CHUNKS + j
        local = lax.rem(pair, 2) * PAIR_CHUNKS + j
        rows = pl.ds(pl.multiple_of(c * GDN_CHUNK, GDN_CHUNK), GDN_CHUNK)
        lrows = pl.ds(pl.multiple_of(local * GDN_CHUNK, GDN_CHUNK), GDN_CHUNK)
        srow = pl.ds(pl.multiple_of(local * LANES, LANES), LANES)
        lhs = jnp.concatenate([wps[hd, srow, :].astype(BF16), qts[hd, lrows, :].astype(BF16)], axis=0)
        r = jnp.dot(lhs, s.astype(BF16), preferred_element_type=F32)
        o = r[LANES:] + o0s[hd, lrows, :]
        dec = jnp.exp(gamb[hd, pl.ds(c * GDN_CHUNK + GDN_CHUNK - 1, 1), :])
        s_new = s * dec + vs[hd, srow, :] - r[:LANES]
        y = o * lax.rsqrt(jnp.mean(o * o, -1, keepdims=True) + EPS) * gain
        zz = z_ref[rows, hd * LANES:(hd + 1) * LANES].astype(F32)
        o_ref[rows, hd * LANES:(hd + 1) * LANES] = (y * _silu(zz)).astype(BF16)
        return s_new

    def phase2_chain(hd, pair, states):
        s = states[hd]
        for j in range(PAIR_CHUNKS):
            s = phase2(hd, pair, j, s)
            yield
        states[hd] = s

    def run_interleaved(chains):
        live = list(chains)
        while live:
            nxt = []
            for c in live:
                try:
                    next(c)
                    nxt.append(c)
                except StopIteration:
                    pass
            live = nxt

    def phase1_chains(pair):
        return [phase1(hd, pair, gg) for gg in range(GDN_PAIR) for hd in range(GDN_HB)]

    run_interleaved(phase1_chains(0))

    def body(pair, states):
        states = list(states)
        run_interleaved([phase2_chain(hd, pair - 1, states) for hd in range(GDN_HB)] + phase1_chains(pair))
        return tuple(states)

    s0 = tuple(jnp.zeros((GDN_DK, GDN_DV), F32) for _ in range(GDN_HB))
    states = list(lax.fori_loop(1, N_PAIRS, body, s0))
    run_interleaved([phase2_chain(hd, N_PAIRS - 1, states) for hd in range(GDN_HB)])


def _gdn(proj, g1, gr, gdn_norm):
    w = GDN_HB * LANES
    per_row = D_MODEL // w
    hblk = lambda off: pl.BlockSpec((SEQ, w), lambda b, h, off=off: (b, off * per_row + h))
    big = lambda dt: pltpu.VMEM((GDN_HB, SEQ, LANES), dt)
    return pl.pallas_call(
        _gdn_body,
        grid=(BATCH, GDN_HEADS // GDN_HB),
        in_specs=[
            hblk(0), hblk(1), hblk(2), hblk(3),
            pl.BlockSpec((SEQ, LANES), lambda b, h: (b, 0)),
            pl.BlockSpec((1, GDN_HB, N_GROUPS_SEQ, GDN_GROUP_ROWS), lambda b, h: (b, h, 0, 0)),
            pl.BlockSpec((1, LANES), lambda b, h: (0, 0)),
        ],
        out_specs=pl.BlockSpec((SEQ, w), lambda b, h: (b, h)),
        out_shape=jax.ShapeDtypeStruct((TOKENS, D_MODEL), BF16),
        scratch_shapes=[
            big(BF16), pltpu.VMEM((GDN_HB, SEQ, 2 * LANES), BF16),
            big(BF16), big(F32), big(BF16),
            pltpu.VMEM((GDN_HB, 2 * PAIR_ROWS, LANES), F32),
            pltpu.VMEM((GDN_HB, 2 * PAIR_ROWS, LANES), F32),
            pltpu.VMEM((GDN_HB, 2 * PAIR_CHUNKS * LANES, LANES), F32),
            pltpu.VMEM((GDN_HB, 2 * PAIR_CHUNKS * LANES, LANES), F32),
        ],
        compiler_params=_cparams(("parallel", "parallel")),
        name="gdn",
    )(proj, proj, proj, proj, g1, gr, gdn_norm)


SWA_PIPE = 3
assert WINDOW == SWA_BLOCK


def _swa_body(sink_ref, q_ref, kc_ref, kp_ref, vc_ref, vp_ref, o_ref):
    i = pl.program_id(1)
    kcat = jnp.concatenate([kp_ref[...], kc_ref[...]], axis=0)
    vcat = jnp.concatenate([vp_ref[...], vc_ref[...]], axis=0)
    qi = lax.broadcasted_iota(I32, (SWA_BLOCK, SWA_BLOCK), 0)
    ki = lax.broadcasted_iota(I32, (SWA_BLOCK, SWA_BLOCK), 1)
    take_cur = ki <= qi
    valid = take_cur | (i > 0)
    lane_kv = lax.broadcasted_iota(I32, (2 * SWA_BLOCK, LANES), 1)
    lane_q = lax.broadcasted_iota(I32, (SWA_BLOCK, LANES), 1)
    nt = (((1,), (1,)), ((), ()))
    scale = SWA_DH ** -0.5

    def head_pair(h0, ks, vs):
        qs = q_ref[:, h0 * SWA_DH:(h0 + 2) * SWA_DH]
        scores = [lax.dot_general(qs, kk, nt, preferred_element_type=F32) for kk in ks]
        for _ in range(SWA_PIPE):
            yield
        acc = None
        inv = None
        for half in range(2):
            sink = sink_ref[h0 + half]
            sc = scores[half]
            s = jnp.where(take_cur, sc[:, SWA_BLOCK:], sc[:, :SWA_BLOCK]) * scale
            s = jnp.where(valid, s, -jnp.inf)
            m = jnp.maximum(jnp.max(s, axis=-1, keepdims=True), sink)
            p = jnp.exp(s - m)
            den = jnp.sum(p, axis=-1, keepdims=True) + jnp.exp(sink - m)
            p2 = jnp.concatenate([jnp.where(take_cur, 0.0, p), jnp.where(take_cur, p, 0.0)], axis=1)
            pv = jnp.dot(p2.astype(BF16), vs[half], preferred_element_type=F32)
            acc = pv if acc is None else acc + pv
            r = 1.0 / den
            inv = r if inv is None else jnp.where(lane_q < SWA_DH, inv, r)
        yield
        o_ref[:, h0 * SWA_DH:(h0 + 2) * SWA_DH] = (acc * inv).astype(BF16)

    chains = []
    for slab in range(SWA_KV_HEADS // 2):
        k2 = kcat[:, slab * LANES:(slab + 1) * LANES].astype(F32)
        v2 = vcat[:, slab * LANES:(slab + 1) * LANES].astype(F32)
        k2r = pltpu.roll(k2, SWA_DH, 1)
        v2r = pltpu.roll(v2, SWA_DH, 1)
        for sub in range(2):
            kvh = slab * 2 + sub
            if sub == 0:
                k_lo = jnp.where(lane_kv < SWA_DH, k2, 0.0)
                k_hi = jnp.where(lane_kv >= SWA_DH, k2r, 0.0)
                v_lo = jnp.where(lane_kv < SWA_DH, v2, 0.0)
                v_hi = jnp.where(lane_kv >= SWA_DH, v2r, 0.0)
            else:
                k_lo = jnp.where(lane_kv < SWA_DH, k2r, 0.0)
                k_hi = jnp.where(lane_kv >= SWA_DH, k2, 0.0)
                v_lo = jnp.where(lane_kv < SWA_DH, v2r, 0.0)
                v_hi = jnp.where(lane_kv >= SWA_DH, v2, 0.0)
            k_lo, k_hi, v_lo, v_hi = (t.astype(BF16) for t in (k_lo, k_hi, v_lo, v_hi))
            for gp in range(SWA_GROUP // 2):
                h0 = kvh * SWA_GROUP + 2 * gp
                chains.append(head_pair(h0, (k_lo, k_hi), (v_lo, v_hi)))

    live = []
    pending = list(chains)
    while pending or live:
        if pending:
            live.append(pending.pop(0))
        nxt = []
        for c in live:
            try:
                next(c)
                nxt.append(c)
            except StopIteration:
                pass
        live = nxt


def _swa(proj, sinks):
    nb = SEQ // SWA_BLOCK
    qcol = 4
    kcol = (7 * D_MODEL) // 256
    vcol = kcol + 1
    cur = lambda col: pl.BlockSpec((SWA_BLOCK, 256), lambda b, i, col=col: (b * nb + i, col))
    prev = lambda col: pl.BlockSpec(
        (SWA_BLOCK, 256), lambda b, i, col=col: (b * nb + jnp.maximum(i - 1, 0), col))
    return pl.pallas_call(
        _swa_body,
        grid=(BATCH, nb),
        in_specs=[
            pl.BlockSpec(memory_space=pltpu.SMEM),
            pl.BlockSpec((SWA_BLOCK, D_MODEL), lambda b, i: (b * nb + i, qcol)),
            cur(kcol), prev(kcol), cur(vcol), prev(vcol),
        ],
        out_specs=pl.BlockSpec((SWA_BLOCK, D_MODEL), lambda b, i: (b * nb + i, 0)),
        out_shape=jax.ShapeDtypeStruct((TOKENS, D_MODEL), BF16),
        compiler_params=_cparams(("parallel", "parallel")),
        name="swa",
    )(sinks, proj, proj, proj, proj, proj)


OUT_TM = 512
OUT_SUB = 256


def _outproj_body(x_ref, yg_ref, ys_ref, gg_ref, gs_ref, w_ref, nf_ref, wr_ref, rb_ref,
                  x1_ref, hp_ref, lg_ref):
    def sub_tile(t):
        rows = pl.ds(t * OUT_SUB, OUT_SUB)
        merged = (_sigmoid(gg_ref[rows, :].astype(F32)) * yg_ref[rows, :].astype(F32)
                  + _sigmoid(gs_ref[rows, :].astype(F32)) * ys_ref[rows, :].astype(F32))
        acc = jnp.dot(merged.astype(BF16), w_ref[...], preferred_element_type=F32)
        yield
        x1 = x_ref[rows, :] + acc
        x1_ref[rows, :] = x1
        ms = jnp.mean(x1 * x1, axis=-1, keepdims=True)
        h = x1 * lax.rsqrt(ms + EPS) * nf_ref[...]
        hh = h.astype(BF16)
        r = jnp.dot(hh, wr_ref[...], preferred_element_type=F32)
        lg_ref[rows, :] = r[:, :LANES] + r[:, LANES:] + rb_ref[...]
        half = D_MODEL // 2
        hf = hh.astype(F32)
        wa = pltpu.bitcast(hf[:, :half], U32)
        wb = pltpu.bitcast(hf[:, half:], U32)
        hp_ref[rows, :] = (wa & jnp.uint32(0xFFFF0000)) | (wb >> 16)

    tiles = [sub_tile(t) for t in range(OUT_TM // OUT_SUB)]
    for _ in range(2):
        for g in tiles:
            next(g, None)


def _outproj(x2d, y_gdn, y_swa, proj, w_out, norm_ffn, wr2, r_bias):
    row = lambda w: pl.BlockSpec((OUT_TM, w), lambda i: (i, 0))
    const = lambda s: pl.BlockSpec(s, lambda i: (0, 0))
    return pl.pallas_call(
        _outproj_body,
        grid=(TOKENS // OUT_TM,),
        in_specs=[
            row(D_MODEL), row(D_MODEL), row(D_MODEL),
            pl.BlockSpec((OUT_TM, D_MODEL), lambda i: (i, 5)),
            pl.BlockSpec((OUT_TM, D_MODEL), lambda i: (i, 6)),
            pl.BlockSpec((D_MODEL, D_MODEL), lambda i: (0, 0), pipeline_mode=pl.Buffered(1)),
            const((1, D_MODEL)),
            const((D_MODEL, 2 * LANES)), const((1, LANES)),
        ],
        out_specs=[row(D_MODEL), row(D_MODEL // 2), row(LANES)],
        out_shape=[
            jax.ShapeDtypeStruct((TOKENS, D_MODEL), F32),
            jax.ShapeDtypeStruct((TOKENS, D_MODEL // 2), U32),
            jax.ShapeDtypeStruct((TOKENS, LANES), F32),
        ],
        compiler_params=_cparams(("parallel",)),
        name="outproj",
    )(x2d, y_gdn, y_swa, proj, proj, w_out, norm_ffn, wr2, r_bias)


ROUTE_TM = 1024


def _route_body(lg_ref, tri_ref, idx_ref, wt_ref, cnt_ref, run_ref):
    @pl.when(pl.program_id(0) == 0)
    def _():
        run_ref[...] = jnp.zeros_like(run_ref)

    lg = lg_ref[...]
    lane = lax.broadcasted_iota(I32, lg.shape, 1)
    ninf = -jnp.inf
    big = jnp.int32(LANES)
    is_g = lane < N_GROUPS
    glog = jnp.where(is_g, lg, ninf)
    gmax = jnp.max(glog, axis=-1, keepdims=True)
    gden = jnp.sum(jnp.where(is_g, jnp.exp(lg - gmax), 0.0), axis=-1, keepdims=True)
    p_sel = 1.0 / gden
    grp = jnp.min(jnp.where(glog == gmax, lane, big), axis=-1, keepdims=True)
    emask = (lane >= N_GROUPS) & (lane < N_GROUPS + N_EXPERTS) & (((lane - N_GROUPS) >> 3) == grp)
    el = jnp.where(emask, lg, ninf)
    v1 = jnp.max(el, axis=-1, keepdims=True)
    i1 = jnp.min(jnp.where(el == v1, lane, big), axis=-1, keepdims=True)
    el2 = jnp.where(lane == i1, ninf, el)
    v2 = jnp.max(el2, axis=-1, keepdims=True)
    i2 = jnp.min(jnp.where(el2 == v2, lane, big), axis=-1, keepdims=True)
    e = jnp.exp(v2 - v1)
    w1 = p_sel / (1.0 + e)
    w2 = p_sel * e / (1.0 + e)
    e0 = i1 - N_GROUPS
    e1 = i2 - N_GROUPS

    oh0 = lane == e0
    oh1 = lane == e1
    onehot = jnp.where(oh0 | oh1, 1.0, 0.0)
    prefix = jnp.dot(tri_ref[...], onehot.astype(BF16), preferred_element_type=F32) + run_ref[0:1, :]
    r0 = jnp.sum(jnp.where(oh0, prefix, 0.0), axis=-1, keepdims=True).astype(I32)
    r1 = jnp.sum(jnp.where(oh1, prefix, 0.0), axis=-1, keepdims=True).astype(I32)
    run = run_ref[0:1, :] + jnp.sum(onehot, axis=0, keepdims=True)
    run_ref[...] = jnp.broadcast_to(run, run_ref.shape)
    cnt_ref[...] = jnp.broadcast_to(run, cnt_ref.shape).astype(I32)

    zi = jnp.zeros(lg.shape, I32)
    idx = jnp.where(lane == 0, e0, zi)
    idx = jnp.where(lane == 1, e1, idx)
    idx = jnp.where(lane == 2, r0, idx)
    idx = jnp.where(lane == 3, r1, idx)
    idx_ref[...] = idx
    wt_ref[...] = jnp.where(lane == 0, w1, jnp.where(lane == 1, w2, 0.0))


def _route(logits, tri):
    row = pl.BlockSpec((ROUTE_TM, LANES), lambda i: (i, 0))
    return pl.pallas_call(
        _route_body,
        grid=(TOKENS // ROUTE_TM,),
        in_specs=[row, pl.BlockSpec((ROUTE_TM, ROUTE_TM), lambda i: (0, 0))],
        out_specs=[row, row, pl.BlockSpec((8, LANES), lambda i: (0, 0))],
        out_shape=[
            jax.ShapeDtypeStruct((TOKENS, LANES), I32),
            jax.ShapeDtypeStruct((TOKENS, LANES), F32),
            jax.ShapeDtypeStruct((8, LANES), I32),
        ],
        scratch_shapes=[pltpu.VMEM((8, LANES), F32)],
        compiler_params=_cparams(("arbitrary",)),
        name="route",
    )(logits, tri)


DISP_TM = 256


def _dispatch_body(meta_ref, dest_ref, h_ref, xs_ref, zbuf, sem, zsem):
    @pl.when(pl.program_id(0) == 0)
    def _():
        zbuf[...] = jnp.zeros_like(zbuf)

        def zero_block(row0):
            return pltpu.make_async_copy(zbuf, xs_ref.at[pl.ds(row0, MOE_BLOCK), :], zsem)

        def per_expert(e, n):
            has = meta_ref[N_EXPERTS + e] > 0

            @pl.when(has)
            def _():
                zero_block(pl.multiple_of(meta_ref[e] - MOE_BLOCK, MOE_BLOCK)).start()

            return n + has.astype(I32)

        n_last = lax.fori_loop(0, N_EXPERTS, per_expert, jnp.int32(0))
        used = meta_ref[2 * N_EXPERTS]

        def tail(j, c):
            zero_block(pl.multiple_of(j * MOE_BLOCK, MOE_BLOCK)).start()
            return c

        lax.fori_loop(used, N_BLOCKS, tail, 0)

        def drain(i, c):
            zero_block(0).wait()
            return c

        lax.fori_loop(0, n_last + (N_BLOCKS - used), drain, 0)

    def row_copy(r, d):
        return pltpu.make_async_copy(h_ref.at[pl.ds(r, 1), :], xs_ref.at[pl.ds(d, 1), :], sem)

    def issue(r, c):
        for k in range(TOP_K):
            row_copy(r, dest_ref[TOP_K * r + k]).start()
        return c

    lax.fori_loop(0, DISP_TM, issue, 0, unroll=4)
    for k in range(TOP_K):
        pltpu.make_async_copy(h_ref, xs_ref.at[pl.ds(0, DISP_TM), :], sem).wait()


def _dispatch(meta, dest_flat, hp):
    grid_spec = pltpu.PrefetchScalarGridSpec(
        num_scalar_prefetch=1,
        grid=(TOKENS // DISP_TM,),
        in_specs=[
            pl.BlockSpec((DISP_TM * TOP_K,), lambda i, m: (i,), memory_space=pltpu.SMEM),
            pl.BlockSpec((DISP_TM, D_MODEL // 2), lambda i, m: (i, 0)),
        ],
        out_specs=pl.BlockSpec(memory_space=pl.ANY),
        scratch_shapes=[
            pltpu.VMEM((MOE_BLOCK, D_MODEL // 2), U32),
            pltpu.SemaphoreType.DMA,
            pltpu.SemaphoreType.DMA,
        ],
    )
    return pl.pallas_call(
        _dispatch_body,
        grid_spec=grid_spec,
        out_shape=jax.ShapeDtypeStruct((N_SLOTS, D_MODEL // 2), U32),
        compiler_params=_cparams(("arbitrary",)),
        name="dispatch",
    )(meta, dest_flat, hp)


def _slots_body(idx_ref, start_ref, dest_ref):
    idx = idx_ref[...]
    lane = lax.broadcasted_iota(I32, idx.shape, 1)
    start = start_ref[...]
    out = jnp.zeros(idx.shape, I32)
    for k in range(TOP_K):
        e = idx[:, k:k + 1]
        base = jnp.sum(jnp.where(lane == e, start, 0.0), axis=-1, keepdims=True).astype(I32)
        out = jnp.where(lane == k, base + idx[:, TOP_K + k:TOP_K + k + 1], out)
    dest_ref[...] = out


SLOTS_TM = 2048


def _slots(idx, pad_start_row):
    row = pl.BlockSpec((SLOTS_TM, LANES), lambda i: (i, 0))
    return pl.pallas_call(
        _slots_body,
        grid=(TOKENS // SLOTS_TM,),
        in_specs=[row, pl.BlockSpec((1, LANES), lambda i: (0, 0))],
        out_specs=row,
        out_shape=jax.ShapeDtypeStruct((TOKENS, LANES), I32),
        compiler_params=_cparams(("parallel",)),
        name="slots",
    )(idx, pad_start_row)


def _experts_body(sched_ref, x_ref, wg_hbm, wu_hbm, wd_hbm, y_ref,
                  wg_st, wu_st, wd_st, wg_bf, wu_bf, wd_bf, sem):
    j = pl.program_id(0)
    e = sched_ref[j]
    slot = sched_ref[N_BLOCKS + j]
    nxt = sched_ref[2 * N_BLOCKS + j]
    used = sched_ref[3 * N_BLOCKS]
    new_expert = (j == 0) | (e != sched_ref[jnp.maximum(j - 1, 0)])

    def weight_copies(expert, s):
        return (pltpu.make_async_copy(wg_hbm.at[expert], wg_st.at[s], sem.at[s, 0]),
                pltpu.make_async_copy(wu_hbm.at[expert], wu_st.at[s], sem.at[s, 1]),
                pltpu.make_async_copy(wd_hbm.at[expert], wd_st.at[s], sem.at[s, 2]))

    @pl.when(j == 0)
    def _():
        for c in weight_copies(e, slot):
            c.start(priority=1)

    @pl.when(new_expert)
    def _():
        for c in weight_copies(e, slot):
            c.wait()

        @pl.when(nxt >= 0)
        def _():
            for c in weight_copies(nxt, 1 - slot):
                c.start(priority=1)

        wg_bf[...] = wg_st[slot].astype(BF16)
        wu_bf[...] = wu_st[slot].astype(BF16)
        wd_bf[...] = wd_st[slot].astype(BF16)

    @pl.when(j < used)
    def _():
        w = x_ref[...]
        xa = pltpu.bitcast(w & jnp.uint32(0xFFFF0000), F32).astype(BF16)
        xb = pltpu.bitcast(w << 16, F32).astype(BF16)
        x = jnp.concatenate([xa, xb], axis=1)
        g = jnp.dot(x, wg_bf[...], preferred_element_type=F32)
        u = jnp.dot(x, wu_bf[...], preferred_element_type=F32)
        hb = (_silu(g) * u).astype(BF16)
        y = jnp.dot(hb, wd_bf[...], preferred_element_type=F32)
        half = D_MODEL // 2
        ya = pltpu.bitcast(y[:, :half].astype(BF16).astype(F32), U32)
        yb = pltpu.bitcast(y[:, half:].astype(BF16).astype(F32), U32)
        y_ref[...] = ya | (yb >> 16)

    @pl.when(j >= used)
    def _():
        y_ref[...] = jnp.zeros_like(y_ref)


def _experts(sched, xs, w_gate, w_up, w_down):
    grid_spec = pltpu.PrefetchScalarGridSpec(
        num_scalar_prefetch=1,
        grid=(N_BLOCKS,),
        in_specs=[
            pl.BlockSpec((MOE_BLOCK, D_MODEL // 2), lambda j, s: (j, 0)),
            pl.BlockSpec(memory_space=pl.ANY),
            pl.BlockSpec(memory_space=pl.ANY),
            pl.BlockSpec(memory_space=pl.ANY),
        ],
        out_specs=pl.BlockSpec((MOE_BLOCK, D_MODEL // 2), lambda j, s: (j, 0)),
        scratch_shapes=[
            pltpu.VMEM((2, D_MODEL, D_EXPERT), F32),
            pltpu.VMEM((2, D_MODEL, D_EXPERT), F32),
            pltpu.VMEM((2, D_EXPERT, D_MODEL), F32),
            pltpu.VMEM((D_MODEL, D_EXPERT), BF16),
            pltpu.VMEM((D_MODEL, D_EXPERT), BF16),
            pltpu.VMEM((D_EXPERT, D_MODEL), BF16),
            pltpu.SemaphoreType.DMA((2, 3)),
        ],
    )
    return pl.pallas_call(
        _experts_body,
        grid_spec=grid_spec,
        out_shape=jax.ShapeDtypeStruct((N_SLOTS, D_MODEL // 2), U32),
        compiler_params=_cparams(("arbitrary",)),
        name="experts",
    )(sched, xs, w_gate, w_up, w_down)


COMB_TM = 256
N_COMB = TOKENS // COMB_TM


def _unpack_pairs(w):
    hi = pltpu.bitcast(w & jnp.uint32(0xFFFF0000), F32)
    lo = pltpu.bitcast(w << 16, F32)
    return jnp.concatenate([hi, lo], axis=1)


def _combine_body(dest_ref, dest_next_ref, wt_ref, x1_ref, gain_ref, ys_ref, o_ref, buf, sem):
    i = pl.program_id(0)
    slot = lax.rem(i, 2)

    def start_gathers(dref, s):
        def issue(r, c):
            for k in range(TOP_K):
                pltpu.make_async_copy(ys_ref.at[pl.ds(dref[TOP_K * r + k], 1), :],
                                      buf.at[s, k, pl.ds(r, 1), :], sem.at[s]).start()
            return c
        lax.fori_loop(0, COMB_TM, issue, 0, unroll=4)

    @pl.when(i == 0)
    def _():
        start_gathers(dest_ref, slot)

    @pl.when(i + 1 < N_COMB)
    def _():
        start_gathers(dest_next_ref, 1 - slot)

    for k in range(TOP_K):
        pltpu.make_async_copy(ys_ref.at[pl.ds(0, COMB_TM), :], buf.at[slot, k], sem.at[slot]).wait()

    wt = wt_ref[...]
    y = (x1_ref[...] + wt[:, 0:1] * _unpack_pairs(buf[slot, 0]) + wt[:, 1:2] * _unpack_pairs(buf[slot, 1]))
    ms = jnp.mean(y * y, axis=-1, keepdims=True)
    o_ref[...] = y * lax.rsqrt(ms + EPS) * gain_ref[...]


def _combine(dest_flat, wts, x1, norm_final, ys):
    dblk = lambda f: pl.BlockSpec((COMB_TM * TOP_K,), f, memory_space=pltpu.SMEM)
    return pl.pallas_call(
        _combine_body,
        grid=(N_COMB,),
        in_specs=[
            dblk(lambda i: (i,)),
            dblk(lambda i: (jnp.minimum(i + 1, N_COMB - 1),)),
            pl.BlockSpec((COMB_TM, LANES), lambda i: (i, 0)),
            pl.BlockSpec((COMB_TM, D_MODEL), lambda i: (i, 0)),
            pl.BlockSpec((1, D_MODEL), lambda i: (0, 0)),
            pl.BlockSpec(memory_space=pl.ANY),
        ],
        out_specs=pl.BlockSpec((COMB_TM, D_MODEL), lambda i: (i, 0)),
        out_shape=jax.ShapeDtypeStruct((TOKENS, D_MODEL), F32),
        scratch_shapes=[pltpu.VMEM((2, TOP_K, COMB_TM, D_MODEL // 2), U32), pltpu.SemaphoreType.DMA((2,))],
        compiler_params=_cparams(("arbitrary",)),
        name="combine",
    )(dest_flat, dest_flat, wts, x1, norm_final, ys)


def _pad_lanes(v):
    v = v.reshape(1, -1).astype(F32)
    return jnp.pad(v, ((0, 0), (0, LANES - v.shape[1])))


def kernel(x, norm_mix, w_in, conv_w, gdn_a_log, gdn_dt_bias, gdn_norm, swa_sinks, w_out, norm_ffn,
           w_router_group, b_router_group, w_router_expert, b_router_expert, w_gate, w_up, w_down,
           norm_final):
    l = 0
    x2d = x.reshape(TOKENS, D_MODEL)
    w = w_in[l]
    o_z = 3 * D_MODEL
    o_a = o_z + D_MODEL
    o_sq = o_a + 2 * GDN_HEADS
    o_sk = o_sq + D_MODEL
    o_sv = o_sk + SWA_KV_HEADS * SWA_DH
    o_gg = o_sv + SWA_KV_HEADS * SWA_DH
    o_gs = o_gg + D_MODEL
    w_main = jnp.concatenate(
        [w[:, :o_a], w[:, o_sq:o_sk], w[:, o_gg:o_gs], w[:, o_gs:], w[:, o_sk:o_sv], w[:, o_sv:o_gg]],
        axis=1).astype(BF16)
    w_ab = jnp.pad(w[:, o_a:o_sq], ((0, 0), (0, LANES - 2 * GDN_HEADS))).astype(BF16)

    proj, ab = _inproj(x2d, norm_mix[l].reshape(1, D_MODEL), w_main, w_ab, conv_w[l])

    g1, gt = _gdn_prep(ab, _pad_lanes(gdn_a_log[l]), _pad_lanes(gdn_dt_bias[l]))
    gr = gt[:, :GDN_HEADS, :].reshape(BATCH, GDN_HEADS, N_GROUPS_SEQ, GDN_GROUP_ROWS)
    y_gdn = _gdn(proj, g1, gr, gdn_norm[l].reshape(1, GDN_DV))

    y_swa = _swa(proj, swa_sinks[l].astype(F32))

    w_r = jnp.concatenate([w_router_group[l], w_router_expert[l]], axis=1).astype(F32)
    w_r = jnp.pad(w_r, ((0, 0), (0, LANES - w_r.shape[1])))
    wr_hi = w_r.astype(BF16)
    wr_lo = (w_r - wr_hi.astype(F32)).astype(BF16)
    r_bias = _pad_lanes(jnp.concatenate([b_router_group[l], b_router_expert[l]]))
    x1, hp, logits = _outproj(x2d, y_gdn, y_swa, proj, w_out[l].astype(BF16),
                              norm_ffn[l].reshape(1, D_MODEL), jnp.concatenate([wr_hi, wr_lo], axis=1), r_bias)

    tri = (lax.broadcasted_iota(I32, (ROUTE_TM, ROUTE_TM), 1)
           < lax.broadcasted_iota(I32, (ROUTE_TM, ROUTE_TM), 0)).astype(BF16)
    idx, wts, cnt = _route(logits, tri)

    counts = cnt[0, :N_EXPERTS]
    padded = (counts + MOE_BLOCK - 1) // MOE_BLOCK * MOE_BLOCK
    pad_end = jnp.cumsum(padded)
    pad_start = pad_end - padded
    dest = _slots(idx, _pad_lanes(pad_start))
    dest_flat = dest[:, :TOP_K].reshape(N_ASSIGN)
    blk_pos = jnp.arange(N_BLOCKS, dtype=I32) * MOE_BLOCK
    blk_e = jnp.minimum(jnp.sum((pad_end[None, :] <= blk_pos[:, None]).astype(I32), axis=1), N_EXPERTS - 1)
    used = pad_end[-1:] // MOE_BLOCK
    meta = jnp.concatenate([pad_end, padded, used]).astype(I32)
    is_new = jnp.concatenate([jnp.ones((1,), I32), (blk_e[1:] != blk_e[:-1]).astype(I32)])
    ordinal = jnp.cumsum(is_new) - 1
    nxt_pos = jnp.sum((ordinal[None, :] <= ordinal[:, None]).astype(I32), axis=1)
    nxt = jnp.where(nxt_pos < N_BLOCKS, blk_e[jnp.minimum(nxt_pos, N_BLOCKS - 1)], -1)
    sched = jnp.concatenate([blk_e, ordinal & 1, nxt, used]).astype(I32)

    xs = _dispatch(meta, dest_flat, hp)
    ys = _experts(sched, xs, w_gate[l], w_up[l], w_down[l])
    out = _combine(dest_flat, wts, x1, norm_final.reshape(1, D_MODEL), ys)
    return out.reshape(BATCH, SEQ, D_MODEL)
```

```python
import math

import jax
import jax.numpy as jnp
from jax import lax
from jax.experimental import pallas as pl
from jax.experimental.pallas import tpu as pltpu

F32 = jnp.float32
BF16 = jnp.bfloat16
I32 = jnp.int32
U32 = jnp.uint32

D_MODEL = 2048
BATCH = 16
SEQ = 2048
TOKENS = BATCH * SEQ
EPS = 1e-6

GDN_HEADS = 16
GDN_DK = 128
GDN_DV = 128
GDN_CHUNK = 64
CONV_W = 4
GDN_GROUP = 4
GDN_GROUP_ROWS = GDN_GROUP * GDN_CHUNK
N_CHUNKS = SEQ // GDN_CHUNK
N_GROUPS_SEQ = SEQ // GDN_GROUP_ROWS

SWA_HEADS = 32
SWA_KV_HEADS = 4
SWA_DH = 64
SWA_GROUP = 8
SWA_BLOCK = 128
WINDOW = 128

N_GROUPS = 8
EXPERTS_PER_GROUP = 8
N_EXPERTS = 64
TOP_K = 2
D_EXPERT = 512
MOE_BLOCK = 256
N_ASSIGN = TOKENS * TOP_K
N_BLOCKS = N_ASSIGN // MOE_BLOCK + N_EXPERTS
N_SLOTS = N_BLOCKS * MOE_BLOCK

PROJ_DIM = 7 * D_MODEL + 2 * SWA_KV_HEADS * SWA_DH
LANES = 128

VMEM_LIMIT = 56 * 1024 * 1024


def _cparams(sem, vmem=VMEM_LIMIT):
    return pltpu.CompilerParams(dimension_semantics=sem, vmem_limit_bytes=vmem)


def _sigmoid(x):
    return 0.5 * jnp.tanh(0.5 * x) + 0.5


def _silu(x):
    h = 0.5 * x
    return h * jnp.tanh(h) + h


INPROJ_TM = 2048
INPROJ_TN = 512
INPROJ_VMEM = 60 * 1024 * 1024


INPROJ_RC = 256
INPROJ_RING = 3
QKV_TILES = 3 * D_MODEL // INPROJ_TN
QK_TILES = 2 * D_MODEL // INPROJ_TN
Q_TILES = D_MODEL // INPROJ_TN
assert INPROJ_TM == SEQ


def _inproj_body(x_ref, g_ref, w_ref, wab_ref, cw_ref, o_ref, ab_ref, h_ref, cpad):
    j = pl.program_id(1)

    @pl.when(j == 0)
    def _():
        def chunk(i, c):
            r = pl.ds(pl.multiple_of(i * 128, 128), 128)
            x = x_ref[r, :]
            ms = jnp.mean(x * x, axis=-1, keepdims=True)
            h_ref[r, :] = (x * lax.rsqrt(ms + EPS) * g_ref[...]).astype(BF16)
            return c
        lax.fori_loop(0, INPROJ_TM // 128, chunk, 0)
        ab_ref[...] = jnp.dot(h_ref[...], wab_ref[...], preferred_element_type=F32)

    @pl.when(j >= QKV_TILES)
    def _():
        o_ref[...] = jnp.dot(h_ref[...], w_ref[...], preferred_element_type=F32).astype(BF16)

    @pl.when(j < QKV_TILES)
    def _():
        cw = cw_ref[...]
        is_qk = j < QK_TILES
        qscale = jnp.where(j < Q_TILES, GDN_DK ** -0.5, 1.0).astype(F32)
        n_chunks = INPROJ_TM // INPROJ_RC

        def matmul(c):
            rows = pl.ds(c * INPROJ_RC, INPROJ_RC)
            cpad[c % INPROJ_RING, pl.ds(8, INPROJ_RC), :] = jnp.dot(
                h_ref[rows, :], w_ref[...], preferred_element_type=F32)

        def epilogue(c):
            buf = cpad.at[c % INPROJ_RING]
            if c == 0:
                buf[pl.ds(0, 8), :] = jnp.zeros((8, INPROJ_TN), F32)
            else:
                buf[pl.ds(0, 8), :] = cpad[(c - 1) % INPROJ_RING, pl.ds(INPROJ_RC, 8), :]
            y = None
            for s in range(CONV_W):
                t = buf[pl.ds(8 - s, INPROJ_RC), :] * cw[CONV_W - 1 - s:CONV_W - s, :]
                y = t if y is None else y + t
            y = _silu(y)
            parts = []
            for g in range(INPROJ_TN // LANES):
                yg = y[:, g * LANES:(g + 1) * LANES]
                inv = lax.rsqrt(jnp.sum(yg * yg, -1, keepdims=True) + EPS) * qscale
                parts.append(yg * jnp.where(is_qk, inv, 1.0))
            o_ref[pl.ds(c * INPROJ_RC, INPROJ_RC), :] = jnp.concatenate(parts, axis=1).astype(BF16)

        matmul(0)
        for c in range(n_chunks):
            if c + 1 < n_chunks:
                matmul(c + 1)
            epilogue(c)


def _inproj(x2d, gain, w_main, w_ab, conv_w):
    grid = (TOKENS // INPROJ_TM, PROJ_DIM // INPROJ_TN)
    return pl.pallas_call(
        _inproj_body,
        grid=grid,
        in_specs=[
            pl.BlockSpec((INPROJ_TM, D_MODEL), lambda i, j: (i, 0)),
            pl.BlockSpec((1, D_MODEL), lambda i, j: (0, 0)),
            pl.BlockSpec((D_MODEL, INPROJ_TN), lambda i, j: (0, j)),
            pl.BlockSpec((D_MODEL, LANES), lambda i, j: (0, 0)),
            pl.BlockSpec((CONV_W, INPROJ_TN), lambda i, j: (0, jnp.minimum(j, QKV_TILES - 1))),
        ],
        out_specs=[
            pl.BlockSpec((INPROJ_TM, INPROJ_TN), lambda i, j: (i, j)),
            pl.BlockSpec((INPROJ_TM, LANES), lambda i, j: (i, 0)),
        ],
        out_shape=[
            jax.ShapeDtypeStruct((TOKENS, PROJ_DIM), BF16),
            jax.ShapeDtypeStruct((TOKENS, LANES), F32),
        ],
        scratch_shapes=[pltpu.VMEM((INPROJ_TM, D_MODEL), BF16),
                        pltpu.VMEM((INPROJ_RING, 8 + INPROJ_RC, INPROJ_TN), F32)],
        compiler_params=_cparams(("parallel", "arbitrary"), vmem=INPROJ_VMEM),
        name="inproj",
    )(x2d, gain, w_main, w_ab, conv_w)


def _gdn_prep_body(ab_ref, alog_ref, dtb_ref, g1_ref, gt_ref):
    ab = ab_ref[...]
    lane = lax.broadcasted_iota(I32, ab.shape, 1)
    row = lax.broadcasted_iota(I32, ab.shape, 0) % GDN_CHUNK
    xa = ab + dtb_ref[...]
    softplus = jnp.maximum(xa, 0.0) + jnp.log(1.0 + jnp.exp(-jnp.abs(xa)))
    g = jnp.where(lane < GDN_HEADS, -jnp.exp(alog_ref[...]) * softplus, 0.0)
    gam = g
    s = 1
    while s < GDN_CHUNK:
        gam = gam + jnp.where(row >= s, pltpu.roll(gam, s, 0), 0.0)
        s *= 2
    g1_ref[...] = jnp.where(lane < GDN_HEADS, gam, _sigmoid(ab))
    gt_ref[0] = gam.T


def _gdn_prep(ab, alog_pad, dtb_pad):
    return pl.pallas_call(
        _gdn_prep_body,
        grid=(BATCH,),
        in_specs=[
            pl.BlockSpec((SEQ, LANES), lambda b: (b, 0)),
            pl.BlockSpec((1, LANES), lambda b: (0, 0)),
            pl.BlockSpec((1, LANES), lambda b: (0, 0)),
        ],
        out_specs=[
            pl.BlockSpec((SEQ, LANES), lambda b: (b, 0)),
            pl.BlockSpec((1, LANES, SEQ), lambda b: (b, 0, 0)),
        ],
        out_shape=[
            jax.ShapeDtypeStruct((TOKENS, LANES), F32),
            jax.ShapeDtypeStruct((BATCH, LANES, SEQ), F32),
        ],
        compiler_params=_cparams(("parallel",)),
        name="gdn_prep",
    )(ab, alog_pad, dtb_pad)


GDN_HB = 4
GDN_PAIR = 2
N_PAIRS = N_GROUPS_SEQ // GDN_PAIR
PAIR_CHUNKS = GDN_PAIR * GDN_GROUP
PAIR_ROWS = GDN_PAIR * GDN_GROUP_ROWS
GDN_VMEM = 59 * 1024 * 1024


def _block_diag(x):
    t = jnp.concatenate([x] * GDN_GROUP, axis=0)
    rb = lax.broadcasted_iota(I32, t.shape, 0) // GDN_CHUNK
    cb = lax.broadcasted_iota(I32, t.shape, 1) // GDN_CHUNK
    return jnp.where(rb == cb, t, jnp.zeros_like(t))


def _sbs_product(lhs_list, y):
    bd = _block_diag(y.astype(BF16))
    xs = [x.astype(BF16) for x in lhs_list]
    lhs = xs[0] if len(xs) == 1 else jnp.concatenate(xs, axis=0)
    r = jnp.dot(lhs, bd, preferred_element_type=F32)
    c = GDN_CHUNK
    return [r[i * c:(i + 1) * c] for i in range(len(xs))]


def _gdn_body(q_ref, k_ref, v_ref, z_ref, g1_ref, gr_ref, gain_ref,
              o_ref,
              kdbf, rhsbf, qd, gamb, betab, o0s, qts, vs, wps):
    hg = pl.program_id(1)
    shape = (PAIR_ROWS, LANES)
    lane = lax.broadcasted_iota(I32, shape, 1)

    def phase0(pair):
        rows = pl.ds(pl.multiple_of(pair * PAIR_ROWS, PAIR_ROWS), PAIR_ROWS)
        g1 = g1_ref[rows, :]
        for hd in range(GDN_HB):
            hs = slice(hd * LANES, (hd + 1) * LANES)
            head = hg * GDN_HB + hd

            def col(off):
                c = jnp.sum(jnp.where(lane == off + head, g1, 0.0), axis=-1, keepdims=True)
                return jnp.broadcast_to(c, shape)

            gam = col(0)
            beta = col(GDN_HEADS)
            gamb[hd, rows, :] = gam
            betab[hd, rows, :] = beta.astype(BF16)
            gam3 = gam.reshape(PAIR_CHUNKS, GDN_CHUNK, LANES)
            glast = jnp.broadcast_to(gam3[:, GDN_CHUNK - 1:GDN_CHUNK, :], gam3.shape).reshape(shape)
            eg = jnp.exp(gam)
            kdf = jnp.exp(glast - gam)

            qd[hd, rows, :] = (q_ref[rows, hs].astype(F32) * eg).astype(BF16)
            k = k_ref[rows, hs].astype(F32)
            kdbf[hd, rows, :] = (k * kdf).astype(BF16)
            rhsbf[hd, rows, LANES:] = (k * (beta * eg)).astype(BF16)
            rhsbf[hd, rows, :LANES] = (v_ref[rows, hs].astype(F32) * beta).astype(BF16)

    gshape = (GDN_CHUNK, GDN_GROUP_ROWS)
    ii = lax.broadcasted_iota(I32, gshape, 0)
    jj = lax.broadcasted_iota(I32, gshape, 1) % GDN_CHUNK
    eye = jnp.where(ii == jj, 1.0, 0.0).astype(F32)
    pr = lax.broadcasted_iota(I32, (2 * GDN_CHUNK, LANES), 0) // GDN_CHUNK
    tl = lax.broadcasted_iota(I32, (LANES, GDN_GROUP_ROWS), 1) // GDN_CHUNK
    lane_c = lax.broadcasted_iota(I32, (GDN_CHUNK, LANES), 1)
    nt = (((1,), (1,)), ((), ()))

    def sbs_bcast(ref, hd, base):
        parts = [ref[hd, pl.ds(base + m * GDN_CHUNK, GDN_CHUNK), :] for m in range(GDN_GROUP)]
        a = jnp.where(lane_c < GDN_CHUNK, parts[0], parts[1])
        b = jnp.where(lane_c < GDN_CHUNK, parts[2], parts[3])
        return jnp.concatenate([a, b], axis=1)

    def phase1(hd, pair, gg):
        g = pair * GDN_PAIR + gg
        base = pl.multiple_of(g * GDN_GROUP_ROWS, GDN_GROUP_ROWS)
        slot = lax.rem(pair, 2)
        res = []
        for p in range(2):
            rows = pl.ds(base + p * 2 * GDN_CHUNK, 2 * GDN_CHUNK)
            kst = k_ref[rows, hd * LANES:(hd + 1) * LANES]
            qst = q_ref[rows, hd * LANES:(hd + 1) * LANES]
            lhs_k = jnp.concatenate([kst[:GDN_CHUNK], kst[GDN_CHUNK:]], axis=1)
            lhs_q = jnp.concatenate([qst[:GDN_CHUNK], qst[GDN_CHUNK:]], axis=1)
            lhs = jnp.concatenate([lhs_q, lhs_k], axis=0)
            zero = jnp.zeros_like(kst)
            bt = jnp.concatenate([jnp.where(pr == 0, kst, zero), jnp.where(pr == 1, kst, zero)], axis=1)
            res.append(lax.dot_general(lhs, bt, nt, preferred_element_type=F32))
        yield
        qk = jnp.concatenate([res[0][:GDN_CHUNK], res[1][:GDN_CHUNK]], axis=1)
        kk = jnp.concatenate([res[0][GDN_CHUNK:], res[1][GDN_CHUNK:]], axis=1)

        gc = sbs_bcast(gamb, hd, base)
        bc = sbs_bcast(betab, hd, base).astype(F32)
        grow = gr_ref[0, hd, pl.ds(g, 1), :]
        decay = jnp.exp(jnp.where(ii >= jj, gc - grow, -jnp.inf))
        a = jnp.where(ii > jj, kk * decay * bc, 0.0)
        qkd = qk * decay

        u = eye - a
        (x,) = _sbs_product([a], a)
        yield
        for lvl in range(1, 6):
            if lvl < 5:
                x2, ux = _sbs_product([x, u], x)
                u = u + ux
                x = x2
            else:
                (ux,) = _sbs_product([u], x)
                u = u + ux
            yield

        rows4 = pl.ds(base, GDN_GROUP_ROWS)
        so = jnp.dot(_block_diag(u.astype(BF16)), rhsbf[hd, rows4, :], preferred_element_type=F32)
        yield
        sol = so.astype(BF16)

        kdt = kdbf[hd, rows4, :].astype(F32).T.astype(BF16)
        zt = jnp.zeros_like(kdt)
        lhs2 = jnp.concatenate(
            [_block_diag(qkd.astype(BF16))] + [jnp.where(tl == m, kdt, zt) for m in range(GDN_GROUP)],
            axis=0)
        r = jnp.dot(lhs2, sol, preferred_element_type=F32)
        srows = pl.ds(pl.multiple_of(slot * PAIR_ROWS + gg * GDN_GROUP_ROWS, GDN_GROUP_ROWS), GDN_GROUP_ROWS)
        o0s[hd, srows, :] = r[:GDN_GROUP_ROWS, :LANES]
        qts[hd, srows, :] = qd[hd, rows4, :].astype(F32) - r[:GDN_GROUP_ROWS, LANES:]
        for m in range(GDN_GROUP):
            blk = r[GDN_GROUP_ROWS + m * LANES:GDN_GROUP_ROWS + (m + 1) * LANES]
            dst = pl.ds(pl.multiple_of((slot * PAIR_CHUNKS + gg * GDN_GROUP + m) * LANES, LANES), LANES)
            vs[hd, dst, :] = blk[:, :LANES]
            wps[hd, dst, :] = blk[:, LANES:]

    gain = gain_ref[...]

    def phase2(hd, pair, j, s):
        c = pair * PAIR_CHUNKS + j
        local = lax.rem(pair, 2) * PAIR_CHUNKS + j
        rows = pl.ds(pl.multiple_of(c * GDN_CHUNK, GDN_CHUNK), GDN_CHUNK)
        lrows = pl.ds(pl.multiple_of(local * GDN_CHUNK, GDN_CHUNK), GDN_CHUNK)
        srow = pl.ds(pl.multiple_of(local * LANES, LANES), LANES)
        lhs = jnp.concatenate([wps[hd, srow, :].astype(BF16), qts[hd, lrows, :].astype(BF16)], axis=0)
        r = jnp.dot(lhs, s.astype(BF16), preferred_element_type=F32)
        o = r[LANES:] + o0s[hd, lrows, :]
        dec = jnp.exp(gamb[hd, pl.ds(c * GDN_CHUNK + GDN_CHUNK - 1, 1), :])
        s_new = s * dec + vs[hd, srow, :] - r[:LANES]
        y = o * lax.rsqrt(jnp.mean(o * o, -1, keepdims=True) + EPS) * gain
        zz = z_ref[rows, hd * LANES:(hd + 1) * LANES].astype(F32)
        o_ref[rows, hd * LANES:(hd + 1) * LANES] = (y * _silu(zz)).astype(BF16)
        return s_new

    def phase2_chain(hd, pair, states):
        s = states[hd]
        for j in range(PAIR_CHUNKS):
            s = phase2(hd, pair, j, s)
            yield
        states[hd] = s

    def run_interleaved(chains):
        live = list(chains)
        while live:
            nxt = []
            for c in live:
                try:
                    next(c)
                    nxt.append(c)
                except StopIteration:
                    pass
            live = nxt

    def phase1_chains(pair):
        return [phase1(hd, pair, gg) for gg in range(GDN_PAIR) for hd in range(GDN_HB)]

    for pair in range(N_PAIRS):
        phase0(pair)
    run_interleaved(phase1_chains(0))

    def body(pair, states):
        states = list(states)
        run_interleaved([phase2_chain(hd, pair - 1, states) for hd in range(GDN_HB)] + phase1_chains(pair))
        return tuple(states)

    s0 = tuple(jnp.zeros((GDN_DK, GDN_DV), F32) for _ in range(GDN_HB))
    states = list(lax.fori_loop(1, N_PAIRS, body, s0))
    run_interleaved([phase2_chain(hd, N_PAIRS - 1, states) for hd in range(GDN_HB)])


def _gdn(proj, g1, gr, gdn_norm):
    w = GDN_HB * LANES
    per_row = D_MODEL // w
    hblk = lambda off: pl.BlockSpec((SEQ, w), lambda b, h, off=off: (b, off * per_row + h))
    big = lambda dt: pltpu.VMEM((GDN_HB, SEQ, LANES), dt)
    return pl.pallas_call(
        _gdn_body,
        grid=(BATCH, GDN_HEADS // GDN_HB),
        in_specs=[
            hblk(0), hblk(1), hblk(2), hblk(3),
            pl.BlockSpec((SEQ, LANES), lambda b, h: (b, 0)),
            pl.BlockSpec((1, GDN_HB, N_GROUPS_SEQ, GDN_GROUP_ROWS), lambda b, h: (b, h, 0, 0)),
            pl.BlockSpec((1, LANES), lambda b, h: (0, 0)),
        ],
        out_specs=pl.BlockSpec((SEQ, w), lambda b, h: (b, h)),
        out_shape=jax.ShapeDtypeStruct((TOKENS, D_MODEL), BF16),
        scratch_shapes=[
            big(BF16), pltpu.VMEM((GDN_HB, SEQ, 2 * LANES), BF16),
            big(BF16), big(F32), big(BF16),
            pltpu.VMEM((GDN_HB, 2 * PAIR_ROWS, LANES), F32),
            pltpu.VMEM((GDN_HB, 2 * PAIR_ROWS, LANES), F32),
            pltpu.VMEM((GDN_HB, 2 * PAIR_CHUNKS * LANES, LANES), F32),
            pltpu.VMEM((GDN_HB, 2 * PAIR_CHUNKS * LANES, LANES), F32),
        ],
        compiler_params=_cparams(("parallel", "parallel"), vmem=GDN_VMEM),
        name="gdn",
    )(proj, proj, proj, proj, g1, gr, gdn_norm)


SWA_PIPE = 3
assert WINDOW == SWA_BLOCK


def _swa_body(sink_ref, q_ref, kc_ref, kp_ref, vc_ref, vp_ref, o_ref):
    i = pl.program_id(1)
    kcat = jnp.concatenate([kp_ref[...], kc_ref[...]], axis=0)
    vcat = jnp.concatenate([vp_ref[...], vc_ref[...]], axis=0)
    qi = lax.broadcasted_iota(I32, (SWA_BLOCK, SWA_BLOCK), 0)
    ki = lax.broadcasted_iota(I32, (SWA_BLOCK, SWA_BLOCK), 1)
    take_cur = ki <= qi
    valid = take_cur | (i > 0)
    lane_kv = lax.broadcasted_iota(I32, (2 * SWA_BLOCK, LANES), 1)
    lane_q = lax.broadcasted_iota(I32, (SWA_BLOCK, LANES), 1)
    nt = (((1,), (1,)), ((), ()))
    scale = SWA_DH ** -0.5
    assert math.frexp(scale)[0] == 0.5

    def head_pair(h0, ks, vs):
        qs = q_ref[:, h0 * SWA_DH:(h0 + 2) * SWA_DH]
        scores = [lax.dot_general(qs, kk, nt, preferred_element_type=F32) for kk in ks]
        for _ in range(SWA_PIPE):
            yield
        acc = None
        inv = None
        for half in range(2):
            sink = sink_ref[h0 + half]
            sc = scores[half]
            s = jnp.where(take_cur, sc[:, SWA_BLOCK:], sc[:, :SWA_BLOCK])
            s = jnp.where(valid, s, -jnp.inf)
            m = jnp.maximum(jnp.max(s, axis=-1, keepdims=True), sink)
            p = jnp.exp(s - m)
            den = jnp.sum(p, axis=-1, keepdims=True) + jnp.exp(sink - m)
            p2 = jnp.concatenate([jnp.where(take_cur, 0.0, p), jnp.where(take_cur, p, 0.0)], axis=1)
            pv = jnp.dot(p2.astype(BF16), vs[half], preferred_element_type=F32)
            acc = pv if acc is None else acc + pv
            r = 1.0 / den
            inv = r if inv is None else jnp.where(lane_q < SWA_DH, inv, r)
        yield
        o_ref[:, h0 * SWA_DH:(h0 + 2) * SWA_DH] = (acc * inv).astype(BF16)

    chains = []
    for slab in range(SWA_KV_HEADS // 2):
        k2 = kcat[:, slab * LANES:(slab + 1) * LANES].astype(F32) * scale
        v2 = vcat[:, slab * LANES:(slab + 1) * LANES].astype(F32)
        k2r = pltpu.roll(k2, SWA_DH, 1)
        v2r = pltpu.roll(v2, SWA_DH, 1)
        for sub in range(2):
            kvh = slab * 2 + sub
            if sub == 0:
                k_lo = jnp.where(lane_kv < SWA_DH, k2, 0.0)
                k_hi = jnp.where(lane_kv >= SWA_DH, k2r, 0.0)
                v_lo = jnp.where(lane_kv < SWA_DH, v2, 0.0)
                v_hi = jnp.where(lane_kv >= SWA_DH, v2r, 0.0)
            else:
                k_lo = jnp.where(lane_kv < SWA_DH, k2r, 0.0)
                k_hi = jnp.where(lane_kv >= SWA_DH, k2, 0.0)
                v_lo = jnp.where(lane_kv < SWA_DH, v2r, 0.0)
                v_hi = jnp.where(lane_kv >= SWA_DH, v2, 0.0)
            k_lo, k_hi, v_lo, v_hi = (t.astype(BF16) for t in (k_lo, k_hi, v_lo, v_hi))
            for gp in range(SWA_GROUP // 2):
                h0 = kvh * SWA_GROUP + 2 * gp
                chains.append(head_pair(h0, (k_lo, k_hi), (v_lo, v_hi)))

    live = []
    pending = list(chains)
    while pending or live:
        if pending:
            live.append(pending.pop(0))
        nxt = []
        for c in live:
            try:
                next(c)
                nxt.append(c)
            except StopIteration:
                pass
        live = nxt


def _swa(proj, sinks):
    nb = SEQ // SWA_BLOCK
    qcol = 4
    kcol = (7 * D_MODEL) // 256
    vcol = kcol + 1
    cur = lambda col: pl.BlockSpec((SWA_BLOCK, 256), lambda b, i, col=col: (b * nb + i, col))
    prev = lambda col: pl.BlockSpec(
        (SWA_BLOCK, 256), lambda b, i, col=col: (b * nb + jnp.maximum(i - 1, 0), col))
    return pl.pallas_call(
        _swa_body,
        grid=(BATCH, nb),
        in_specs=[
            pl.BlockSpec(memory_space=pltpu.SMEM),
            pl.BlockSpec((SWA_BLOCK, D_MODEL), lambda b, i: (b * nb + i, qcol)),
            cur(kcol), prev(kcol), cur(vcol), prev(vcol),
        ],
        out_specs=pl.BlockSpec((SWA_BLOCK, D_MODEL), lambda b, i: (b * nb + i, 0)),
        out_shape=jax.ShapeDtypeStruct((TOKENS, D_MODEL), BF16),
        compiler_params=_cparams(("parallel", "parallel")),
        name="swa",
    )(sinks, proj, proj, proj, proj, proj)


OUT_TM = 512
OUT_SUB = 256


def _outproj_body(x_ref, yg_ref, ys_ref, gg_ref, gs_ref, w_ref, nf_ref, wr_ref, rb_ref,
                  x1_ref, hp_ref, lg_ref):
    def sub_tile(t):
        rows = pl.ds(t * OUT_SUB, OUT_SUB)
        merged = (_sigmoid(gg_ref[rows, :].astype(F32)) * yg_ref[rows, :].astype(F32)
                  + _sigmoid(gs_ref[rows, :].astype(F32)) * ys_ref[rows, :].astype(F32))
        acc = jnp.dot(merged.astype(BF16), w_ref[...], preferred_element_type=F32)
        yield
        x1 = x_ref[rows, :] + acc
        x1_ref[rows, :] = x1
        ms = jnp.mean(x1 * x1, axis=-1, keepdims=True)
        h = x1 * lax.rsqrt(ms + EPS) * nf_ref[...]
        hh = h.astype(BF16)
        r = jnp.dot(hh, wr_ref[...], preferred_element_type=F32)
        lg_ref[rows, :] = r[:, :LANES] + r[:, LANES:] + rb_ref[...]
        half = D_MODEL // 2
        hf = hh.astype(F32)
        wa = pltpu.bitcast(hf[:, :half], U32)
        wb = pltpu.bitcast(hf[:, half:], U32)
        hp_ref[rows, :] = (wa & jnp.uint32(0xFFFF0000)) | (wb >> 16)

    tiles = [sub_tile(t) for t in range(OUT_TM // OUT_SUB)]
    for _ in range(2):
        for g in tiles:
            next(g, None)


def _outproj(x2d, y_gdn, y_swa, proj, w_out, norm_ffn, wr2, r_bias):
    row = lambda w: pl.BlockSpec((OUT_TM, w), lambda i: (i, 0))
    const = lambda s: pl.BlockSpec(s, lambda i: (0, 0))
    return pl.pallas_call(
        _outproj_body,
        grid=(TOKENS // OUT_TM,),
        in_specs=[
            row(D_MODEL), row(D_MODEL), row(D_MODEL),
            pl.BlockSpec((OUT_TM, D_MODEL), lambda i: (i, 5)),
            pl.BlockSpec((OUT_TM, D_MODEL), lambda i: (i, 6)),
            pl.BlockSpec((D_MODEL, D_MODEL), lambda i: (0, 0), pipeline_mode=pl.Buffered(1)),
            const((1, D_MODEL)),
            const((D_MODEL, 2 * LANES)), const((1, LANES)),
        ],
        out_specs=[row(D_MODEL), row(D_MODEL // 2), row(LANES)],
        out_shape=[
            jax.ShapeDtypeStruct((TOKENS, D_MODEL), F32),
            jax.ShapeDtypeStruct((TOKENS, D_MODEL // 2), U32),
            jax.ShapeDtypeStruct((TOKENS, LANES), F32),
        ],
        compiler_params=_cparams(("parallel",)),
        name="outproj",
    )(x2d, y_gdn, y_swa, proj, proj, w_out, norm_ffn, wr2, r_bias)


ROUTE_TM = 1024


def _route_body(lg_ref, tri_ref, idx_ref, wt_ref, cnt_ref, run_ref):
    @pl.when(pl.program_id(0) == 0)
    def _():
        run_ref[...] = jnp.zeros_like(run_ref)

    lg = lg_ref[...]
    lane = lax.broadcasted_iota(I32, lg.shape, 1)
    ninf = -jnp.inf
    big = jnp.int32(LANES)
    is_g = lane < N_GROUPS
    glog = jnp.where(is_g, lg, ninf)
    gmax = jnp.max(glog, axis=-1, keepdims=True)
    gden = jnp.sum(jnp.where(is_g, jnp.exp(lg - gmax), 0.0), axis=-1, keepdims=True)
    p_sel = 1.0 / gden
    grp = jnp.min(jnp.where(glog == gmax, lane, big), axis=-1, keepdims=True)
    emask = (lane >= N_GROUPS) & (lane < N_GROUPS + N_EXPERTS) & (((lane - N_GROUPS) >> 3) == grp)
    el = jnp.where(emask, lg, ninf)
    v1 = jnp.max(el, axis=-1, keepdims=True)
    i1 = jnp.min(jnp.where(el == v1, lane, big), axis=-1, keepdims=True)
    el2 = jnp.where(lane == i1, ninf, el)
    v2 = jnp.max(el2, axis=-1, keepdims=True)
    i2 = jnp.min(jnp.where(el2 == v2, lane, big), axis=-1, keepdims=True)
    e = jnp.exp(v2 - v1)
    w1 = p_sel / (1.0 + e)
    w2 = p_sel * e / (1.0 + e)
    e0 = i1 - N_GROUPS
    e1 = i2 - N_GROUPS

    oh0 = lane == e0
    oh1 = lane == e1
    onehot = jnp.where(oh0 | oh1, 1.0, 0.0)
    prefix = jnp.dot(tri_ref[...], onehot.astype(BF16), preferred_element_type=F32) + run_ref[0:1, :]
    r0 = jnp.sum(jnp.where(oh0, prefix, 0.0), axis=-1, keepdims=True).astype(I32)
    r1 = jnp.sum(jnp.where(oh1, prefix, 0.0), axis=-1, keepdims=True).astype(I32)
    run = run_ref[0:1, :] + jnp.sum(onehot, axis=0, keepdims=True)
    run_ref[...] = jnp.broadcast_to(run, run_ref.shape)
    cnt_ref[...] = jnp.broadcast_to(run, cnt_ref.shape).astype(I32)

    zi = jnp.zeros(lg.shape, I32)
    idx = jnp.where(lane == 0, e0, zi)
    idx = jnp.where(lane == 1, e1, idx)
    idx = jnp.where(lane == 2, r0, idx)
    idx = jnp.where(lane == 3, r1, idx)
    idx_ref[...] = idx
    wt_ref[...] = jnp.where(lane == 0, w1, jnp.where(lane == 1, w2, 0.0))


def _route(logits, tri):
    row = pl.BlockSpec((ROUTE_TM, LANES), lambda i: (i, 0))
    return pl.pallas_call(
        _route_body,
        grid=(TOKENS // ROUTE_TM,),
        in_specs=[row, pl.BlockSpec((ROUTE_TM, ROUTE_TM), lambda i: (0, 0))],
        out_specs=[row, row, pl.BlockSpec((8, LANES), lambda i: (0, 0))],
        out_shape=[
            jax.ShapeDtypeStruct((TOKENS, LANES), I32),
            jax.ShapeDtypeStruct((TOKENS, LANES), F32),
            jax.ShapeDtypeStruct((8, LANES), I32),
        ],
        scratch_shapes=[pltpu.VMEM((8, LANES), F32)],
        compiler_params=_cparams(("arbitrary",)),
        name="route",
    )(logits, tri)


DISP_TM = 256
N_DISP = TOKENS // DISP_TM
DISP_RING = 3


def _dispatch_body(meta_ref, dest_ref, h_hbm, xs_ref, zbuf, hbuf, in_sem, out_sem, zsem):
    @pl.when(pl.program_id(0) == 0)
    def _():
        zbuf[...] = jnp.zeros_like(zbuf)

        def zero_block(row0):
            return pltpu.make_async_copy(zbuf, xs_ref.at[pl.ds(row0, MOE_BLOCK), :], zsem)

        def per_expert(e, n):
            has = meta_ref[N_EXPERTS + e] > 0

            @pl.when(has)
            def _():
                zero_block(pl.multiple_of(meta_ref[e] - MOE_BLOCK, MOE_BLOCK)).start()

            return n + has.astype(I32)

        n_last = lax.fori_loop(0, N_EXPERTS, per_expert, jnp.int32(0))
        used = meta_ref[2 * N_EXPERTS]

        def tail(j, c):
            zero_block(pl.multiple_of(j * MOE_BLOCK, MOE_BLOCK)).start()
            return c

        lax.fori_loop(used, N_BLOCKS, tail, 0)

        def drain(i, c):
            zero_block(0).wait()
            return c

        lax.fori_loop(0, n_last + (N_BLOCKS - used), drain, 0)

    i = pl.program_id(0)
    b = lax.rem(i, DISP_RING)

    def load(step, slot):
        return pltpu.make_async_copy(
            h_hbm.at[pl.ds(pl.multiple_of(step * DISP_TM, DISP_TM), DISP_TM), :], hbuf.at[slot], in_sem.at[slot])

    def wait_scatters(slot):
        for k in range(TOP_K):
            pltpu.make_async_copy(hbuf.at[slot], xs_ref.at[pl.ds(0, DISP_TM), :], out_sem.at[slot]).wait()

    @pl.when(i == 0)
    def _():
        load(0, 0).start()

    load(i, b).wait()

    @pl.when(i + 1 < N_DISP)
    def _():
        load(i + 1, lax.rem(i + 1, DISP_RING)).start()

    def issue(r, c):
        for k in range(TOP_K):
            pltpu.make_async_copy(hbuf.at[b, pl.ds(r, 1), :], xs_ref.at[pl.ds(dest_ref[TOP_K * r + k], 1), :],
                                  out_sem.at[b]).start()
        return c

    lax.fori_loop(0, DISP_TM, issue, 0, unroll=4)

    @pl.when(i >= 1)
    def _():
        wait_scatters(lax.rem(i + DISP_RING - 1, DISP_RING))

    @pl.when(i == N_DISP - 1)
    def _():
        wait_scatters(b)


def _dispatch(meta, dest_flat, hp):
    grid_spec = pltpu.PrefetchScalarGridSpec(
        num_scalar_prefetch=1,
        grid=(N_DISP,),
        in_specs=[
            pl.BlockSpec((DISP_TM * TOP_K,), lambda i, m: (i,), memory_space=pltpu.SMEM),
            pl.BlockSpec(memory_space=pl.ANY),
        ],
        out_specs=pl.BlockSpec(memory_space=pl.ANY),
        scratch_shapes=[
            pltpu.VMEM((MOE_BLOCK, D_MODEL // 2), U32),
            pltpu.VMEM((DISP_RING, DISP_TM, D_MODEL // 2), U32),
            pltpu.SemaphoreType.DMA((DISP_RING,)),
            pltpu.SemaphoreType.DMA((DISP_RING,)),
            pltpu.SemaphoreType.DMA,
        ],
    )
    return pl.pallas_call(
        _dispatch_body,
        grid_spec=grid_spec,
        out_shape=jax.ShapeDtypeStruct((N_SLOTS, D_MODEL // 2), U32),
        compiler_params=_cparams(("arbitrary",)),
        name="dispatch",
    )(meta, dest_flat, hp)


def _slots_body(idx_ref, start_ref, dest_ref):
    idx = idx_ref[...]
    lane = lax.broadcasted_iota(I32, idx.shape, 1)
    start = start_ref[...]
    out = jnp.zeros(idx.shape, I32)
    for k in range(TOP_K):
        e = idx[:, k:k + 1]
        base = jnp.sum(jnp.where(lane == e, start, 0.0), axis=-1, keepdims=True).astype(I32)
        out = jnp.where(lane == k, base + idx[:, TOP_K + k:TOP_K + k + 1], out)
    dest_ref[...] = out


SLOTS_TM = 2048


def _slots(idx, pad_start_row):
    row = pl.BlockSpec((SLOTS_TM, LANES), lambda i: (i, 0))
    return pl.pallas_call(
        _slots_body,
        grid=(TOKENS // SLOTS_TM,),
        in_specs=[row, pl.BlockSpec((1, LANES), lambda i: (0, 0))],
        out_specs=row,
        out_shape=jax.ShapeDtypeStruct((TOKENS, LANES), I32),
        compiler_params=_cparams(("parallel",)),
        name="slots",
    )(idx, pad_start_row)


def _experts_body(sched_ref, x_ref, wg_hbm, wu_hbm, wd_hbm, y_ref,
                  wg_st, wu_st, wd_st, wg_bf, wu_bf, wd_bf, sem):
    j = pl.program_id(0)
    e = sched_ref[j]
    slot = sched_ref[N_BLOCKS + j]
    nxt = sched_ref[2 * N_BLOCKS + j]
    used = sched_ref[3 * N_BLOCKS]
    new_expert = (j == 0) | (e != sched_ref[jnp.maximum(j - 1, 0)])

    def weight_copies(expert, s):
        return (pltpu.make_async_copy(wg_hbm.at[expert], wg_st.at[s], sem.at[s, 0]),
                pltpu.make_async_copy(wu_hbm.at[expert], wu_st.at[s], sem.at[s, 1]),
                pltpu.make_async_copy(wd_hbm.at[expert], wd_st.at[s], sem.at[s, 2]))

    @pl.when(j == 0)
    def _():
        for c in weight_copies(e, slot):
            c.start(priority=1)

    @pl.when(new_expert)
    def _():
        for c in weight_copies(e, slot):
            c.wait()

        @pl.when(nxt >= 0)
        def _():
            for c in weight_copies(nxt, 1 - slot):
                c.start(priority=1)

        wg_bf[...] = wg_st[slot].astype(BF16)
        wu_bf[...] = wu_st[slot].astype(BF16)
        wd_bf[...] = wd_st[slot].astype(BF16)

    @pl.when(j < used)
    def _():
        w = x_ref[...]
        xa = pltpu.bitcast(w & jnp.uint32(0xFFFF0000), F32).astype(BF16)
        xb = pltpu.bitcast(w << 16, F32).astype(BF16)
        x = jnp.concatenate([xa, xb], axis=1)
        g = jnp.dot(x, wg_bf[...], preferred_element_type=F32)
        u = jnp.dot(x, wu_bf[...], preferred_element_type=F32)
        hb = (_silu(g) * u).astype(BF16)
        y = jnp.dot(hb, wd_bf[...], preferred_element_type=F32)
        half = D_MODEL // 2
        ya = pltpu.bitcast(y[:, :half].astype(BF16).astype(F32), U32)
        yb = pltpu.bitcast(y[:, half:].astype(BF16).astype(F32), U32)
        y_ref[...] = ya | (yb >> 16)

    @pl.when(j >= used)
    def _():
        y_ref[...] = jnp.zeros_like(y_ref)


def _experts(sched, xs, w_gate, w_up, w_down):
    grid_spec = pltpu.PrefetchScalarGridSpec(
        num_scalar_prefetch=1,
        grid=(N_BLOCKS,),
        in_specs=[
            pl.BlockSpec((MOE_BLOCK, D_MODEL // 2), lambda j, s: (j, 0)),
            pl.BlockSpec(memory_space=pl.ANY),
            pl.BlockSpec(memory_space=pl.ANY),
            pl.BlockSpec(memory_space=pl.ANY),
        ],
        out_specs=pl.BlockSpec((MOE_BLOCK, D_MODEL // 2), lambda j, s: (j, 0)),
        scratch_shapes=[
            pltpu.VMEM((2, D_MODEL, D_EXPERT), F32),
            pltpu.VMEM((2, D_MODEL, D_EXPERT), F32),
            pltpu.VMEM((2, D_EXPERT, D_MODEL), F32),
            pltpu.VMEM((D_MODEL, D_EXPERT), BF16),
            pltpu.VMEM((D_MODEL, D_EXPERT), BF16),
            pltpu.VMEM((D_EXPERT, D_MODEL), BF16),
            pltpu.SemaphoreType.DMA((2, 3)),
        ],
    )
    return pl.pallas_call(
        _experts_body,
        grid_spec=grid_spec,
        out_shape=jax.ShapeDtypeStruct((N_SLOTS, D_MODEL // 2), U32),
        compiler_params=_cparams(("arbitrary",)),
        name="experts",
    )(sched, xs, w_gate, w_up, w_down)


COMB_TM = 256
N_COMB = TOKENS // COMB_TM


def _unpack_pairs(w):
    hi = pltpu.bitcast(w & jnp.uint32(0xFFFF0000), F32)
    lo = pltpu.bitcast(w << 16, F32)
    return jnp.concatenate([hi, lo], axis=1)


def _combine_body(dest_ref, dest_next_ref, wt_ref, x1_ref, gain_ref, ys_ref, o_ref, buf, sem):
    i = pl.program_id(0)
    slot = lax.rem(i, 2)

    def start_gathers(dref, s):
        def issue(r, c):
            for k in range(TOP_K):
                pltpu.make_async_copy(ys_ref.at[pl.ds(dref[TOP_K * r + k], 1), :],
                                      buf.at[s, k, pl.ds(r, 1), :], sem.at[s]).start()
            return c
        lax.fori_loop(0, COMB_TM, issue, 0, unroll=4)

    @pl.when(i == 0)
    def _():
        start_gathers(dest_ref, slot)

    @pl.when(i + 1 < N_COMB)
    def _():
        start_gathers(dest_next_ref, 1 - slot)

    for k in range(TOP_K):
        pltpu.make_async_copy(ys_ref.at[pl.ds(0, COMB_TM), :], buf.at[slot, k], sem.at[slot]).wait()

    wt = wt_ref[...]
    y = (x1_ref[...] + wt[:, 0:1] * _unpack_pairs(buf[slot, 0]) + wt[:, 1:2] * _unpack_pairs(buf[slot, 1]))
    ms = jnp.mean(y * y, axis=-1, keepdims=True)
    o_ref[...] = y * lax.rsqrt(ms + EPS) * gain_ref[...]


def _combine(dest_flat, wts, x1, norm_final, ys):
    dblk = lambda f: pl.BlockSpec((COMB_TM * TOP_K,), f, memory_space=pltpu.SMEM)
    return pl.pallas_call(
        _combine_body,
        grid=(N_COMB,),
        in_specs=[
            dblk(lambda i: (i,)),
            dblk(lambda i: (jnp.minimum(i + 1, N_COMB - 1),)),
            pl.BlockSpec((COMB_TM, LANES), lambda i: (i, 0)),
            pl.BlockSpec((COMB_TM, D_MODEL), lambda i: (i, 0)),
            pl.BlockSpec((1, D_MODEL), lambda i: (0, 0)),
            pl.BlockSpec(memory_space=pl.ANY),
        ],
        out_specs=pl.BlockSpec((COMB_TM, D_MODEL), lambda i: (i, 0)),
        out_shape=jax.ShapeDtypeStruct((TOKENS, D_MODEL), F32),
        scratch_shapes=[pltpu.VMEM((2, TOP_K, COMB_TM, D_MODEL // 2), U32), pltpu.SemaphoreType.DMA((2,))],
        compiler_params=_cparams(("arbitrary",)),
        name="combine",
    )(dest_flat, dest_flat, wts, x1, norm_final, ys)


def _pad_lanes(v):
    v = v.reshape(1, -1).astype(F32)
    return jnp.pad(v, ((0, 0), (0, LANES - v.shape[1])))


def kernel(x, norm_mix, w_in, conv_w, gdn_a_log, gdn_dt_bias, gdn_norm, swa_sinks, w_out, norm_ffn,
           w_router_group, b_router_group, w_router_expert, b_router_expert, w_gate, w_up, w_down,
           norm_final):
    l = 0
    x2d = x.reshape(TOKENS, D_MODEL)
    w = w_in[l]
    o_z = 3 * D_MODEL
    o_a = o_z + D_MODEL
    o_sq = o_a + 2 * GDN_HEADS
    o_sk = o_sq + D_MODEL
    o_sv = o_sk + SWA_KV_HEADS * SWA_DH
    o_gg = o_sv + SWA_KV_HEADS * SWA_DH
    o_gs = o_gg + D_MODEL
    w_main = jnp.concatenate(
        [w[:, :o_a], w[:, o_sq:o_sk], w[:, o_gg:o_gs], w[:, o_gs:], w[:, o_sk:o_sv], w[:, o_sv:o_gg]],
        axis=1).astype(BF16)
    w_ab = jnp.pad(w[:, o_a:o_sq], ((0, 0), (0, LANES - 2 * GDN_HEADS))).astype(BF16)

    proj, ab = _inproj(x2d, norm_mix[l].reshape(1, D_MODEL), w_main, w_ab, conv_w[l])

    g1, gt = _gdn_prep(ab, _pad_lanes(gdn_a_log[l]), _pad_lanes(gdn_dt_bias[l]))
    gr = gt[:, :GDN_HEADS, :].reshape(BATCH, GDN_HEADS, N_GROUPS_SEQ, GDN_GROUP_ROWS)
    y_gdn = _gdn(proj, g1, gr, gdn_norm[l].reshape(1, GDN_DV))

    y_swa = _swa(proj, swa_sinks[l].astype(F32))

    w_r = jnp.concatenate([w_router_group[l], w_router_expert[l]], axis=1).astype(F32)
    w_r = jnp.pad(w_r, ((0, 0), (0, LANES - w_r.shape[1])))
    wr_hi = w_r.astype(BF16)
    wr_lo = (w_r - wr_hi.astype(F32)).astype(BF16)
    r_bias = _pad_lanes(jnp.concatenate([b_router_group[l], b_router_expert[l]]))
    x1, hp, logits = _outproj(x2d, y_gdn, y_swa, proj, w_out[l].astype(BF16),
                              norm_ffn[l].reshape(1, D_MODEL), jnp.concatenate([wr_hi, wr_lo], axis=1), r_bias)

    tri = (lax.broadcasted_iota(I32, (ROUTE_TM, ROUTE_TM), 1)
           < lax.broadcasted_iota(I32, (ROUTE_TM, ROUTE_TM), 0)).astype(BF16)
    idx, wts, cnt = _route(logits, tri)

    counts = cnt[0, :N_EXPERTS]
    padded = (counts + MOE_BLOCK - 1) // MOE_BLOCK * MOE_BLOCK
    pad_end = jnp.cumsum(padded)
    pad_start = pad_end - padded
    dest = _slots(idx, _pad_lanes(pad_start))
    dest_flat = dest[:, :TOP_K].reshape(N_ASSIGN)
    blk_pos = jnp.arange(N_BLOCKS, dtype=I32) * MOE_BLOCK
    blk_e = jnp.minimum(jnp.sum((pad_end[None, :] <= blk_pos[:, None]).astype(I32), axis=1), N_EXPERTS - 1)
    used = pad_end[-1:] // MOE_BLOCK
    meta = jnp.concatenate([pad_end, padded, used]).astype(I32)
    is_new = jnp.concatenate([jnp.ones((1,), I32), (blk_e[1:] != blk_e[:-1]).astype(I32)])
    ordinal = jnp.cumsum(is_new) - 1
    nxt_pos = jnp.sum((ordinal[None, :] <= ordinal[:, None]).astype(I32), axis=1)
    nxt = jnp.where(nxt_pos < N_BLOCKS, blk_e[jnp.minimum(nxt_pos, N_BLOCKS - 1)], -1)
    sched = jnp.concatenate([blk_e, ordinal & 1, nxt, used]).astype(I32)

    xs = _dispatch(meta, dest_flat, hp)
    ys = _experts(sched, xs, w_gate[l], w_up[l], w_down[l])
    out = _combine(dest_flat, wts, x1, norm_final.reshape(1, D_MODEL), ys)
    return out.reshape(BATCH, SEQ, D_MODEL)
```

```python
import math

import jax
import jax.numpy as jnp
from jax import lax
from jax.experimental import pallas as pl
from jax.experimental.pallas import tpu as pltpu

F32 = jnp.float32
BF16 = jnp.bfloat16
I32 = jnp.int32
U32 = jnp.uint32

D_MODEL = 2048
BATCH = 16
SEQ = 2048
TOKENS = BATCH * SEQ
EPS = 1e-6

GDN_HEADS = 16
GDN_DK = 128
GDN_DV = 128
GDN_CHUNK = 64
CONV_W = 4
GDN_GROUP = 4
GDN_GROUP_ROWS = GDN_GROUP * GDN_CHUNK
N_CHUNKS = SEQ // GDN_CHUNK
N_GROUPS_SEQ = SEQ // GDN_GROUP_ROWS

SWA_HEADS = 32
SWA_KV_HEADS = 4
SWA_DH = 64
SWA_GROUP = 8
SWA_BLOCK = 128
WINDOW = 128

N_GROUPS = 8
EXPERTS_PER_GROUP = 8
N_EXPERTS = 64
TOP_K = 2
D_EXPERT = 512
MOE_BLOCK = 256
N_ASSIGN = TOKENS * TOP_K
N_BLOCKS = N_ASSIGN // MOE_BLOCK + N_EXPERTS
N_SLOTS = N_BLOCKS * MOE_BLOCK

PROJ_DIM = 7 * D_MODEL + 2 * SWA_KV_HEADS * SWA_DH
LANES = 128

VMEM_LIMIT = 56 * 1024 * 1024


def _cparams(sem, vmem=VMEM_LIMIT):
    return pltpu.CompilerParams(dimension_semantics=sem, vmem_limit_bytes=vmem)


TILE_ROWS = (D_MODEL // 2) // LANES


def _store_token_tiles(ref, first_token, words):
    n = words.shape[0]
    for s in range(TILE_ROWS):
        ref[pl.ds(first_token * TILE_ROWS + s, n, stride=TILE_ROWS), :] = words[:, s * LANES:(s + 1) * LANES]


def _load_token_tiles(ref, n):
    return jnp.concatenate([ref[pl.ds(s, n, stride=TILE_ROWS), :] for s in range(TILE_ROWS)], axis=1)


def _sigmoid(x):
    return 0.5 * jnp.tanh(0.5 * x) + 0.5


def _silu(x):
    h = 0.5 * x
    return h * jnp.tanh(h) + h


INPROJ_TM = 2048
INPROJ_TN = 512
INPROJ_VMEM = 60 * 1024 * 1024


INPROJ_RC = 256
INPROJ_RING = 3
QKV_TILES = 3 * D_MODEL // INPROJ_TN
QK_TILES = 2 * D_MODEL // INPROJ_TN
Q_TILES = D_MODEL // INPROJ_TN
assert INPROJ_TM == SEQ


def _inproj_body(x_ref, g_ref, w_ref, wab_ref, cw_ref, o_ref, ab_ref, h_ref, cpad):
    j = pl.program_id(1)

    @pl.when(j == 0)
    def _():
        def chunk(i, c):
            r = pl.ds(pl.multiple_of(i * 128, 128), 128)
            x = x_ref[r, :]
            ms = jnp.mean(x * x, axis=-1, keepdims=True)
            h_ref[r, :] = (x * lax.rsqrt(ms + EPS) * g_ref[...]).astype(BF16)
            return c
        lax.fori_loop(0, INPROJ_TM // 128, chunk, 0)
        ab_ref[...] = jnp.dot(h_ref[...], wab_ref[...], preferred_element_type=F32)

    @pl.when(j >= QKV_TILES)
    def _():
        o_ref[...] = jnp.dot(h_ref[...], w_ref[...], preferred_element_type=F32).astype(BF16)

    @pl.when(j < QKV_TILES)
    def _():
        cw = cw_ref[...]
        is_qk = j < QK_TILES
        qscale = jnp.where(j < Q_TILES, GDN_DK ** -0.5, 1.0).astype(F32)
        n_chunks = INPROJ_TM // INPROJ_RC

        def matmul(c):
            rows = pl.ds(c * INPROJ_RC, INPROJ_RC)
            cpad[c % INPROJ_RING, pl.ds(8, INPROJ_RC), :] = jnp.dot(
                h_ref[rows, :], w_ref[...], preferred_element_type=F32)

        def epilogue(c):
            buf = cpad.at[c % INPROJ_RING]
            if c == 0:
                buf[pl.ds(0, 8), :] = jnp.zeros((8, INPROJ_TN), F32)
            else:
                buf[pl.ds(0, 8), :] = cpad[(c - 1) % INPROJ_RING, pl.ds(INPROJ_RC, 8), :]
            y = None
            for s in range(CONV_W):
                t = buf[pl.ds(8 - s, INPROJ_RC), :] * cw[CONV_W - 1 - s:CONV_W - s, :]
                y = t if y is None else y + t
            y = _silu(y)
            parts = []
            for g in range(INPROJ_TN // LANES):
                yg = y[:, g * LANES:(g + 1) * LANES]
                inv = lax.rsqrt(jnp.sum(yg * yg, -1, keepdims=True) + EPS) * qscale
                parts.append(yg * jnp.where(is_qk, inv, 1.0))
            o_ref[pl.ds(c * INPROJ_RC, INPROJ_RC), :] = jnp.concatenate(parts, axis=1).astype(BF16)

        matmul(0)
        for c in range(n_chunks):
            if c + 1 < n_chunks:
                matmul(c + 1)
            epilogue(c)


def _inproj(x2d, gain, w_main, w_ab, conv_w):
    grid = (TOKENS // INPROJ_TM, PROJ_DIM // INPROJ_TN)
    return pl.pallas_call(
        _inproj_body,
        grid=grid,
        in_specs=[
            pl.BlockSpec((INPROJ_TM, D_MODEL), lambda i, j: (i, 0)),
            pl.BlockSpec((1, D_MODEL), lambda i, j: (0, 0)),
            pl.BlockSpec((D_MODEL, INPROJ_TN), lambda i, j: (0, j)),
            pl.BlockSpec((D_MODEL, LANES), lambda i, j: (0, 0)),
            pl.BlockSpec((CONV_W, INPROJ_TN), lambda i, j: (0, jnp.minimum(j, QKV_TILES - 1))),
        ],
        out_specs=[
            pl.BlockSpec((INPROJ_TM, INPROJ_TN), lambda i, j: (i, j)),
            pl.BlockSpec((INPROJ_TM, LANES), lambda i, j: (i, 0)),
        ],
        out_shape=[
            jax.ShapeDtypeStruct((TOKENS, PROJ_DIM), BF16),
            jax.ShapeDtypeStruct((TOKENS, LANES), F32),
        ],
        scratch_shapes=[pltpu.VMEM((INPROJ_TM, D_MODEL), BF16),
                        pltpu.VMEM((INPROJ_RING, 8 + INPROJ_RC, INPROJ_TN), F32)],
        compiler_params=_cparams(("parallel", "arbitrary"), vmem=INPROJ_VMEM),
        name="inproj",
    )(x2d, gain, w_main, w_ab, conv_w)


def _gdn_prep_body(ab_ref, alog_ref, dtb_ref, g1_ref, gt_ref):
    ab = ab_ref[...]
    lane = lax.broadcasted_iota(I32, ab.shape, 1)
    row = lax.broadcasted_iota(I32, ab.shape, 0) % GDN_CHUNK
    xa = ab + dtb_ref[...]
    softplus = jnp.maximum(xa, 0.0) + jnp.log(1.0 + jnp.exp(-jnp.abs(xa)))
    g = jnp.where(lane < GDN_HEADS, -jnp.exp(alog_ref[...]) * softplus, 0.0)
    gam = g
    s = 1
    while s < GDN_CHUNK:
        gam = gam + jnp.where(row >= s, pltpu.roll(gam, s, 0), 0.0)
        s *= 2
    g1_ref[...] = jnp.where(lane < GDN_HEADS, gam, _sigmoid(ab))
    gt_ref[0] = gam.T


def _gdn_prep(ab, alog_pad, dtb_pad):
    return pl.pallas_call(
        _gdn_prep_body,
        grid=(BATCH,),
        in_specs=[
            pl.BlockSpec((SEQ, LANES), lambda b: (b, 0)),
            pl.BlockSpec((1, LANES), lambda b: (0, 0)),
            pl.BlockSpec((1, LANES), lambda b: (0, 0)),
        ],
        out_specs=[
            pl.BlockSpec((SEQ, LANES), lambda b: (b, 0)),
            pl.BlockSpec((1, LANES, SEQ), lambda b: (b, 0, 0)),
        ],
        out_shape=[
            jax.ShapeDtypeStruct((TOKENS, LANES), F32),
            jax.ShapeDtypeStruct((BATCH, LANES, SEQ), F32),
        ],
        compiler_params=_cparams(("parallel",)),
        name="gdn_prep",
    )(ab, alog_pad, dtb_pad)


GDN_HB = 4
GDN_PAIR = 2
N_PAIRS = N_GROUPS_SEQ // GDN_PAIR
PAIR_CHUNKS = GDN_PAIR * GDN_GROUP
PAIR_ROWS = GDN_PAIR * GDN_GROUP_ROWS
GDN_VMEM = 59 * 1024 * 1024


def _block_diag(x):
    t = jnp.concatenate([x] * GDN_GROUP, axis=0)
    rb = lax.broadcasted_iota(I32, t.shape, 0) // GDN_CHUNK
    cb = lax.broadcasted_iota(I32, t.shape, 1) // GDN_CHUNK
    return jnp.where(rb == cb, t, jnp.zeros_like(t))


def _sbs_product(lhs_list, y):
    bd = _block_diag(y.astype(BF16))
    xs = [x.astype(BF16) for x in lhs_list]
    lhs = xs[0] if len(xs) == 1 else jnp.concatenate(xs, axis=0)
    r = jnp.dot(lhs, bd, preferred_element_type=F32)
    c = GDN_CHUNK
    return [r[i * c:(i + 1) * c] for i in range(len(xs))]


def _gdn_body(q_ref, k_ref, v_ref, z_ref, g1_ref, gr_ref, gain_ref,
              o_ref,
              kdbf, rhsbf, qd, gamb, betab, o0s, qts, vs, wps):
    hg = pl.program_id(1)
    shape = (PAIR_ROWS, LANES)
    lane = lax.broadcasted_iota(I32, shape, 1)

    def phase0(pair):
        rows = pl.ds(pl.multiple_of(pair * PAIR_ROWS, PAIR_ROWS), PAIR_ROWS)
        g1 = g1_ref[rows, :]
        for hd in range(GDN_HB):
            hs = slice(hd * LANES, (hd + 1) * LANES)
            head = hg * GDN_HB + hd

            def col(off):
                c = jnp.sum(jnp.where(lane == off + head, g1, 0.0), axis=-1, keepdims=True)
                return jnp.broadcast_to(c, shape)

            gam = col(0)
            beta = col(GDN_HEADS)
            gamb[hd, rows, :] = gam
            betab[hd, rows, :] = beta.astype(BF16)
            gam3 = gam.reshape(PAIR_CHUNKS, GDN_CHUNK, LANES)
            glast = jnp.broadcast_to(gam3[:, GDN_CHUNK - 1:GDN_CHUNK, :], gam3.shape).reshape(shape)
            eg = jnp.exp(gam)
            kdf = jnp.exp(glast - gam)

            qd[hd, rows, :] = (q_ref[rows, hs].astype(F32) * eg).astype(BF16)
            k = k_ref[rows, hs].astype(F32)
            kdbf[hd, rows, :] = (k * kdf).astype(BF16)
            rhsbf[hd, rows, LANES:] = (k * (beta * eg)).astype(BF16)
            rhsbf[hd, rows, :LANES] = (v_ref[rows, hs].astype(F32) * beta).astype(BF16)

    gshape = (GDN_CHUNK, GDN_GROUP_ROWS)
    ii = lax.broadcasted_iota(I32, gshape, 0)
    jj = lax.broadcasted_iota(I32, gshape, 1) % GDN_CHUNK
    eye = jnp.where(ii == jj, 1.0, 0.0).astype(F32)
    pr = lax.broadcasted_iota(I32, (2 * GDN_CHUNK, LANES), 0) // GDN_CHUNK
    tl = lax.broadcasted_iota(I32, (LANES, GDN_GROUP_ROWS), 1) // GDN_CHUNK
    lane_c = lax.broadcasted_iota(I32, (GDN_CHUNK, LANES), 1)
    nt = (((1,), (1,)), ((), ()))

    def sbs_bcast(ref, hd, base):
        parts = [ref[hd, pl.ds(base + m * GDN_CHUNK, GDN_CHUNK), :] for m in range(GDN_GROUP)]
        a = jnp.where(lane_c < GDN_CHUNK, parts[0], parts[1])
        b = jnp.where(lane_c < GDN_CHUNK, parts[2], parts[3])
        return jnp.concatenate([a, b], axis=1)

    def phase1(hd, pair, gg):
        g = pair * GDN_PAIR + gg
        base = pl.multiple_of(g * GDN_GROUP_ROWS, GDN_GROUP_ROWS)
        slot = lax.rem(pair, 2)
        res = []
        for p in range(2):
            rows = pl.ds(base + p * 2 * GDN_CHUNK, 2 * GDN_CHUNK)
            kst = k_ref[rows, hd * LANES:(hd + 1) * LANES]
            qst = q_ref[rows, hd * LANES:(hd + 1) * LANES]
            lhs_k = jnp.concatenate([kst[:GDN_CHUNK], kst[GDN_CHUNK:]], axis=1)
            lhs_q = jnp.concatenate([qst[:GDN_CHUNK], qst[GDN_CHUNK:]], axis=1)
            lhs = jnp.concatenate([lhs_q, lhs_k], axis=0)
            zero = jnp.zeros_like(kst)
            bt = jnp.concatenate([jnp.where(pr == 0, kst, zero), jnp.where(pr == 1, kst, zero)], axis=1)
            res.append(lax.dot_general(lhs, bt, nt, preferred_element_type=F32))
        yield
        qk = jnp.concatenate([res[0][:GDN_CHUNK], res[1][:GDN_CHUNK]], axis=1)
        kk = jnp.concatenate([res[0][GDN_CHUNK:], res[1][GDN_CHUNK:]], axis=1)

        gc = sbs_bcast(gamb, hd, base)
        bc = sbs_bcast(betab, hd, base).astype(F32)
        grow = gr_ref[0, hd, pl.ds(g, 1), :]
        decay = jnp.exp(jnp.where(ii >= jj, gc - grow, -jnp.inf))
        a = jnp.where(ii > jj, kk * decay * bc, 0.0)
        qkd = qk * decay

        u = eye - a
        (x,) = _sbs_product([a], a)
        yield
        for lvl in range(1, 6):
            if lvl < 5:
                x2, ux = _sbs_product([x, u], x)
                u = u + ux
                x = x2
            else:
                (ux,) = _sbs_product([u], x)
                u = u + ux
            yield

        rows4 = pl.ds(base, GDN_GROUP_ROWS)
        so = jnp.dot(_block_diag(u.astype(BF16)), rhsbf[hd, rows4, :], preferred_element_type=F32)
        yield
        sol = so.astype(BF16)

        kdt = kdbf[hd, rows4, :].astype(F32).T.astype(BF16)
        zt = jnp.zeros_like(kdt)
        lhs2 = jnp.concatenate(
            [_block_diag(qkd.astype(BF16))] + [jnp.where(tl == m, kdt, zt) for m in range(GDN_GROUP)],
            axis=0)
        r = jnp.dot(lhs2, sol, preferred_element_type=F32)
        srows = pl.ds(pl.multiple_of(slot * PAIR_ROWS + gg * GDN_GROUP_ROWS, GDN_GROUP_ROWS), GDN_GROUP_ROWS)
        o0s[hd, srows, :] = r[:GDN_GROUP_ROWS, :LANES]
        qts[hd, srows, :] = qd[hd, rows4, :].astype(F32) - r[:GDN_GROUP_ROWS, LANES:]
        for m in range(GDN_GROUP):
            blk = r[GDN_GROUP_ROWS + m * LANES:GDN_GROUP_ROWS + (m + 1) * LANES]
            dst = pl.ds(pl.multiple_of((slot * PAIR_CHUNKS + gg * GDN_GROUP + m) * LANES, LANES), LANES)
            vs[hd, dst, :] = blk[:, :LANES]
            wps[hd, dst, :] = blk[:, LANES:]

    gain = gain_ref[...]

    def phase2(hd, pair, j, s):
        c = pair * PAIR_CHUNKS + j
        local = lax.rem(pair, 2) * PAIR_CHUNKS + j
        rows = pl.ds(pl.multiple_of(c * GDN_CHUNK, GDN_CHUNK), GDN_CHUNK)
        lrows = pl.ds(pl.multiple_of(local * GDN_CHUNK, GDN_CHUNK), GDN_CHUNK)
        srow = pl.ds(pl.multiple_of(local * LANES, LANES), LANES)
        lhs = jnp.concatenate([wps[hd, srow, :].astype(BF16), qts[hd, lrows, :].astype(BF16)], axis=0)
        r = jnp.dot(lhs, s.astype(BF16), preferred_element_type=F32)
        o = r[LANES:] + o0s[hd, lrows, :]
        dec = jnp.exp(gamb[hd, pl.ds(c * GDN_CHUNK + GDN_CHUNK - 1, 1), :])
        s_new = s * dec + vs[hd, srow, :] - r[:LANES]
        y = o * lax.rsqrt(jnp.mean(o * o, -1, keepdims=True) + EPS) * gain
        zz = z_ref[rows, hd * LANES:(hd + 1) * LANES].astype(F32)
        o_ref[rows, hd * LANES:(hd + 1) * LANES] = (y * _silu(zz)).astype(BF16)
        return s_new

    def phase2_chain(hd, pair, states):
        s = states[hd]
        for j in range(PAIR_CHUNKS):
            s = phase2(hd, pair, j, s)
            yield
        states[hd] = s

    def run_interleaved(chains):
        live = list(chains)
        while live:
            nxt = []
            for c in live:
                try:
                    next(c)
                    nxt.append(c)
                except StopIteration:
                    pass
            live = nxt

    def phase1_chains(pair):
        return [phase1(hd, pair, gg) for gg in range(GDN_PAIR) for hd in range(GDN_HB)]

    for pair in range(N_PAIRS):
        phase0(pair)
    run_interleaved(phase1_chains(0))

    def body(pair, states):
        states = list(states)
        run_interleaved([phase2_chain(hd, pair - 1, states) for hd in range(GDN_HB)] + phase1_chains(pair))
        return tuple(states)

    s0 = tuple(jnp.zeros((GDN_DK, GDN_DV), F32) for _ in range(GDN_HB))
    states = list(lax.fori_loop(1, N_PAIRS, body, s0))
    run_interleaved([phase2_chain(hd, N_PAIRS - 1, states) for hd in range(GDN_HB)])


def _gdn(proj, g1, gr, gdn_norm):
    w = GDN_HB * LANES
    per_row = D_MODEL // w
    hblk = lambda off: pl.BlockSpec((SEQ, w), lambda b, h, off=off: (b, off * per_row + h))
    big = lambda dt: pltpu.VMEM((GDN_HB, SEQ, LANES), dt)
    return pl.pallas_call(
        _gdn_body,
        grid=(BATCH, GDN_HEADS // GDN_HB),
        in_specs=[
            hblk(0), hblk(1), hblk(2), hblk(3),
            pl.BlockSpec((SEQ, LANES), lambda b, h: (b, 0)),
            pl.BlockSpec((1, GDN_HB, N_GROUPS_SEQ, GDN_GROUP_ROWS), lambda b, h: (b, h, 0, 0)),
            pl.BlockSpec((1, LANES), lambda b, h: (0, 0)),
        ],
        out_specs=pl.BlockSpec((SEQ, w), lambda b, h: (b, h)),
        out_shape=jax.ShapeDtypeStruct((TOKENS, D_MODEL), BF16),
        scratch_shapes=[
            big(BF16), pltpu.VMEM((GDN_HB, SEQ, 2 * LANES), BF16),
            big(BF16), big(F32), big(BF16),
            pltpu.VMEM((GDN_HB, 2 * PAIR_ROWS, LANES), F32),
            pltpu.VMEM((GDN_HB, 2 * PAIR_ROWS, LANES), F32),
            pltpu.VMEM((GDN_HB, 2 * PAIR_CHUNKS * LANES, LANES), F32),
            pltpu.VMEM((GDN_HB, 2 * PAIR_CHUNKS * LANES, LANES), F32),
        ],
        compiler_params=_cparams(("parallel", "parallel"), vmem=GDN_VMEM),
        name="gdn",
    )(proj, proj, proj, proj, g1, gr, gdn_norm)


SWA_PIPE = 3
assert WINDOW == SWA_BLOCK


def _swa_body(sink_ref, q_ref, kc_ref, kp_ref, vc_ref, vp_ref, o_ref):
    i = pl.program_id(1)
    kcat = jnp.concatenate([kp_ref[...], kc_ref[...]], axis=0)
    vcat = jnp.concatenate([vp_ref[...], vc_ref[...]], axis=0)
    qi = lax.broadcasted_iota(I32, (SWA_BLOCK, SWA_BLOCK), 0)
    ki = lax.broadcasted_iota(I32, (SWA_BLOCK, SWA_BLOCK), 1)
    take_cur = ki <= qi
    valid = take_cur | (i > 0)
    lane_kv = lax.broadcasted_iota(I32, (2 * SWA_BLOCK, LANES), 1)
    lane_q = lax.broadcasted_iota(I32, (SWA_BLOCK, LANES), 1)
    nt = (((1,), (1,)), ((), ()))
    scale = SWA_DH ** -0.5
    assert math.frexp(scale)[0] == 0.5

    def head_pair(h0, ks, vs):
        qs = q_ref[:, h0 * SWA_DH:(h0 + 2) * SWA_DH]
        scores = [lax.dot_general(qs, kk, nt, preferred_element_type=F32) for kk in ks]
        for _ in range(SWA_PIPE):
            yield
        acc = None
        inv = None
        for half in range(2):
            sink = sink_ref[h0 + half]
            sc = scores[half]
            s = jnp.where(take_cur, sc[:, SWA_BLOCK:], sc[:, :SWA_BLOCK])
            s = jnp.where(valid, s, -jnp.inf)
            m = jnp.maximum(jnp.max(s, axis=-1, keepdims=True), sink)
            p = jnp.exp(s - m)
            den = jnp.sum(p, axis=-1, keepdims=True) + jnp.exp(sink - m)
            p2 = jnp.concatenate([jnp.where(take_cur, 0.0, p), jnp.where(take_cur, p, 0.0)], axis=1)
            pv = jnp.dot(p2.astype(BF16), vs[half], preferred_element_type=F32)
            acc = pv if acc is None else acc + pv
            r = 1.0 / den
            inv = r if inv is None else jnp.where(lane_q < SWA_DH, inv, r)
        yield
        o_ref[:, h0 * SWA_DH:(h0 + 2) * SWA_DH] = (acc * inv).astype(BF16)

    chains = []
    for slab in range(SWA_KV_HEADS // 2):
        k2 = kcat[:, slab * LANES:(slab + 1) * LANES].astype(F32) * scale
        v2 = vcat[:, slab * LANES:(slab + 1) * LANES].astype(F32)
        k2r = pltpu.roll(k2, SWA_DH, 1)
        v2r = pltpu.roll(v2, SWA_DH, 1)
        for sub in range(2):
            kvh = slab * 2 + sub
            if sub == 0:
                k_lo = jnp.where(lane_kv < SWA_DH, k2, 0.0)
                k_hi = jnp.where(lane_kv >= SWA_DH, k2r, 0.0)
                v_lo = jnp.where(lane_kv < SWA_DH, v2, 0.0)
                v_hi = jnp.where(lane_kv >= SWA_DH, v2r, 0.0)
            else:
                k_lo = jnp.where(lane_kv < SWA_DH, k2r, 0.0)
                k_hi = jnp.where(lane_kv >= SWA_DH, k2, 0.0)
                v_lo = jnp.where(lane_kv < SWA_DH, v2r, 0.0)
                v_hi = jnp.where(lane_kv >= SWA_DH, v2, 0.0)
            k_lo, k_hi, v_lo, v_hi = (t.astype(BF16) for t in (k_lo, k_hi, v_lo, v_hi))
            for gp in range(SWA_GROUP // 2):
                h0 = kvh * SWA_GROUP + 2 * gp
                chains.append(head_pair(h0, (k_lo, k_hi), (v_lo, v_hi)))

    live = []
    pending = list(chains)
    while pending or live:
        if pending:
            live.append(pending.pop(0))
        nxt = []
        for c in live:
            try:
                next(c)
                nxt.append(c)
            except StopIteration:
                pass
        live = nxt


def _swa(proj, sinks):
    nb = SEQ // SWA_BLOCK
    qcol = 4
    kcol = (7 * D_MODEL) // 256
    vcol = kcol + 1
    cur = lambda col: pl.BlockSpec((SWA_BLOCK, 256), lambda b, i, col=col: (b * nb + i, col))
    prev = lambda col: pl.BlockSpec(
        (SWA_BLOCK, 256), lambda b, i, col=col: (b * nb + jnp.maximum(i - 1, 0), col))
    return pl.pallas_call(
        _swa_body,
        grid=(BATCH, nb),
        in_specs=[
            pl.BlockSpec(memory_space=pltpu.SMEM),
            pl.BlockSpec((SWA_BLOCK, D_MODEL), lambda b, i: (b * nb + i, qcol)),
            cur(kcol), prev(kcol), cur(vcol), prev(vcol),
        ],
        out_specs=pl.BlockSpec((SWA_BLOCK, D_MODEL), lambda b, i: (b * nb + i, 0)),
        out_shape=jax.ShapeDtypeStruct((TOKENS, D_MODEL), BF16),
        compiler_params=_cparams(("parallel", "parallel")),
        name="swa",
    )(sinks, proj, proj, proj, proj, proj)


OUT_TM = 512
OUT_SUB = 256


def _outproj_body(x_ref, yg_ref, ys_ref, gg_ref, gs_ref, w_ref, nf_ref, wr_ref, rb_ref,
                  x1_ref, hp_ref, lg_ref):
    def sub_tile(t):
        rows = pl.ds(t * OUT_SUB, OUT_SUB)
        merged = (_sigmoid(gg_ref[rows, :].astype(F32)) * yg_ref[rows, :].astype(F32)
                  + _sigmoid(gs_ref[rows, :].astype(F32)) * ys_ref[rows, :].astype(F32))
        acc = jnp.dot(merged.astype(BF16), w_ref[...], preferred_element_type=F32)
        yield
        x1 = x_ref[rows, :] + acc
        x1_ref[rows, :] = x1
        ms = jnp.mean(x1 * x1, axis=-1, keepdims=True)
        h = x1 * lax.rsqrt(ms + EPS) * nf_ref[...]
        hh = h.astype(BF16)
        r = jnp.dot(hh, wr_ref[...], preferred_element_type=F32)
        lg_ref[rows, :] = r[:, :LANES] + r[:, LANES:] + rb_ref[...]
        half = D_MODEL // 2
        hf = hh.astype(F32)
        wa = pltpu.bitcast(hf[:, :half], U32)
        wb = pltpu.bitcast(hf[:, half:], U32)
        _store_token_tiles(hp_ref, t * OUT_SUB, (wa & jnp.uint32(0xFFFF0000)) | (wb >> 16))

    tiles = [sub_tile(t) for t in range(OUT_TM // OUT_SUB)]
    for _ in range(2):
        for g in tiles:
            next(g, None)


def _outproj(x2d, y_gdn, y_swa, proj, w_out, norm_ffn, wr2, r_bias):
    row = lambda w: pl.BlockSpec((OUT_TM, w), lambda i: (i, 0))
    const = lambda s: pl.BlockSpec(s, lambda i: (0, 0))
    return pl.pallas_call(
        _outproj_body,
        grid=(TOKENS // OUT_TM,),
        in_specs=[
            row(D_MODEL), row(D_MODEL), row(D_MODEL),
            pl.BlockSpec((OUT_TM, D_MODEL), lambda i: (i, 5)),
            pl.BlockSpec((OUT_TM, D_MODEL), lambda i: (i, 6)),
            pl.BlockSpec((D_MODEL, D_MODEL), lambda i: (0, 0), pipeline_mode=pl.Buffered(1)),
            const((1, D_MODEL)),
            const((D_MODEL, 2 * LANES)), const((1, LANES)),
        ],
        out_specs=[row(D_MODEL), pl.BlockSpec((OUT_TM * TILE_ROWS, LANES), lambda i: (i, 0)), row(LANES)],
        out_shape=[
            jax.ShapeDtypeStruct((TOKENS, D_MODEL), F32),
            jax.ShapeDtypeStruct((TOKENS * TILE_ROWS, LANES), U32),
            jax.ShapeDtypeStruct((TOKENS, LANES), F32),
        ],
        compiler_params=_cparams(("parallel",)),
        name="outproj",
    )(x2d, y_gdn, y_swa, proj, proj, w_out, norm_ffn, wr2, r_bias)


ROUTE_TM = 1024


def _route_body(lg_ref, tri_ref, idx_ref, wt_ref, cnt_ref, run_ref):
    @pl.when(pl.program_id(0) == 0)
    def _():
        run_ref[...] = jnp.zeros_like(run_ref)

    lg = lg_ref[...]
    lane = lax.broadcasted_iota(I32, lg.shape, 1)
    ninf = -jnp.inf
    big = jnp.int32(LANES)
    is_g = lane < N_GROUPS
    glog = jnp.where(is_g, lg, ninf)
    gmax = jnp.max(glog, axis=-1, keepdims=True)
    gden = jnp.sum(jnp.where(is_g, jnp.exp(lg - gmax), 0.0), axis=-1, keepdims=True)
    p_sel = 1.0 / gden
    grp = jnp.min(jnp.where(glog == gmax, lane, big), axis=-1, keepdims=True)
    emask = (lane >= N_GROUPS) & (lane < N_GROUPS + N_EXPERTS) & (((lane - N_GROUPS) >> 3) == grp)
    el = jnp.where(emask, lg, ninf)
    v1 = jnp.max(el, axis=-1, keepdims=True)
    i1 = jnp.min(jnp.where(el == v1, lane, big), axis=-1, keepdims=True)
    el2 = jnp.where(lane == i1, ninf, el)
    v2 = jnp.max(el2, axis=-1, keepdims=True)
    i2 = jnp.min(jnp.where(el2 == v2, lane, big), axis=-1, keepdims=True)
    e = jnp.exp(v2 - v1)
    w1 = p_sel / (1.0 + e)
    w2 = p_sel * e / (1.0 + e)
    e0 = i1 - N_GROUPS
    e1 = i2 - N_GROUPS

    oh0 = lane == e0
    oh1 = lane == e1
    onehot = jnp.where(oh0 | oh1, 1.0, 0.0)
    prefix = jnp.dot(tri_ref[...], onehot.astype(BF16), preferred_element_type=F32) + run_ref[0:1, :]
    r0 = jnp.sum(jnp.where(oh0, prefix, 0.0), axis=-1, keepdims=True).astype(I32)
    r1 = jnp.sum(jnp.where(oh1, prefix, 0.0), axis=-1, keepdims=True).astype(I32)
    run = run_ref[0:1, :] + jnp.sum(onehot, axis=0, keepdims=True)
    run_ref[...] = jnp.broadcast_to(run, run_ref.shape)
    cnt_ref[...] = jnp.broadcast_to(run, cnt_ref.shape).astype(I32)

    zi = jnp.zeros(lg.shape, I32)
    idx = jnp.where(lane == 0, e0, zi)
    idx = jnp.where(lane == 1, e1, idx)
    idx = jnp.where(lane == 2, r0, idx)
    idx = jnp.where(lane == 3, r1, idx)
    idx_ref[...] = idx
    wt_ref[...] = jnp.where(lane == 0, w1, jnp.where(lane == 1, w2, 0.0))


def _route(logits, tri):
    row = pl.BlockSpec((ROUTE_TM, LANES), lambda i: (i, 0))
    return pl.pallas_call(
        _route_body,
        grid=(TOKENS // ROUTE_TM,),
        in_specs=[row, pl.BlockSpec((ROUTE_TM, ROUTE_TM), lambda i: (0, 0))],
        out_specs=[row, row, pl.BlockSpec((8, LANES), lambda i: (0, 0))],
        out_shape=[
            jax.ShapeDtypeStruct((TOKENS, LANES), I32),
            jax.ShapeDtypeStruct((TOKENS, LANES), F32),
            jax.ShapeDtypeStruct((8, LANES), I32),
        ],
        scratch_shapes=[pltpu.VMEM((8, LANES), F32)],
        compiler_params=_cparams(("arbitrary",)),
        name="route",
    )(logits, tri)


DISP_TM = 256
N_DISP = TOKENS // DISP_TM
DISP_RING = 3
DISP_ROWS = DISP_TM * TILE_ROWS


def _dispatch_body(meta_ref, dest_ref, h_hbm, xs_ref, zbuf, hbuf, in_sem, out_sem, zsem):
    @pl.when(pl.program_id(0) == 0)
    def _():
        zbuf[...] = jnp.zeros_like(zbuf)

        def zero_block(row0):
            return pltpu.make_async_copy(
                zbuf, xs_ref.at[pl.ds(row0 * TILE_ROWS, MOE_BLOCK * TILE_ROWS), :], zsem)

        def per_expert(e, n):
            has = meta_ref[N_EXPERTS + e] > 0

            @pl.when(has)
            def _():
                zero_block(pl.multiple_of(meta_ref[e] - MOE_BLOCK, MOE_BLOCK)).start()

            return n + has.astype(I32)

        n_last = lax.fori_loop(0, N_EXPERTS, per_expert, jnp.int32(0))
        used = meta_ref[2 * N_EXPERTS]

        def tail(j, c):
            zero_block(pl.multiple_of(j * MOE_BLOCK, MOE_BLOCK)).start()
            return c

        lax.fori_loop(used, N_BLOCKS, tail, 0)

        def drain(i, c):
            zero_block(0).wait()
            return c

        lax.fori_loop(0, n_last + (N_BLOCKS - used), drain, 0)

    i = pl.program_id(0)
    b = lax.rem(i, DISP_RING)

    def load(step, slot):
        return pltpu.make_async_copy(
            h_hbm.at[pl.ds(pl.multiple_of(step * DISP_ROWS, DISP_ROWS), DISP_ROWS), :], hbuf.at[slot],
            in_sem.at[slot])

    def wait_scatters(slot):
        for k in range(TOP_K):
            pltpu.make_async_copy(hbuf.at[slot], xs_ref.at[pl.ds(0, DISP_ROWS), :], out_sem.at[slot]).wait()

    @pl.when(i == 0)
    def _():
        load(0, 0).start()

    load(i, b).wait()

    @pl.when(i + 1 < N_DISP)
    def _():
        load(i + 1, lax.rem(i + 1, DISP_RING)).start()

    def issue(r, c):
        for k in range(TOP_K):
            d = dest_ref[TOP_K * r + k]
            pltpu.make_async_copy(hbuf.at[b, pl.ds(pl.multiple_of(r * TILE_ROWS, TILE_ROWS), TILE_ROWS), :],
                                  xs_ref.at[pl.ds(pl.multiple_of(d * TILE_ROWS, TILE_ROWS), TILE_ROWS), :],
                                  out_sem.at[b]).start()
        return c

    lax.fori_loop(0, DISP_TM, issue, 0, unroll=4)

    @pl.when(i >= 1)
    def _():
        wait_scatters(lax.rem(i + DISP_RING - 1, DISP_RING))

    @pl.when(i == N_DISP - 1)
    def _():
        wait_scatters(b)


def _dispatch(meta, dest_flat, hp):
    grid_spec = pltpu.PrefetchScalarGridSpec(
        num_scalar_prefetch=1,
        grid=(N_DISP,),
        in_specs=[
            pl.BlockSpec((DISP_TM * TOP_K,), lambda i, m: (i,), memory_space=pltpu.SMEM),
            pl.BlockSpec(memory_space=pl.ANY),
        ],
        out_specs=pl.BlockSpec(memory_space=pl.ANY),
        scratch_shapes=[
            pltpu.VMEM((MOE_BLOCK * TILE_ROWS, LANES), U32),
            pltpu.VMEM((DISP_RING, DISP_ROWS, LANES), U32),
            pltpu.SemaphoreType.DMA((DISP_RING,)),
            pltpu.SemaphoreType.DMA((DISP_RING,)),
            pltpu.SemaphoreType.DMA,
        ],
    )
    return pl.pallas_call(
        _dispatch_body,
        grid_spec=grid_spec,
        out_shape=jax.ShapeDtypeStruct((N_SLOTS * TILE_ROWS, LANES), U32),
        compiler_params=_cparams(("arbitrary",)),
        name="dispatch",
    )(meta, dest_flat, hp)


def _slots_body(idx_ref, start_ref, dest_ref):
    idx = idx_ref[...]
    lane = lax.broadcasted_iota(I32, idx.shape, 1)
    start = start_ref[...]
    out = jnp.zeros(idx.shape, I32)
    for k in range(TOP_K):
        e = idx[:, k:k + 1]
        base = jnp.sum(jnp.where(lane == e, start, 0.0), axis=-1, keepdims=True).astype(I32)
        out = jnp.where(lane == k, base + idx[:, TOP_K + k:TOP_K + k + 1], out)
    dest_ref[...] = out


SLOTS_TM = 2048


def _slots(idx, pad_start_row):
    row = pl.BlockSpec((SLOTS_TM, LANES), lambda i: (i, 0))
    return pl.pallas_call(
        _slots_body,
        grid=(TOKENS // SLOTS_TM,),
        in_specs=[row, pl.BlockSpec((1, LANES), lambda i: (0, 0))],
        out_specs=row,
        out_shape=jax.ShapeDtypeStruct((TOKENS, LANES), I32),
        compiler_params=_cparams(("parallel",)),
        name="slots",
    )(idx, pad_start_row)


def _experts_body(sched_ref, x_ref, wg_hbm, wu_hbm, wd_hbm, y_ref,
                  wg_st, wu_st, wd_st, wg_bf, wu_bf, wd_bf, sem):
    j = pl.program_id(0)
    e = sched_ref[j]
    slot = sched_ref[N_BLOCKS + j]
    nxt = sched_ref[2 * N_BLOCKS + j]
    used = sched_ref[3 * N_BLOCKS]
    new_expert = (j == 0) | (e != sched_ref[jnp.maximum(j - 1, 0)])

    def weight_copies(expert, s):
        return (pltpu.make_async_copy(wg_hbm.at[expert], wg_st.at[s], sem.at[s, 0]),
                pltpu.make_async_copy(wu_hbm.at[expert], wu_st.at[s], sem.at[s, 1]),
                pltpu.make_async_copy(wd_hbm.at[expert], wd_st.at[s], sem.at[s, 2]))

    @pl.when(j == 0)
    def _():
        for c in weight_copies(e, slot):
            c.start(priority=1)

    @pl.when(new_expert)
    def _():
        for c in weight_copies(e, slot):
            c.wait()

        @pl.when(nxt >= 0)
        def _():
            for c in weight_copies(nxt, 1 - slot):
                c.start(priority=1)

        wg_bf[...] = wg_st[slot].astype(BF16)
        wu_bf[...] = wu_st[slot].astype(BF16)
        wd_bf[...] = wd_st[slot].astype(BF16)

    @pl.when(j < used)
    def _():
        w = _load_token_tiles(x_ref, MOE_BLOCK)
        xa = pltpu.bitcast(w & jnp.uint32(0xFFFF0000), F32).astype(BF16)
        xb = pltpu.bitcast(w << 16, F32).astype(BF16)
        x = jnp.concatenate([xa, xb], axis=1)
        g = jnp.dot(x, wg_bf[...], preferred_element_type=F32)
        u = jnp.dot(x, wu_bf[...], preferred_element_type=F32)
        hb = (_silu(g) * u).astype(BF16)
        y = jnp.dot(hb, wd_bf[...], preferred_element_type=F32)
        half = D_MODEL // 2
        ya = pltpu.bitcast(y[:, :half].astype(BF16).astype(F32), U32)
        yb = pltpu.bitcast(y[:, half:].astype(BF16).astype(F32), U32)
        _store_token_tiles(y_ref, 0, ya | (yb >> 16))

    @pl.when(j >= used)
    def _():
        y_ref[...] = jnp.zeros_like(y_ref)


def _experts(sched, xs, w_gate, w_up, w_down):
    grid_spec = pltpu.PrefetchScalarGridSpec(
        num_scalar_prefetch=1,
        grid=(N_BLOCKS,),
        in_specs=[
            pl.BlockSpec((MOE_BLOCK * TILE_ROWS, LANES), lambda j, s: (j, 0)),
            pl.BlockSpec(memory_space=pl.ANY),
            pl.BlockSpec(memory_space=pl.ANY),
            pl.BlockSpec(memory_space=pl.ANY),
        ],
        out_specs=pl.BlockSpec((MOE_BLOCK * TILE_ROWS, LANES), lambda j, s: (j, 0)),
        scratch_shapes=[
            pltpu.VMEM((2, D_MODEL, D_EXPERT), F32),
            pltpu.VMEM((2, D_MODEL, D_EXPERT), F32),
            pltpu.VMEM((2, D_EXPERT, D_MODEL), F32),
            pltpu.VMEM((D_MODEL, D_EXPERT), BF16),
            pltpu.VMEM((D_MODEL, D_EXPERT), BF16),
            pltpu.VMEM((D_EXPERT, D_MODEL), BF16),
            pltpu.SemaphoreType.DMA((2, 3)),
        ],
    )
    return pl.pallas_call(
        _experts_body,
        grid_spec=grid_spec,
        out_shape=jax.ShapeDtypeStruct((N_SLOTS * TILE_ROWS, LANES), U32),
        compiler_params=_cparams(("arbitrary",)),
        name="experts",
    )(sched, xs, w_gate, w_up, w_down)


COMB_TM = 256
N_COMB = TOKENS // COMB_TM


def _unpack_pairs(w):
    hi = pltpu.bitcast(w & jnp.uint32(0xFFFF0000), F32)
    lo = pltpu.bitcast(w << 16, F32)
    return jnp.concatenate([hi, lo], axis=1)


def _combine_body(dest_ref, dest_next_ref, wt_ref, x1_ref, gain_ref, ys_ref, o_ref, buf, sem):
    i = pl.program_id(0)
    slot = lax.rem(i, 2)

    def start_gathers(dref, s):
        def issue(r, c):
            for k in range(TOP_K):
                d = dref[TOP_K * r + k]
                pltpu.make_async_copy(
                    ys_ref.at[pl.ds(pl.multiple_of(d * TILE_ROWS, TILE_ROWS), TILE_ROWS), :],
                    buf.at[s, k, pl.ds(pl.multiple_of(r * TILE_ROWS, TILE_ROWS), TILE_ROWS), :], sem.at[s]).start()
            return c
        lax.fori_loop(0, COMB_TM, issue, 0, unroll=4)

    @pl.when(i == 0)
    def _():
        start_gathers(dest_ref, slot)

    @pl.when(i + 1 < N_COMB)
    def _():
        start_gathers(dest_next_ref, 1 - slot)

    for k in range(TOP_K):
        pltpu.make_async_copy(ys_ref.at[pl.ds(0, COMB_TM * TILE_ROWS), :], buf.at[slot, k], sem.at[slot]).wait()

    wt = wt_ref[...]
    y = (x1_ref[...] + wt[:, 0:1] * _unpack_pairs(_load_token_tiles(buf.at[slot, 0], COMB_TM))
         + wt[:, 1:2] * _unpack_pairs(_load_token_tiles(buf.at[slot, 1], COMB_TM)))
    ms = jnp.mean(y * y, axis=-1, keepdims=True)
    o_ref[...] = y * lax.rsqrt(ms + EPS) * gain_ref[...]


def _combine(dest_flat, wts, x1, norm_final, ys):
    dblk = lambda f: pl.BlockSpec((COMB_TM * TOP_K,), f, memory_space=pltpu.SMEM)
    return pl.pallas_call(
        _combine_body,
        grid=(N_COMB,),
        in_specs=[
            dblk(lambda i: (i,)),
            dblk(lambda i: (jnp.minimum(i + 1, N_COMB - 1),)),
            pl.BlockSpec((COMB_TM, LANES), lambda i: (i, 0)),
            pl.BlockSpec((COMB_TM, D_MODEL), lambda i: (i, 0)),
            pl.BlockSpec((1, D_MODEL), lambda i: (0, 0)),
            pl.BlockSpec(memory_space=pl.ANY),
        ],
        out_specs=pl.BlockSpec((COMB_TM, D_MODEL), lambda i: (i, 0)),
        out_shape=jax.ShapeDtypeStruct((TOKENS, D_MODEL), F32),
        scratch_shapes=[pltpu.VMEM((2, TOP_K, COMB_TM * TILE_ROWS, LANES), U32), pltpu.SemaphoreType.DMA((2,))],
        compiler_params=_cparams(("arbitrary",)),
        name="combine",
    )(dest_flat, dest_flat, wts, x1, norm_final, ys)


def _pad_lanes(v):
    v = v.reshape(1, -1).astype(F32)
    return jnp.pad(v, ((0, 0), (0, LANES - v.shape[1])))


def kernel(x, norm_mix, w_in, conv_w, gdn_a_log, gdn_dt_bias, gdn_norm, swa_sinks, w_out, norm_ffn,
           w_router_group, b_router_group, w_router_expert, b_router_expert, w_gate, w_up, w_down,
           norm_final):
    l = 0
    x2d = x.reshape(TOKENS, D_MODEL)
    w = w_in[l]
    o_z = 3 * D_MODEL
    o_a = o_z + D_MODEL
    o_sq = o_a + 2 * GDN_HEADS
    o_sk = o_sq + D_MODEL
    o_sv = o_sk + SWA_KV_HEADS * SWA_DH
    o_gg = o_sv + SWA_KV_HEADS * SWA_DH
    o_gs = o_gg + D_MODEL
    w_main = jnp.concatenate(
        [w[:, :o_a], w[:, o_sq:o_sk], w[:, o_gg:o_gs], w[:, o_gs:], w[:, o_sk:o_sv], w[:, o_sv:o_gg]],
        axis=1).astype(BF16)
    w_ab = jnp.pad(w[:, o_a:o_sq], ((0, 0), (0, LANES - 2 * GDN_HEADS))).astype(BF16)

    proj, ab = _inproj(x2d, norm_mix[l].reshape(1, D_MODEL), w_main, w_ab, conv_w[l])

    g1, gt = _gdn_prep(ab, _pad_lanes(gdn_a_log[l]), _pad_lanes(gdn_dt_bias[l]))
    gr = gt[:, :GDN_HEADS, :].reshape(BATCH, GDN_HEADS, N_GROUPS_SEQ, GDN_GROUP_ROWS)
    y_gdn = _gdn(proj, g1, gr, gdn_norm[l].reshape(1, GDN_DV))

    y_swa = _swa(proj, swa_sinks[l].astype(F32))

    w_r = jnp.concatenate([w_router_group[l], w_router_expert[l]], axis=1).astype(F32)
    w_r = jnp.pad(w_r, ((0, 0), (0, LANES - w_r.shape[1])))
    wr_hi = w_r.astype(BF16)
    wr_lo = (w_r - wr_hi.astype(F32)).astype(BF16)
    r_bias = _pad_lanes(jnp.concatenate([b_router_group[l], b_router_expert[l]]))
    x1, hp, logits = _outproj(x2d, y_gdn, y_swa, proj, w_out[l].astype(BF16),
                              norm_ffn[l].reshape(1, D_MODEL), jnp.concatenate([wr_hi, wr_lo], axis=1), r_bias)

    tri = (lax.broadcasted_iota(I32, (ROUTE_TM, ROUTE_TM), 1)
           < lax.broadcasted_iota(I32, (ROUTE_TM, ROUTE_TM), 0)).astype(BF16)
    idx, wts, cnt = _route(logits, tri)

    counts = cnt[0, :N_EXPERTS]
    padded = (counts + MOE_BLOCK - 1) // MOE_BLOCK * MOE_BLOCK
    pad_end = jnp.cumsum(padded)
    pad_start = pad_end - padded
    dest = _slots(idx, _pad_lanes(pad_start))
    dest_flat = dest[:, :TOP_K].reshape(N_ASSIGN)
    blk_pos = jnp.arange(N_BLOCKS, dtype=I32) * MOE_BLOCK
    blk_e = jnp.minimum(jnp.sum((pad_end[None, :] <= blk_pos[:, None]).astype(I32), axis=1), N_EXPERTS - 1)
    used = pad_end[-1:] // MOE_BLOCK
    meta = jnp.concatenate([pad_end, padded, used]).astype(I32)
    is_new = jnp.concatenate([jnp.ones((1,), I32), (blk_e[1:] != blk_e[:-1]).astype(I32)])
    ordinal = jnp.cumsum(is_new) - 1
    nxt_pos = jnp.sum((ordinal[None, :] <= ordinal[:, None]).astype(I32), axis=1)
    nxt = jnp.where(nxt_pos < N_BLOCKS, blk_e[jnp.minimum(nxt_pos, N_BLOCKS - 1)], -1)
    sched = jnp.concatenate([blk_e, ordinal & 1, nxt, used]).astype(I32)

    xs = _dispatch(meta, dest_flat, hp)
    ys = _experts(sched, xs, w_gate[l], w_up[l], w_down[l])
    out = _combine(dest_flat, wts, x1, norm_final.reshape(1, D_MODEL), ys)
    return out.reshape(BATCH, SEQ, D_MODEL)
```

```python
import math

import jax
import jax.numpy as jnp
from jax import lax
from jax.experimental import pallas as pl
from jax.experimental.pallas import tpu as pltpu

F32 = jnp.float32
BF16 = jnp.bfloat16
I32 = jnp.int32
U32 = jnp.uint32

D_MODEL = 2048
BATCH = 16
SEQ = 2048
TOKENS = BATCH * SEQ
EPS = 1e-6

GDN_HEADS = 16
GDN_DK = 128
GDN_DV = 128
GDN_CHUNK = 64
CONV_W = 4
GDN_GROUP = 4
GDN_GROUP_ROWS = GDN_GROUP * GDN_CHUNK
N_CHUNKS = SEQ // GDN_CHUNK
N_GROUPS_SEQ = SEQ // GDN_GROUP_ROWS

SWA_HEADS = 32
SWA_KV_HEADS = 4
SWA_DH = 64
SWA_GROUP = 8
SWA_BLOCK = 128
WINDOW = 128

N_GROUPS = 8
EXPERTS_PER_GROUP = 8
N_EXPERTS = 64
TOP_K = 2
D_EXPERT = 512
MOE_BLOCK = 256
N_ASSIGN = TOKENS * TOP_K
N_BLOCKS = N_ASSIGN // MOE_BLOCK + N_EXPERTS
N_SLOTS = N_BLOCKS * MOE_BLOCK

PROJ_DIM = 7 * D_MODEL + 2 * SWA_KV_HEADS * SWA_DH
LANES = 128

VMEM_LIMIT = 56 * 1024 * 1024


def _cparams(sem, vmem=VMEM_LIMIT):
    return pltpu.CompilerParams(dimension_semantics=sem, vmem_limit_bytes=vmem)


TILE_ROWS = (D_MODEL // 2) // LANES


def _store_token_tiles(ref, first_token, words):
    n = words.shape[0]
    for s in range(TILE_ROWS):
        ref[pl.ds(first_token * TILE_ROWS + s, n, stride=TILE_ROWS), :] = words[:, s * LANES:(s + 1) * LANES]


def _load_token_tiles(ref, n):
    return jnp.concatenate([ref[pl.ds(s, n, stride=TILE_ROWS), :] for s in range(TILE_ROWS)], axis=1)


def _sigmoid(x):
    return 0.5 * jnp.tanh(0.5 * x) + 0.5


def _silu(x):
    h = 0.5 * x
    return h * jnp.tanh(h) + h


INPROJ_TM = 2048
INPROJ_TN = 512
INPROJ_VMEM = 60 * 1024 * 1024


INPROJ_RC = 256
INPROJ_RING = 3
QKV_TILES = 3 * D_MODEL // INPROJ_TN
QK_TILES = 2 * D_MODEL // INPROJ_TN
Q_TILES = D_MODEL // INPROJ_TN
assert INPROJ_TM == SEQ


def _inproj_body(x_ref, g_ref, w_ref, wab_ref, cw_ref, o_ref, ab_ref, h_ref, cpad):
    j = pl.program_id(1)

    @pl.when(j == 0)
    def _():
        def chunk(i, c):
            r = pl.ds(pl.multiple_of(i * 128, 128), 128)
            x = x_ref[r, :]
            ms = jnp.mean(x * x, axis=-1, keepdims=True)
            h_ref[r, :] = (x * lax.rsqrt(ms + EPS) * g_ref[...]).astype(BF16)
            return c
        lax.fori_loop(0, INPROJ_TM // 128, chunk, 0)
        ab_ref[...] = jnp.dot(h_ref[...], wab_ref[...], preferred_element_type=F32)

    @pl.when(j >= QKV_TILES)
    def _():
        o_ref[...] = jnp.dot(h_ref[...], w_ref[...], preferred_element_type=F32).astype(BF16)

    @pl.when(j < QKV_TILES)
    def _():
        cw = cw_ref[...]
        is_qk = j < QK_TILES
        qscale = jnp.where(j < Q_TILES, GDN_DK ** -0.5, 1.0).astype(F32)
        n_chunks = INPROJ_TM // INPROJ_RC

        def matmul(c):
            rows = pl.ds(c * INPROJ_RC, INPROJ_RC)
            cpad[c % INPROJ_RING, pl.ds(8, INPROJ_RC), :] = jnp.dot(
                h_ref[rows, :], w_ref[...], preferred_element_type=F32)

        def epilogue(c):
            buf = cpad.at[c % INPROJ_RING]
            if c == 0:
                buf[pl.ds(0, 8), :] = jnp.zeros((8, INPROJ_TN), F32)
            else:
                buf[pl.ds(0, 8), :] = cpad[(c - 1) % INPROJ_RING, pl.ds(INPROJ_RC, 8), :]
            y = None
            for s in range(CONV_W):
                t = buf[pl.ds(8 - s, INPROJ_RC), :] * cw[CONV_W - 1 - s:CONV_W - s, :]
                y = t if y is None else y + t
            y = _silu(y)
            parts = []
            for g in range(INPROJ_TN // LANES):
                yg = y[:, g * LANES:(g + 1) * LANES]
                inv = lax.rsqrt(jnp.sum(yg * yg, -1, keepdims=True) + EPS) * qscale
                parts.append(yg * jnp.where(is_qk, inv, 1.0))
            o_ref[pl.ds(c * INPROJ_RC, INPROJ_RC), :] = jnp.concatenate(parts, axis=1).astype(BF16)

        matmul(0)
        for c in range(n_chunks):
            if c + 1 < n_chunks:
                matmul(c + 1)
            epilogue(c)


def _inproj(x2d, gain, w_main, w_ab, conv_w):
    grid = (TOKENS // INPROJ_TM, PROJ_DIM // INPROJ_TN)
    return pl.pallas_call(
        _inproj_body,
        grid=grid,
        in_specs=[
            pl.BlockSpec((INPROJ_TM, D_MODEL), lambda i, j: (i, 0)),
            pl.BlockSpec((1, D_MODEL), lambda i, j: (0, 0)),
            pl.BlockSpec((D_MODEL, INPROJ_TN), lambda i, j: (0, j)),
            pl.BlockSpec((D_MODEL, LANES), lambda i, j: (0, 0)),
            pl.BlockSpec((CONV_W, INPROJ_TN), lambda i, j: (0, jnp.minimum(j, QKV_TILES - 1))),
        ],
        out_specs=[
            pl.BlockSpec((INPROJ_TM, INPROJ_TN), lambda i, j: (i, j)),
            pl.BlockSpec((INPROJ_TM, LANES), lambda i, j: (i, 0)),
        ],
        out_shape=[
            jax.ShapeDtypeStruct((TOKENS, PROJ_DIM), BF16),
            jax.ShapeDtypeStruct((TOKENS, LANES), F32),
        ],
        scratch_shapes=[pltpu.VMEM((INPROJ_TM, D_MODEL), BF16),
                        pltpu.VMEM((INPROJ_RING, 8 + INPROJ_RC, INPROJ_TN), F32)],
        compiler_params=_cparams(("parallel", "arbitrary"), vmem=INPROJ_VMEM),
        name="inproj",
    )(x2d, gain, w_main, w_ab, conv_w)


def _gdn_prep_body(ab_ref, alog_ref, dtb_ref, g1_ref, gt_ref):
    ab = ab_ref[...]
    lane = lax.broadcasted_iota(I32, ab.shape, 1)
    row = lax.broadcasted_iota(I32, ab.shape, 0) % GDN_CHUNK
    xa = ab + dtb_ref[...]
    softplus = jnp.maximum(xa, 0.0) + jnp.log(1.0 + jnp.exp(-jnp.abs(xa)))
    g = jnp.where(lane < GDN_HEADS, -jnp.exp(alog_ref[...]) * softplus, 0.0)
    gam = g
    s = 1
    while s < GDN_CHUNK:
        gam = gam + jnp.where(row >= s, pltpu.roll(gam, s, 0), 0.0)
        s *= 2
    g1_ref[...] = jnp.where(lane < GDN_HEADS, gam, _sigmoid(ab))
    gt_ref[0] = gam.T


def _gdn_prep(ab, alog_pad, dtb_pad):
    return pl.pallas_call(
        _gdn_prep_body,
        grid=(BATCH,),
        in_specs=[
            pl.BlockSpec((SEQ, LANES), lambda b: (b, 0)),
            pl.BlockSpec((1, LANES), lambda b: (0, 0)),
            pl.BlockSpec((1, LANES), lambda b: (0, 0)),
        ],
        out_specs=[
            pl.BlockSpec((SEQ, LANES), lambda b: (b, 0)),
            pl.BlockSpec((1, LANES, SEQ), lambda b: (b, 0, 0)),
        ],
        out_shape=[
            jax.ShapeDtypeStruct((TOKENS, LANES), F32),
            jax.ShapeDtypeStruct((BATCH, LANES, SEQ), F32),
        ],
        compiler_params=_cparams(("parallel",)),
        name="gdn_prep",
    )(ab, alog_pad, dtb_pad)


GDN_HB = 4
GDN_PAIR = 2
N_PAIRS = N_GROUPS_SEQ // GDN_PAIR
PAIR_CHUNKS = GDN_PAIR * GDN_GROUP
PAIR_ROWS = GDN_PAIR * GDN_GROUP_ROWS
GDN_VMEM = 59 * 1024 * 1024


def _block_diag(x):
    t = jnp.concatenate([x] * GDN_GROUP, axis=0)
    rb = lax.broadcasted_iota(I32, t.shape, 0) // GDN_CHUNK
    cb = lax.broadcasted_iota(I32, t.shape, 1) // GDN_CHUNK
    return jnp.where(rb == cb, t, jnp.zeros_like(t))


def _sbs_product(lhs_list, y):
    bd = _block_diag(y.astype(BF16))
    xs = [x.astype(BF16) for x in lhs_list]
    lhs = xs[0] if len(xs) == 1 else jnp.concatenate(xs, axis=0)
    r = jnp.dot(lhs, bd, preferred_element_type=F32)
    c = GDN_CHUNK
    return [r[i * c:(i + 1) * c] for i in range(len(xs))]


def _gdn_body(q_ref, k_ref, v_ref, z_ref, g1_ref, gr_ref, gain_ref,
              o_ref,
              kdbf, rhsbf, qd, gamb, betab, o0s, qts, vs, wps):
    hg = pl.program_id(1)
    shape = (PAIR_ROWS, LANES)
    lane = lax.broadcasted_iota(I32, shape, 1)

    def phase0(pair):
        rows = pl.ds(pl.multiple_of(pair * PAIR_ROWS, PAIR_ROWS), PAIR_ROWS)
        g1 = g1_ref[rows, :]
        for hd in range(GDN_HB):
            hs = slice(hd * LANES, (hd + 1) * LANES)
            head = hg * GDN_HB + hd

            def col(off):
                c = jnp.sum(jnp.where(lane == off + head, g1, 0.0), axis=-1, keepdims=True)
                return jnp.broadcast_to(c, shape)

            gam = col(0)
            beta = col(GDN_HEADS)
            gamb[hd, rows, :] = gam
            betab[hd, rows, :] = beta.astype(BF16)
            gam3 = gam.reshape(PAIR_CHUNKS, GDN_CHUNK, LANES)
            glast = jnp.broadcast_to(gam3[:, GDN_CHUNK - 1:GDN_CHUNK, :], gam3.shape).reshape(shape)
            eg = jnp.exp(gam)
            kdf = jnp.exp(glast - gam)

            qd[hd, rows, :] = (q_ref[rows, hs].astype(F32) * eg).astype(BF16)
            k = k_ref[rows, hs].astype(F32)
            kdbf[hd, rows, :] = (k * kdf).astype(BF16)
            rhsbf[hd, rows, LANES:] = (k * (beta * eg)).astype(BF16)
            rhsbf[hd, rows, :LANES] = (v_ref[rows, hs].astype(F32) * beta).astype(BF16)

    gshape = (GDN_CHUNK, GDN_GROUP_ROWS)
    ii = lax.broadcasted_iota(I32, gshape, 0)
    jj = lax.broadcasted_iota(I32, gshape, 1) % GDN_CHUNK
    eye = jnp.where(ii == jj, 1.0, 0.0).astype(F32)
    pr = lax.broadcasted_iota(I32, (2 * GDN_CHUNK, LANES), 0) // GDN_CHUNK
    tl = lax.broadcasted_iota(I32, (LANES, GDN_GROUP_ROWS), 1) // GDN_CHUNK
    lane_c = lax.broadcasted_iota(I32, (GDN_CHUNK, LANES), 1)
    nt = (((1,), (1,)), ((), ()))

    def sbs_bcast(ref, hd, base):
        parts = [ref[hd, pl.ds(base + m * GDN_CHUNK, GDN_CHUNK), :] for m in range(GDN_GROUP)]
        a = jnp.where(lane_c < GDN_CHUNK, parts[0], parts[1])
        b = jnp.where(lane_c < GDN_CHUNK, parts[2], parts[3])
        return jnp.concatenate([a, b], axis=1)

    def phase1(hd, pair, gg):
        g = pair * GDN_PAIR + gg
        base = pl.multiple_of(g * GDN_GROUP_ROWS, GDN_GROUP_ROWS)
        slot = lax.rem(pair, 2)
        res = []
        for p in range(2):
            rows = pl.ds(base + p * 2 * GDN_CHUNK, 2 * GDN_CHUNK)
            kst = k_ref[rows, hd * LANES:(hd + 1) * LANES]
            qst = q_ref[rows, hd * LANES:(hd + 1) * LANES]
            lhs_k = jnp.concatenate([kst[:GDN_CHUNK], kst[GDN_CHUNK:]], axis=1)
            lhs_q = jnp.concatenate([qst[:GDN_CHUNK], qst[GDN_CHUNK:]], axis=1)
            lhs = jnp.concatenate([lhs_q, lhs_k], axis=0)
            zero = jnp.zeros_like(kst)
            bt = jnp.concatenate([jnp.where(pr == 0, kst, zero), jnp.where(pr == 1, kst, zero)], axis=1)
            res.append(lax.dot_general(lhs, bt, nt, preferred_element_type=F32))
        yield
        qk = jnp.concatenate([res[0][:GDN_CHUNK], res[1][:GDN_CHUNK]], axis=1)
        kk = jnp.concatenate([res[0][GDN_CHUNK:], res[1][GDN_CHUNK:]], axis=1)

        gc = sbs_bcast(gamb, hd, base)
        bc = sbs_bcast(betab, hd, base).astype(F32)
        grow = gr_ref[0, hd, pl.ds(g, 1), :]
        decay = jnp.exp(jnp.where(ii >= jj, gc - grow, -jnp.inf))
        a = jnp.where(ii > jj, kk * decay * bc, 0.0)
        qkd = qk * decay

        u = eye - a
        (x,) = _sbs_product([a], a)
        yield
        for lvl in range(1, 6):
            if lvl < 5:
                x2, ux = _sbs_product([x, u], x)
                u = u + ux
                x = x2
            else:
                (ux,) = _sbs_product([u], x)
                u = u + ux
            yield

        rows4 = pl.ds(base, GDN_GROUP_ROWS)
        so = jnp.dot(_block_diag(u.astype(BF16)), rhsbf[hd, rows4, :], preferred_element_type=F32)
        yield
        sol = so.astype(BF16)

        kdt = kdbf[hd, rows4, :].astype(F32).T.astype(BF16)
        zt = jnp.zeros_like(kdt)
        lhs2 = jnp.concatenate(
            [_block_diag(qkd.astype(BF16))] + [jnp.where(tl == m, kdt, zt) for m in range(GDN_GROUP)],
            axis=0)
        r = jnp.dot(lhs2, sol, preferred_element_type=F32)
        srows = pl.ds(pl.multiple_of(slot * PAIR_ROWS + gg * GDN_GROUP_ROWS, GDN_GROUP_ROWS), GDN_GROUP_ROWS)
        o0s[hd, srows, :] = r[:GDN_GROUP_ROWS, :LANES]
        qts[hd, srows, :] = qd[hd, rows4, :].astype(F32) - r[:GDN_GROUP_ROWS, LANES:]
        for m in range(GDN_GROUP):
            blk = r[GDN_GROUP_ROWS + m * LANES:GDN_GROUP_ROWS + (m + 1) * LANES]
            dst = pl.ds(pl.multiple_of((slot * PAIR_CHUNKS + gg * GDN_GROUP + m) * LANES, LANES), LANES)
            vs[hd, dst, :] = blk[:, :LANES]
            wps[hd, dst, :] = blk[:, LANES:]

    gain = gain_ref[...]

    def phase2(hd, pair, j, s):
        c = pair * PAIR_CHUNKS + j
        local = lax.rem(pair, 2) * PAIR_CHUNKS + j
        rows = pl.ds(pl.multiple_of(c * GDN_CHUNK, GDN_CHUNK), GDN_CHUNK)
        lrows = pl.ds(pl.multiple_of(local * GDN_CHUNK, GDN_CHUNK), GDN_CHUNK)
        srow = pl.ds(pl.multiple_of(local * LANES, LANES), LANES)
        lhs = jnp.concatenate([wps[hd, srow, :].astype(BF16), qts[hd, lrows, :].astype(BF16)], axis=0)
        r = jnp.dot(lhs, s.astype(BF16), preferred_element_type=F32)
        o = r[LANES:] + o0s[hd, lrows, :]
        dec = jnp.exp(gamb[hd, pl.ds(c * GDN_CHUNK + GDN_CHUNK - 1, 1), :])
        s_new = s * dec + vs[hd, srow, :] - r[:LANES]
        y = o * lax.rsqrt(jnp.mean(o * o, -1, keepdims=True) + EPS) * gain
        zz = z_ref[rows, hd * LANES:(hd + 1) * LANES].astype(F32)
        o_ref[rows, hd * LANES:(hd + 1) * LANES] = (y * _silu(zz)).astype(BF16)
        return s_new

    def phase2_chain(hd, pair, states):
        s = states[hd]
        for j in range(PAIR_CHUNKS):
            s = phase2(hd, pair, j, s)
            yield
        states[hd] = s

    def run_interleaved(chains):
        live = list(chains)
        while live:
            nxt = []
            for c in live:
                try:
                    next(c)
                    nxt.append(c)
                except StopIteration:
                    pass
            live = nxt

    def phase1_chains(pair):
        return [phase1(hd, pair, gg) for gg in range(GDN_PAIR) for hd in range(GDN_HB)]

    for pair in range(N_PAIRS):
        phase0(pair)
    run_interleaved(phase1_chains(0))

    def body(pair, states):
        states = list(states)
        run_interleaved([phase2_chain(hd, pair - 1, states) for hd in range(GDN_HB)] + phase1_chains(pair))
        return tuple(states)

    s0 = tuple(jnp.zeros((GDN_DK, GDN_DV), F32) for _ in range(GDN_HB))
    states = list(lax.fori_loop(1, N_PAIRS, body, s0))
    run_interleaved([phase2_chain(hd, N_PAIRS - 1, states) for hd in range(GDN_HB)])


def _gdn(proj, g1, gr, gdn_norm):
    w = GDN_HB * LANES
    per_row = D_MODEL // w
    hblk = lambda off: pl.BlockSpec((SEQ, w), lambda b, h, off=off: (b, off * per_row + h))
    big = lambda dt: pltpu.VMEM((GDN_HB, SEQ, LANES), dt)
    return pl.pallas_call(
        _gdn_body,
        grid=(BATCH, GDN_HEADS // GDN_HB),
        in_specs=[
            hblk(0), hblk(1), hblk(2), hblk(3),
            pl.BlockSpec((SEQ, LANES), lambda b, h: (b, 0)),
            pl.BlockSpec((1, GDN_HB, N_GROUPS_SEQ, GDN_GROUP_ROWS), lambda b, h: (b, h, 0, 0)),
            pl.BlockSpec((1, LANES), lambda b, h: (0, 0)),
        ],
        out_specs=pl.BlockSpec((SEQ, w), lambda b, h: (b, h)),
        out_shape=jax.ShapeDtypeStruct((TOKENS, D_MODEL), BF16),
        scratch_shapes=[
            big(BF16), pltpu.VMEM((GDN_HB, SEQ, 2 * LANES), BF16),
            big(BF16), big(F32), big(BF16),
            pltpu.VMEM((GDN_HB, 2 * PAIR_ROWS, LANES), F32),
            pltpu.VMEM((GDN_HB, 2 * PAIR_ROWS, LANES), F32),
            pltpu.VMEM((GDN_HB, 2 * PAIR_CHUNKS * LANES, LANES), F32),
            pltpu.VMEM((GDN_HB, 2 * PAIR_CHUNKS * LANES, LANES), F32),
        ],
        compiler_params=_cparams(("parallel", "parallel"), vmem=GDN_VMEM),
        name="gdn",
    )(proj, proj, proj, proj, g1, gr, gdn_norm)


SWA_PIPE = 3
assert WINDOW == SWA_BLOCK


def _swa_body(sink_ref, q_ref, kc_ref, kp_ref, vc_ref, vp_ref, o_ref):
    i = pl.program_id(1)
    kcat = jnp.concatenate([kp_ref[...], kc_ref[...]], axis=0)
    vcat = jnp.concatenate([vp_ref[...], vc_ref[...]], axis=0)
    qi = lax.broadcasted_iota(I32, (SWA_BLOCK, SWA_BLOCK), 0)
    ki = lax.broadcasted_iota(I32, (SWA_BLOCK, SWA_BLOCK), 1)
    take_cur = ki <= qi
    valid = take_cur | (i > 0)
    lane_kv = lax.broadcasted_iota(I32, (2 * SWA_BLOCK, LANES), 1)
    lane_q = lax.broadcasted_iota(I32, (SWA_BLOCK, LANES), 1)
    nt = (((1,), (1,)), ((), ()))
    scale = SWA_DH ** -0.5
    assert math.frexp(scale)[0] == 0.5

    def head_pair(h0, ks, vs):
        qs = q_ref[:, h0 * SWA_DH:(h0 + 2) * SWA_DH]
        scores = []
        for kk in ks:
            sc = lax.dot_general(qs, kk, nt, preferred_element_type=F32)
            s = jnp.where(take_cur, sc[:, SWA_BLOCK:], sc[:, :SWA_BLOCK])
            scores.append(jnp.where(valid, s, -jnp.inf))
        for _ in range(SWA_PIPE):
            yield
        acc = None
        inv = None
        for half in range(2):
            sink = sink_ref[h0 + half]
            s = scores[half]
            m = jnp.maximum(jnp.max(s, axis=-1, keepdims=True), sink)
            p = jnp.exp(s - m)
            den = jnp.sum(p, axis=-1, keepdims=True) + jnp.exp(sink - m)
            p2 = jnp.concatenate([jnp.where(take_cur, 0.0, p), jnp.where(take_cur, p, 0.0)], axis=1)
            pv = jnp.dot(p2.astype(BF16), vs[half], preferred_element_type=F32)
            acc = pv if acc is None else acc + pv
            r = 1.0 / den
            inv = r if inv is None else jnp.where(lane_q < SWA_DH, inv, r)
        yield
        o_ref[:, h0 * SWA_DH:(h0 + 2) * SWA_DH] = (acc * inv).astype(BF16)

    chains = []
    for slab in range(SWA_KV_HEADS // 2):
        k2 = kcat[:, slab * LANES:(slab + 1) * LANES].astype(F32) * scale
        v2 = vcat[:, slab * LANES:(slab + 1) * LANES].astype(F32)
        k2r = pltpu.roll(k2, SWA_DH, 1)
        v2r = pltpu.roll(v2, SWA_DH, 1)
        for sub in range(2):
            kvh = slab * 2 + sub
            if sub == 0:
                k_lo = jnp.where(lane_kv < SWA_DH, k2, 0.0)
                k_hi = jnp.where(lane_kv >= SWA_DH, k2r, 0.0)
                v_lo = jnp.where(lane_kv < SWA_DH, v2, 0.0)
                v_hi = jnp.where(lane_kv >= SWA_DH, v2r, 0.0)
            else:
                k_lo = jnp.where(lane_kv < SWA_DH, k2r, 0.0)
                k_hi = jnp.where(lane_kv >= SWA_DH, k2, 0.0)
                v_lo = jnp.where(lane_kv < SWA_DH, v2r, 0.0)
                v_hi = jnp.where(lane_kv >= SWA_DH, v2, 0.0)
            k_lo, k_hi, v_lo, v_hi = (t.astype(BF16) for t in (k_lo, k_hi, v_lo, v_hi))
            for gp in range(SWA_GROUP // 2):
                h0 = kvh * SWA_GROUP + 2 * gp
                chains.append(head_pair(h0, (k_lo, k_hi), (v_lo, v_hi)))

    live = []
    pending = list(chains)
    while pending or live:
        if pending:
            live.append(pending.pop(0))
        nxt = []
        for c in live:
            try:
                next(c)
                nxt.append(c)
            except StopIteration:
                pass
        live = nxt


def _swa(proj, sinks):
    nb = SEQ // SWA_BLOCK
    qcol = 4
    kcol = (7 * D_MODEL) // 256
    vcol = kcol + 1
    cur = lambda col: pl.BlockSpec((SWA_BLOCK, 256), lambda b, i, col=col: (b * nb + i, col))
    prev = lambda col: pl.BlockSpec(
        (SWA_BLOCK, 256), lambda b, i, col=col: (b * nb + jnp.maximum(i - 1, 0), col))
    return pl.pallas_call(
        _swa_body,
        grid=(BATCH, nb),
        in_specs=[
            pl.BlockSpec(memory_space=pltpu.SMEM),
            pl.BlockSpec((SWA_BLOCK, D_MODEL), lambda b, i: (b * nb + i, qcol)),
            cur(kcol), prev(kcol), cur(vcol), prev(vcol),
        ],
        out_specs=pl.BlockSpec((SWA_BLOCK, D_MODEL), lambda b, i: (b * nb + i, 0)),
        out_shape=jax.ShapeDtypeStruct((TOKENS, D_MODEL), BF16),
        compiler_params=_cparams(("parallel", "parallel")),
        name="swa",
    )(sinks, proj, proj, proj, proj, proj)


OUT_TM = 512
OUT_SUB = 256


def _outproj_body(x_ref, yg_ref, ys_ref, gg_ref, gs_ref, w_ref, nf_ref, wr_ref, rb_ref,
                  x1_ref, hp_ref, lg_ref):
    def sub_tile(t):
        rows = pl.ds(t * OUT_SUB, OUT_SUB)
        merged = (_sigmoid(gg_ref[rows, :].astype(F32)) * yg_ref[rows, :].astype(F32)
                  + _sigmoid(gs_ref[rows, :].astype(F32)) * ys_ref[rows, :].astype(F32))
        acc = jnp.dot(merged.astype(BF16), w_ref[...], preferred_element_type=F32)
        yield
        x1 = x_ref[rows, :] + acc
        x1_ref[rows, :] = x1
        ms = jnp.mean(x1 * x1, axis=-1, keepdims=True)
        h = x1 * lax.rsqrt(ms + EPS) * nf_ref[...]
        hh = h.astype(BF16)
        r = jnp.dot(hh, wr_ref[...], preferred_element_type=F32)
        lg_ref[rows, :] = r[:, :LANES] + r[:, LANES:] + rb_ref[...]
        half = D_MODEL // 2
        hf = hh.astype(F32)
        wa = pltpu.bitcast(hf[:, :half], U32)
        wb = pltpu.bitcast(hf[:, half:], U32)
        _store_token_tiles(hp_ref, t * OUT_SUB, (wa & jnp.uint32(0xFFFF0000)) | (wb >> 16))

    tiles = [sub_tile(t) for t in range(OUT_TM // OUT_SUB)]
    for _ in range(2):
        for g in tiles:
            next(g, None)


def _outproj(x2d, y_gdn, y_swa, proj, w_out, norm_ffn, wr2, r_bias):
    row = lambda w: pl.BlockSpec((OUT_TM, w), lambda i: (i, 0))
    const = lambda s: pl.BlockSpec(s, lambda i: (0, 0))
    return pl.pallas_call(
        _outproj_body,
        grid=(TOKENS // OUT_TM,),
        in_specs=[
            row(D_MODEL), row(D_MODEL), row(D_MODEL),
            pl.BlockSpec((OUT_TM, D_MODEL), lambda i: (i, 5)),
            pl.BlockSpec((OUT_TM, D_MODEL), lambda i: (i, 6)),
            pl.BlockSpec((D_MODEL, D_MODEL), lambda i: (0, 0), pipeline_mode=pl.Buffered(1)),
            const((1, D_MODEL)),
            const((D_MODEL, 2 * LANES)), const((1, LANES)),
        ],
        out_specs=[row(D_MODEL), pl.BlockSpec((OUT_TM * TILE_ROWS, LANES), lambda i: (i, 0)), row(LANES)],
        out_shape=[
            jax.ShapeDtypeStruct((TOKENS, D_MODEL), F32),
            jax.ShapeDtypeStruct((TOKENS * TILE_ROWS, LANES), U32),
            jax.ShapeDtypeStruct((TOKENS, LANES), F32),
        ],
        compiler_params=_cparams(("parallel",)),
        name="outproj",
    )(x2d, y_gdn, y_swa, proj, proj, w_out, norm_ffn, wr2, r_bias)


ROUTE_TM = 1024


def _route_body(lg_ref, tri_ref, idx_ref, wt_ref, cnt_ref, run_ref):
    @pl.when(pl.program_id(0) == 0)
    def _():
        run_ref[...] = jnp.zeros_like(run_ref)

    lg = lg_ref[...]
    lane = lax.broadcasted_iota(I32, lg.shape, 1)
    ninf = -jnp.inf
    big = jnp.int32(LANES)
    is_g = lane < N_GROUPS
    glog = jnp.where(is_g, lg, ninf)
    gmax = jnp.max(glog, axis=-1, keepdims=True)
    gden = jnp.sum(jnp.where(is_g, jnp.exp(lg - gmax), 0.0), axis=-1, keepdims=True)
    p_sel = 1.0 / gden
    grp = jnp.min(jnp.where(glog == gmax, lane, big), axis=-1, keepdims=True)
    emask = (lane >= N_GROUPS) & (lane < N_GROUPS + N_EXPERTS) & (((lane - N_GROUPS) >> 3) == grp)
    el = jnp.where(emask, lg, ninf)
    v1 = jnp.max(el, axis=-1, keepdims=True)
    i1 = jnp.min(jnp.where(el == v1, lane, big), axis=-1, keepdims=True)
    el2 = jnp.where(lane == i1, ninf, el)
    v2 = jnp.max(el2, axis=-1, keepdims=True)
    i2 = jnp.min(jnp.where(el2 == v2, lane, big), axis=-1, keepdims=True)
    e = jnp.exp(v2 - v1)
    w1 = p_sel / (1.0 + e)
    w2 = p_sel * e / (1.0 + e)
    e0 = i1 - N_GROUPS
    e1 = i2 - N_GROUPS

    oh0 = lane == e0
    oh1 = lane == e1
    onehot = jnp.where(oh0 | oh1, 1.0, 0.0)
    prefix = jnp.dot(tri_ref[...], onehot.astype(BF16), preferred_element_type=F32) + run_ref[0:1, :]
    r0 = jnp.sum(jnp.where(oh0, prefix, 0.0), axis=-1, keepdims=True).astype(I32)
    r1 = jnp.sum(jnp.where(oh1, prefix, 0.0), axis=-1, keepdims=True).astype(I32)
    run = run_ref[0:1, :] + jnp.sum(onehot, axis=0, keepdims=True)
    run_ref[...] = jnp.broadcast_to(run, run_ref.shape)
    cnt_ref[...] = jnp.broadcast_to(run, cnt_ref.shape).astype(I32)

    zi = jnp.zeros(lg.shape, I32)
    idx = jnp.where(lane == 0, e0, zi)
    idx = jnp.where(lane == 1, e1, idx)
    idx = jnp.where(lane == 2, r0, idx)
    idx = jnp.where(lane == 3, r1, idx)
    idx_ref[...] = idx
    wt_ref[...] = jnp.where(lane == 0, w1, jnp.where(lane == 1, w2, 0.0))


def _route(logits, tri):
    row = pl.BlockSpec((ROUTE_TM, LANES), lambda i: (i, 0))
    return pl.pallas_call(
        _route_body,
        grid=(TOKENS // ROUTE_TM,),
        in_specs=[row, pl.BlockSpec((ROUTE_TM, ROUTE_TM), lambda i: (0, 0))],
        out_specs=[row, row, pl.BlockSpec((8, LANES), lambda i: (0, 0))],
        out_shape=[
            jax.ShapeDtypeStruct((TOKENS, LANES), I32),
            jax.ShapeDtypeStruct((TOKENS, LANES), F32),
            jax.ShapeDtypeStruct((8, LANES), I32),
        ],
        scratch_shapes=[pltpu.VMEM((8, LANES), F32)],
        compiler_params=_cparams(("arbitrary",)),
        name="route",
    )(logits, tri)


DISP_TM = 256
N_DISP = TOKENS // DISP_TM
DISP_RING = 3
DISP_ROWS = DISP_TM * TILE_ROWS


def _dispatch_body(meta_ref, dest_ref, h_hbm, xs_ref, zbuf, hbuf, in_sem, out_sem, zsem):
    @pl.when(pl.program_id(0) == 0)
    def _():
        zbuf[...] = jnp.zeros_like(zbuf)

        def zero_block(row0):
            return pltpu.make_async_copy(
                zbuf, xs_ref.at[pl.ds(row0 * TILE_ROWS, MOE_BLOCK * TILE_ROWS), :], zsem)

        def per_expert(e, n):
            has = meta_ref[N_EXPERTS + e] > 0

            @pl.when(has)
            def _():
                zero_block(pl.multiple_of(meta_ref[e] - MOE_BLOCK, MOE_BLOCK)).start()

            return n + has.astype(I32)

        n_last = lax.fori_loop(0, N_EXPERTS, per_expert, jnp.int32(0))
        used = meta_ref[2 * N_EXPERTS]

        def tail(j, c):
            zero_block(pl.multiple_of(j * MOE_BLOCK, MOE_BLOCK)).start()
            return c

        lax.fori_loop(used, N_BLOCKS, tail, 0)

        def drain(i, c):
            zero_block(0).wait()
            return c

        lax.fori_loop(0, n_last + (N_BLOCKS - used), drain, 0)

    i = pl.program_id(0)
    b = lax.rem(i, DISP_RING)

    def load(step, slot):
        return pltpu.make_async_copy(
            h_hbm.at[pl.ds(pl.multiple_of(step * DISP_ROWS, DISP_ROWS), DISP_ROWS), :], hbuf.at[slot],
            in_sem.at[slot])

    def wait_scatters(slot):
        for k in range(TOP_K):
            pltpu.make_async_copy(hbuf.at[slot], xs_ref.at[pl.ds(0, DISP_ROWS), :], out_sem.at[slot]).wait()

    @pl.when(i == 0)
    def _():
        load(0, 0).start()

    load(i, b).wait()

    @pl.when(i + 1 < N_DISP)
    def _():
        load(i + 1, lax.rem(i + 1, DISP_RING)).start()

    def issue(r, c):
        for k in range(TOP_K):
            d = dest_ref[TOP_K * r + k]
            pltpu.make_async_copy(hbuf.at[b, pl.ds(pl.multiple_of(r * TILE_ROWS, TILE_ROWS), TILE_ROWS), :],
                                  xs_ref.at[pl.ds(pl.multiple_of(d * TILE_ROWS, TILE_ROWS), TILE_ROWS), :],
                                  out_sem.at[b]).start(priority=k)
        return c

    lax.fori_loop(0, DISP_TM, issue, 0, unroll=4)

    @pl.when(i >= 1)
    def _():
        wait_scatters(lax.rem(i + DISP_RING - 1, DISP_RING))

    @pl.when(i == N_DISP - 1)
    def _():
        wait_scatters(b)


def _dispatch(meta, dest_flat, hp):
    grid_spec = pltpu.PrefetchScalarGridSpec(
        num_scalar_prefetch=1,
        grid=(N_DISP,),
        in_specs=[
            pl.BlockSpec((DISP_TM * TOP_K,), lambda i, m: (i,), memory_space=pltpu.SMEM),
            pl.BlockSpec(memory_space=pl.ANY),
        ],
        out_specs=pl.BlockSpec(memory_space=pl.ANY),
        scratch_shapes=[
            pltpu.VMEM((MOE_BLOCK * TILE_ROWS, LANES), U32),
            pltpu.VMEM((DISP_RING, DISP_ROWS, LANES), U32),
            pltpu.SemaphoreType.DMA((DISP_RING,)),
            pltpu.SemaphoreType.DMA((DISP_RING,)),
            pltpu.SemaphoreType.DMA,
        ],
    )
    return pl.pallas_call(
        _dispatch_body,
        grid_spec=grid_spec,
        out_shape=jax.ShapeDtypeStruct((N_SLOTS * TILE_ROWS, LANES), U32),
        compiler_params=_cparams(("arbitrary",)),
        name="dispatch",
    )(meta, dest_flat, hp)


def _slots_body(idx_ref, start_ref, dest_ref):
    idx = idx_ref[...]
    lane = lax.broadcasted_iota(I32, idx.shape, 1)
    start = start_ref[...]
    out = jnp.zeros(idx.shape, I32)
    for k in range(TOP_K):
        e = idx[:, k:k + 1]
        base = jnp.sum(jnp.where(lane == e, start, 0.0), axis=-1, keepdims=True).astype(I32)
        out = jnp.where(lane == k, base + idx[:, TOP_K + k:TOP_K + k + 1], out)
    dest_ref[...] = out


SLOTS_TM = 2048


def _slots(idx, pad_start_row):
    row = pl.BlockSpec((SLOTS_TM, LANES), lambda i: (i, 0))
    return pl.pallas_call(
        _slots_body,
        grid=(TOKENS // SLOTS_TM,),
        in_specs=[row, pl.BlockSpec((1, LANES), lambda i: (0, 0))],
        out_specs=row,
        out_shape=jax.ShapeDtypeStruct((TOKENS, LANES), I32),
        compiler_params=_cparams(("parallel",)),
        name="slots",
    )(idx, pad_start_row)


def _experts_body(sched_ref, x_ref, wg_hbm, wu_hbm, wd_hbm, y_ref,
                  wg_st, wu_st, wd_st, wg_bf, wu_bf, wd_bf, sem):
    j = pl.program_id(0)
    e = sched_ref[j]
    slot = sched_ref[N_BLOCKS + j]
    nxt = sched_ref[2 * N_BLOCKS + j]
    used = sched_ref[3 * N_BLOCKS]
    new_expert = (j == 0) | (e != sched_ref[jnp.maximum(j - 1, 0)])

    def weight_copies(expert, s):
        return (pltpu.make_async_copy(wg_hbm.at[expert], wg_st.at[s], sem.at[s, 0]),
                pltpu.make_async_copy(wu_hbm.at[expert], wu_st.at[s], sem.at[s, 1]),
                pltpu.make_async_copy(wd_hbm.at[expert], wd_st.at[s], sem.at[s, 2]))

    @pl.when(j == 0)
    def _():
        for c in weight_copies(e, slot):
            c.start(priority=1)

    @pl.when(new_expert)
    def _():
        for c in weight_copies(e, slot):
            c.wait()

        @pl.when(nxt >= 0)
        def _():
            for c in weight_copies(nxt, 1 - slot):
                c.start(priority=1)

        wg_bf[...] = wg_st[slot].astype(BF16)
        wu_bf[...] = wu_st[slot].astype(BF16)
        wd_bf[...] = wd_st[slot].astype(BF16)

    @pl.when(j < used)
    def _():
        w = _load_token_tiles(x_ref, MOE_BLOCK)
        xa = pltpu.bitcast(w & jnp.uint32(0xFFFF0000), F32).astype(BF16)
        xb = pltpu.bitcast(w << 16, F32).astype(BF16)
        x = jnp.concatenate([xa, xb], axis=1)
        g = jnp.dot(x, wg_bf[...], preferred_element_type=F32)
        u = jnp.dot(x, wu_bf[...], preferred_element_type=F32)
        hb = (_silu(g) * u).astype(BF16)
        y = jnp.dot(hb, wd_bf[...], preferred_element_type=F32)
        half = D_MODEL // 2
        ya = pltpu.bitcast(y[:, :half].astype(BF16).astype(F32), U32)
        yb = pltpu.bitcast(y[:, half:].astype(BF16).astype(F32), U32)
        _store_token_tiles(y_ref, 0, ya | (yb >> 16))

    @pl.when(j >= used)
    def _():
        y_ref[...] = jnp.zeros_like(y_ref)


def _experts(sched, xs, w_gate, w_up, w_down):
    grid_spec = pltpu.PrefetchScalarGridSpec(
        num_scalar_prefetch=1,
        grid=(N_BLOCKS,),
        in_specs=[
            pl.BlockSpec((MOE_BLOCK * TILE_ROWS, LANES), lambda j, s: (j, 0)),
            pl.BlockSpec(memory_space=pl.ANY),
            pl.BlockSpec(memory_space=pl.ANY),
            pl.BlockSpec(memory_space=pl.ANY),
        ],
        out_specs=pl.BlockSpec((MOE_BLOCK * TILE_ROWS, LANES), lambda j, s: (j, 0)),
        scratch_shapes=[
            pltpu.VMEM((2, D_MODEL, D_EXPERT), F32),
            pltpu.VMEM((2, D_MODEL, D_EXPERT), F32),
            pltpu.VMEM((2, D_EXPERT, D_MODEL), F32),
            pltpu.VMEM((D_MODEL, D_EXPERT), BF16),
            pltpu.VMEM((D_MODEL, D_EXPERT), BF16),
            pltpu.VMEM((D_EXPERT, D_MODEL), BF16),
            pltpu.SemaphoreType.DMA((2, 3)),
        ],
    )
    return pl.pallas_call(
        _experts_body,
        grid_spec=grid_spec,
        out_shape=jax.ShapeDtypeStruct((N_SLOTS * TILE_ROWS, LANES), U32),
        compiler_params=_cparams(("arbitrary",)),
        name="experts",
    )(sched, xs, w_gate, w_up, w_down)


COMB_TM = 256
N_COMB = TOKENS // COMB_TM


def _unpack_pairs(w):
    hi = pltpu.bitcast(w & jnp.uint32(0xFFFF0000), F32)
    lo = pltpu.bitcast(w << 16, F32)
    return jnp.concatenate([hi, lo], axis=1)


def _combine_body(dest_ref, dest_next_ref, wt_ref, x1_ref, gain_ref, ys_ref, o_ref, buf, sem):
    i = pl.program_id(0)
    slot = lax.rem(i, 2)

    def start_gathers(dref, s):
        def issue(r, c):
            for k in range(TOP_K):
                d = dref[TOP_K * r + k]
                pltpu.make_async_copy(
                    ys_ref.at[pl.ds(pl.multiple_of(d * TILE_ROWS, TILE_ROWS), TILE_ROWS), :],
                    buf.at[s, k, pl.ds(pl.multiple_of(r * TILE_ROWS, TILE_ROWS), TILE_ROWS), :],
                    sem.at[s]).start(priority=k)
            return c
        lax.fori_loop(0, COMB_TM, issue, 0, unroll=4)

    @pl.when(i == 0)
    def _():
        start_gathers(dest_ref, slot)

    @pl.when(i + 1 < N_COMB)
    def _():
        start_gathers(dest_next_ref, 1 - slot)

    for k in range(TOP_K):
        pltpu.make_async_copy(ys_ref.at[pl.ds(0, COMB_TM * TILE_ROWS), :], buf.at[slot, k], sem.at[slot]).wait()

    wt = wt_ref[...]
    y = (x1_ref[...] + wt[:, 0:1] * _unpack_pairs(_load_token_tiles(buf.at[slot, 0], COMB_TM))
         + wt[:, 1:2] * _unpack_pairs(_load_token_tiles(buf.at[slot, 1], COMB_TM)))
    ms = jnp.mean(y * y, axis=-1, keepdims=True)
    o_ref[...] = y * lax.rsqrt(ms + EPS) * gain_ref[...]


def _combine(dest_flat, wts, x1, norm_final, ys):
    dblk = lambda f: pl.BlockSpec((COMB_TM * TOP_K,), f, memory_space=pltpu.SMEM)
    return pl.pallas_call(
        _combine_body,
        grid=(N_COMB,),
        in_specs=[
            dblk(lambda i: (i,)),
            dblk(lambda i: (jnp.minimum(i + 1, N_COMB - 1),)),
            pl.BlockSpec((COMB_TM, LANES), lambda i: (i, 0)),
            pl.BlockSpec((COMB_TM, D_MODEL), lambda i: (i, 0)),
            pl.BlockSpec((1, D_MODEL), lambda i: (0, 0)),
            pl.BlockSpec(memory_space=pl.ANY),
        ],
        out_specs=pl.BlockSpec((COMB_TM, D_MODEL), lambda i: (i, 0)),
        out_shape=jax.ShapeDtypeStruct((TOKENS, D_MODEL), F32),
        scratch_shapes=[pltpu.VMEM((2, TOP_K, COMB_TM * TILE_ROWS, LANES), U32), pltpu.SemaphoreType.DMA((2,))],
        compiler_params=_cparams(("arbitrary",)),
        name="combine",
    )(dest_flat, dest_flat, wts, x1, norm_final, ys)


def _pad_lanes(v):
    v = v.reshape(1, -1).astype(F32)
    return jnp.pad(v, ((0, 0), (0, LANES - v.shape[1])))


def kernel(x, norm_mix, w_in, conv_w, gdn_a_log, gdn_dt_bias, gdn_norm, swa_sinks, w_out, norm_ffn,
           w_router_group, b_router_group, w_router_expert, b_router_expert, w_gate, w_up, w_down,
           norm_final):
    l = 0
    x2d = x.reshape(TOKENS, D_MODEL)
    w = w_in[l]
    o_z = 3 * D_MODEL
    o_a = o_z + D_MODEL
    o_sq = o_a + 2 * GDN_HEADS
    o_sk = o_sq + D_MODEL
    o_sv = o_sk + SWA_KV_HEADS * SWA_DH
    o_gg = o_sv + SWA_KV_HEADS * SWA_DH
    o_gs = o_gg + D_MODEL
    w_main = jnp.concatenate(
        [w[:, :o_a], w[:, o_sq:o_sk], w[:, o_gg:o_gs], w[:, o_gs:], w[:, o_sk:o_sv], w[:, o_sv:o_gg]],
        axis=1).astype(BF16)
    w_ab = jnp.pad(w[:, o_a:o_sq], ((0, 0), (0, LANES - 2 * GDN_HEADS))).astype(BF16)

    proj, ab = _inproj(x2d, norm_mix[l].reshape(1, D_MODEL), w_main, w_ab, conv_w[l])

    g1, gt = _gdn_prep(ab, _pad_lanes(gdn_a_log[l]), _pad_lanes(gdn_dt_bias[l]))
    gr = gt[:, :GDN_HEADS, :].reshape(BATCH, GDN_HEADS, N_GROUPS_SEQ, GDN_GROUP_ROWS)
    y_gdn = _gdn(proj, g1, gr, gdn_norm[l].reshape(1, GDN_DV))

    y_swa = _swa(proj, swa_sinks[l].astype(F32))

    w_r = jnp.concatenate([w_router_group[l], w_router_expert[l]], axis=1).astype(F32)
    w_r = jnp.pad(w_r, ((0, 0), (0, LANES - w_r.shape[1])))
    wr_hi = w_r.astype(BF16)
    wr_lo = (w_r - wr_hi.astype(F32)).astype(BF16)
    r_bias = _pad_lanes(jnp.concatenate([b_router_group[l], b_router_expert[l]]))
    x1, hp, logits = _outproj(x2d, y_gdn, y_swa, proj, w_out[l].astype(BF16),
                              norm_ffn[l].reshape(1, D_MODEL), jnp.concatenate([wr_hi, wr_lo], axis=1), r_bias)

    tri = (lax.broadcasted_iota(I32, (ROUTE_TM, ROUTE_TM), 1)
           < lax.broadcasted_iota(I32, (ROUTE_TM, ROUTE_TM), 0)).astype(BF16)
    idx, wts, cnt = _route(logits, tri)

    counts = cnt[0, :N_EXPERTS]
    padded = (counts + MOE_BLOCK - 1) // MOE_BLOCK * MOE_BLOCK
    pad_end = jnp.cumsum(padded)
    pad_start = pad_end - padded
    dest = _slots(idx, _pad_lanes(pad_start))
    dest_flat = dest[:, :TOP_K].reshape(N_ASSIGN)
    blk_pos = jnp.arange(N_BLOCKS, dtype=I32) * MOE_BLOCK
    blk_e = jnp.minimum(jnp.sum((pad_end[None, :] <= blk_pos[:, None]).astype(I32), axis=1), N_EXPERTS - 1)
    used = pad_end[-1:] // MOE_BLOCK
    meta = jnp.concatenate([pad_end, padded, used]).astype(I32)
    is_new = jnp.concatenate([jnp.ones((1,), I32), (blk_e[1:] != blk_e[:-1]).astype(I32)])
    ordinal = jnp.cumsum(is_new) - 1
    nxt_pos = jnp.sum((ordinal[None, :] <= ordinal[:, None]).astype(I32), axis=1)
    nxt = jnp.where(nxt_pos < N_BLOCKS, blk_e[jnp.minimum(nxt_pos, N_BLOCKS - 1)], -1)
    sched = jnp.concatenate([blk_e, ordinal & 1, nxt, used]).astype(I32)

    xs = _dispatch(meta, dest_flat, hp)
    ys = _experts(sched, xs, w_gate[l], w_up[l], w_down[l])
    out = _combine(dest_flat, wts, x1, norm_final.reshape(1, D_MODEL), ys)
    return out.reshape(BATCH, SEQ, D_MODEL)
```

```python
import math

import jax
import jax.numpy as jnp
from jax import lax
from jax.experimental import pallas as pl
from jax.experimental.pallas import tpu as pltpu

F32 = jnp.float32
BF16 = jnp.bfloat16
I32 = jnp.int32
U32 = jnp.uint32

D_MODEL = 2048
BATCH = 16
SEQ = 2048
TOKENS = BATCH * SEQ
EPS = 1e-6

GDN_HEADS = 16
GDN_DK = 128
GDN_DV = 128
GDN_CHUNK = 64
CONV_W = 4
GDN_GROUP = 4
GDN_GROUP_ROWS = GDN_GROUP * GDN_CHUNK
N_CHUNKS = SEQ // GDN_CHUNK
N_GROUPS_SEQ = SEQ // GDN_GROUP_ROWS

SWA_HEADS = 32
SWA_KV_HEADS = 4
SWA_DH = 64
SWA_GROUP = 8
SWA_BLOCK = 128
WINDOW = 128

N_GROUPS = 8
EXPERTS_PER_GROUP = 8
N_EXPERTS = 64
TOP_K = 2
D_EXPERT = 512
MOE_BLOCK = 256
N_ASSIGN = TOKENS * TOP_K
N_BLOCKS = N_ASSIGN // MOE_BLOCK + N_EXPERTS
N_SLOTS = N_BLOCKS * MOE_BLOCK

PROJ_DIM = 7 * D_MODEL + 2 * SWA_KV_HEADS * SWA_DH
LANES = 128

VMEM_LIMIT = 56 * 1024 * 1024


def _cparams(sem, vmem=VMEM_LIMIT):
    return pltpu.CompilerParams(dimension_semantics=sem, vmem_limit_bytes=vmem)


TILE_ROWS = (D_MODEL // 2) // LANES


def _store_token_tiles(ref, first_token, words, first_block=0):
    n = words.shape[0]
    for s in range(words.shape[1] // LANES):
        ref[pl.ds(first_token * TILE_ROWS + first_block + s, n, stride=TILE_ROWS), :] = (
            words[:, s * LANES:(s + 1) * LANES])


def _load_token_tiles(ref, n):
    return jnp.concatenate([ref[pl.ds(s, n, stride=TILE_ROWS), :] for s in range(TILE_ROWS)], axis=1)


def _sigmoid(x):
    return 0.5 * jnp.tanh(0.5 * x) + 0.5


def _silu(x):
    h = 0.5 * x
    return h * jnp.tanh(h) + h


INPROJ_TM = 2048
INPROJ_TN = 512
INPROJ_VMEM = 60 * 1024 * 1024


INPROJ_RC = 256
INPROJ_RING = 3
QKV_TILES = 3 * D_MODEL // INPROJ_TN
QK_TILES = 2 * D_MODEL // INPROJ_TN
Q_TILES = D_MODEL // INPROJ_TN
assert INPROJ_TM == SEQ


def _inproj_body(x_ref, g_ref, w_ref, wab_ref, cw_ref, o_ref, ab_ref, h_ref, cpad):
    j = pl.program_id(1)

    @pl.when(j == 0)
    def _():
        def chunk(i, c):
            r = pl.ds(pl.multiple_of(i * 128, 128), 128)
            x = x_ref[r, :]
            ms = jnp.mean(x * x, axis=-1, keepdims=True)
            h_ref[r, :] = (x * lax.rsqrt(ms + EPS) * g_ref[...]).astype(BF16)
            return c
        lax.fori_loop(0, INPROJ_TM // 128, chunk, 0)
        ab_ref[...] = jnp.dot(h_ref[...], wab_ref[...], preferred_element_type=F32)

    @pl.when(j >= QKV_TILES)
    def _():
        o_ref[...] = jnp.dot(h_ref[...], w_ref[...], preferred_element_type=F32).astype(BF16)

    @pl.when(j < QKV_TILES)
    def _():
        cw = cw_ref[...]
        is_qk = j < QK_TILES
        qscale = jnp.where(j < Q_TILES, GDN_DK ** -0.5, 1.0).astype(F32)
        n_chunks = INPROJ_TM // INPROJ_RC

        def matmul(c):
            rows = pl.ds(c * INPROJ_RC, INPROJ_RC)
            cpad[c % INPROJ_RING, pl.ds(8, INPROJ_RC), :] = jnp.dot(
                h_ref[rows, :], w_ref[...], preferred_element_type=F32)

        def epilogue(c):
            buf = cpad.at[c % INPROJ_RING]
            if c == 0:
                buf[pl.ds(0, 8), :] = jnp.zeros((8, INPROJ_TN), F32)
            else:
                buf[pl.ds(0, 8), :] = cpad[(c - 1) % INPROJ_RING, pl.ds(INPROJ_RC, 8), :]
            y = None
            for s in range(CONV_W):
                t = buf[pl.ds(8 - s, INPROJ_RC), :] * cw[CONV_W - 1 - s:CONV_W - s, :]
                y = t if y is None else y + t
            y = _silu(y)
            parts = []
            for g in range(INPROJ_TN // LANES):
                yg = y[:, g * LANES:(g + 1) * LANES]
                inv = lax.rsqrt(jnp.sum(yg * yg, -1, keepdims=True) + EPS) * qscale
                parts.append(yg * jnp.where(is_qk, inv, 1.0))
            o_ref[pl.ds(c * INPROJ_RC, INPROJ_RC), :] = jnp.concatenate(parts, axis=1).astype(BF16)

        matmul(0)
        for c in range(n_chunks):
            if c + 1 < n_chunks:
                matmul(c + 1)
            epilogue(c)


def _inproj(x2d, gain, w_main, w_ab, conv_w):
    grid = (TOKENS // INPROJ_TM, PROJ_DIM // INPROJ_TN)
    return pl.pallas_call(
        _inproj_body,
        grid=grid,
        in_specs=[
            pl.BlockSpec((INPROJ_TM, D_MODEL), lambda i, j: (i, 0)),
            pl.BlockSpec((1, D_MODEL), lambda i, j: (0, 0)),
            pl.BlockSpec((D_MODEL, INPROJ_TN), lambda i, j: (0, j)),
            pl.BlockSpec((D_MODEL, LANES), lambda i, j: (0, 0)),
            pl.BlockSpec((CONV_W, INPROJ_TN), lambda i, j: (0, jnp.minimum(j, QKV_TILES - 1))),
        ],
        out_specs=[
            pl.BlockSpec((INPROJ_TM, INPROJ_TN), lambda i, j: (i, j)),
            pl.BlockSpec((INPROJ_TM, LANES), lambda i, j: (i, 0)),
        ],
        out_shape=[
            jax.ShapeDtypeStruct((TOKENS, PROJ_DIM), BF16),
            jax.ShapeDtypeStruct((TOKENS, LANES), F32),
        ],
        scratch_shapes=[pltpu.VMEM((INPROJ_TM, D_MODEL), BF16),
                        pltpu.VMEM((INPROJ_RING, 8 + INPROJ_RC, INPROJ_TN), F32)],
        compiler_params=_cparams(("parallel", "arbitrary"), vmem=INPROJ_VMEM),
        name="inproj",
    )(x2d, gain, w_main, w_ab, conv_w)


def _gdn_prep_body(ab_ref, alog_ref, dtb_ref, g1_ref, gt_ref):
    ab = ab_ref[...]
    lane = lax.broadcasted_iota(I32, ab.shape, 1)
    row = lax.broadcasted_iota(I32, ab.shape, 0) % GDN_CHUNK
    xa = ab + dtb_ref[...]
    softplus = jnp.maximum(xa, 0.0) + jnp.log(1.0 + jnp.exp(-jnp.abs(xa)))
    g = jnp.where(lane < GDN_HEADS, -jnp.exp(alog_ref[...]) * softplus, 0.0)
    gam = g
    s = 1
    while s < GDN_CHUNK:
        gam = gam + jnp.where(row >= s, pltpu.roll(gam, s, 0), 0.0)
        s *= 2
    g1_ref[...] = jnp.where(lane < GDN_HEADS, gam, _sigmoid(ab))
    gt_ref[0] = gam.T


def _gdn_prep(ab, alog_pad, dtb_pad):
    return pl.pallas_call(
        _gdn_prep_body,
        grid=(BATCH,),
        in_specs=[
            pl.BlockSpec((SEQ, LANES), lambda b: (b, 0)),
            pl.BlockSpec((1, LANES), lambda b: (0, 0)),
            pl.BlockSpec((1, LANES), lambda b: (0, 0)),
        ],
        out_specs=[
            pl.BlockSpec((SEQ, LANES), lambda b: (b, 0)),
            pl.BlockSpec((1, LANES, SEQ), lambda b: (b, 0, 0)),
        ],
        out_shape=[
            jax.ShapeDtypeStruct((TOKENS, LANES), F32),
            jax.ShapeDtypeStruct((BATCH, LANES, SEQ), F32),
        ],
        compiler_params=_cparams(("parallel",)),
        name="gdn_prep",
    )(ab, alog_pad, dtb_pad)


GDN_HB = 4
GDN_PAIR = 2
N_PAIRS = N_GROUPS_SEQ // GDN_PAIR
PAIR_CHUNKS = GDN_PAIR * GDN_GROUP
PAIR_ROWS = GDN_PAIR * GDN_GROUP_ROWS
GDN_VMEM = 59 * 1024 * 1024


def _block_diag(x):
    t = jnp.concatenate([x] * GDN_GROUP, axis=0)
    rb = lax.broadcasted_iota(I32, t.shape, 0) // GDN_CHUNK
    cb = lax.broadcasted_iota(I32, t.shape, 1) // GDN_CHUNK
    return jnp.where(rb == cb, t, jnp.zeros_like(t))


def _sbs_product(lhs_list, y):
    bd = _block_diag(y.astype(BF16))
    xs = [x.astype(BF16) for x in lhs_list]
    lhs = xs[0] if len(xs) == 1 else jnp.concatenate(xs, axis=0)
    r = jnp.dot(lhs, bd, preferred_element_type=F32)
    c = GDN_CHUNK
    return [r[i * c:(i + 1) * c] for i in range(len(xs))]


def _gdn_body(q_ref, k_ref, v_ref, z_ref, g1_ref, gr_ref, gain_ref,
              o_ref,
              kdbf, rhsbf, qd, gamb, betab, o0s, qts, vs, wps):
    hg = pl.program_id(1)
    shape = (PAIR_ROWS, LANES)
    lane = lax.broadcasted_iota(I32, shape, 1)

    def phase0(pair):
        rows = pl.ds(pl.multiple_of(pair * PAIR_ROWS, PAIR_ROWS), PAIR_ROWS)
        g1 = g1_ref[rows, :]
        for hd in range(GDN_HB):
            hs = slice(hd * LANES, (hd + 1) * LANES)
            head = hg * GDN_HB + hd

            def col(off):
                c = jnp.sum(jnp.where(lane == off + head, g1, 0.0), axis=-1, keepdims=True)
                return jnp.broadcast_to(c, shape)

            gam = col(0)
            beta = col(GDN_HEADS)
            gamb[hd, rows, :] = gam
            betab[hd, rows, :] = beta.astype(BF16)
            gam3 = gam.reshape(PAIR_CHUNKS, GDN_CHUNK, LANES)
            glast = jnp.broadcast_to(gam3[:, GDN_CHUNK - 1:GDN_CHUNK, :], gam3.shape).reshape(shape)
            eg = jnp.exp(gam)
            kdf = jnp.exp(glast - gam)

            qd[hd, rows, :] = (q_ref[rows, hs].astype(F32) * eg).astype(BF16)
            k = k_ref[rows, hs].astype(F32)
            kdbf[hd, rows, :] = (k * kdf).astype(BF16)
            rhsbf[hd, rows, LANES:] = (k * (beta * eg)).astype(BF16)
            rhsbf[hd, rows, :LANES] = (v_ref[rows, hs].astype(F32) * beta).astype(BF16)

    gshape = (GDN_CHUNK, GDN_GROUP_ROWS)
    ii = lax.broadcasted_iota(I32, gshape, 0)
    jj = lax.broadcasted_iota(I32, gshape, 1) % GDN_CHUNK
    eye = jnp.where(ii == jj, 1.0, 0.0).astype(F32)
    pr = lax.broadcasted_iota(I32, (2 * GDN_CHUNK, LANES), 0) // GDN_CHUNK
    tl = lax.broadcasted_iota(I32, (LANES, GDN_GROUP_ROWS), 1) // GDN_CHUNK
    lane_c = lax.broadcasted_iota(I32, (GDN_CHUNK, LANES), 1)
    nt = (((1,), (1,)), ((), ()))

    def sbs_bcast(ref, hd, base):
        parts = [ref[hd, pl.ds(base + m * GDN_CHUNK, GDN_CHUNK), :] for m in range(GDN_GROUP)]
        a = jnp.where(lane_c < GDN_CHUNK, parts[0], parts[1])
        b = jnp.where(lane_c < GDN_CHUNK, parts[2], parts[3])
        return jnp.concatenate([a, b], axis=1)

    def phase1(hd, pair, gg):
        g = pair * GDN_PAIR + gg
        base = pl.multiple_of(g * GDN_GROUP_ROWS, GDN_GROUP_ROWS)
        slot = lax.rem(pair, 2)
        res = []
        for p in range(2):
            rows = pl.ds(base + p * 2 * GDN_CHUNK, 2 * GDN_CHUNK)
            kst = k_ref[rows, hd * LANES:(hd + 1) * LANES]
            qst = q_ref[rows, hd * LANES:(hd + 1) * LANES]
            lhs_k = jnp.concatenate([kst[:GDN_CHUNK], kst[GDN_CHUNK:]], axis=1)
            lhs_q = jnp.concatenate([qst[:GDN_CHUNK], qst[GDN_CHUNK:]], axis=1)
            lhs = jnp.concatenate([lhs_q, lhs_k], axis=0)
            zero = jnp.zeros_like(kst)
            bt = jnp.concatenate([jnp.where(pr == 0, kst, zero), jnp.where(pr == 1, kst, zero)], axis=1)
            res.append(lax.dot_general(lhs, bt, nt, preferred_element_type=F32))
        yield
        qk = jnp.concatenate([res[0][:GDN_CHUNK], res[1][:GDN_CHUNK]], axis=1)
        kk = jnp.concatenate([res[0][GDN_CHUNK:], res[1][GDN_CHUNK:]], axis=1)

        gc = sbs_bcast(gamb, hd, base)
        bc = sbs_bcast(betab, hd, base).astype(F32)
        grow = gr_ref[0, hd, pl.ds(g, 1), :]
        decay = jnp.exp(jnp.where(ii >= jj, gc - grow, -jnp.inf))
        a = jnp.where(ii > jj, kk * decay * bc, 0.0)
        qkd = qk * decay

        u = eye - a
        (x,) = _sbs_product([a], a)
        yield
        for lvl in range(1, 6):
            if lvl < 5:
                x2, ux = _sbs_product([x, u], x)
                u = u + ux
                x = x2
            else:
                (ux,) = _sbs_product([u], x)
                u = u + ux
            yield

        rows4 = pl.ds(base, GDN_GROUP_ROWS)
        so = jnp.dot(_block_diag(u.astype(BF16)), rhsbf[hd, rows4, :], preferred_element_type=F32)
        yield
        sol = so.astype(BF16)

        kdt = kdbf[hd, rows4, :].astype(F32).T.astype(BF16)
        zt = jnp.zeros_like(kdt)
        lhs2 = jnp.concatenate(
            [_block_diag(qkd.astype(BF16))] + [jnp.where(tl == m, kdt, zt) for m in range(GDN_GROUP)],
            axis=0)
        r = jnp.dot(lhs2, sol, preferred_element_type=F32)
        srows = pl.ds(pl.multiple_of(slot * PAIR_ROWS + gg * GDN_GROUP_ROWS, GDN_GROUP_ROWS), GDN_GROUP_ROWS)
        o0s[hd, srows, :] = r[:GDN_GROUP_ROWS, :LANES]
        qts[hd, srows, :] = qd[hd, rows4, :].astype(F32) - r[:GDN_GROUP_ROWS, LANES:]
        for m in range(GDN_GROUP):
            blk = r[GDN_GROUP_ROWS + m * LANES:GDN_GROUP_ROWS + (m + 1) * LANES]
            dst = pl.ds(pl.multiple_of((slot * PAIR_CHUNKS + gg * GDN_GROUP + m) * LANES, LANES), LANES)
            vs[hd, dst, :] = blk[:, :LANES]
            wps[hd, dst, :] = blk[:, LANES:]

    gain = gain_ref[...]

    def phase2(hd, pair, j, s):
        c = pair * PAIR_CHUNKS + j
        local = lax.rem(pair, 2) * PAIR_CHUNKS + j
        rows = pl.ds(pl.multiple_of(c * GDN_CHUNK, GDN_CHUNK), GDN_CHUNK)
        lrows = pl.ds(pl.multiple_of(local * GDN_CHUNK, GDN_CHUNK), GDN_CHUNK)
        srow = pl.ds(pl.multiple_of(local * LANES, LANES), LANES)
        lhs = jnp.concatenate([wps[hd, srow, :].astype(BF16), qts[hd, lrows, :].astype(BF16)], axis=0)
        r = jnp.dot(lhs, s.astype(BF16), preferred_element_type=F32)
        o = r[LANES:] + o0s[hd, lrows, :]
        dec = jnp.exp(gamb[hd, pl.ds(c * GDN_CHUNK + GDN_CHUNK - 1, 1), :])
        s_new = s * dec + vs[hd, srow, :] - r[:LANES]
        y = o * lax.rsqrt(jnp.mean(o * o, -1, keepdims=True) + EPS) * gain
        zz = z_ref[rows, hd * LANES:(hd + 1) * LANES].astype(F32)
        o_ref[rows, hd * LANES:(hd + 1) * LANES] = (y * _silu(zz)).astype(BF16)
        return s_new

    def phase2_chain(hd, pair, states):
        s = states[hd]
        for j in range(PAIR_CHUNKS):
            s = phase2(hd, pair, j, s)
            yield
        states[hd] = s

    def run_interleaved(chains):
        live = list(chains)
        while live:
            nxt = []
            for c in live:
                try:
                    next(c)
                    nxt.append(c)
                except StopIteration:
                    pass
            live = nxt

    def phase1_chains(pair):
        return [phase1(hd, pair, gg) for gg in range(GDN_PAIR) for hd in range(GDN_HB)]

    for pair in range(N_PAIRS):
        phase0(pair)
    run_interleaved(phase1_chains(0))

    def body(pair, states):
        states = list(states)
        run_interleaved([phase2_chain(hd, pair - 1, states) for hd in range(GDN_HB)] + phase1_chains(pair))
        return tuple(states)

    s0 = tuple(jnp.zeros((GDN_DK, GDN_DV), F32) for _ in range(GDN_HB))
    states = list(lax.fori_loop(1, N_PAIRS, body, s0))
    run_interleaved([phase2_chain(hd, N_PAIRS - 1, states) for hd in range(GDN_HB)])


def _gdn(proj, g1, gr, gdn_norm):
    w = GDN_HB * LANES
    per_row = D_MODEL // w
    hblk = lambda off: pl.BlockSpec((SEQ, w), lambda b, h, off=off: (b, off * per_row + h))
    big = lambda dt: pltpu.VMEM((GDN_HB, SEQ, LANES), dt)
    return pl.pallas_call(
        _gdn_body,
        grid=(BATCH, GDN_HEADS // GDN_HB),
        in_specs=[
            hblk(0), hblk(1), hblk(2), hblk(3),
            pl.BlockSpec((SEQ, LANES), lambda b, h: (b, 0)),
            pl.BlockSpec((1, GDN_HB, N_GROUPS_SEQ, GDN_GROUP_ROWS), lambda b, h: (b, h, 0, 0)),
            pl.BlockSpec((1, LANES), lambda b, h: (0, 0)),
        ],
        out_specs=pl.BlockSpec((SEQ, w), lambda b, h: (b, h)),
        out_shape=jax.ShapeDtypeStruct((TOKENS, D_MODEL), BF16),
        scratch_shapes=[
            big(BF16), pltpu.VMEM((GDN_HB, SEQ, 2 * LANES), BF16),
            big(BF16), big(F32), big(BF16),
            pltpu.VMEM((GDN_HB, 2 * PAIR_ROWS, LANES), F32),
            pltpu.VMEM((GDN_HB, 2 * PAIR_ROWS, LANES), F32),
            pltpu.VMEM((GDN_HB, 2 * PAIR_CHUNKS * LANES, LANES), F32),
            pltpu.VMEM((GDN_HB, 2 * PAIR_CHUNKS * LANES, LANES), F32),
        ],
        compiler_params=_cparams(("parallel", "parallel"), vmem=GDN_VMEM),
        name="gdn",
    )(proj, proj, proj, proj, g1, gr, gdn_norm)


SWA_PIPE = 4
assert WINDOW == SWA_BLOCK


def _swa_body(sink_ref, q_ref, kc_ref, kp_ref, vc_ref, vp_ref, o_ref):
    i = pl.program_id(1)
    kcat = jnp.concatenate([kp_ref[...], kc_ref[...]], axis=0)
    vcat = jnp.concatenate([vp_ref[...], vc_ref[...]], axis=0)
    qi = lax.broadcasted_iota(I32, (SWA_BLOCK, SWA_BLOCK), 0)
    ki = lax.broadcasted_iota(I32, (SWA_BLOCK, SWA_BLOCK), 1)
    take_cur = ki <= qi
    valid = take_cur | (i > 0)
    lane_kv = lax.broadcasted_iota(I32, (2 * SWA_BLOCK, LANES), 1)
    lane_q = lax.broadcasted_iota(I32, (SWA_BLOCK, LANES), 1)
    nt = (((1,), (1,)), ((), ()))
    scale = SWA_DH ** -0.5
    assert math.frexp(scale)[0] == 0.5

    def head_pair(h0, ks, vs):
        qs = q_ref[:, h0 * SWA_DH:(h0 + 2) * SWA_DH]
        scores = []
        for kk in ks:
            sc = lax.dot_general(qs, kk, nt, preferred_element_type=F32)
            s = jnp.where(take_cur, sc[:, SWA_BLOCK:], sc[:, :SWA_BLOCK])
            scores.append(jnp.where(valid, s, -jnp.inf))
        for _ in range(SWA_PIPE):
            yield
        acc = None
        inv = None
        for half in range(2):
            sink = sink_ref[h0 + half]
            s = scores[half]
            m = jnp.maximum(jnp.max(s, axis=-1, keepdims=True), sink)
            p = jnp.exp(s - m)
            den = jnp.sum(p, axis=-1, keepdims=True) + jnp.exp(sink - m)
            p2 = jnp.concatenate([jnp.where(take_cur, 0.0, p), jnp.where(take_cur, p, 0.0)], axis=1)
            pv = jnp.dot(p2.astype(BF16), vs[half], preferred_element_type=F32)
            acc = pv if acc is None else acc + pv
            r = 1.0 / den
            inv = r if inv is None else jnp.where(lane_q < SWA_DH, inv, r)
        yield
        o_ref[:, h0 * SWA_DH:(h0 + 2) * SWA_DH] = (acc * inv).astype(BF16)

    chains = []
    for slab in range(SWA_KV_HEADS // 2):
        k2 = kcat[:, slab * LANES:(slab + 1) * LANES].astype(F32) * scale
        v2 = vcat[:, slab * LANES:(slab + 1) * LANES].astype(F32)
        k2r = pltpu.roll(k2, SWA_DH, 1)
        v2r = pltpu.roll(v2, SWA_DH, 1)
        for sub in range(2):
            kvh = slab * 2 + sub
            if sub == 0:
                k_lo = jnp.where(lane_kv < SWA_DH, k2, 0.0)
                k_hi = jnp.where(lane_kv >= SWA_DH, k2r, 0.0)
                v_lo = jnp.where(lane_kv < SWA_DH, v2, 0.0)
                v_hi = jnp.where(lane_kv >= SWA_DH, v2r, 0.0)
            else:
                k_lo = jnp.where(lane_kv < SWA_DH, k2r, 0.0)
                k_hi = jnp.where(lane_kv >= SWA_DH, k2, 0.0)
                v_lo = jnp.where(lane_kv < SWA_DH, v2r, 0.0)
                v_hi = jnp.where(lane_kv >= SWA_DH, v2, 0.0)
            k_lo, k_hi, v_lo, v_hi = (t.astype(BF16) for t in (k_lo, k_hi, v_lo, v_hi))
            for gp in range(SWA_GROUP // 2):
                h0 = kvh * SWA_GROUP + 2 * gp
                chains.append(head_pair(h0, (k_lo, k_hi), (v_lo, v_hi)))

    live = []
    pending = list(chains)
    while pending or live:
        if pending:
            live.append(pending.pop(0))
        nxt = []
        for c in live:
            try:
                next(c)
                nxt.append(c)
            except StopIteration:
                pass
        live = nxt


def _swa(proj, sinks):
    nb = SEQ // SWA_BLOCK
    qcol = 4
    kcol = (7 * D_MODEL) // 256
    vcol = kcol + 1
    cur = lambda col: pl.BlockSpec((SWA_BLOCK, 256), lambda b, i, col=col: (b * nb + i, col))
    prev = lambda col: pl.BlockSpec(
        (SWA_BLOCK, 256), lambda b, i, col=col: (b * nb + jnp.maximum(i - 1, 0), col))
    return pl.pallas_call(
        _swa_body,
        grid=(BATCH, nb),
        in_specs=[
            pl.BlockSpec(memory_space=pltpu.SMEM),
            pl.BlockSpec((SWA_BLOCK, D_MODEL), lambda b, i: (b * nb + i, qcol)),
            cur(kcol), prev(kcol), cur(vcol), prev(vcol),
        ],
        out_specs=pl.BlockSpec((SWA_BLOCK, D_MODEL), lambda b, i: (b * nb + i, 0)),
        out_shape=jax.ShapeDtypeStruct((TOKENS, D_MODEL), BF16),
        compiler_params=_cparams(("parallel", "parallel")),
        name="swa",
    )(sinks, proj, proj, proj, proj, proj)


OUT_TM = 512
OUT_SUB = 256


def _outproj_body(x_ref, yg_ref, ys_ref, gg_ref, gs_ref, w_ref, nf_ref, wr_ref, rb_ref,
                  x1_ref, hp_ref, lg_ref):
    def sub_tile(t):
        rows = pl.ds(t * OUT_SUB, OUT_SUB)
        merged = (_sigmoid(gg_ref[rows, :].astype(F32)) * yg_ref[rows, :].astype(F32)
                  + _sigmoid(gs_ref[rows, :].astype(F32)) * ys_ref[rows, :].astype(F32))
        acc = jnp.dot(merged.astype(BF16), w_ref[...], preferred_element_type=F32)
        yield
        x1 = x_ref[rows, :] + acc
        x1_ref[rows, :] = x1
        ms = jnp.mean(x1 * x1, axis=-1, keepdims=True)
        h = x1 * lax.rsqrt(ms + EPS) * nf_ref[...]
        hh = h.astype(BF16)
        r = jnp.dot(hh, wr_ref[...], preferred_element_type=F32)
        lg_ref[rows, :] = r[:, :LANES] + r[:, LANES:] + rb_ref[...]
        half = D_MODEL // 2
        hf = hh.astype(F32)
        wa = pltpu.bitcast(hf[:, :half], U32)
        wb = pltpu.bitcast(hf[:, half:], U32)
        _store_token_tiles(hp_ref, t * OUT_SUB, (wa & jnp.uint32(0xFFFF0000)) | (wb >> 16))

    tiles = [sub_tile(t) for t in range(OUT_TM // OUT_SUB)]
    for _ in range(2):
        for g in tiles:
            next(g, None)


def _outproj(x2d, y_gdn, y_swa, proj, w_out, norm_ffn, wr2, r_bias):
    row = lambda w: pl.BlockSpec((OUT_TM, w), lambda i: (i, 0))
    const = lambda s: pl.BlockSpec(s, lambda i: (0, 0))
    return pl.pallas_call(
        _outproj_body,
        grid=(TOKENS // OUT_TM,),
        in_specs=[
            row(D_MODEL), row(D_MODEL), row(D_MODEL),
            pl.BlockSpec((OUT_TM, D_MODEL), lambda i: (i, 5)),
            pl.BlockSpec((OUT_TM, D_MODEL), lambda i: (i, 6)),
            pl.BlockSpec((D_MODEL, D_MODEL), lambda i: (0, 0), pipeline_mode=pl.Buffered(1)),
            const((1, D_MODEL)),
            const((D_MODEL, 2 * LANES)), const((1, LANES)),
        ],
        out_specs=[row(D_MODEL), pl.BlockSpec((OUT_TM * TILE_ROWS, LANES), lambda i: (i, 0)), row(LANES)],
        out_shape=[
            jax.ShapeDtypeStruct((TOKENS, D_MODEL), F32),
            jax.ShapeDtypeStruct((TOKENS * TILE_ROWS, LANES), U32),
            jax.ShapeDtypeStruct((TOKENS, LANES), F32),
        ],
        compiler_params=_cparams(("parallel",)),
        name="outproj",
    )(x2d, y_gdn, y_swa, proj, proj, w_out, norm_ffn, wr2, r_bias)


ROUTE_TM = 1024


def _route_body(lg_ref, tri_ref, idx_ref, wt_ref, cnt_ref, run_ref):
    @pl.when(pl.program_id(0) == 0)
    def _():
        run_ref[...] = jnp.zeros_like(run_ref)

    lg = lg_ref[...]
    lane = lax.broadcasted_iota(I32, lg.shape, 1)
    ninf = -jnp.inf
    big = jnp.int32(LANES)
    is_g = lane < N_GROUPS
    glog = jnp.where(is_g, lg, ninf)
    gmax = jnp.max(glog, axis=-1, keepdims=True)
    gden = jnp.sum(jnp.where(is_g, jnp.exp(lg - gmax), 0.0), axis=-1, keepdims=True)
    p_sel = 1.0 / gden
    grp = jnp.min(jnp.where(glog == gmax, lane, big), axis=-1, keepdims=True)
    emask = (lane >= N_GROUPS) & (lane < N_GROUPS + N_EXPERTS) & (((lane - N_GROUPS) >> 3) == grp)
    el = jnp.where(emask, lg, ninf)
    v1 = jnp.max(el, axis=-1, keepdims=True)
    i1 = jnp.min(jnp.where(el == v1, lane, big), axis=-1, keepdims=True)
    el2 = jnp.where(lane == i1, ninf, el)
    v2 = jnp.max(el2, axis=-1, keepdims=True)
    i2 = jnp.min(jnp.where(el2 == v2, lane, big), axis=-1, keepdims=True)
    e = jnp.exp(v2 - v1)
    w1 = p_sel / (1.0 + e)
    w2 = p_sel * e / (1.0 + e)
    e0 = i1 - N_GROUPS
    e1 = i2 - N_GROUPS

    oh0 = lane == e0
    oh1 = lane == e1
    onehot = jnp.where(oh0 | oh1, 1.0, 0.0)
    prefix = jnp.dot(tri_ref[...], onehot.astype(BF16), preferred_element_type=F32) + run_ref[0:1, :]
    r0 = jnp.sum(jnp.where(oh0, prefix, 0.0), axis=-1, keepdims=True).astype(I32)
    r1 = jnp.sum(jnp.where(oh1, prefix, 0.0), axis=-1, keepdims=True).astype(I32)
    run = run_ref[0:1, :] + jnp.sum(onehot, axis=0, keepdims=True)
    run_ref[...] = jnp.broadcast_to(run, run_ref.shape)
    cnt_ref[...] = jnp.broadcast_to(run, cnt_ref.shape).astype(I32)

    zi = jnp.zeros(lg.shape, I32)
    idx = jnp.where(lane == 0, e0, zi)
    idx = jnp.where(lane == 1, e1, idx)
    idx = jnp.where(lane == 2, r0, idx)
    idx = jnp.where(lane == 3, r1, idx)
    idx_ref[...] = idx
    wt_ref[...] = jnp.where(lane == 0, w1, jnp.where(lane == 1, w2, 0.0))


def _route(logits, tri):
    row = pl.BlockSpec((ROUTE_TM, LANES), lambda i: (i, 0))
    return pl.pallas_call(
        _route_body,
        grid=(TOKENS // ROUTE_TM,),
        in_specs=[row, pl.BlockSpec((ROUTE_TM, ROUTE_TM), lambda i: (0, 0))],
        out_specs=[row, row, pl.BlockSpec((8, LANES), lambda i: (0, 0))],
        out_shape=[
            jax.ShapeDtypeStruct((TOKENS, LANES), I32),
            jax.ShapeDtypeStruct((TOKENS, LANES), F32),
            jax.ShapeDtypeStruct((8, LANES), I32),
        ],
        scratch_shapes=[pltpu.VMEM((8, LANES), F32)],
        compiler_params=_cparams(("arbitrary",)),
        name="route",
    )(logits, tri)


DISP_TM = 256
N_DISP = TOKENS // DISP_TM
DISP_RING = 3
DISP_ROWS = DISP_TM * TILE_ROWS


def _dispatch_body(meta_ref, dest_ref, h_hbm, xs_ref, zbuf, hbuf, in_sem, out_sem, zsem):
    @pl.when(pl.program_id(0) == 0)
    def _():
        zbuf[...] = jnp.zeros_like(zbuf)

        def zero_block(row0):
            return pltpu.make_async_copy(
                zbuf, xs_ref.at[pl.ds(row0 * TILE_ROWS, MOE_BLOCK * TILE_ROWS), :], zsem)

        def per_expert(e, n):
            has = meta_ref[N_EXPERTS + e] > 0

            @pl.when(has)
            def _():
                zero_block(pl.multiple_of(meta_ref[e] - MOE_BLOCK, MOE_BLOCK)).start()

            return n + has.astype(I32)

        n_last = lax.fori_loop(0, N_EXPERTS, per_expert, jnp.int32(0))
        used = meta_ref[2 * N_EXPERTS]

        def tail(j, c):
            zero_block(pl.multiple_of(j * MOE_BLOCK, MOE_BLOCK)).start()
            return c

        lax.fori_loop(used, N_BLOCKS, tail, 0)

        def drain(i, c):
            zero_block(0).wait()
            return c

        lax.fori_loop(0, n_last + (N_BLOCKS - used), drain, 0)

    i = pl.program_id(0)
    b = lax.rem(i, DISP_RING)

    def load(step, slot):
        return pltpu.make_async_copy(
            h_hbm.at[pl.ds(pl.multiple_of(step * DISP_ROWS, DISP_ROWS), DISP_ROWS), :], hbuf.at[slot],
            in_sem.at[slot])

    def wait_scatters(slot):
        for k in range(TOP_K):
            pltpu.make_async_copy(hbuf.at[slot], xs_ref.at[pl.ds(0, DISP_ROWS), :], out_sem.at[slot]).wait()

    @pl.when(i == 0)
    def _():
        load(0, 0).start()

    load(i, b).wait()

    @pl.when(i + 1 < N_DISP)
    def _():
        load(i + 1, lax.rem(i + 1, DISP_RING)).start()

    def issue(r, c):
        for k in range(TOP_K):
            d = dest_ref[TOP_K * r + k]
            pltpu.make_async_copy(hbuf.at[b, pl.ds(pl.multiple_of(r * TILE_ROWS, TILE_ROWS), TILE_ROWS), :],
                                  xs_ref.at[pl.ds(pl.multiple_of(d * TILE_ROWS, TILE_ROWS), TILE_ROWS), :],
                                  out_sem.at[b]).start(priority=k)
        return c

    lax.fori_loop(0, DISP_TM, issue, 0, unroll=4)

    @pl.when(i >= 1)
    def _():
        wait_scatters(lax.rem(i + DISP_RING - 1, DISP_RING))

    @pl.when(i == N_DISP - 1)
    def _():
        wait_scatters(b)


def _dispatch(meta, dest_flat, hp):
    grid_spec = pltpu.PrefetchScalarGridSpec(
        num_scalar_prefetch=1,
        grid=(N_DISP,),
        in_specs=[
            pl.BlockSpec((DISP_TM * TOP_K,), lambda i, m: (i,), memory_space=pltpu.SMEM),
            pl.BlockSpec(memory_space=pl.ANY),
        ],
        out_specs=pl.BlockSpec(memory_space=pl.ANY),
        scratch_shapes=[
            pltpu.VMEM((MOE_BLOCK * TILE_ROWS, LANES), U32),
            pltpu.VMEM((DISP_RING, DISP_ROWS, LANES), U32),
            pltpu.SemaphoreType.DMA((DISP_RING,)),
            pltpu.SemaphoreType.DMA((DISP_RING,)),
            pltpu.SemaphoreType.DMA,
        ],
    )
    return pl.pallas_call(
        _dispatch_body,
        grid_spec=grid_spec,
        out_shape=jax.ShapeDtypeStruct((N_SLOTS * TILE_ROWS, LANES), U32),
        compiler_params=_cparams(("arbitrary",)),
        name="dispatch",
    )(meta, dest_flat, hp)


def _slots_body(idx_ref, start_ref, dest_ref):
    idx = idx_ref[...]
    lane = lax.broadcasted_iota(I32, idx.shape, 1)
    start = start_ref[...]
    out = jnp.zeros(idx.shape, I32)
    for k in range(TOP_K):
        e = idx[:, k:k + 1]
        base = jnp.sum(jnp.where(lane == e, start, 0.0), axis=-1, keepdims=True).astype(I32)
        out = jnp.where(lane == k, base + idx[:, TOP_K + k:TOP_K + k + 1], out)
    dest_ref[...] = out


SLOTS_TM = 2048


def _slots(idx, pad_start_row):
    row = pl.BlockSpec((SLOTS_TM, LANES), lambda i: (i, 0))
    return pl.pallas_call(
        _slots_body,
        grid=(TOKENS // SLOTS_TM,),
        in_specs=[row, pl.BlockSpec((1, LANES), lambda i: (0, 0))],
        out_specs=row,
        out_shape=jax.ShapeDtypeStruct((TOKENS, LANES), I32),
        compiler_params=_cparams(("parallel",)),
        name="slots",
    )(idx, pad_start_row)


EXPERT_OUT_COLS = 256


def _experts_body(sched_ref, x_ref, wg_hbm, wu_hbm, wd_hbm, y_ref,
                  wg_st, wu_st, wd_st, wg_bf, wu_bf, wd_bf, sem):
    j = pl.program_id(0)
    e = sched_ref[j]
    slot = sched_ref[N_BLOCKS + j]
    nxt = sched_ref[2 * N_BLOCKS + j]
    used = sched_ref[3 * N_BLOCKS]
    new_expert = (j == 0) | (e != sched_ref[jnp.maximum(j - 1, 0)])

    def weight_copies(expert, s):
        return (pltpu.make_async_copy(wg_hbm.at[expert], wg_st.at[s], sem.at[s, 0]),
                pltpu.make_async_copy(wu_hbm.at[expert], wu_st.at[s], sem.at[s, 1]),
                pltpu.make_async_copy(wd_hbm.at[expert], wd_st.at[s], sem.at[s, 2]))

    @pl.when(j == 0)
    def _():
        for c in weight_copies(e, slot):
            c.start(priority=1)

    @pl.when(new_expert)
    def _():
        for c in weight_copies(e, slot):
            c.wait()

        @pl.when(nxt >= 0)
        def _():
            for c in weight_copies(nxt, 1 - slot):
                c.start(priority=1)

        wg_bf[...] = wg_st[slot].astype(BF16)
        wu_bf[...] = wu_st[slot].astype(BF16)
        wd_bf[...] = wd_st[slot].astype(BF16)

    @pl.when(j < used)
    def _():
        w = _load_token_tiles(x_ref, MOE_BLOCK)
        xa = pltpu.bitcast(w & jnp.uint32(0xFFFF0000), F32).astype(BF16)
        xb = pltpu.bitcast(w << 16, F32).astype(BF16)
        half = D_MODEL // 2
        g = (jnp.dot(xa, wg_bf[:half, :], preferred_element_type=F32)
             + jnp.dot(xb, wg_bf[half:, :], preferred_element_type=F32))
        u = (jnp.dot(xa, wu_bf[:half, :], preferred_element_type=F32)
             + jnp.dot(xb, wu_bf[half:, :], preferred_element_type=F32))
        hb = (_silu(g) * u).astype(BF16)
        for c in range(half // EXPERT_OUT_COLS):
            lo = c * EXPERT_OUT_COLS
            ya = jnp.dot(hb, wd_bf[:, lo:lo + EXPERT_OUT_COLS], preferred_element_type=F32)
            yb = jnp.dot(hb, wd_bf[:, half + lo:half + lo + EXPERT_OUT_COLS], preferred_element_type=F32)
            wa = pltpu.bitcast(ya.astype(BF16).astype(F32), U32)
            wb = pltpu.bitcast(yb.astype(BF16).astype(F32), U32)
            _store_token_tiles(y_ref, 0, wa | (wb >> 16), first_block=lo // LANES)

    @pl.when(j >= used)
    def _():
        y_ref[...] = jnp.zeros_like(y_ref)


def _experts(sched, xs, w_gate, w_up, w_down):
    grid_spec = pltpu.PrefetchScalarGridSpec(
        num_scalar_prefetch=1,
        grid=(N_BLOCKS,),
        in_specs=[
            pl.BlockSpec((MOE_BLOCK * TILE_ROWS, LANES), lambda j, s: (j, 0)),
            pl.BlockSpec(memory_space=pl.ANY),
            pl.BlockSpec(memory_space=pl.ANY),
            pl.BlockSpec(memory_space=pl.ANY),
        ],
        out_specs=pl.BlockSpec((MOE_BLOCK * TILE_ROWS, LANES), lambda j, s: (j, 0)),
        scratch_shapes=[
            pltpu.VMEM((2, D_MODEL, D_EXPERT), F32),
            pltpu.VMEM((2, D_MODEL, D_EXPERT), F32),
            pltpu.VMEM((2, D_EXPERT, D_MODEL), F32),
            pltpu.VMEM((D_MODEL, D_EXPERT), BF16),
            pltpu.VMEM((D_MODEL, D_EXPERT), BF16),
            pltpu.VMEM((D_EXPERT, D_MODEL), BF16),
            pltpu.SemaphoreType.DMA((2, 3)),
        ],
    )
    return pl.pallas_call(
        _experts_body,
        grid_spec=grid_spec,
        out_shape=jax.ShapeDtypeStruct((N_SLOTS * TILE_ROWS, LANES), U32),
        compiler_params=_cparams(("arbitrary",)),
        name="experts",
    )(sched, xs, w_gate, w_up, w_down)


COMB_TM = 256
N_COMB = TOKENS // COMB_TM


def _unpack_pairs(w):
    hi = pltpu.bitcast(w & jnp.uint32(0xFFFF0000), F32)
    lo = pltpu.bitcast(w << 16, F32)
    return jnp.concatenate([hi, lo], axis=1)


def _combine_body(dest_ref, dest_next_ref, wt_ref, x1_ref, gain_ref, ys_ref, o_ref, buf, sem):
    i = pl.program_id(0)
    slot = lax.rem(i, 2)

    def start_gathers(dref, s):
        def issue(r, c):
            for k in range(TOP_K):
                d = dref[TOP_K * r + k]
                pltpu.make_async_copy(
                    ys_ref.at[pl.ds(pl.multiple_of(d * TILE_ROWS, TILE_ROWS), TILE_ROWS), :],
                    buf.at[s, k, pl.ds(pl.multiple_of(r * TILE_ROWS, TILE_ROWS), TILE_ROWS), :],
                    sem.at[s]).start(priority=k)
            return c
        lax.fori_loop(0, COMB_TM, issue, 0, unroll=4)

    @pl.when(i == 0)
    def _():
        start_gathers(dest_ref, slot)

    @pl.when(i + 1 < N_COMB)
    def _():
        start_gathers(dest_next_ref, 1 - slot)

    for k in range(TOP_K):
        pltpu.make_async_copy(ys_ref.at[pl.ds(0, COMB_TM * TILE_ROWS), :], buf.at[slot, k], sem.at[slot]).wait()

    wt = wt_ref[...]
    y = (x1_ref[...] + wt[:, 0:1] * _unpack_pairs(_load_token_tiles(buf.at[slot, 0], COMB_TM))
         + wt[:, 1:2] * _unpack_pairs(_load_token_tiles(buf.at[slot, 1], COMB_TM)))
    ms = jnp.mean(y * y, axis=-1, keepdims=True)
    o_ref[...] = y * lax.rsqrt(ms + EPS) * gain_ref[...]


def _combine(dest_flat, wts, x1, norm_final, ys):
    dblk = lambda f: pl.BlockSpec((COMB_TM * TOP_K,), f, memory_space=pltpu.SMEM)
    return pl.pallas_call(
        _combine_body,
        grid=(N_COMB,),
        in_specs=[
            dblk(lambda i: (i,)),
            dblk(lambda i: (jnp.minimum(i + 1, N_COMB - 1),)),
            pl.BlockSpec((COMB_TM, LANES), lambda i: (i, 0)),
            pl.BlockSpec((COMB_TM, D_MODEL), lambda i: (i, 0)),
            pl.BlockSpec((1, D_MODEL), lambda i: (0, 0)),
            pl.BlockSpec(memory_space=pl.ANY),
        ],
        out_specs=pl.BlockSpec((COMB_TM, D_MODEL), lambda i: (i, 0)),
        out_shape=jax.ShapeDtypeStruct((TOKENS, D_MODEL), F32),
        scratch_shapes=[pltpu.VMEM((2, TOP_K, COMB_TM * TILE_ROWS, LANES), U32), pltpu.SemaphoreType.DMA((2,))],
        compiler_params=_cparams(("arbitrary",)),
        name="combine",
    )(dest_flat, dest_flat, wts, x1, norm_final, ys)


def _pad_lanes(v):
    v = v.reshape(1, -1).astype(F32)
    return jnp.pad(v, ((0, 0), (0, LANES - v.shape[1])))


def kernel(x, norm_mix, w_in, conv_w, gdn_a_log, gdn_dt_bias, gdn_norm, swa_sinks, w_out, norm_ffn,
           w_router_group, b_router_group, w_router_expert, b_router_expert, w_gate, w_up, w_down,
           norm_final):
    l = 0
    x2d = x.reshape(TOKENS, D_MODEL)
    w = w_in[l]
    o_z = 3 * D_MODEL
    o_a = o_z + D_MODEL
    o_sq = o_a + 2 * GDN_HEADS
    o_sk = o_sq + D_MODEL
    o_sv = o_sk + SWA_KV_HEADS * SWA_DH
    o_gg = o_sv + SWA_KV_HEADS * SWA_DH
    o_gs = o_gg + D_MODEL
    w_main = jnp.concatenate(
        [w[:, :o_a], w[:, o_sq:o_sk], w[:, o_gg:o_gs], w[:, o_gs:], w[:, o_sk:o_sv], w[:, o_sv:o_gg]],
        axis=1).astype(BF16)
    w_ab = jnp.pad(w[:, o_a:o_sq], ((0, 0), (0, LANES - 2 * GDN_HEADS))).astype(BF16)

    proj, ab = _inproj(x2d, norm_mix[l].reshape(1, D_MODEL), w_main, w_ab, conv_w[l])

    g1, gt = _gdn_prep(ab, _pad_lanes(gdn_a_log[l]), _pad_lanes(gdn_dt_bias[l]))
    gr = gt[:, :GDN_HEADS, :].reshape(BATCH, GDN_HEADS, N_GROUPS_SEQ, GDN_GROUP_ROWS)
    y_gdn = _gdn(proj, g1, gr, gdn_norm[l].reshape(1, GDN_DV))

    y_swa = _swa(proj, swa_sinks[l].astype(F32))

    w_r = jnp.concatenate([w_router_group[l], w_router_expert[l]], axis=1).astype(F32)
    w_r = jnp.pad(w_r, ((0, 0), (0, LANES - w_r.shape[1])))
    wr_hi = w_r.astype(BF16)
    wr_lo = (w_r - wr_hi.astype(F32)).astype(BF16)
    r_bias = _pad_lanes(jnp.concatenate([b_router_group[l], b_router_expert[l]]))
    x1, hp, logits = _outproj(x2d, y_gdn, y_swa, proj, w_out[l].astype(BF16),
                              norm_ffn[l].reshape(1, D_MODEL), jnp.concatenate([wr_hi, wr_lo], axis=1), r_bias)

    tri = (lax.broadcasted_iota(I32, (ROUTE_TM, ROUTE_TM), 1)
           < lax.broadcasted_iota(I32, (ROUTE_TM, ROUTE_TM), 0)).astype(BF16)
    idx, wts, cnt = _route(logits, tri)

    counts = cnt[0, :N_EXPERTS]
    padded = (counts + MOE_BLOCK - 1) // MOE_BLOCK * MOE_BLOCK
    pad_end = jnp.cumsum(padded)
    pad_start = pad_end - padded
    dest = _slots(idx, _pad_lanes(pad_start))
    dest_flat = dest[:, :TOP_K].reshape(N_ASSIGN)
    blk_pos = jnp.arange(N_BLOCKS, dtype=I32) * MOE_BLOCK
    blk_e = jnp.minimum(jnp.sum((pad_end[None, :] <= blk_pos[:, None]).astype(I32), axis=1), N_EXPERTS - 1)
    used = pad_end[-1:] // MOE_BLOCK
    meta = jnp.concatenate([pad_end, padded, used]).astype(I32)
    is_new = jnp.concatenate([jnp.ones((1,), I32), (blk_e[1:] != blk_e[:-1]).astype(I32)])
    ordinal = jnp.cumsum(is_new) - 1
    nxt_pos = jnp.sum((ordinal[None, :] <= ordinal[:, None]).astype(I32), axis=1)
    nxt = jnp.where(nxt_pos < N_BLOCKS, blk_e[jnp.minimum(nxt_pos, N_BLOCKS - 1)], -1)
    sched = jnp.concatenate([blk_e, ordinal & 1, nxt, used]).astype(I32)

    xs = _dispatch(meta, dest_flat, hp)
    ys = _experts(sched, xs, w_gate[l], w_up[l], w_down[l])
    out = _combine(dest_flat, wts, x1, norm_final.reshape(1, D_MODEL), ys)
    return out.reshape(BATCH, SEQ, D_MODEL)
```

```python
import math

import jax
import jax.numpy as jnp
from jax import lax
from jax.experimental import pallas as pl
from jax.experimental.pallas import tpu as pltpu

F32 = jnp.float32
BF16 = jnp.bfloat16
I32 = jnp.int32
U32 = jnp.uint32

D_MODEL = 2048
BATCH = 16
SEQ = 2048
TOKENS = BATCH * SEQ
EPS = 1e-6

GDN_HEADS = 16
GDN_DK = 128
GDN_DV = 128
GDN_CHUNK = 64
CONV_W = 4
GDN_GROUP = 4
GDN_GROUP_ROWS = GDN_GROUP * GDN_CHUNK
N_CHUNKS = SEQ // GDN_CHUNK
N_GROUPS_SEQ = SEQ // GDN_GROUP_ROWS

SWA_HEADS = 32
SWA_KV_HEADS = 4
SWA_DH = 64
SWA_GROUP = 8
SWA_BLOCK = 128
WINDOW = 128

N_GROUPS = 8
EXPERTS_PER_GROUP = 8
N_EXPERTS = 64
TOP_K = 2
D_EXPERT = 512
MOE_BLOCK = 256
N_ASSIGN = TOKENS * TOP_K
N_BLOCKS = N_ASSIGN // MOE_BLOCK + N_EXPERTS
N_SLOTS = N_BLOCKS * MOE_BLOCK

PROJ_DIM = 7 * D_MODEL + 2 * SWA_KV_HEADS * SWA_DH
LANES = 128

VMEM_LIMIT = 56 * 1024 * 1024


def _cparams(sem, vmem=VMEM_LIMIT):
    return pltpu.CompilerParams(dimension_semantics=sem, vmem_limit_bytes=vmem)


TILE_ROWS = (D_MODEL // 2) // LANES


def _store_token_tiles(ref, first_token, words, first_block=0):
    n = words.shape[0]
    for s in range(words.shape[1] // LANES):
        ref[pl.ds(first_token * TILE_ROWS + first_block + s, n, stride=TILE_ROWS), :] = (
            words[:, s * LANES:(s + 1) * LANES])


def _load_token_tiles(ref, n):
    return jnp.concatenate([ref[pl.ds(s, n, stride=TILE_ROWS), :] for s in range(TILE_ROWS)], axis=1)


def _sigmoid(x):
    return 0.5 * jnp.tanh(0.5 * x) + 0.5


def _silu(x):
    h = 0.5 * x
    return h * jnp.tanh(h) + h


INPROJ_TM = 2048
INPROJ_TN = 512
INPROJ_VMEM = 60 * 1024 * 1024


INPROJ_RC = 256
INPROJ_RING = 3
QKV_TILES = 3 * D_MODEL // INPROJ_TN
QK_TILES = 2 * D_MODEL // INPROJ_TN
Q_TILES = D_MODEL // INPROJ_TN
assert INPROJ_TM == SEQ


def _inproj_body(x_ref, g_ref, w_ref, wab_ref, cw_ref, o_ref, ab_ref, h_ref, cpad):
    j = pl.program_id(1)

    @pl.when(j == 0)
    def _():
        def chunk(i, c):
            r = pl.ds(pl.multiple_of(i * 128, 128), 128)
            x = x_ref[r, :]
            ms = jnp.mean(x * x, axis=-1, keepdims=True)
            h_ref[r, :] = (x * lax.rsqrt(ms + EPS) * g_ref[...]).astype(BF16)
            return c
        lax.fori_loop(0, INPROJ_TM // 128, chunk, 0)
        ab_ref[...] = jnp.dot(h_ref[...], wab_ref[...], preferred_element_type=F32)

    @pl.when(j >= QKV_TILES)
    def _():
        o_ref[...] = jnp.dot(h_ref[...], w_ref[...], preferred_element_type=F32).astype(BF16)

    @pl.when(j < QKV_TILES)
    def _():
        cw = cw_ref[...]
        is_qk = j < QK_TILES
        qscale = jnp.where(j < Q_TILES, GDN_DK ** -0.5, 1.0).astype(F32)
        n_chunks = INPROJ_TM // INPROJ_RC

        def matmul(c):
            rows = pl.ds(c * INPROJ_RC, INPROJ_RC)
            cpad[c % INPROJ_RING, pl.ds(8, INPROJ_RC), :] = jnp.dot(
                h_ref[rows, :], w_ref[...], preferred_element_type=F32)

        def epilogue(c):
            buf = cpad.at[c % INPROJ_RING]
            if c == 0:
                buf[pl.ds(0, 8), :] = jnp.zeros((8, INPROJ_TN), F32)
            else:
                buf[pl.ds(0, 8), :] = cpad[(c - 1) % INPROJ_RING, pl.ds(INPROJ_RC, 8), :]
            y = None
            for s in range(CONV_W):
                t = buf[pl.ds(8 - s, INPROJ_RC), :] * cw[CONV_W - 1 - s:CONV_W - s, :]
                y = t if y is None else y + t
            y = _silu(y)
            parts = []
            for g in range(INPROJ_TN // LANES):
                yg = y[:, g * LANES:(g + 1) * LANES]
                inv = lax.rsqrt(jnp.sum(yg * yg, -1, keepdims=True) + EPS) * qscale
                parts.append(yg * jnp.where(is_qk, inv, 1.0))
            o_ref[pl.ds(c * INPROJ_RC, INPROJ_RC), :] = jnp.concatenate(parts, axis=1).astype(BF16)

        matmul(0)
        for c in range(n_chunks):
            if c + 1 < n_chunks:
                matmul(c + 1)
            epilogue(c)


def _inproj(x2d, gain, w_main, w_ab, conv_w):
    grid = (TOKENS // INPROJ_TM, PROJ_DIM // INPROJ_TN)
    return pl.pallas_call(
        _inproj_body,
        grid=grid,
        in_specs=[
            pl.BlockSpec((INPROJ_TM, D_MODEL), lambda i, j: (i, 0)),
            pl.BlockSpec((1, D_MODEL), lambda i, j: (0, 0)),
            pl.BlockSpec((D_MODEL, INPROJ_TN), lambda i, j: (0, j)),
            pl.BlockSpec((D_MODEL, LANES), lambda i, j: (0, 0)),
            pl.BlockSpec((CONV_W, INPROJ_TN), lambda i, j: (0, jnp.minimum(j, QKV_TILES - 1))),
        ],
        out_specs=[
            pl.BlockSpec((INPROJ_TM, INPROJ_TN), lambda i, j: (i, j)),
            pl.BlockSpec((INPROJ_TM, LANES), lambda i, j: (i, 0)),
        ],
        out_shape=[
            jax.ShapeDtypeStruct((TOKENS, PROJ_DIM), BF16),
            jax.ShapeDtypeStruct((TOKENS, LANES), F32),
        ],
        scratch_shapes=[pltpu.VMEM((INPROJ_TM, D_MODEL), BF16),
                        pltpu.VMEM((INPROJ_RING, 8 + INPROJ_RC, INPROJ_TN), F32)],
        compiler_params=_cparams(("parallel", "arbitrary"), vmem=INPROJ_VMEM),
        name="inproj",
    )(x2d, gain, w_main, w_ab, conv_w)


def _gdn_prep_body(ab_ref, alog_ref, dtb_ref, g1_ref, gt_ref):
    ab = ab_ref[...]
    lane = lax.broadcasted_iota(I32, ab.shape, 1)
    row = lax.broadcasted_iota(I32, ab.shape, 0) % GDN_CHUNK
    xa = ab + dtb_ref[...]
    softplus = jnp.maximum(xa, 0.0) + jnp.log(1.0 + jnp.exp(-jnp.abs(xa)))
    g = jnp.where(lane < GDN_HEADS, -jnp.exp(alog_ref[...]) * softplus, 0.0)
    gam = g
    s = 1
    while s < GDN_CHUNK:
        gam = gam + jnp.where(row >= s, pltpu.roll(gam, s, 0), 0.0)
        s *= 2
    g1_ref[...] = jnp.where(lane < GDN_HEADS, gam, _sigmoid(ab))
    gt_ref[0] = gam.T


def _gdn_prep(ab, alog_pad, dtb_pad):
    return pl.pallas_call(
        _gdn_prep_body,
        grid=(BATCH,),
        in_specs=[
            pl.BlockSpec((SEQ, LANES), lambda b: (b, 0)),
            pl.BlockSpec((1, LANES), lambda b: (0, 0)),
            pl.BlockSpec((1, LANES), lambda b: (0, 0)),
        ],
        out_specs=[
            pl.BlockSpec((SEQ, LANES), lambda b: (b, 0)),
            pl.BlockSpec((1, LANES, SEQ), lambda b: (b, 0, 0)),
        ],
        out_shape=[
            jax.ShapeDtypeStruct((TOKENS, LANES), F32),
            jax.ShapeDtypeStruct((BATCH, LANES, SEQ), F32),
        ],
        compiler_params=_cparams(("parallel",)),
        name="gdn_prep",
    )(ab, alog_pad, dtb_pad)


GDN_HB = 4
GDN_PAIR = 2
N_PAIRS = N_GROUPS_SEQ // GDN_PAIR
PAIR_CHUNKS = GDN_PAIR * GDN_GROUP
PAIR_ROWS = GDN_PAIR * GDN_GROUP_ROWS
GDN_VMEM = 59 * 1024 * 1024


def _block_diag(x):
    t = jnp.concatenate([x] * GDN_GROUP, axis=0)
    rb = lax.broadcasted_iota(I32, t.shape, 0) // GDN_CHUNK
    cb = lax.broadcasted_iota(I32, t.shape, 1) // GDN_CHUNK
    return jnp.where(rb == cb, t, jnp.zeros_like(t))


def _sbs_product(lhs_list, y):
    bd = _block_diag(y.astype(BF16))
    xs = [x.astype(BF16) for x in lhs_list]
    lhs = xs[0] if len(xs) == 1 else jnp.concatenate(xs, axis=0)
    r = jnp.dot(lhs, bd, preferred_element_type=F32)
    c = GDN_CHUNK
    return [r[i * c:(i + 1) * c] for i in range(len(xs))]


def _gdn_body(q_ref, k_ref, v_ref, z_ref, g1_ref, gr_ref, gain_ref,
              o_ref,
              kdbf, rhsbf, qd, gamb, betab, o0s, qts, vs, wps):
    hg = pl.program_id(1)
    shape = (PAIR_ROWS, LANES)
    lane = lax.broadcasted_iota(I32, shape, 1)

    def phase0(pair):
        rows = pl.ds(pl.multiple_of(pair * PAIR_ROWS, PAIR_ROWS), PAIR_ROWS)
        g1 = g1_ref[rows, :]
        for hd in range(GDN_HB):
            hs = slice(hd * LANES, (hd + 1) * LANES)
            head = hg * GDN_HB + hd

            def col(off):
                c = jnp.sum(jnp.where(lane == off + head, g1, 0.0), axis=-1, keepdims=True)
                return jnp.broadcast_to(c, shape)

            gam = col(0)
            beta = col(GDN_HEADS)
            gamb[hd, rows, :] = gam
            betab[hd, rows, :] = beta.astype(BF16)
            gam3 = gam.reshape(PAIR_CHUNKS, GDN_CHUNK, LANES)
            glast = jnp.broadcast_to(gam3[:, GDN_CHUNK - 1:GDN_CHUNK, :], gam3.shape).reshape(shape)
            eg = jnp.exp(gam)
            kdf = jnp.exp(glast - gam)

            qd[hd, rows, :] = (q_ref[rows, hs].astype(F32) * eg).astype(BF16)
            k = k_ref[rows, hs].astype(F32)
            kdbf[hd, rows, :] = (k * kdf).astype(BF16)
            rhsbf[hd, rows, LANES:] = (k * (beta * eg)).astype(BF16)
            rhsbf[hd, rows, :LANES] = (v_ref[rows, hs].astype(F32) * beta).astype(BF16)

    gshape = (GDN_CHUNK, GDN_GROUP_ROWS)
    ii = lax.broadcasted_iota(I32, gshape, 0)
    jj = lax.broadcasted_iota(I32, gshape, 1) % GDN_CHUNK
    eye = jnp.where(ii == jj, 1.0, 0.0).astype(F32)
    pr = lax.broadcasted_iota(I32, (2 * GDN_CHUNK, LANES), 0) // GDN_CHUNK
    tl = lax.broadcasted_iota(I32, (LANES, GDN_GROUP_ROWS), 1) // GDN_CHUNK
    lane_c = lax.broadcasted_iota(I32, (GDN_CHUNK, LANES), 1)
    nt = (((1,), (1,)), ((), ()))

    def sbs_bcast(ref, hd, base):
        parts = [ref[hd, pl.ds(base + m * GDN_CHUNK, GDN_CHUNK), :] for m in range(GDN_GROUP)]
        a = jnp.where(lane_c < GDN_CHUNK, parts[0], parts[1])
        b = jnp.where(lane_c < GDN_CHUNK, parts[2], parts[3])
        return jnp.concatenate([a, b], axis=1)

    def phase1(hd, pair, gg):
        g = pair * GDN_PAIR + gg
        base = pl.multiple_of(g * GDN_GROUP_ROWS, GDN_GROUP_ROWS)
        slot = lax.rem(pair, 2)
        res = []
        for p in range(2):
            rows = pl.ds(base + p * 2 * GDN_CHUNK, 2 * GDN_CHUNK)
            kst = k_ref[rows, hd * LANES:(hd + 1) * LANES]
            qst = q_ref[rows, hd * LANES:(hd + 1) * LANES]
            lhs_k = jnp.concatenate([kst[:GDN_CHUNK], kst[GDN_CHUNK:]], axis=1)
            lhs_q = jnp.concatenate([qst[:GDN_CHUNK], qst[GDN_CHUNK:]], axis=1)
            lhs = jnp.concatenate([lhs_q, lhs_k], axis=0)
            zero = jnp.zeros_like(kst)
            bt = jnp.concatenate([jnp.where(pr == 0, kst, zero), jnp.where(pr == 1, kst, zero)], axis=1)
            res.append(lax.dot_general(lhs, bt, nt, preferred_element_type=F32))
        yield
        qk = jnp.concatenate([res[0][:GDN_CHUNK], res[1][:GDN_CHUNK]], axis=1)
        kk = jnp.concatenate([res[0][GDN_CHUNK:], res[1][GDN_CHUNK:]], axis=1)

        gc = sbs_bcast(gamb, hd, base)
        bc = sbs_bcast(betab, hd, base).astype(F32)
        grow = gr_ref[0, hd, pl.ds(g, 1), :]
        decay = jnp.exp(jnp.where(ii >= jj, gc - grow, -jnp.inf))
        a = jnp.where(ii > jj, kk * decay * bc, 0.0)
        qkd = qk * decay

        u = eye - a
        (x,) = _sbs_product([a], a)
        yield
        for lvl in range(1, 6):
            if lvl < 5:
                x2, ux = _sbs_product([x, u], x)
                u = u + ux
                x = x2
            else:
                (ux,) = _sbs_product([u], x)
                u = u + ux
            yield

        rows4 = pl.ds(base, GDN_GROUP_ROWS)
        so = jnp.dot(_block_diag(u.astype(BF16)), rhsbf[hd, rows4, :], preferred_element_type=F32)
        yield
        sol = so.astype(BF16)

        kdt = kdbf[hd, rows4, :].astype(F32).T.astype(BF16)
        zt = jnp.zeros_like(kdt)
        lhs2 = jnp.concatenate(
            [_block_diag(qkd.astype(BF16))] + [jnp.where(tl == m, kdt, zt) for m in range(GDN_GROUP)],
            axis=0)
        r = jnp.dot(lhs2, sol, preferred_element_type=F32)
        srows = pl.ds(pl.multiple_of(slot * PAIR_ROWS + gg * GDN_GROUP_ROWS, GDN_GROUP_ROWS), GDN_GROUP_ROWS)
        o0s[hd, srows, :] = r[:GDN_GROUP_ROWS, :LANES]
        qts[hd, srows, :] = qd[hd, rows4, :].astype(F32) - r[:GDN_GROUP_ROWS, LANES:]
        for m in range(GDN_GROUP):
            blk = r[GDN_GROUP_ROWS + m * LANES:GDN_GROUP_ROWS + (m + 1) * LANES]
            dst = pl.ds(pl.multiple_of((slot * PAIR_CHUNKS + gg * GDN_GROUP + m) * LANES, LANES), LANES)
            vs[hd, dst, :] = blk[:, :LANES]
            wps[hd, dst, :] = blk[:, LANES:]

    gain = gain_ref[...]

    def phase2(hd, pair, j, s):
        c = pair * PAIR_CHUNKS + j
        local = lax.rem(pair, 2) * PAIR_CHUNKS + j
        rows = pl.ds(pl.multiple_of(c * GDN_CHUNK, GDN_CHUNK), GDN_CHUNK)
        lrows = pl.ds(pl.multiple_of(local * GDN_CHUNK, GDN_CHUNK), GDN_CHUNK)
        srow = pl.ds(pl.multiple_of(local * LANES, LANES), LANES)
        lhs = jnp.concatenate([wps[hd, srow, :].astype(BF16), qts[hd, lrows, :].astype(BF16)], axis=0)
        r = jnp.dot(lhs, s.astype(BF16), preferred_element_type=F32)
        o = r[LANES:] + o0s[hd, lrows, :]
        dec = jnp.exp(gamb[hd, pl.ds(c * GDN_CHUNK + GDN_CHUNK - 1, 1), :])
        s_new = s * dec + vs[hd, srow, :] - r[:LANES]
        y = o * lax.rsqrt(jnp.mean(o * o, -1, keepdims=True) + EPS) * gain
        zz = z_ref[rows, hd * LANES:(hd + 1) * LANES].astype(F32)
        o_ref[rows, hd * LANES:(hd + 1) * LANES] = (y * _silu(zz)).astype(BF16)
        return s_new

    def phase2_chain(hd, pair, states):
        s = states[hd]
        for j in range(PAIR_CHUNKS):
            s = phase2(hd, pair, j, s)
            yield
        states[hd] = s

    def run_interleaved(chains):
        live = list(chains)
        while live:
            nxt = []
            for c in live:
                try:
                    next(c)
                    nxt.append(c)
                except StopIteration:
                    pass
            live = nxt

    def phase1_chains(pair):
        return [phase1(hd, pair, gg) for gg in range(GDN_PAIR) for hd in range(GDN_HB)]

    for pair in range(N_PAIRS):
        phase0(pair)
    run_interleaved(phase1_chains(0))

    def body(pair, states):
        states = list(states)
        run_interleaved([phase2_chain(hd, pair - 1, states) for hd in range(GDN_HB)] + phase1_chains(pair))
        return tuple(states)

    s0 = tuple(jnp.zeros((GDN_DK, GDN_DV), F32) for _ in range(GDN_HB))
    states = list(lax.fori_loop(1, N_PAIRS, body, s0))
    run_interleaved([phase2_chain(hd, N_PAIRS - 1, states) for hd in range(GDN_HB)])


def _gdn(proj, g1, gr, gdn_norm):
    w = GDN_HB * LANES
    per_row = D_MODEL // w
    hblk = lambda off: pl.BlockSpec((SEQ, w), lambda b, h, off=off: (b, off * per_row + h))
    big = lambda dt: pltpu.VMEM((GDN_HB, SEQ, LANES), dt)
    return pl.pallas_call(
        _gdn_body,
        grid=(BATCH, GDN_HEADS // GDN_HB),
        in_specs=[
            hblk(0), hblk(1), hblk(2), hblk(3),
            pl.BlockSpec((SEQ, LANES), lambda b, h: (b, 0)),
            pl.BlockSpec((1, GDN_HB, N_GROUPS_SEQ, GDN_GROUP_ROWS), lambda b, h: (b, h, 0, 0)),
            pl.BlockSpec((1, LANES), lambda b, h: (0, 0)),
        ],
        out_specs=pl.BlockSpec((SEQ, w), lambda b, h: (b, h)),
        out_shape=jax.ShapeDtypeStruct((TOKENS, D_MODEL), BF16),
        scratch_shapes=[
            big(BF16), pltpu.VMEM((GDN_HB, SEQ, 2 * LANES), BF16),
            big(BF16), big(F32), big(BF16),
            pltpu.VMEM((GDN_HB, 2 * PAIR_ROWS, LANES), F32),
            pltpu.VMEM((GDN_HB, 2 * PAIR_ROWS, LANES), F32),
            pltpu.VMEM((GDN_HB, 2 * PAIR_CHUNKS * LANES, LANES), F32),
            pltpu.VMEM((GDN_HB, 2 * PAIR_CHUNKS * LANES, LANES), F32),
        ],
        compiler_params=_cparams(("parallel", "parallel"), vmem=GDN_VMEM),
        name="gdn",
    )(proj, proj, proj, proj, g1, gr, gdn_norm)


SWA_PIPE = 4
assert WINDOW == SWA_BLOCK


def _swa_body(sink_ref, q_ref, kc_ref, kp_ref, vc_ref, vp_ref, o_ref):
    i = pl.program_id(1)
    kcat = jnp.concatenate([kp_ref[...], kc_ref[...]], axis=0)
    vcat = jnp.concatenate([vp_ref[...], vc_ref[...]], axis=0)
    qi = lax.broadcasted_iota(I32, (SWA_BLOCK, SWA_BLOCK), 0)
    ki = lax.broadcasted_iota(I32, (SWA_BLOCK, SWA_BLOCK), 1)
    take_cur = ki <= qi
    valid = take_cur | (i > 0)
    lane_kv = lax.broadcasted_iota(I32, (2 * SWA_BLOCK, LANES), 1)
    lane_q = lax.broadcasted_iota(I32, (SWA_BLOCK, LANES), 1)
    nt = (((1,), (1,)), ((), ()))
    scale = SWA_DH ** -0.5
    assert math.frexp(scale)[0] == 0.5

    def head_pair(h0, ks, vs):
        qs = q_ref[:, h0 * SWA_DH:(h0 + 2) * SWA_DH]
        scores = []
        for kk in ks:
            sc = lax.dot_general(qs, kk, nt, preferred_element_type=F32)
            s = jnp.where(take_cur, sc[:, SWA_BLOCK:], sc[:, :SWA_BLOCK])
            scores.append(jnp.where(valid, s, -jnp.inf))
        for _ in range(SWA_PIPE):
            yield
        acc = None
        inv = None
        for half in range(2):
            sink = sink_ref[h0 + half]
            s = scores[half]
            m = jnp.maximum(jnp.max(s, axis=-1, keepdims=True), sink)
            p = jnp.exp(s - m)
            den = jnp.sum(p, axis=-1, keepdims=True) + jnp.exp(sink - m)
            p2 = jnp.concatenate([jnp.where(take_cur, 0.0, p), jnp.where(take_cur, p, 0.0)], axis=1)
            pv = jnp.dot(p2.astype(BF16), vs[half], preferred_element_type=F32)
            acc = pv if acc is None else acc + pv
            r = 1.0 / den
            inv = r if inv is None else jnp.where(lane_q < SWA_DH, inv, r)
        yield
        o_ref[:, h0 * SWA_DH:(h0 + 2) * SWA_DH] = (acc * inv).astype(BF16)

    chains = []
    for slab in range(SWA_KV_HEADS // 2):
        k2 = kcat[:, slab * LANES:(slab + 1) * LANES].astype(F32) * scale
        v2 = vcat[:, slab * LANES:(slab + 1) * LANES].astype(F32)
        k2r = pltpu.roll(k2, SWA_DH, 1)
        v2r = pltpu.roll(v2, SWA_DH, 1)
        for sub in range(2):
            kvh = slab * 2 + sub
            if sub == 0:
                k_lo = jnp.where(lane_kv < SWA_DH, k2, 0.0)
                k_hi = jnp.where(lane_kv >= SWA_DH, k2r, 0.0)
                v_lo = jnp.where(lane_kv < SWA_DH, v2, 0.0)
                v_hi = jnp.where(lane_kv >= SWA_DH, v2r, 0.0)
            else:
                k_lo = jnp.where(lane_kv < SWA_DH, k2r, 0.0)
                k_hi = jnp.where(lane_kv >= SWA_DH, k2, 0.0)
                v_lo = jnp.where(lane_kv < SWA_DH, v2r, 0.0)
                v_hi = jnp.where(lane_kv >= SWA_DH, v2, 0.0)
            k_lo, k_hi, v_lo, v_hi = (t.astype(BF16) for t in (k_lo, k_hi, v_lo, v_hi))
            for gp in range(SWA_GROUP // 2):
                h0 = kvh * SWA_GROUP + 2 * gp
                chains.append(head_pair(h0, (k_lo, k_hi), (v_lo, v_hi)))

    live = []
    pending = list(chains)
    while pending or live:
        if pending:
            live.append(pending.pop(0))
        nxt = []
        for c in live:
            try:
                next(c)
                nxt.append(c)
            except StopIteration:
                pass
        live = nxt


def _swa(proj, sinks):
    nb = SEQ // SWA_BLOCK
    qcol = 4
    kcol = (7 * D_MODEL) // 256
    vcol = kcol + 1
    cur = lambda col: pl.BlockSpec((SWA_BLOCK, 256), lambda b, i, col=col: (b * nb + i, col))
    prev = lambda col: pl.BlockSpec(
        (SWA_BLOCK, 256), lambda b, i, col=col: (b * nb + jnp.maximum(i - 1, 0), col))
    return pl.pallas_call(
        _swa_body,
        grid=(BATCH, nb),
        in_specs=[
            pl.BlockSpec(memory_space=pltpu.SMEM),
            pl.BlockSpec((SWA_BLOCK, D_MODEL), lambda b, i: (b * nb + i, qcol)),
            cur(kcol), prev(kcol), cur(vcol), prev(vcol),
        ],
        out_specs=pl.BlockSpec((SWA_BLOCK, D_MODEL), lambda b, i: (b * nb + i, 0)),
        out_shape=jax.ShapeDtypeStruct((TOKENS, D_MODEL), BF16),
        compiler_params=_cparams(("parallel", "parallel")),
        name="swa",
    )(sinks, proj, proj, proj, proj, proj)


OUT_TM = 512
OUT_SUB = 256


def _outproj_body(x_ref, yg_ref, ys_ref, gg_ref, gs_ref, w_ref, nf_ref, wr_ref, rb_ref,
                  x1_ref, hp_ref, lg_ref):
    def sub_tile(t):
        rows = pl.ds(t * OUT_SUB, OUT_SUB)
        merged = (_sigmoid(gg_ref[rows, :].astype(F32)) * yg_ref[rows, :].astype(F32)
                  + _sigmoid(gs_ref[rows, :].astype(F32)) * ys_ref[rows, :].astype(F32))
        acc = jnp.dot(merged.astype(BF16), w_ref[...], preferred_element_type=F32)
        yield
        x1 = x_ref[rows, :] + acc
        x1_ref[rows, :] = x1
        ms = jnp.mean(x1 * x1, axis=-1, keepdims=True)
        h = x1 * lax.rsqrt(ms + EPS) * nf_ref[...]
        hh = h.astype(BF16)
        r = jnp.dot(hh, wr_ref[...], preferred_element_type=F32)
        lg_ref[rows, :] = r[:, :LANES] + r[:, LANES:] + rb_ref[...]
        half = D_MODEL // 2
        hf = hh.astype(F32)
        wa = pltpu.bitcast(hf[:, :half], U32)
        wb = pltpu.bitcast(hf[:, half:], U32)
        _store_token_tiles(hp_ref, t * OUT_SUB, (wa & jnp.uint32(0xFFFF0000)) | (wb >> 16))

    tiles = [sub_tile(t) for t in range(OUT_TM // OUT_SUB)]
    for _ in range(2):
        for g in tiles:
            next(g, None)


def _outproj(x2d, y_gdn, y_swa, proj, w_out, norm_ffn, wr2, r_bias):
    row = lambda w: pl.BlockSpec((OUT_TM, w), lambda i: (i, 0))
    const = lambda s: pl.BlockSpec(s, lambda i: (0, 0))
    return pl.pallas_call(
        _outproj_body,
        grid=(TOKENS // OUT_TM,),
        in_specs=[
            row(D_MODEL), row(D_MODEL), row(D_MODEL),
            pl.BlockSpec((OUT_TM, D_MODEL), lambda i: (i, 5)),
            pl.BlockSpec((OUT_TM, D_MODEL), lambda i: (i, 6)),
            pl.BlockSpec((D_MODEL, D_MODEL), lambda i: (0, 0), pipeline_mode=pl.Buffered(1)),
            const((1, D_MODEL)),
            const((D_MODEL, 2 * LANES)), const((1, LANES)),
        ],
        out_specs=[row(D_MODEL), pl.BlockSpec((OUT_TM * TILE_ROWS, LANES), lambda i: (i, 0)), row(LANES)],
        out_shape=[
            jax.ShapeDtypeStruct((TOKENS, D_MODEL), F32),
            jax.ShapeDtypeStruct((TOKENS * TILE_ROWS, LANES), U32),
            jax.ShapeDtypeStruct((TOKENS, LANES), F32),
        ],
        compiler_params=_cparams(("parallel",)),
        name="outproj",
    )(x2d, y_gdn, y_swa, proj, proj, w_out, norm_ffn, wr2, r_bias)


ROUTE_TM = 1024


def _route_body(lg_ref, tri_ref, idx_ref, wt_ref, cnt_ref, run_ref):
    @pl.when(pl.program_id(0) == 0)
    def _():
        run_ref[...] = jnp.zeros_like(run_ref)

    lg = lg_ref[...]
    lane = lax.broadcasted_iota(I32, lg.shape, 1)
    ninf = -jnp.inf
    big = jnp.int32(LANES)
    is_g = lane < N_GROUPS
    glog = jnp.where(is_g, lg, ninf)
    gmax = jnp.max(glog, axis=-1, keepdims=True)
    gden = jnp.sum(jnp.where(is_g, jnp.exp(lg - gmax), 0.0), axis=-1, keepdims=True)
    p_sel = 1.0 / gden
    grp = jnp.min(jnp.where(glog == gmax, lane, big), axis=-1, keepdims=True)
    emask = (lane >= N_GROUPS) & (lane < N_GROUPS + N_EXPERTS) & (((lane - N_GROUPS) >> 3) == grp)
    el = jnp.where(emask, lg, ninf)
    v1 = jnp.max(el, axis=-1, keepdims=True)
    i1 = jnp.min(jnp.where(el == v1, lane, big), axis=-1, keepdims=True)
    el2 = jnp.where(lane == i1, ninf, el)
    v2 = jnp.max(el2, axis=-1, keepdims=True)
    i2 = jnp.min(jnp.where(el2 == v2, lane, big), axis=-1, keepdims=True)
    e = jnp.exp(v2 - v1)
    w1 = p_sel / (1.0 + e)
    w2 = p_sel * e / (1.0 + e)
    e0 = i1 - N_GROUPS
    e1 = i2 - N_GROUPS

    oh0 = lane == e0
    oh1 = lane == e1
    onehot = jnp.where(oh0 | oh1, 1.0, 0.0)
    prefix = jnp.dot(tri_ref[...], onehot.astype(BF16), preferred_element_type=F32) + run_ref[0:1, :]
    r0 = jnp.sum(jnp.where(oh0, prefix, 0.0), axis=-1, keepdims=True).astype(I32)
    r1 = jnp.sum(jnp.where(oh1, prefix, 0.0), axis=-1, keepdims=True).astype(I32)
    run = run_ref[0:1, :] + jnp.sum(onehot, axis=0, keepdims=True)
    run_ref[...] = jnp.broadcast_to(run, run_ref.shape)
    cnt_ref[...] = jnp.broadcast_to(run, cnt_ref.shape).astype(I32)

    zi = jnp.zeros(lg.shape, I32)
    idx = jnp.where(lane == 0, e0, zi)
    idx = jnp.where(lane == 1, e1, idx)
    idx = jnp.where(lane == 2, r0, idx)
    idx = jnp.where(lane == 3, r1, idx)
    idx_ref[...] = idx
    wt_ref[...] = jnp.where(lane == 0, w1, jnp.where(lane == 1, w2, 0.0))


def _route(logits, tri):
    row = pl.BlockSpec((ROUTE_TM, LANES), lambda i: (i, 0))
    return pl.pallas_call(
        _route_body,
        grid=(TOKENS // ROUTE_TM,),
        in_specs=[row, pl.BlockSpec((ROUTE_TM, ROUTE_TM), lambda i: (0, 0))],
        out_specs=[row, row, pl.BlockSpec((8, LANES), lambda i: (0, 0))],
        out_shape=[
            jax.ShapeDtypeStruct((TOKENS, LANES), I32),
            jax.ShapeDtypeStruct((TOKENS, LANES), F32),
            jax.ShapeDtypeStruct((8, LANES), I32),
        ],
        scratch_shapes=[pltpu.VMEM((8, LANES), F32)],
        compiler_params=_cparams(("arbitrary",)),
        name="route",
    )(logits, tri)


DISP_TM = 512
N_DISP = TOKENS // DISP_TM
DISP_RING = 3
DISP_ROWS = DISP_TM * TILE_ROWS


def _dispatch_body(meta_ref, dest_ref, h_hbm, xs_ref, zbuf, hbuf, in_sem, out_sem, zsem):
    @pl.when(pl.program_id(0) == 0)
    def _():
        zbuf[...] = jnp.zeros_like(zbuf)

        def zero_block(row0):
            return pltpu.make_async_copy(
                zbuf, xs_ref.at[pl.ds(row0 * TILE_ROWS, MOE_BLOCK * TILE_ROWS), :], zsem)

        def per_expert(e, n):
            has = meta_ref[N_EXPERTS + e] > 0

            @pl.when(has)
            def _():
                zero_block(pl.multiple_of(meta_ref[e] - MOE_BLOCK, MOE_BLOCK)).start()

            return n + has.astype(I32)

        n_last = lax.fori_loop(0, N_EXPERTS, per_expert, jnp.int32(0))
        used = meta_ref[2 * N_EXPERTS]

        def tail(j, c):
            zero_block(pl.multiple_of(j * MOE_BLOCK, MOE_BLOCK)).start()
            return c

        lax.fori_loop(used, N_BLOCKS, tail, 0)

        def drain(i, c):
            zero_block(0).wait()
            return c

        lax.fori_loop(0, n_last + (N_BLOCKS - used), drain, 0)

    i = pl.program_id(0)
    b = lax.rem(i, DISP_RING)

    def load(step, slot):
        return pltpu.make_async_copy(
            h_hbm.at[pl.ds(pl.multiple_of(step * DISP_ROWS, DISP_ROWS), DISP_ROWS), :], hbuf.at[slot],
            in_sem.at[slot])

    def wait_scatters(slot):
        for k in range(TOP_K):
            pltpu.make_async_copy(hbuf.at[slot], xs_ref.at[pl.ds(0, DISP_ROWS), :], out_sem.at[slot]).wait()

    @pl.when(i == 0)
    def _():
        load(0, 0).start()

    load(i, b).wait()

    @pl.when(i + 1 < N_DISP)
    def _():
        load(i + 1, lax.rem(i + 1, DISP_RING)).start()

    def issue(r, c):
        for k in range(TOP_K):
            d = dest_ref[TOP_K * r + k]
            pltpu.make_async_copy(hbuf.at[b, pl.ds(pl.multiple_of(r * TILE_ROWS, TILE_ROWS), TILE_ROWS), :],
                                  xs_ref.at[pl.ds(pl.multiple_of(d * TILE_ROWS, TILE_ROWS), TILE_ROWS), :],
                                  out_sem.at[b]).start(priority=k)
        return c

    lax.fori_loop(0, DISP_TM, issue, 0, unroll=4)

    @pl.when(i >= 1)
    def _():
        wait_scatters(lax.rem(i + DISP_RING - 1, DISP_RING))

    @pl.when(i == N_DISP - 1)
    def _():
        wait_scatters(b)


def _dispatch(meta, dest_flat, hp):
    grid_spec = pltpu.PrefetchScalarGridSpec(
        num_scalar_prefetch=1,
        grid=(N_DISP,),
        in_specs=[
            pl.BlockSpec((DISP_TM * TOP_K,), lambda i, m: (i,), memory_space=pltpu.SMEM),
            pl.BlockSpec(memory_space=pl.ANY),
        ],
        out_specs=pl.BlockSpec(memory_space=pl.ANY),
        scratch_shapes=[
            pltpu.VMEM((MOE_BLOCK * TILE_ROWS, LANES), U32),
            pltpu.VMEM((DISP_RING, DISP_ROWS, LANES), U32),
            pltpu.SemaphoreType.DMA((DISP_RING,)),
            pltpu.SemaphoreType.DMA((DISP_RING,)),
            pltpu.SemaphoreType.DMA,
        ],
    )
    return pl.pallas_call(
        _dispatch_body,
        grid_spec=grid_spec,
        out_shape=jax.ShapeDtypeStruct((N_SLOTS * TILE_ROWS, LANES), U32),
        compiler_params=_cparams(("arbitrary",)),
        name="dispatch",
    )(meta, dest_flat, hp)


def _slots_body(idx_ref, start_ref, dest_ref):
    idx = idx_ref[...]
    lane = lax.broadcasted_iota(I32, idx.shape, 1)
    start = start_ref[...]
    out = jnp.zeros(idx.shape, I32)
    for k in range(TOP_K):
        e = idx[:, k:k + 1]
        base = jnp.sum(jnp.where(lane == e, start, 0.0), axis=-1, keepdims=True).astype(I32)
        out = jnp.where(lane == k, base + idx[:, TOP_K + k:TOP_K + k + 1], out)
    dest_ref[...] = out


SLOTS_TM = 2048


def _slots(idx, pad_start_row):
    row = pl.BlockSpec((SLOTS_TM, LANES), lambda i: (i, 0))
    return pl.pallas_call(
        _slots_body,
        grid=(TOKENS // SLOTS_TM,),
        in_specs=[row, pl.BlockSpec((1, LANES), lambda i: (0, 0))],
        out_specs=row,
        out_shape=jax.ShapeDtypeStruct((TOKENS, LANES), I32),
        compiler_params=_cparams(("parallel",)),
        name="slots",
    )(idx, pad_start_row)


EXPERT_OUT_COLS = 256


def _experts_body(sched_ref, x_ref, wg_hbm, wu_hbm, wd_hbm, y_ref,
                  wg_st, wu_st, wd_st, wg_bf, wu_bf, wd_bf, sem):
    j = pl.program_id(0)
    e = sched_ref[j]
    slot = sched_ref[N_BLOCKS + j]
    nxt = sched_ref[2 * N_BLOCKS + j]
    used = sched_ref[3 * N_BLOCKS]
    new_expert = (j == 0) | (e != sched_ref[jnp.maximum(j - 1, 0)])

    def weight_copies(expert, s):
        return (pltpu.make_async_copy(wg_hbm.at[expert], wg_st.at[s], sem.at[s, 0]),
                pltpu.make_async_copy(wu_hbm.at[expert], wu_st.at[s], sem.at[s, 1]),
                pltpu.make_async_copy(wd_hbm.at[expert], wd_st.at[s], sem.at[s, 2]))

    @pl.when(j == 0)
    def _():
        for c in weight_copies(e, slot):
            c.start(priority=1)

    @pl.when(new_expert)
    def _():
        for c in weight_copies(e, slot):
            c.wait()

        @pl.when(nxt >= 0)
        def _():
            for c in weight_copies(nxt, 1 - slot):
                c.start(priority=1)

        wg_bf[...] = wg_st[slot].astype(BF16)
        wu_bf[...] = wu_st[slot].astype(BF16)
        wd_bf[...] = wd_st[slot].astype(BF16)

    @pl.when(j < used)
    def _():
        w = _load_token_tiles(x_ref, MOE_BLOCK)
        xa = pltpu.bitcast(w & jnp.uint32(0xFFFF0000), F32).astype(BF16)
        xb = pltpu.bitcast(w << 16, F32).astype(BF16)
        half = D_MODEL // 2
        g = (jnp.dot(xa, wg_bf[:half, :], preferred_element_type=F32)
             + jnp.dot(xb, wg_bf[half:, :], preferred_element_type=F32))
        u = (jnp.dot(xa, wu_bf[:half, :], preferred_element_type=F32)
             + jnp.dot(xb, wu_bf[half:, :], preferred_element_type=F32))
        hb = (_silu(g) * u).astype(BF16)
        for c in range(half // EXPERT_OUT_COLS):
            lo = c * EXPERT_OUT_COLS
            ya = jnp.dot(hb, wd_bf[:, lo:lo + EXPERT_OUT_COLS], preferred_element_type=F32)
            yb = jnp.dot(hb, wd_bf[:, half + lo:half + lo + EXPERT_OUT_COLS], preferred_element_type=F32)
            wa = pltpu.bitcast(ya.astype(BF16).astype(F32), U32)
            wb = pltpu.bitcast(yb.astype(BF16).astype(F32), U32)
            _store_token_tiles(y_ref, 0, wa | (wb >> 16), first_block=lo // LANES)

    @pl.when(j >= used)
    def _():
        y_ref[...] = jnp.zeros_like(y_ref)


def _experts(sched, xs, w_gate, w_up, w_down):
    grid_spec = pltpu.PrefetchScalarGridSpec(
        num_scalar_prefetch=1,
        grid=(N_BLOCKS,),
        in_specs=[
            pl.BlockSpec((MOE_BLOCK * TILE_ROWS, LANES), lambda j, s: (j, 0)),
            pl.BlockSpec(memory_space=pl.ANY),
            pl.BlockSpec(memory_space=pl.ANY),
            pl.BlockSpec(memory_space=pl.ANY),
        ],
        out_specs=pl.BlockSpec((MOE_BLOCK * TILE_ROWS, LANES), lambda j, s: (j, 0)),
        scratch_shapes=[
            pltpu.VMEM((2, D_MODEL, D_EXPERT), F32),
            pltpu.VMEM((2, D_MODEL, D_EXPERT), F32),
            pltpu.VMEM((2, D_EXPERT, D_MODEL), F32),
            pltpu.VMEM((D_MODEL, D_EXPERT), BF16),
            pltpu.VMEM((D_MODEL, D_EXPERT), BF16),
            pltpu.VMEM((D_EXPERT, D_MODEL), BF16),
            pltpu.SemaphoreType.DMA((2, 3)),
        ],
    )
    return pl.pallas_call(
        _experts_body,
        grid_spec=grid_spec,
        out_shape=jax.ShapeDtypeStruct((N_SLOTS * TILE_ROWS, LANES), U32),
        compiler_params=_cparams(("arbitrary",)),
        name="experts",
    )(sched, xs, w_gate, w_up, w_down)


COMB_TM = 512
N_COMB = TOKENS // COMB_TM


def _unpack_pairs(w):
    hi = pltpu.bitcast(w & jnp.uint32(0xFFFF0000), F32)
    lo = pltpu.bitcast(w << 16, F32)
    return jnp.concatenate([hi, lo], axis=1)


def _combine_body(dest_ref, dest_next_ref, wt_ref, x1_ref, gain_ref, ys_ref, o_ref, buf, sem):
    i = pl.program_id(0)
    slot = lax.rem(i, 2)

    def start_gathers(dref, s):
        def issue(r, c):
            for k in range(TOP_K):
                d = dref[TOP_K * r + k]
                pltpu.make_async_copy(
                    ys_ref.at[pl.ds(pl.multiple_of(d * TILE_ROWS, TILE_ROWS), TILE_ROWS), :],
                    buf.at[s, k, pl.ds(pl.multiple_of(r * TILE_ROWS, TILE_ROWS), TILE_ROWS), :],
                    sem.at[s]).start(priority=k)
            return c
        lax.fori_loop(0, COMB_TM, issue, 0, unroll=4)

    @pl.when(i == 0)
    def _():
        start_gathers(dest_ref, slot)

    @pl.when(i + 1 < N_COMB)
    def _():
        start_gathers(dest_next_ref, 1 - slot)

    for k in range(TOP_K):
        pltpu.make_async_copy(ys_ref.at[pl.ds(0, COMB_TM * TILE_ROWS), :], buf.at[slot, k], sem.at[slot]).wait()

    wt = wt_ref[...]
    y = (x1_ref[...] + wt[:, 0:1] * _unpack_pairs(_load_token_tiles(buf.at[slot, 0], COMB_TM))
         + wt[:, 1:2] * _unpack_pairs(_load_token_tiles(buf.at[slot, 1], COMB_TM)))
    ms = jnp.mean(y * y, axis=-1, keepdims=True)
    o_ref[...] = y * lax.rsqrt(ms + EPS) * gain_ref[...]


def _combine(dest_flat, wts, x1, norm_final, ys):
    dblk = lambda f: pl.BlockSpec((COMB_TM * TOP_K,), f, memory_space=pltpu.SMEM)
    return pl.pallas_call(
        _combine_body,
        grid=(N_COMB,),
        in_specs=[
            dblk(lambda i: (i,)),
            dblk(lambda i: (jnp.minimum(i + 1, N_COMB - 1),)),
            pl.BlockSpec((COMB_TM, LANES), lambda i: (i, 0)),
            pl.BlockSpec((COMB_TM, D_MODEL), lambda i: (i, 0)),
            pl.BlockSpec((1, D_MODEL), lambda i: (0, 0)),
            pl.BlockSpec(memory_space=pl.ANY),
        ],
        out_specs=pl.BlockSpec((COMB_TM, D_MODEL), lambda i: (i, 0)),
        out_shape=jax.ShapeDtypeStruct((TOKENS, D_MODEL), F32),
        scratch_shapes=[pltpu.VMEM((2, TOP_K, COMB_TM * TILE_ROWS, LANES), U32), pltpu.SemaphoreType.DMA((2,))],
        compiler_params=_cparams(("arbitrary",)),
        name="combine",
    )(dest_flat, dest_flat, wts, x1, norm_final, ys)


def _pad_lanes(v):
    v = v.reshape(1, -1).astype(F32)
    return jnp.pad(v, ((0, 0), (0, LANES - v.shape[1])))


def kernel(x, norm_mix, w_in, conv_w, gdn_a_log, gdn_dt_bias, gdn_norm, swa_sinks, w_out, norm_ffn,
           w_router_group, b_router_group, w_router_expert, b_router_expert, w_gate, w_up, w_down,
           norm_final):
    l = 0
    x2d = x.reshape(TOKENS, D_MODEL)
    w = w_in[l]
    o_z = 3 * D_MODEL
    o_a = o_z + D_MODEL
    o_sq = o_a + 2 * GDN_HEADS
    o_sk = o_sq + D_MODEL
    o_sv = o_sk + SWA_KV_HEADS * SWA_DH
    o_gg = o_sv + SWA_KV_HEADS * SWA_DH
    o_gs = o_gg + D_MODEL
    w_main = jnp.concatenate(
        [w[:, :o_a], w[:, o_sq:o_sk], w[:, o_gg:o_gs], w[:, o_gs:], w[:, o_sk:o_sv], w[:, o_sv:o_gg]],
        axis=1).astype(BF16)
    w_ab = jnp.pad(w[:, o_a:o_sq], ((0, 0), (0, LANES - 2 * GDN_HEADS))).astype(BF16)

    proj, ab = _inproj(x2d, norm_mix[l].reshape(1, D_MODEL), w_main, w_ab, conv_w[l])

    g1, gt = _gdn_prep(ab, _pad_lanes(gdn_a_log[l]), _pad_lanes(gdn_dt_bias[l]))
    gr = gt[:, :GDN_HEADS, :].reshape(BATCH, GDN_HEADS, N_GROUPS_SEQ, GDN_GROUP_ROWS)
    y_gdn = _gdn(proj, g1, gr, gdn_norm[l].reshape(1, GDN_DV))

    y_swa = _swa(proj, swa_sinks[l].astype(F32))

    w_r = jnp.concatenate([w_router_group[l], w_router_expert[l]], axis=1).astype(F32)
    w_r = jnp.pad(w_r, ((0, 0), (0, LANES - w_r.shape[1])))
    wr_hi = w_r.astype(BF16)
    wr_lo = (w_r - wr_hi.astype(F32)).astype(BF16)
    r_bias = _pad_lanes(jnp.concatenate([b_router_group[l], b_router_expert[l]]))
    x1, hp, logits = _outproj(x2d, y_gdn, y_swa, proj, w_out[l].astype(BF16),
                              norm_ffn[l].reshape(1, D_MODEL), jnp.concatenate([wr_hi, wr_lo], axis=1), r_bias)

    tri = (lax.broadcasted_iota(I32, (ROUTE_TM, ROUTE_TM), 1)
           < lax.broadcasted_iota(I32, (ROUTE_TM, ROUTE_TM), 0)).astype(BF16)
    idx, wts, cnt = _route(logits, tri)

    counts = cnt[0, :N_EXPERTS]
    padded = (counts + MOE_BLOCK - 1) // MOE_BLOCK * MOE_BLOCK
    pad_end = jnp.cumsum(padded)
    pad_start = pad_end - padded
    dest = _slots(idx, _pad_lanes(pad_start))
    dest_flat = dest[:, :TOP_K].reshape(N_ASSIGN)
    blk_pos = jnp.arange(N_BLOCKS, dtype=I32) * MOE_BLOCK
    blk_e = jnp.minimum(jnp.sum((pad_end[None, :] <= blk_pos[:, None]).astype(I32), axis=1), N_EXPERTS - 1)
    used = pad_end[-1:] // MOE_BLOCK
    meta = jnp.concatenate([pad_end, padded, used]).astype(I32)
    is_new = jnp.concatenate([jnp.ones((1,), I32), (blk_e[1:] != blk_e[:-1]).astype(I32)])
    ordinal = jnp.cumsum(is_new) - 1
    nxt_pos = jnp.sum((ordinal[None, :] <= ordinal[:, None]).astype(I32), axis=1)
    nxt = jnp.where(nxt_pos < N_BLOCKS, blk_e[jnp.minimum(nxt_pos, N_BLOCKS - 1)], -1)
    sched = jnp.concatenate([blk_e, ordinal & 1, nxt, used]).astype(I32)

    xs = _dispatch(meta, dest_flat, hp)
    ys = _experts(sched, xs, w_gate[l], w_up[l], w_down[l])
    out = _combine(dest_flat, wts, x1, norm_final.reshape(1, D_MODEL), ys)
    return out.reshape(BATCH, SEQ, D_MODEL)
```

```python
import math

import jax
import jax.numpy as jnp
from jax import lax
from jax.experimental import pallas as pl
from jax.experimental.pallas import tpu as pltpu

F32 = jnp.float32
BF16 = jnp.bfloat16
I32 = jnp.int32
U32 = jnp.uint32

D_MODEL = 2048
BATCH = 16
SEQ = 2048
TOKENS = BATCH * SEQ
EPS = 1e-6

GDN_HEADS = 16
GDN_DK = 128
GDN_DV = 128
GDN_CHUNK = 64
CONV_W = 4
GDN_GROUP = 4
GDN_GROUP_ROWS = GDN_GROUP * GDN_CHUNK
N_CHUNKS = SEQ // GDN_CHUNK
N_GROUPS_SEQ = SEQ // GDN_GROUP_ROWS

SWA_HEADS = 32
SWA_KV_HEADS = 4
SWA_DH = 64
SWA_GROUP = 8
SWA_BLOCK = 128
WINDOW = 128

N_GROUPS = 8
EXPERTS_PER_GROUP = 8
N_EXPERTS = 64
TOP_K = 2
D_EXPERT = 512
MOE_BLOCK = 256
N_ASSIGN = TOKENS * TOP_K
N_BLOCKS = N_ASSIGN // MOE_BLOCK + N_EXPERTS
N_SLOTS = N_BLOCKS * MOE_BLOCK

PROJ_DIM = 7 * D_MODEL + 2 * SWA_KV_HEADS * SWA_DH
LANES = 128

VMEM_LIMIT = 56 * 1024 * 1024


def _cparams(sem, vmem=VMEM_LIMIT):
    return pltpu.CompilerParams(dimension_semantics=sem, vmem_limit_bytes=vmem)


TILE_ROWS = (D_MODEL // 2) // LANES


def _store_token_tiles(ref, first_token, words, first_block=0):
    n = words.shape[0]
    for s in range(words.shape[1] // LANES):
        ref[pl.ds(first_token * TILE_ROWS + first_block + s, n, stride=TILE_ROWS), :] = (
            words[:, s * LANES:(s + 1) * LANES])


def _load_token_tiles(ref, n):
    return jnp.concatenate([ref[pl.ds(s, n, stride=TILE_ROWS), :] for s in range(TILE_ROWS)], axis=1)


def _sigmoid(x):
    return 0.5 * jnp.tanh(0.5 * x) + 0.5


def _silu(x):
    h = 0.5 * x
    return h * jnp.tanh(h) + h


INPROJ_TM = 2048
INPROJ_TN = 512
INPROJ_VMEM = 60 * 1024 * 1024


INPROJ_RC = 256
INPROJ_RING = 3
QKV_TILES = 3 * D_MODEL // INPROJ_TN
QK_TILES = 2 * D_MODEL // INPROJ_TN
Q_TILES = D_MODEL // INPROJ_TN
assert INPROJ_TM == SEQ


def _inproj_body(x_ref, g_ref, w_ref, wab_ref, cw_ref, o_ref, ab_ref, h_ref, cpad):
    j = pl.program_id(1)

    @pl.when(j == 0)
    def _():
        def chunk(i, c):
            r = pl.ds(pl.multiple_of(i * 128, 128), 128)
            x = x_ref[r, :]
            ms = jnp.mean(x * x, axis=-1, keepdims=True)
            h_ref[r, :] = (x * lax.rsqrt(ms + EPS) * g_ref[...]).astype(BF16)
            return c
        lax.fori_loop(0, INPROJ_TM // 128, chunk, 0)
        ab_ref[...] = jnp.dot(h_ref[...], wab_ref[...], preferred_element_type=F32)

    @pl.when(j >= QKV_TILES)
    def _():
        o_ref[...] = jnp.dot(h_ref[...], w_ref[...], preferred_element_type=F32).astype(BF16)

    @pl.when(j < QKV_TILES)
    def _():
        cw = cw_ref[...]
        is_qk = j < QK_TILES
        qscale = jnp.where(j < Q_TILES, GDN_DK ** -0.5, 1.0).astype(F32)
        n_chunks = INPROJ_TM // INPROJ_RC

        def matmul(c):
            rows = pl.ds(c * INPROJ_RC, INPROJ_RC)
            cpad[c % INPROJ_RING, pl.ds(8, INPROJ_RC), :] = jnp.dot(
                h_ref[rows, :], w_ref[...], preferred_element_type=F32)

        def epilogue(c):
            buf = cpad.at[c % INPROJ_RING]
            if c == 0:
                buf[pl.ds(0, 8), :] = jnp.zeros((8, INPROJ_TN), F32)
            else:
                buf[pl.ds(0, 8), :] = cpad[(c - 1) % INPROJ_RING, pl.ds(INPROJ_RC, 8), :]
            y = None
            for s in range(CONV_W):
                t = buf[pl.ds(8 - s, INPROJ_RC), :] * cw[CONV_W - 1 - s:CONV_W - s, :]
                y = t if y is None else y + t
            y = _silu(y)
            parts = []
            for g in range(INPROJ_TN // LANES):
                yg = y[:, g * LANES:(g + 1) * LANES]
                inv = lax.rsqrt(jnp.sum(yg * yg, -1, keepdims=True) + EPS) * qscale
                parts.append(yg * jnp.where(is_qk, inv, 1.0))
            o_ref[pl.ds(c * INPROJ_RC, INPROJ_RC), :] = jnp.concatenate(parts, axis=1).astype(BF16)

        matmul(0)
        for c in range(n_chunks):
            if c + 1 < n_chunks:
                matmul(c + 1)
            epilogue(c)


def _inproj(x2d, gain, w_main, w_ab, conv_w):
    grid = (TOKENS // INPROJ_TM, PROJ_DIM // INPROJ_TN)
    return pl.pallas_call(
        _inproj_body,
        grid=grid,
        in_specs=[
            pl.BlockSpec((INPROJ_TM, D_MODEL), lambda i, j: (i, 0)),
            pl.BlockSpec((1, D_MODEL), lambda i, j: (0, 0)),
            pl.BlockSpec((D_MODEL, INPROJ_TN), lambda i, j: (0, j)),
            pl.BlockSpec((D_MODEL, LANES), lambda i, j: (0, 0)),
            pl.BlockSpec((CONV_W, INPROJ_TN), lambda i, j: (0, jnp.minimum(j, QKV_TILES - 1))),
        ],
        out_specs=[
            pl.BlockSpec((INPROJ_TM, INPROJ_TN), lambda i, j: (i, j)),
            pl.BlockSpec((INPROJ_TM, LANES), lambda i, j: (i, 0)),
        ],
        out_shape=[
            jax.ShapeDtypeStruct((TOKENS, PROJ_DIM), BF16),
            jax.ShapeDtypeStruct((TOKENS, LANES), F32),
        ],
        scratch_shapes=[pltpu.VMEM((INPROJ_TM, D_MODEL), BF16),
                        pltpu.VMEM((INPROJ_RING, 8 + INPROJ_RC, INPROJ_TN), F32)],
        compiler_params=_cparams(("parallel", "arbitrary"), vmem=INPROJ_VMEM),
        name="inproj",
    )(x2d, gain, w_main, w_ab, conv_w)


def _gdn_prep_body(ab_ref, alog_ref, dtb_ref, g1_ref, gt_ref):
    ab = ab_ref[...]
    lane = lax.broadcasted_iota(I32, ab.shape, 1)
    row = lax.broadcasted_iota(I32, ab.shape, 0) % GDN_CHUNK
    xa = ab + dtb_ref[...]
    softplus = jnp.maximum(xa, 0.0) + jnp.log(1.0 + jnp.exp(-jnp.abs(xa)))
    g = jnp.where(lane < GDN_HEADS, -jnp.exp(alog_ref[...]) * softplus, 0.0)
    gam = g
    s = 1
    while s < GDN_CHUNK:
        gam = gam + jnp.where(row >= s, pltpu.roll(gam, s, 0), 0.0)
        s *= 2
    g1_ref[...] = jnp.where(lane < GDN_HEADS, gam, _sigmoid(ab))
    gt_ref[0] = gam.T


def _gdn_prep(ab, alog_pad, dtb_pad):
    return pl.pallas_call(
        _gdn_prep_body,
        grid=(BATCH,),
        in_specs=[
            pl.BlockSpec((SEQ, LANES), lambda b: (b, 0)),
            pl.BlockSpec((1, LANES), lambda b: (0, 0)),
            pl.BlockSpec((1, LANES), lambda b: (0, 0)),
        ],
        out_specs=[
            pl.BlockSpec((SEQ, LANES), lambda b: (b, 0)),
            pl.BlockSpec((1, LANES, SEQ), lambda b: (b, 0, 0)),
        ],
        out_shape=[
            jax.ShapeDtypeStruct((TOKENS, LANES), F32),
            jax.ShapeDtypeStruct((BATCH, LANES, SEQ), F32),
        ],
        compiler_params=_cparams(("parallel",)),
        name="gdn_prep",
    )(ab, alog_pad, dtb_pad)


GDN_HB = 4
GDN_PAIR = 2
N_PAIRS = N_GROUPS_SEQ // GDN_PAIR
PAIR_CHUNKS = GDN_PAIR * GDN_GROUP
PAIR_ROWS = GDN_PAIR * GDN_GROUP_ROWS
GDN_VMEM = 59 * 1024 * 1024


def _block_diag(x):
    t = jnp.concatenate([x] * GDN_GROUP, axis=0)
    rb = lax.broadcasted_iota(I32, t.shape, 0) // GDN_CHUNK
    cb = lax.broadcasted_iota(I32, t.shape, 1) // GDN_CHUNK
    return jnp.where(rb == cb, t, jnp.zeros_like(t))


def _sbs_product(lhs_list, y):
    bd = _block_diag(y.astype(BF16))
    xs = [x.astype(BF16) for x in lhs_list]
    lhs = xs[0] if len(xs) == 1 else jnp.concatenate(xs, axis=0)
    r = jnp.dot(lhs, bd, preferred_element_type=F32)
    c = GDN_CHUNK
    return [r[i * c:(i + 1) * c] for i in range(len(xs))]


def _gdn_body(q_ref, k_ref, v_ref, z_ref, g1_ref, gr_ref, gain_ref,
              o_ref,
              kdbf, rhsbf, qd, gamb, betab, o0s, qts, vs, wps):
    hg = pl.program_id(1)
    shape = (PAIR_ROWS, LANES)
    lane = lax.broadcasted_iota(I32, shape, 1)

    def phase0(pair):
        rows = pl.ds(pl.multiple_of(pair * PAIR_ROWS, PAIR_ROWS), PAIR_ROWS)
        g1 = g1_ref[rows, :]
        for hd in range(GDN_HB):
            hs = slice(hd * LANES, (hd + 1) * LANES)
            head = hg * GDN_HB + hd

            def col(off):
                c = jnp.sum(jnp.where(lane == off + head, g1, 0.0), axis=-1, keepdims=True)
                return jnp.broadcast_to(c, shape)

            gam = col(0)
            beta = col(GDN_HEADS)
            gamb[hd, rows, :] = gam
            betab[hd, rows, :] = beta.astype(BF16)
            gam3 = gam.reshape(PAIR_CHUNKS, GDN_CHUNK, LANES)
            glast = jnp.broadcast_to(gam3[:, GDN_CHUNK - 1:GDN_CHUNK, :], gam3.shape).reshape(shape)
            eg = jnp.exp(gam)
            kdf = jnp.exp(glast - gam)

            qd[hd, rows, :] = (q_ref[rows, hs].astype(F32) * eg).astype(BF16)
            k = k_ref[rows, hs].astype(F32)
            kdbf[hd, rows, :] = (k * kdf).astype(BF16)
            rhsbf[hd, rows, LANES:] = (k * (beta * eg)).astype(BF16)
            rhsbf[hd, rows, :LANES] = (v_ref[rows, hs].astype(F32) * beta).astype(BF16)

    gshape = (GDN_CHUNK, GDN_GROUP_ROWS)
    ii = lax.broadcasted_iota(I32, gshape, 0)
    jj = lax.broadcasted_iota(I32, gshape, 1) % GDN_CHUNK
    eye = jnp.where(ii == jj, 1.0, 0.0).astype(F32)
    pr = lax.broadcasted_iota(I32, (2 * GDN_CHUNK, LANES), 0) // GDN_CHUNK
    tl = lax.broadcasted_iota(I32, (LANES, GDN_GROUP_ROWS), 1) // GDN_CHUNK
    lane_c = lax.broadcasted_iota(I32, (GDN_CHUNK, LANES), 1)
    nt = (((1,), (1,)), ((), ()))

    def sbs_bcast(ref, hd, base):
        parts = [ref[hd, pl.ds(base + m * GDN_CHUNK, GDN_CHUNK), :] for m in range(GDN_GROUP)]
        a = jnp.where(lane_c < GDN_CHUNK, parts[0], parts[1])
        b = jnp.where(lane_c < GDN_CHUNK, parts[2], parts[3])
        return jnp.concatenate([a, b], axis=1)

    def phase1(hd, pair, gg):
        g = pair * GDN_PAIR + gg
        base = pl.multiple_of(g * GDN_GROUP_ROWS, GDN_GROUP_ROWS)
        slot = lax.rem(pair, 2)
        res = []
        for p in range(2):
            rows = pl.ds(base + p * 2 * GDN_CHUNK, 2 * GDN_CHUNK)
            kst = k_ref[rows, hd * LANES:(hd + 1) * LANES]
            qst = q_ref[rows, hd * LANES:(hd + 1) * LANES]
            lhs_k = jnp.concatenate([kst[:GDN_CHUNK], kst[GDN_CHUNK:]], axis=1)
            lhs_q = jnp.concatenate([qst[:GDN_CHUNK], qst[GDN_CHUNK:]], axis=1)
            lhs = jnp.concatenate([lhs_q, lhs_k], axis=0)
            zero = jnp.zeros_like(kst)
            bt = jnp.concatenate([jnp.where(pr == 0, kst, zero), jnp.where(pr == 1, kst, zero)], axis=1)
            res.append(lax.dot_general(lhs, bt, nt, preferred_element_type=F32))
        yield
        qk = jnp.concatenate([res[0][:GDN_CHUNK], res[1][:GDN_CHUNK]], axis=1)
        kk = jnp.concatenate([res[0][GDN_CHUNK:], res[1][GDN_CHUNK:]], axis=1)

        gc = sbs_bcast(gamb, hd, base)
        bc = sbs_bcast(betab, hd, base).astype(F32)
        grow = gr_ref[0, hd, pl.ds(g, 1), :]
        decay = jnp.exp(jnp.where(ii >= jj, gc - grow, -jnp.inf))
        a = jnp.where(ii > jj, kk * decay * bc, 0.0)
        qkd = qk * decay

        u = eye - a
        (x,) = _sbs_product([a], a)
        yield
        for lvl in range(1, 6):
            if lvl < 5:
                x2, ux = _sbs_product([x, u], x)
                u = u + ux
                x = x2
            else:
                (ux,) = _sbs_product([u], x)
                u = u + ux
            yield

        rows4 = pl.ds(base, GDN_GROUP_ROWS)
        so = jnp.dot(_block_diag(u.astype(BF16)), rhsbf[hd, rows4, :], preferred_element_type=F32)
        yield
        sol = so.astype(BF16)

        kdt = kdbf[hd, rows4, :].astype(F32).T.astype(BF16)
        zt = jnp.zeros_like(kdt)
        lhs2 = jnp.concatenate(
            [_block_diag(qkd.astype(BF16))] + [jnp.where(tl == m, kdt, zt) for m in range(GDN_GROUP)],
            axis=0)
        r = jnp.dot(lhs2, sol, preferred_element_type=F32)
        srows = pl.ds(pl.multiple_of(slot * PAIR_ROWS + gg * GDN_GROUP_ROWS, GDN_GROUP_ROWS), GDN_GROUP_ROWS)
        o0s[hd, srows, :] = r[:GDN_GROUP_ROWS, :LANES]
        qts[hd, srows, :] = qd[hd, rows4, :].astype(F32) - r[:GDN_GROUP_ROWS, LANES:]
        for m in range(GDN_GROUP):
            blk = r[GDN_GROUP_ROWS + m * LANES:GDN_GROUP_ROWS + (m + 1) * LANES]
            dst = pl.ds(pl.multiple_of((slot * PAIR_CHUNKS + gg * GDN_GROUP + m) * LANES, LANES), LANES)
            vs[hd, dst, :] = blk[:, :LANES]
            wps[hd, dst, :] = blk[:, LANES:]

    gain = gain_ref[...]

    def phase2(hd, pair, j, s):
        c = pair * PAIR_CHUNKS + j
        local = lax.rem(pair, 2) * PAIR_CHUNKS + j
        rows = pl.ds(pl.multiple_of(c * GDN_CHUNK, GDN_CHUNK), GDN_CHUNK)
        lrows = pl.ds(pl.multiple_of(local * GDN_CHUNK, GDN_CHUNK), GDN_CHUNK)
        srow = pl.ds(pl.multiple_of(local * LANES, LANES), LANES)
        lhs = jnp.concatenate([wps[hd, srow, :].astype(BF16), qts[hd, lrows, :].astype(BF16)], axis=0)
        r = jnp.dot(lhs, s.astype(BF16), preferred_element_type=F32)
        o = r[LANES:] + o0s[hd, lrows, :]
        dec = jnp.exp(gamb[hd, pl.ds(c * GDN_CHUNK + GDN_CHUNK - 1, 1), :])
        s_new = s * dec + vs[hd, srow, :] - r[:LANES]
        y = o * lax.rsqrt(jnp.mean(o * o, -1, keepdims=True) + EPS) * gain
        zz = z_ref[rows, hd * LANES:(hd + 1) * LANES].astype(F32)
        o_ref[rows, hd * LANES:(hd + 1) * LANES] = (y * _silu(zz)).astype(BF16)
        return s_new

    def phase2_chain(hd, pair, states):
        s = states[hd]
        for j in range(PAIR_CHUNKS):
            s = phase2(hd, pair, j, s)
            yield
        states[hd] = s

    def run_interleaved(chains):
        live = list(chains)
        while live:
            nxt = []
            for c in live:
                try:
                    next(c)
                    nxt.append(c)
                except StopIteration:
                    pass
            live = nxt

    def phase1_chains(pair):
        return [phase1(hd, pair, gg) for gg in range(GDN_PAIR) for hd in range(GDN_HB)]

    for pair in range(N_PAIRS):
        phase0(pair)
    run_interleaved(phase1_chains(0))

    def body(pair, states):
        states = list(states)
        run_interleaved([phase2_chain(hd, pair - 1, states) for hd in range(GDN_HB)] + phase1_chains(pair))
        return tuple(states)

    s0 = tuple(jnp.zeros((GDN_DK, GDN_DV), F32) for _ in range(GDN_HB))
    states = list(lax.fori_loop(1, N_PAIRS, body, s0))
    run_interleaved([phase2_chain(hd, N_PAIRS - 1, states) for hd in range(GDN_HB)])


def _gdn(proj, g1, gr, gdn_norm):
    w = GDN_HB * LANES
    per_row = D_MODEL // w
    hblk = lambda off: pl.BlockSpec((SEQ, w), lambda b, h, off=off: (b, off * per_row + h))
    big = lambda dt: pltpu.VMEM((GDN_HB, SEQ, LANES), dt)
    return pl.pallas_call(
        _gdn_body,
        grid=(BATCH, GDN_HEADS // GDN_HB),
        in_specs=[
            hblk(0), hblk(1), hblk(2), hblk(3),
            pl.BlockSpec((SEQ, LANES), lambda b, h: (b, 0)),
            pl.BlockSpec((1, GDN_HB, N_GROUPS_SEQ, GDN_GROUP_ROWS), lambda b, h: (b, h, 0, 0)),
            pl.BlockSpec((1, LANES), lambda b, h: (0, 0)),
        ],
        out_specs=pl.BlockSpec((SEQ, w), lambda b, h: (b, h)),
        out_shape=jax.ShapeDtypeStruct((TOKENS, D_MODEL), BF16),
        scratch_shapes=[
            big(BF16), pltpu.VMEM((GDN_HB, SEQ, 2 * LANES), BF16),
            big(BF16), big(F32), big(BF16),
            pltpu.VMEM((GDN_HB, 2 * PAIR_ROWS, LANES), F32),
            pltpu.VMEM((GDN_HB, 2 * PAIR_ROWS, LANES), F32),
            pltpu.VMEM((GDN_HB, 2 * PAIR_CHUNKS * LANES, LANES), F32),
            pltpu.VMEM((GDN_HB, 2 * PAIR_CHUNKS * LANES, LANES), F32),
        ],
        compiler_params=_cparams(("parallel", "parallel"), vmem=GDN_VMEM),
        name="gdn",
    )(proj, proj, proj, proj, g1, gr, gdn_norm)


SWA_PIPE = 4
assert WINDOW == SWA_BLOCK


def _swa_body(sink_ref, q_ref, kc_ref, kp_ref, vc_ref, vp_ref, o_ref):
    i = pl.program_id(1)
    kcat = jnp.concatenate([kp_ref[...], kc_ref[...]], axis=0)
    vcat = jnp.concatenate([vp_ref[...], vc_ref[...]], axis=0)
    qi = lax.broadcasted_iota(I32, (SWA_BLOCK, SWA_BLOCK), 0)
    ki = lax.broadcasted_iota(I32, (SWA_BLOCK, SWA_BLOCK), 1)
    take_cur = ki <= qi
    valid = take_cur | (i > 0)
    lane_kv = lax.broadcasted_iota(I32, (2 * SWA_BLOCK, LANES), 1)
    lane_q = lax.broadcasted_iota(I32, (SWA_BLOCK, LANES), 1)
    nt = (((1,), (1,)), ((), ()))
    scale = SWA_DH ** -0.5
    assert math.frexp(scale)[0] == 0.5

    def head_pair(h0, ks, vs):
        qs = q_ref[:, h0 * SWA_DH:(h0 + 2) * SWA_DH]
        scores = []
        for kk in ks:
            sc = lax.dot_general(qs, kk, nt, preferred_element_type=F32)
            s = jnp.where(take_cur, sc[:, SWA_BLOCK:], sc[:, :SWA_BLOCK])
            scores.append(jnp.where(valid, s, -jnp.inf))
        for _ in range(SWA_PIPE):
            yield
        acc = None
        inv = None
        for half in range(2):
            sink = sink_ref[h0 + half]
            s = scores[half]
            m = jnp.maximum(jnp.max(s, axis=-1, keepdims=True), sink)
            p = jnp.exp(s - m)
            den = jnp.sum(p, axis=-1, keepdims=True) + jnp.exp(sink - m)
            p2 = jnp.concatenate([jnp.where(take_cur, 0.0, p), jnp.where(take_cur, p, 0.0)], axis=1)
            pv = jnp.dot(p2.astype(BF16), vs[half], preferred_element_type=F32)
            acc = pv if acc is None else acc + pv
            r = 1.0 / den
            inv = r if inv is None else jnp.where(lane_q < SWA_DH, inv, r)
        yield
        o_ref[:, h0 * SWA_DH:(h0 + 2) * SWA_DH] = (acc * inv).astype(BF16)

    chains = []
    for slab in range(SWA_KV_HEADS // 2):
        k2 = kcat[:, slab * LANES:(slab + 1) * LANES].astype(F32) * scale
        v2 = vcat[:, slab * LANES:(slab + 1) * LANES].astype(F32)
        k2r = pltpu.roll(k2, SWA_DH, 1)
        v2r = pltpu.roll(v2, SWA_DH, 1)
        for sub in range(2):
            kvh = slab * 2 + sub
            if sub == 0:
                k_lo = jnp.where(lane_kv < SWA_DH, k2, 0.0)
                k_hi = jnp.where(lane_kv >= SWA_DH, k2r, 0.0)
                v_lo = jnp.where(lane_kv < SWA_DH, v2, 0.0)
                v_hi = jnp.where(lane_kv >= SWA_DH, v2r, 0.0)
            else:
                k_lo = jnp.where(lane_kv < SWA_DH, k2r, 0.0)
                k_hi = jnp.where(lane_kv >= SWA_DH, k2, 0.0)
                v_lo = jnp.where(lane_kv < SWA_DH, v2r, 0.0)
                v_hi = jnp.where(lane_kv >= SWA_DH, v2, 0.0)
            k_lo, k_hi, v_lo, v_hi = (t.astype(BF16) for t in (k_lo, k_hi, v_lo, v_hi))
            for gp in range(SWA_GROUP // 2):
                h0 = kvh * SWA_GROUP + 2 * gp
                chains.append(head_pair(h0, (k_lo, k_hi), (v_lo, v_hi)))

    live = []
    pending = list(chains)
    while pending or live:
        if pending:
            live.append(pending.pop(0))
        nxt = []
        for c in live:
            try:
                next(c)
                nxt.append(c)
            except StopIteration:
                pass
        live = nxt


def _swa(proj, sinks):
    nb = SEQ // SWA_BLOCK
    qcol = 4
    kcol = (7 * D_MODEL) // 256
    vcol = kcol + 1
    cur = lambda col: pl.BlockSpec((SWA_BLOCK, 256), lambda b, i, col=col: (b * nb + i, col))
    prev = lambda col: pl.BlockSpec(
        (SWA_BLOCK, 256), lambda b, i, col=col: (b * nb + jnp.maximum(i - 1, 0), col))
    return pl.pallas_call(
        _swa_body,
        grid=(BATCH, nb),
        in_specs=[
            pl.BlockSpec(memory_space=pltpu.SMEM),
            pl.BlockSpec((SWA_BLOCK, D_MODEL), lambda b, i: (b * nb + i, qcol)),
            cur(kcol), prev(kcol), cur(vcol), prev(vcol),
        ],
        out_specs=pl.BlockSpec((SWA_BLOCK, D_MODEL), lambda b, i: (b * nb + i, 0)),
        out_shape=jax.ShapeDtypeStruct((TOKENS, D_MODEL), BF16),
        compiler_params=_cparams(("parallel", "parallel")),
        name="swa",
    )(sinks, proj, proj, proj, proj, proj)


OUT_TM = 512
OUT_SUB = 256


def _outproj_body(x_ref, yg_ref, ys_ref, gg_ref, gs_ref, w_ref, nf_ref, wr_ref, rb_ref,
                  x1_ref, hp_ref, lg_ref):
    def sub_tile(t):
        rows = pl.ds(t * OUT_SUB, OUT_SUB)
        merged = (_sigmoid(gg_ref[rows, :].astype(F32)) * yg_ref[rows, :].astype(F32)
                  + _sigmoid(gs_ref[rows, :].astype(F32)) * ys_ref[rows, :].astype(F32))
        acc = jnp.dot(merged.astype(BF16), w_ref[...], preferred_element_type=F32)
        yield
        x1 = x_ref[rows, :] + acc
        x1_ref[rows, :] = x1
        ms = jnp.mean(x1 * x1, axis=-1, keepdims=True)
        h = x1 * lax.rsqrt(ms + EPS) * nf_ref[...]
        hh = h.astype(BF16)
        r = jnp.dot(hh, wr_ref[...], preferred_element_type=F32)
        lg_ref[rows, :] = r[:, :LANES] + r[:, LANES:] + rb_ref[...]
        half = D_MODEL // 2
        hf = hh.astype(F32)
        wa = pltpu.bitcast(hf[:, :half], U32)
        wb = pltpu.bitcast(hf[:, half:], U32)
        _store_token_tiles(hp_ref, t * OUT_SUB, (wa & jnp.uint32(0xFFFF0000)) | (wb >> 16))

    tiles = [sub_tile(t) for t in range(OUT_TM // OUT_SUB)]
    for _ in range(2):
        for g in tiles:
            next(g, None)


def _outproj(x2d, y_gdn, y_swa, proj, w_out, norm_ffn, wr2, r_bias):
    row = lambda w: pl.BlockSpec((OUT_TM, w), lambda i: (i, 0))
    const = lambda s: pl.BlockSpec(s, lambda i: (0, 0))
    return pl.pallas_call(
        _outproj_body,
        grid=(TOKENS // OUT_TM,),
        in_specs=[
            row(D_MODEL), row(D_MODEL), row(D_MODEL),
            pl.BlockSpec((OUT_TM, D_MODEL), lambda i: (i, 5)),
            pl.BlockSpec((OUT_TM, D_MODEL), lambda i: (i, 6)),
            pl.BlockSpec((D_MODEL, D_MODEL), lambda i: (0, 0), pipeline_mode=pl.Buffered(1)),
            const((1, D_MODEL)),
            const((D_MODEL, 2 * LANES)), const((1, LANES)),
        ],
        out_specs=[row(D_MODEL), pl.BlockSpec((OUT_TM * TILE_ROWS, LANES), lambda i: (i, 0)), row(LANES)],
        out_shape=[
            jax.ShapeDtypeStruct((TOKENS, D_MODEL), F32),
            jax.ShapeDtypeStruct((TOKENS * TILE_ROWS, LANES), U32),
            jax.ShapeDtypeStruct((TOKENS, LANES), F32),
        ],
        compiler_params=_cparams(("parallel",)),
        name="outproj",
    )(x2d, y_gdn, y_swa, proj, proj, w_out, norm_ffn, wr2, r_bias)


ROUTE_TM = 1024


def _route_body(lg_ref, tri_ref, idx_ref, wt_ref, cnt_ref, run_ref):
    @pl.when(pl.program_id(0) == 0)
    def _():
        run_ref[...] = jnp.zeros_like(run_ref)

    lg = lg_ref[...]
    lane = lax.broadcasted_iota(I32, lg.shape, 1)
    ninf = -jnp.inf
    big = jnp.int32(LANES)
    is_g = lane < N_GROUPS
    glog = jnp.where(is_g, lg, ninf)
    gmax = jnp.max(glog, axis=-1, keepdims=True)
    gden = jnp.sum(jnp.where(is_g, jnp.exp(lg - gmax), 0.0), axis=-1, keepdims=True)
    p_sel = 1.0 / gden
    grp = jnp.min(jnp.where(glog == gmax, lane, big), axis=-1, keepdims=True)
    emask = (lane >= N_GROUPS) & (lane < N_GROUPS + N_EXPERTS) & (((lane - N_GROUPS) >> 3) == grp)
    el = jnp.where(emask, lg, ninf)
    v1 = jnp.max(el, axis=-1, keepdims=True)
    i1 = jnp.min(jnp.where(el == v1, lane, big), axis=-1, keepdims=True)
    el2 = jnp.where(lane == i1, ninf, el)
    v2 = jnp.max(el2, axis=-1, keepdims=True)
    i2 = jnp.min(jnp.where(el2 == v2, lane, big), axis=-1, keepdims=True)
    e = jnp.exp(v2 - v1)
    w1 = p_sel / (1.0 + e)
    w2 = p_sel * e / (1.0 + e)
    e0 = i1 - N_GROUPS
    e1 = i2 - N_GROUPS

    oh0 = lane == e0
    oh1 = lane == e1
    onehot = jnp.where(oh0 | oh1, 1.0, 0.0)
    prefix = jnp.dot(tri_ref[...], onehot.astype(BF16), preferred_element_type=F32) + run_ref[0:1, :]
    r0 = jnp.sum(jnp.where(oh0, prefix, 0.0), axis=-1, keepdims=True).astype(I32)
    r1 = jnp.sum(jnp.where(oh1, prefix, 0.0), axis=-1, keepdims=True).astype(I32)
    run = run_ref[0:1, :] + jnp.sum(onehot, axis=0, keepdims=True)
    run_ref[...] = jnp.broadcast_to(run, run_ref.shape)
    cnt_ref[...] = jnp.broadcast_to(run, cnt_ref.shape).astype(I32)

    zi = jnp.zeros(lg.shape, I32)
    idx = jnp.where(lane == 0, e0, zi)
    idx = jnp.where(lane == 1, e1, idx)
    idx = jnp.where(lane == 2, r0, idx)
    idx = jnp.where(lane == 3, r1, idx)
    idx_ref[...] = idx
    wt_ref[...] = jnp.where(lane == 0, w1, jnp.where(lane == 1, w2, 0.0))


def _route(logits, tri):
    row = pl.BlockSpec((ROUTE_TM, LANES), lambda i: (i, 0))
    return pl.pallas_call(
        _route_body,
        grid=(TOKENS // ROUTE_TM,),
        in_specs=[row, pl.BlockSpec((ROUTE_TM, ROUTE_TM), lambda i: (0, 0))],
        out_specs=[row, row, pl.BlockSpec((8, LANES), lambda i: (0, 0))],
        out_shape=[
            jax.ShapeDtypeStruct((TOKENS, LANES), I32),
            jax.ShapeDtypeStruct((TOKENS, LANES), F32),
            jax.ShapeDtypeStruct((8, LANES), I32),
        ],
        scratch_shapes=[pltpu.VMEM((8, LANES), F32)],
        compiler_params=_cparams(("arbitrary",)),
        name="route",
    )(logits, tri)


DISP_TM = 512
N_DISP = TOKENS // DISP_TM
DISP_RING = 3
DISP_ROWS = DISP_TM * TILE_ROWS


def _dispatch_body(meta_ref, dest_ref, h_hbm, xs_ref, zbuf, hbuf, in_sem, out_sem, zsem):
    @pl.when(pl.program_id(0) == 0)
    def _():
        zbuf[...] = jnp.zeros_like(zbuf)

        def zero_block(row0):
            return pltpu.make_async_copy(
                zbuf, xs_ref.at[pl.ds(row0 * TILE_ROWS, MOE_BLOCK * TILE_ROWS), :], zsem)

        def per_expert(e, n):
            has = meta_ref[N_EXPERTS + e] > 0

            @pl.when(has)
            def _():
                zero_block(pl.multiple_of(meta_ref[e] - MOE_BLOCK, MOE_BLOCK)).start()

            return n + has.astype(I32)

        n_last = lax.fori_loop(0, N_EXPERTS, per_expert, jnp.int32(0))
        used = meta_ref[2 * N_EXPERTS]

        def tail(j, c):
            zero_block(pl.multiple_of(j * MOE_BLOCK, MOE_BLOCK)).start()
            return c

        lax.fori_loop(used, N_BLOCKS, tail, 0)

        def drain(i, c):
            zero_block(0).wait()
            return c

        lax.fori_loop(0, n_last + (N_BLOCKS - used), drain, 0)

    i = pl.program_id(0)
    b = lax.rem(i, DISP_RING)

    def load(step, slot):
        return pltpu.make_async_copy(
            h_hbm.at[pl.ds(pl.multiple_of(step * DISP_ROWS, DISP_ROWS), DISP_ROWS), :], hbuf.at[slot],
            in_sem.at[slot])

    def wait_scatters(slot):
        for k in range(TOP_K):
            pltpu.make_async_copy(hbuf.at[slot], xs_ref.at[pl.ds(0, DISP_ROWS), :], out_sem.at[slot]).wait()

    @pl.when(i == 0)
    def _():
        load(0, 0).start()

    load(i, b).wait()

    @pl.when(i + 1 < N_DISP)
    def _():
        load(i + 1, lax.rem(i + 1, DISP_RING)).start()

    def issue(r, c):
        for k in range(TOP_K):
            d = dest_ref[TOP_K * r + k]
            pltpu.make_async_copy(hbuf.at[b, pl.ds(pl.multiple_of(r * TILE_ROWS, TILE_ROWS), TILE_ROWS), :],
                                  xs_ref.at[pl.ds(pl.multiple_of(d * TILE_ROWS, TILE_ROWS), TILE_ROWS), :],
                                  out_sem.at[b]).start(priority=k)
        return c

    lax.fori_loop(0, DISP_TM, issue, 0, unroll=4)

    @pl.when(i >= 1)
    def _():
        wait_scatters(lax.rem(i + DISP_RING - 1, DISP_RING))

    @pl.when(i == N_DISP - 1)
    def _():
        wait_scatters(b)


def _dispatch(meta, dest_flat, hp):
    grid_spec = pltpu.PrefetchScalarGridSpec(
        num_scalar_prefetch=1,
        grid=(N_DISP,),
        in_specs=[
            pl.BlockSpec((DISP_TM * TOP_K,), lambda i, m: (i,), memory_space=pltpu.SMEM),
            pl.BlockSpec(memory_space=pl.ANY),
        ],
        out_specs=pl.BlockSpec(memory_space=pl.ANY),
        scratch_shapes=[
            pltpu.VMEM((MOE_BLOCK * TILE_ROWS, LANES), U32),
            pltpu.VMEM((DISP_RING, DISP_ROWS, LANES), U32),
            pltpu.SemaphoreType.DMA((DISP_RING,)),
            pltpu.SemaphoreType.DMA((DISP_RING,)),
            pltpu.SemaphoreType.DMA,
        ],
    )
    return pl.pallas_call(
        _dispatch_body,
        grid_spec=grid_spec,
        out_shape=jax.ShapeDtypeStruct((N_SLOTS * TILE_ROWS, LANES), U32),
        compiler_params=_cparams(("arbitrary",)),
        name="dispatch",
    )(meta, dest_flat, hp)


def _slots_body(idx_ref, start_ref, dest_ref):
    idx = idx_ref[...]
    lane = lax.broadcasted_iota(I32, idx.shape, 1)
    start = start_ref[...]
    out = jnp.zeros(idx.shape, I32)
    for k in range(TOP_K):
        e = idx[:, k:k + 1]
        base = jnp.sum(jnp.where(lane == e, start, 0.0), axis=-1, keepdims=True).astype(I32)
        out = jnp.where(lane == k, base + idx[:, TOP_K + k:TOP_K + k + 1], out)
    dest_ref[...] = out


SLOTS_TM = 2048


def _slots(idx, pad_start_row):
    row = pl.BlockSpec((SLOTS_TM, LANES), lambda i: (i, 0))
    return pl.pallas_call(
        _slots_body,
        grid=(TOKENS // SLOTS_TM,),
        in_specs=[row, pl.BlockSpec((1, LANES), lambda i: (0, 0))],
        out_specs=row,
        out_shape=jax.ShapeDtypeStruct((TOKENS, LANES), I32),
        compiler_params=_cparams(("parallel",)),
        name="slots",
    )(idx, pad_start_row)


EXPERT_OUT_COLS = 256


def _experts_body(sched_ref, x_ref, wg_hbm, wu_hbm, wd_hbm, y_ref,
                  wg_st, wu_st, wd_st, wg_bf, wu_bf, wd_bf, sem):
    j = pl.program_id(0)
    e = sched_ref[j]
    slot = sched_ref[N_BLOCKS + j]
    nxt = sched_ref[2 * N_BLOCKS + j]
    used = sched_ref[3 * N_BLOCKS]
    new_expert = (j == 0) | (e != sched_ref[jnp.maximum(j - 1, 0)])

    def weight_copies(expert, s):
        return (pltpu.make_async_copy(wg_hbm.at[expert], wg_st.at[s], sem.at[s, 0]),
                pltpu.make_async_copy(wu_hbm.at[expert], wu_st.at[s], sem.at[s, 1]),
                pltpu.make_async_copy(wd_hbm.at[expert], wd_st.at[s], sem.at[s, 2]))

    @pl.when(j == 0)
    def _():
        for c in weight_copies(e, slot):
            c.start(priority=1)

    @pl.when(new_expert)
    def _():
        for c in weight_copies(e, slot):
            c.wait()

        @pl.when(nxt >= 0)
        def _():
            for c in weight_copies(nxt, 1 - slot):
                c.start(priority=1)

        wg_bf[...] = wg_st[slot].astype(BF16)
        wu_bf[...] = wu_st[slot].astype(BF16)
        wd_bf[...] = wd_st[slot].astype(BF16)

    @pl.when(j < used)
    def _():
        w = _load_token_tiles(x_ref, MOE_BLOCK)
        xa = pltpu.bitcast(w & jnp.uint32(0xFFFF0000), F32).astype(BF16)
        xb = pltpu.bitcast(w << 16, F32).astype(BF16)
        half = D_MODEL // 2
        g = (jnp.dot(xa, wg_bf[:half, :], preferred_element_type=F32)
             + jnp.dot(xb, wg_bf[half:, :], preferred_element_type=F32))
        u = (jnp.dot(xa, wu_bf[:half, :], preferred_element_type=F32)
             + jnp.dot(xb, wu_bf[half:, :], preferred_element_type=F32))
        hb = (_silu(g) * u).astype(BF16)
        for c in range(half // EXPERT_OUT_COLS):
            lo = c * EXPERT_OUT_COLS
            ya = jnp.dot(hb, wd_bf[:, lo:lo + EXPERT_OUT_COLS], preferred_element_type=F32)
            yb = jnp.dot(hb, wd_bf[:, half + lo:half + lo + EXPERT_OUT_COLS], preferred_element_type=F32)
            wa = pltpu.bitcast(ya.astype(BF16).astype(F32), U32)
            wb = pltpu.bitcast(yb.astype(BF16).astype(F32), U32)
            _store_token_tiles(y_ref, 0, wa | (wb >> 16), first_block=lo // LANES)

    @pl.when(j >= used)
    def _():
        y_ref[...] = jnp.zeros_like(y_ref)


def _experts(sched, xs, w_gate, w_up, w_down):
    grid_spec = pltpu.PrefetchScalarGridSpec(
        num_scalar_prefetch=1,
        grid=(N_BLOCKS,),
        in_specs=[
            pl.BlockSpec((MOE_BLOCK * TILE_ROWS, LANES), lambda j, s: (j, 0)),
            pl.BlockSpec(memory_space=pl.ANY),
            pl.BlockSpec(memory_space=pl.ANY),
            pl.BlockSpec(memory_space=pl.ANY),
        ],
        out_specs=pl.BlockSpec((MOE_BLOCK * TILE_ROWS, LANES), lambda j, s: (j, 0)),
        scratch_shapes=[
            pltpu.VMEM((2, D_MODEL, D_EXPERT), F32),
            pltpu.VMEM((2, D_MODEL, D_EXPERT), F32),
            pltpu.VMEM((2, D_EXPERT, D_MODEL), F32),
            pltpu.VMEM((D_MODEL, D_EXPERT), BF16),
            pltpu.VMEM((D_MODEL, D_EXPERT), BF16),
            pltpu.VMEM((D_EXPERT, D_MODEL), BF16),
            pltpu.SemaphoreType.DMA((2, 3)),
        ],
    )
    return pl.pallas_call(
        _experts_body,
        grid_spec=grid_spec,
        out_shape=jax.ShapeDtypeStruct((N_SLOTS * TILE_ROWS, LANES), U32),
        compiler_params=_cparams(("arbitrary",)),
        name="experts",
    )(sched, xs, w_gate, w_up, w_down)


COMB_TM = 256
N_COMB = TOKENS // COMB_TM


def _unpack_pairs(w):
    hi = pltpu.bitcast(w & jnp.uint32(0xFFFF0000), F32)
    lo = pltpu.bitcast(w << 16, F32)
    return jnp.concatenate([hi, lo], axis=1)


def _combine_body(dest_ref, dest_next_ref, wt_ref, x1_ref, gain_ref, ys_ref, o_ref, buf, sem):
    i = pl.program_id(0)
    slot = lax.rem(i, 2)

    def start_gathers(dref, s):
        def issue(r, c):
            for k in range(TOP_K):
                d = dref[TOP_K * r + k]
                pltpu.make_async_copy(
                    ys_ref.at[pl.ds(pl.multiple_of(d * TILE_ROWS, TILE_ROWS), TILE_ROWS), :],
                    buf.at[s, k, pl.ds(pl.multiple_of(r * TILE_ROWS, TILE_ROWS), TILE_ROWS), :],
                    sem.at[s]).start(priority=k)
            return c
        lax.fori_loop(0, COMB_TM, issue, 0, unroll=4)

    @pl.when(i == 0)
    def _():
        start_gathers(dest_ref, slot)

    @pl.when(i + 1 < N_COMB)
    def _():
        start_gathers(dest_next_ref, 1 - slot)

    for k in range(TOP_K):
        pltpu.make_async_copy(ys_ref.at[pl.ds(0, COMB_TM * TILE_ROWS), :], buf.at[slot, k], sem.at[slot]).wait()

    wt = wt_ref[...]
    y = (x1_ref[...] + wt[:, 0:1] * _unpack_pairs(_load_token_tiles(buf.at[slot, 0], COMB_TM))
         + wt[:, 1:2] * _unpack_pairs(_load_token_tiles(buf.at[slot, 1], COMB_TM)))
    ms = jnp.mean(y * y, axis=-1, keepdims=True)
    o_ref[...] = y * lax.rsqrt(ms + EPS) * gain_ref[...]


def _combine(dest_flat, wts, x1, norm_final, ys):
    dblk = lambda f: pl.BlockSpec((COMB_TM * TOP_K,), f, memory_space=pltpu.SMEM)
    return pl.pallas_call(
        _combine_body,
        grid=(N_COMB,),
        in_specs=[
            dblk(lambda i: (i,)),
            dblk(lambda i: (jnp.minimum(i + 1, N_COMB - 1),)),
            pl.BlockSpec((COMB_TM, LANES), lambda i: (i, 0)),
            pl.BlockSpec((COMB_TM, D_MODEL), lambda i: (i, 0)),
            pl.BlockSpec((1, D_MODEL), lambda i: (0, 0)),
            pl.BlockSpec(memory_space=pl.ANY),
        ],
        out_specs=pl.BlockSpec((COMB_TM, D_MODEL), lambda i: (i, 0)),
        out_shape=jax.ShapeDtypeStruct((TOKENS, D_MODEL), F32),
        scratch_shapes=[pltpu.VMEM((2, TOP_K, COMB_TM * TILE_ROWS, LANES), U32), pltpu.SemaphoreType.DMA((2,))],
        compiler_params=_cparams(("arbitrary",)),
        name="combine",
    )(dest_flat, dest_flat, wts, x1, norm_final, ys)


def _pad_lanes(v):
    v = v.reshape(1, -1).astype(F32)
    return jnp.pad(v, ((0, 0), (0, LANES - v.shape[1])))


def kernel(x, norm_mix, w_in, conv_w, gdn_a_log, gdn_dt_bias, gdn_norm, swa_sinks, w_out, norm_ffn,
           w_router_group, b_router_group, w_router_expert, b_router_expert, w_gate, w_up, w_down,
           norm_final):
    l = 0
    x2d = x.reshape(TOKENS, D_MODEL)
    w = w_in[l]
    o_z = 3 * D_MODEL
    o_a = o_z + D_MODEL
    o_sq = o_a + 2 * GDN_HEADS
    o_sk = o_sq + D_MODEL
    o_sv = o_sk + SWA_KV_HEADS * SWA_DH
    o_gg = o_sv + SWA_KV_HEADS * SWA_DH
    o_gs = o_gg + D_MODEL
    w_main = jnp.concatenate(
        [w[:, :o_a], w[:, o_sq:o_sk], w[:, o_gg:o_gs], w[:, o_gs:], w[:, o_sk:o_sv], w[:, o_sv:o_gg]],
        axis=1).astype(BF16)
    w_ab = jnp.pad(w[:, o_a:o_sq], ((0, 0), (0, LANES - 2 * GDN_HEADS))).astype(BF16)

    proj, ab = _inproj(x2d, norm_mix[l].reshape(1, D_MODEL), w_main, w_ab, conv_w[l])

    g1, gt = _gdn_prep(ab, _pad_lanes(gdn_a_log[l]), _pad_lanes(gdn_dt_bias[l]))
    gr = gt[:, :GDN_HEADS, :].reshape(BATCH, GDN_HEADS, N_GROUPS_SEQ, GDN_GROUP_ROWS)
    y_gdn = _gdn(proj, g1, gr, gdn_norm[l].reshape(1, GDN_DV))

    y_swa = _swa(proj, swa_sinks[l].astype(F32))

    w_r = jnp.concatenate([w_router_group[l], w_router_expert[l]], axis=1).astype(F32)
    w_r = jnp.pad(w_r, ((0, 0), (0, LANES - w_r.shape[1])))
    wr_hi = w_r.astype(BF16)
    wr_lo = (w_r - wr_hi.astype(F32)).astype(BF16)
    r_bias = _pad_lanes(jnp.concatenate([b_router_group[l], b_router_expert[l]]))
    x1, hp, logits = _outproj(x2d, y_gdn, y_swa, proj, w_out[l].astype(BF16),
                              norm_ffn[l].reshape(1, D_MODEL), jnp.concatenate([wr_hi, wr_lo], axis=1), r_bias)

    tri = (lax.broadcasted_iota(I32, (ROUTE_TM, ROUTE_TM), 1)
           < lax.broadcasted_iota(I32, (ROUTE_TM, ROUTE_TM), 0)).astype(BF16)
    idx, wts, cnt = _route(logits, tri)

    counts = cnt[0, :N_EXPERTS]
    padded = (counts + MOE_BLOCK - 1) // MOE_BLOCK * MOE_BLOCK
    pad_end = jnp.cumsum(padded)
    pad_start = pad_end - padded
    dest = _slots(idx, _pad_lanes(pad_start))
    dest_flat = dest[:, :TOP_K].reshape(N_ASSIGN)
    blk_pos = jnp.arange(N_BLOCKS, dtype=I32) * MOE_BLOCK
    blk_e = jnp.minimum(jnp.sum((pad_end[None, :] <= blk_pos[:, None]).astype(I32), axis=1), N_EXPERTS - 1)
    used = pad_end[-1:] // MOE_BLOCK
    meta = jnp.concatenate([pad_end, padded, used]).astype(I32)
    is_new = jnp.concatenate([jnp.ones((1,), I32), (blk_e[1:] != blk_e[:-1]).astype(I32)])
    ordinal = jnp.cumsum(is_new) - 1
    nxt_pos = jnp.sum((ordinal[None, :] <= ordinal[:, None]).astype(I32), axis=1)
    nxt = jnp.where(nxt_pos < N_BLOCKS, blk_e[jnp.minimum(nxt_pos, N_BLOCKS - 1)], -1)
    sched = jnp.concatenate([blk_e, ordinal & 1, nxt, used]).astype(I32)

    xs = _dispatch(meta, dest_flat, hp)
    ys = _experts(sched, xs, w_gate[l], w_up[l], w_down[l])
    out = _combine(dest_flat, wts, x1, norm_final.reshape(1, D_MODEL), ys)
    return out.reshape(BATCH, SEQ, D_MODEL)
```

```python
import math

import jax
import jax.numpy as jnp
from jax import lax
from jax.experimental import pallas as pl
from jax.experimental.pallas import tpu as pltpu

F32 = jnp.float32
BF16 = jnp.bfloat16
I32 = jnp.int32
U32 = jnp.uint32

D_MODEL = 2048
BATCH = 16
SEQ = 2048
TOKENS = BATCH * SEQ
EPS = 1e-6

GDN_HEADS = 16
GDN_DK = 128
GDN_DV = 128
GDN_CHUNK = 64
CONV_W = 4
GDN_GROUP = 4
GDN_GROUP_ROWS = GDN_GROUP * GDN_CHUNK
N_CHUNKS = SEQ // GDN_CHUNK
N_GROUPS_SEQ = SEQ // GDN_GROUP_ROWS

SWA_HEADS = 32
SWA_KV_HEADS = 4
SWA_DH = 64
SWA_GROUP = 8
SWA_BLOCK = 128
WINDOW = 128

N_GROUPS = 8
EXPERTS_PER_GROUP = 8
N_EXPERTS = 64
TOP_K = 2
D_EXPERT = 512
MOE_BLOCK = 256
N_ASSIGN = TOKENS * TOP_K
N_BLOCKS = N_ASSIGN // MOE_BLOCK + N_EXPERTS
N_SLOTS = N_BLOCKS * MOE_BLOCK

PROJ_DIM = 7 * D_MODEL + 2 * SWA_KV_HEADS * SWA_DH
LANES = 128

VMEM_LIMIT = 56 * 1024 * 1024


def _cparams(sem, vmem=VMEM_LIMIT):
    return pltpu.CompilerParams(dimension_semantics=sem, vmem_limit_bytes=vmem)


TILE_ROWS = (D_MODEL // 2) // LANES


def _store_token_tiles(ref, first_token, words, first_block=0):
    n = words.shape[0]
    for s in range(words.shape[1] // LANES):
        ref[pl.ds(first_token * TILE_ROWS + first_block + s, n, stride=TILE_ROWS), :] = (
            words[:, s * LANES:(s + 1) * LANES])


def _load_token_tiles(ref, n):
    return jnp.concatenate([ref[pl.ds(s, n, stride=TILE_ROWS), :] for s in range(TILE_ROWS)], axis=1)


def _sigmoid(x):
    return 0.5 * jnp.tanh(0.5 * x) + 0.5


def _silu(x):
    h = 0.5 * x
    return h * jnp.tanh(h) + h


INPROJ_TM = 2048
INPROJ_TN = 512
INPROJ_VMEM = 60 * 1024 * 1024


INPROJ_RC = 256
INPROJ_RING = 3
QKV_TILES = 3 * D_MODEL // INPROJ_TN
QK_TILES = 2 * D_MODEL // INPROJ_TN
Q_TILES = D_MODEL // INPROJ_TN
assert INPROJ_TM == SEQ


def _inproj_body(x_ref, g_ref, w_ref, wab_ref, cw_ref, o_ref, ab_ref, h_ref, cpad):
    j = pl.program_id(1)

    @pl.when(j == 0)
    def _():
        def chunk(i, c):
            r = pl.ds(pl.multiple_of(i * 128, 128), 128)
            x = x_ref[r, :]
            ms = jnp.mean(x * x, axis=-1, keepdims=True)
            h_ref[r, :] = (x * lax.rsqrt(ms + EPS) * g_ref[...]).astype(BF16)
            return c
        lax.fori_loop(0, INPROJ_TM // 128, chunk, 0)
        ab_ref[...] = jnp.dot(h_ref[...], wab_ref[...], preferred_element_type=F32)

    @pl.when(j >= QKV_TILES)
    def _():
        o_ref[...] = jnp.dot(h_ref[...], w_ref[...], preferred_element_type=F32).astype(BF16)

    @pl.when(j < QKV_TILES)
    def _():
        cw = cw_ref[...]
        is_qk = j < QK_TILES
        qscale = jnp.where(j < Q_TILES, GDN_DK ** -0.5, 1.0).astype(F32)
        n_chunks = INPROJ_TM // INPROJ_RC

        def matmul(c):
            rows = pl.ds(c * INPROJ_RC, INPROJ_RC)
            cpad[c % INPROJ_RING, pl.ds(8, INPROJ_RC), :] = jnp.dot(
                h_ref[rows, :], w_ref[...], preferred_element_type=F32)

        def epilogue(c):
            buf = cpad.at[c % INPROJ_RING]
            if c == 0:
                buf[pl.ds(0, 8), :] = jnp.zeros((8, INPROJ_TN), F32)
            else:
                buf[pl.ds(0, 8), :] = cpad[(c - 1) % INPROJ_RING, pl.ds(INPROJ_RC, 8), :]
            y = None
            for s in range(CONV_W):
                t = buf[pl.ds(8 - s, INPROJ_RC), :] * cw[CONV_W - 1 - s:CONV_W - s, :]
                y = t if y is None else y + t
            y = _silu(y)
            parts = []
            for g in range(INPROJ_TN // LANES):
                yg = y[:, g * LANES:(g + 1) * LANES]
                inv = lax.rsqrt(jnp.sum(yg * yg, -1, keepdims=True) + EPS) * qscale
                parts.append(yg * jnp.where(is_qk, inv, 1.0))
            o_ref[pl.ds(c * INPROJ_RC, INPROJ_RC), :] = jnp.concatenate(parts, axis=1).astype(BF16)

        matmul(0)
        for c in range(n_chunks):
            if c + 1 < n_chunks:
                matmul(c + 1)
            epilogue(c)


def _inproj(x2d, gain, w_main, w_ab, conv_w):
    grid = (TOKENS // INPROJ_TM, PROJ_DIM // INPROJ_TN)
    return pl.pallas_call(
        _inproj_body,
        grid=grid,
        in_specs=[
            pl.BlockSpec((INPROJ_TM, D_MODEL), lambda i, j: (i, 0)),
            pl.BlockSpec((1, D_MODEL), lambda i, j: (0, 0)),
            pl.BlockSpec((D_MODEL, INPROJ_TN), lambda i, j: (0, j)),
            pl.BlockSpec((D_MODEL, LANES), lambda i, j: (0, 0)),
            pl.BlockSpec((CONV_W, INPROJ_TN), lambda i, j: (0, jnp.minimum(j, QKV_TILES - 1))),
        ],
        out_specs=[
            pl.BlockSpec((INPROJ_TM, INPROJ_TN), lambda i, j: (i, j)),
            pl.BlockSpec((INPROJ_TM, LANES), lambda i, j: (i, 0)),
        ],
        out_shape=[
            jax.ShapeDtypeStruct((TOKENS, PROJ_DIM), BF16),
            jax.ShapeDtypeStruct((TOKENS, LANES), F32),
        ],
        scratch_shapes=[pltpu.VMEM((INPROJ_TM, D_MODEL), BF16),
                        pltpu.VMEM((INPROJ_RING, 8 + INPROJ_RC, INPROJ_TN), F32)],
        compiler_params=_cparams(("parallel", "arbitrary"), vmem=INPROJ_VMEM),
        name="inproj",
    )(x2d, gain, w_main, w_ab, conv_w)


def _gdn_prep_body(ab_ref, alog_ref, dtb_ref, g1_ref, gt_ref):
    ab = ab_ref[...]
    lane = lax.broadcasted_iota(I32, ab.shape, 1)
    row = lax.broadcasted_iota(I32, ab.shape, 0) % GDN_CHUNK
    xa = ab + dtb_ref[...]
    softplus = jnp.maximum(xa, 0.0) + jnp.log(1.0 + jnp.exp(-jnp.abs(xa)))
    g = jnp.where(lane < GDN_HEADS, -jnp.exp(alog_ref[...]) * softplus, 0.0)
    gam = g
    s = 1
    while s < GDN_CHUNK:
        gam = gam + jnp.where(row >= s, pltpu.roll(gam, s, 0), 0.0)
        s *= 2
    g1_ref[...] = jnp.where(lane < GDN_HEADS, gam, _sigmoid(ab))
    gt_ref[0] = gam.T


def _gdn_prep(ab, alog_pad, dtb_pad):
    return pl.pallas_call(
        _gdn_prep_body,
        grid=(BATCH,),
        in_specs=[
            pl.BlockSpec((SEQ, LANES), lambda b: (b, 0)),
            pl.BlockSpec((1, LANES), lambda b: (0, 0)),
            pl.BlockSpec((1, LANES), lambda b: (0, 0)),
        ],
        out_specs=[
            pl.BlockSpec((SEQ, LANES), lambda b: (b, 0)),
            pl.BlockSpec((1, LANES, SEQ), lambda b: (b, 0, 0)),
        ],
        out_shape=[
            jax.ShapeDtypeStruct((TOKENS, LANES), F32),
            jax.ShapeDtypeStruct((BATCH, LANES, SEQ), F32),
        ],
        compiler_params=_cparams(("parallel",)),
        name="gdn_prep",
    )(ab, alog_pad, dtb_pad)


GDN_HB = 4
GDN_PAIR = 2
N_PAIRS = N_GROUPS_SEQ // GDN_PAIR
PAIR_CHUNKS = GDN_PAIR * GDN_GROUP
PAIR_ROWS = GDN_PAIR * GDN_GROUP_ROWS
GDN_VMEM = 59 * 1024 * 1024


def _block_diag(x):
    t = jnp.concatenate([x] * GDN_GROUP, axis=0)
    rb = lax.broadcasted_iota(I32, t.shape, 0) // GDN_CHUNK
    cb = lax.broadcasted_iota(I32, t.shape, 1) // GDN_CHUNK
    return jnp.where(rb == cb, t, jnp.zeros_like(t))


def _sbs_product(lhs_list, y):
    bd = _block_diag(y.astype(BF16))
    xs = [x.astype(BF16) for x in lhs_list]
    lhs = xs[0] if len(xs) == 1 else jnp.concatenate(xs, axis=0)
    r = jnp.dot(lhs, bd, preferred_element_type=F32)
    c = GDN_CHUNK
    return [r[i * c:(i + 1) * c] for i in range(len(xs))]


def _gdn_body(q_ref, k_ref, v_ref, z_ref, g1_ref, gr_ref, gain_ref,
              o_ref,
              kdbf, rhsbf, qd, gamb, betab, o0s, qts, vs, wps):
    hg = pl.program_id(1)
    shape = (PAIR_ROWS, LANES)
    lane = lax.broadcasted_iota(I32, shape, 1)

    def phase0(pair):
        rows = pl.ds(pl.multiple_of(pair * PAIR_ROWS, PAIR_ROWS), PAIR_ROWS)
        g1 = g1_ref[rows, :]
        for hd in range(GDN_HB):
            hs = slice(hd * LANES, (hd + 1) * LANES)
            head = hg * GDN_HB + hd

            def col(off):
                c = jnp.sum(jnp.where(lane == off + head, g1, 0.0), axis=-1, keepdims=True)
                return jnp.broadcast_to(c, shape)

            gam = col(0)
            beta = col(GDN_HEADS)
            gamb[hd, rows, :] = gam
            betab[hd, rows, :] = beta.astype(BF16)
            gam3 = gam.reshape(PAIR_CHUNKS, GDN_CHUNK, LANES)
            glast = jnp.broadcast_to(gam3[:, GDN_CHUNK - 1:GDN_CHUNK, :], gam3.shape).reshape(shape)
            eg = jnp.exp(gam)
            kdf = jnp.exp(glast - gam)

            qd[hd, rows, :] = (q_ref[rows, hs].astype(F32) * eg).astype(BF16)
            k = k_ref[rows, hs].astype(F32)
            kdbf[hd, rows, :] = (k * kdf).astype(BF16)
            rhsbf[hd, rows, LANES:] = (k * (beta * eg)).astype(BF16)
            rhsbf[hd, rows, :LANES] = (v_ref[rows, hs].astype(F32) * beta).astype(BF16)

    gshape = (GDN_CHUNK, GDN_GROUP_ROWS)
    ii = lax.broadcasted_iota(I32, gshape, 0)
    jj = lax.broadcasted_iota(I32, gshape, 1) % GDN_CHUNK
    eye = jnp.where(ii == jj, 1.0, 0.0).astype(F32)
    pr = lax.broadcasted_iota(I32, (2 * GDN_CHUNK, LANES), 0) // GDN_CHUNK
    tl = lax.broadcasted_iota(I32, (LANES, GDN_GROUP_ROWS), 1) // GDN_CHUNK
    lane_c = lax.broadcasted_iota(I32, (GDN_CHUNK, LANES), 1)
    nt = (((1,), (1,)), ((), ()))

    def sbs_bcast(ref, hd, base):
        parts = [ref[hd, pl.ds(base + m * GDN_CHUNK, GDN_CHUNK), :] for m in range(GDN_GROUP)]
        a = jnp.where(lane_c < GDN_CHUNK, parts[0], parts[1])
        b = jnp.where(lane_c < GDN_CHUNK, parts[2], parts[3])
        return jnp.concatenate([a, b], axis=1)

    def phase1(hd, pair, gg):
        g = pair * GDN_PAIR + gg
        base = pl.multiple_of(g * GDN_GROUP_ROWS, GDN_GROUP_ROWS)
        slot = lax.rem(pair, 2)
        res = []
        for p in range(2):
            rows = pl.ds(base + p * 2 * GDN_CHUNK, 2 * GDN_CHUNK)
            kst = k_ref[rows, hd * LANES:(hd + 1) * LANES]
            qst = q_ref[rows, hd * LANES:(hd + 1) * LANES]
            lhs_k = jnp.concatenate([kst[:GDN_CHUNK], kst[GDN_CHUNK:]], axis=1)
            lhs_q = jnp.concatenate([qst[:GDN_CHUNK], qst[GDN_CHUNK:]], axis=1)
            lhs = jnp.concatenate([lhs_q, lhs_k], axis=0)
            zero = jnp.zeros_like(kst)
            bt = jnp.concatenate([jnp.where(pr == 0, kst, zero), jnp.where(pr == 1, kst, zero)], axis=1)
            res.append(lax.dot_general(lhs, bt, nt, preferred_element_type=F32))
        yield
        qk = jnp.concatenate([res[0][:GDN_CHUNK], res[1][:GDN_CHUNK]], axis=1)
        kk = jnp.concatenate([res[0][GDN_CHUNK:], res[1][GDN_CHUNK:]], axis=1)

        gc = sbs_bcast(gamb, hd, base)
        bc = sbs_bcast(betab, hd, base).astype(F32)
        grow = gr_ref[0, hd, pl.ds(g, 1), :]
        decay = jnp.exp(jnp.where(ii >= jj, gc - grow, -jnp.inf))
        a = jnp.where(ii > jj, kk * decay * bc, 0.0)
        qkd = qk * decay

        u = eye - a
        (x,) = _sbs_product([a], a)
        yield
        for lvl in range(1, 6):
            if lvl < 5:
                x2, ux = _sbs_product([x, u], x)
                u = u + ux
                x = x2
            else:
                (ux,) = _sbs_product([u], x)
                u = u + ux
            yield

        rows4 = pl.ds(base, GDN_GROUP_ROWS)
        so = jnp.dot(_block_diag(u.astype(BF16)), rhsbf[hd, rows4, :], preferred_element_type=F32)
        yield
        sol = so.astype(BF16)

        kdt = kdbf[hd, rows4, :].astype(F32).T.astype(BF16)
        zt = jnp.zeros_like(kdt)
        lhs2 = jnp.concatenate(
            [_block_diag(qkd.astype(BF16))] + [jnp.where(tl == m, kdt, zt) for m in range(GDN_GROUP)],
            axis=0)
        r = jnp.dot(lhs2, sol, preferred_element_type=F32)
        srows = pl.ds(pl.multiple_of(slot * PAIR_ROWS + gg * GDN_GROUP_ROWS, GDN_GROUP_ROWS), GDN_GROUP_ROWS)
        o0s[hd, srows, :] = r[:GDN_GROUP_ROWS, :LANES]
        qts[hd, srows, :] = qd[hd, rows4, :].astype(F32) - r[:GDN_GROUP_ROWS, LANES:]
        for m in range(GDN_GROUP):
            blk = r[GDN_GROUP_ROWS + m * LANES:GDN_GROUP_ROWS + (m + 1) * LANES]
            dst = pl.ds(pl.multiple_of((slot * PAIR_CHUNKS + gg * GDN_GROUP + m) * LANES, LANES), LANES)
            vs[hd, dst, :] = blk[:, :LANES]
            wps[hd, dst, :] = blk[:, LANES:]

    gain = gain_ref[...]

    def phase2(hd, pair, j, s):
        c = pair * PAIR_CHUNKS + j
        local = lax.rem(pair, 2) * PAIR_CHUNKS + j
        rows = pl.ds(pl.multiple_of(c * GDN_CHUNK, GDN_CHUNK), GDN_CHUNK)
        lrows = pl.ds(pl.multiple_of(local * GDN_CHUNK, GDN_CHUNK), GDN_CHUNK)
        srow = pl.ds(pl.multiple_of(local * LANES, LANES), LANES)
        lhs = jnp.concatenate([wps[hd, srow, :].astype(BF16), qts[hd, lrows, :].astype(BF16)], axis=0)
        r = jnp.dot(lhs, s.astype(BF16), preferred_element_type=F32)
        o = r[LANES:] + o0s[hd, lrows, :]
        dec = jnp.exp(gamb[hd, pl.ds(c * GDN_CHUNK + GDN_CHUNK - 1, 1), :])
        s_new = s * dec + vs[hd, srow, :] - r[:LANES]
        y = o * lax.rsqrt(jnp.mean(o * o, -1, keepdims=True) + EPS) * gain
        zz = z_ref[rows, hd * LANES:(hd + 1) * LANES].astype(F32)
        o_ref[rows, hd * LANES:(hd + 1) * LANES] = (y * _silu(zz)).astype(BF16)
        return s_new

    def phase2_chain(hd, pair, states):
        s = states[hd]
        for j in range(PAIR_CHUNKS):
            s = phase2(hd, pair, j, s)
            yield
        states[hd] = s

    def run_interleaved(chains):
        live = list(chains)
        while live:
            nxt = []
            for c in live:
                try:
                    next(c)
                    nxt.append(c)
                except StopIteration:
                    pass
            live = nxt

    def phase1_chains(pair):
        return [phase1(hd, pair, gg) for gg in range(GDN_PAIR) for hd in range(GDN_HB)]

    for pair in range(N_PAIRS):
        phase0(pair)
    run_interleaved(phase1_chains(0))

    def body(pair, states):
        states = list(states)
        run_interleaved([phase2_chain(hd, pair - 1, states) for hd in range(GDN_HB)] + phase1_chains(pair))
        return tuple(states)

    s0 = tuple(jnp.zeros((GDN_DK, GDN_DV), F32) for _ in range(GDN_HB))
    states = list(lax.fori_loop(1, N_PAIRS, body, s0))
    run_interleaved([phase2_chain(hd, N_PAIRS - 1, states) for hd in range(GDN_HB)])


def _gdn(proj, g1, gr, gdn_norm):
    w = GDN_HB * LANES
    per_row = D_MODEL // w
    hblk = lambda off: pl.BlockSpec((SEQ, w), lambda b, h, off=off: (b, off * per_row + h))
    big = lambda dt: pltpu.VMEM((GDN_HB, SEQ, LANES), dt)
    return pl.pallas_call(
        _gdn_body,
        grid=(BATCH, GDN_HEADS // GDN_HB),
        in_specs=[
            hblk(0), hblk(1), hblk(2), hblk(3),
            pl.BlockSpec((SEQ, LANES), lambda b, h: (b, 0)),
            pl.BlockSpec((1, GDN_HB, N_GROUPS_SEQ, GDN_GROUP_ROWS), lambda b, h: (b, h, 0, 0)),
            pl.BlockSpec((1, LANES), lambda b, h: (0, 0)),
        ],
        out_specs=pl.BlockSpec((SEQ, w), lambda b, h: (b, h)),
        out_shape=jax.ShapeDtypeStruct((TOKENS, D_MODEL), BF16),
        scratch_shapes=[
            big(BF16), pltpu.VMEM((GDN_HB, SEQ, 2 * LANES), BF16),
            big(BF16), big(F32), big(BF16),
            pltpu.VMEM((GDN_HB, 2 * PAIR_ROWS, LANES), F32),
            pltpu.VMEM((GDN_HB, 2 * PAIR_ROWS, LANES), F32),
            pltpu.VMEM((GDN_HB, 2 * PAIR_CHUNKS * LANES, LANES), F32),
            pltpu.VMEM((GDN_HB, 2 * PAIR_CHUNKS * LANES, LANES), F32),
        ],
        compiler_params=_cparams(("parallel", "parallel"), vmem=GDN_VMEM),
        name="gdn",
    )(proj, proj, proj, proj, g1, gr, gdn_norm)


SWA_PIPE = 4
assert WINDOW == SWA_BLOCK


def _swa_body(sink_ref, q_ref, kc_ref, kp_ref, vc_ref, vp_ref, o_ref):
    i = pl.program_id(1)
    kcat = jnp.concatenate([kp_ref[...], kc_ref[...]], axis=0)
    vcat = jnp.concatenate([vp_ref[...], vc_ref[...]], axis=0)
    qi = lax.broadcasted_iota(I32, (SWA_BLOCK, SWA_BLOCK), 0)
    ki = lax.broadcasted_iota(I32, (SWA_BLOCK, SWA_BLOCK), 1)
    take_cur = ki <= qi
    valid = take_cur | (i > 0)
    lane_kv = lax.broadcasted_iota(I32, (2 * SWA_BLOCK, LANES), 1)
    lane_q = lax.broadcasted_iota(I32, (SWA_BLOCK, LANES), 1)
    nt = (((1,), (1,)), ((), ()))
    scale = SWA_DH ** -0.5
    assert math.frexp(scale)[0] == 0.5

    def head_pair(h0, ks, vs):
        qs = q_ref[:, h0 * SWA_DH:(h0 + 2) * SWA_DH]
        scores = []
        for kk in ks:
            sc = lax.dot_general(qs, kk, nt, preferred_element_type=F32)
            s = jnp.where(take_cur, sc[:, SWA_BLOCK:], sc[:, :SWA_BLOCK])
            scores.append(jnp.where(valid, s, -jnp.inf))
        for _ in range(SWA_PIPE):
            yield
        acc = None
        inv = None
        for half in range(2):
            sink = sink_ref[h0 + half]
            s = scores[half]
            m = jnp.maximum(jnp.max(s, axis=-1, keepdims=True), sink)
            p = jnp.exp(s - m)
            den = jnp.sum(p, axis=-1, keepdims=True) + jnp.exp(sink - m)
            p2 = jnp.concatenate([jnp.where(take_cur, 0.0, p), jnp.where(take_cur, p, 0.0)], axis=1)
            pv = jnp.dot(p2.astype(BF16), vs[half], preferred_element_type=F32)
            acc = pv if acc is None else acc + pv
            r = 1.0 / den
            inv = r if inv is None else jnp.where(lane_q < SWA_DH, inv, r)
        yield
        o_ref[:, h0 * SWA_DH:(h0 + 2) * SWA_DH] = (acc * inv).astype(BF16)

    chains = []
    for slab in range(SWA_KV_HEADS // 2):
        k2 = kcat[:, slab * LANES:(slab + 1) * LANES].astype(F32) * scale
        v2 = vcat[:, slab * LANES:(slab + 1) * LANES].astype(F32)
        k2r = pltpu.roll(k2, SWA_DH, 1)
        v2r = pltpu.roll(v2, SWA_DH, 1)
        for sub in range(2):
            kvh = slab * 2 + sub
            if sub == 0:
                k_lo = jnp.where(lane_kv < SWA_DH, k2, 0.0)
                k_hi = jnp.where(lane_kv >= SWA_DH, k2r, 0.0)
                v_lo = jnp.where(lane_kv < SWA_DH, v2, 0.0)
                v_hi = jnp.where(lane_kv >= SWA_DH, v2r, 0.0)
            else:
                k_lo = jnp.where(lane_kv < SWA_DH, k2r, 0.0)
                k_hi = jnp.where(lane_kv >= SWA_DH, k2, 0.0)
                v_lo = jnp.where(lane_kv < SWA_DH, v2r, 0.0)
                v_hi = jnp.where(lane_kv >= SWA_DH, v2, 0.0)
            k_lo, k_hi, v_lo, v_hi = (t.astype(BF16) for t in (k_lo, k_hi, v_lo, v_hi))
            for gp in range(SWA_GROUP // 2):
                h0 = kvh * SWA_GROUP + 2 * gp
                chains.append(head_pair(h0, (k_lo, k_hi), (v_lo, v_hi)))

    live = []
    pending = list(chains)
    while pending or live:
        if pending:
            live.append(pending.pop(0))
        nxt = []
        for c in live:
            try:
                next(c)
                nxt.append(c)
            except StopIteration:
                pass
        live = nxt


def _swa(proj, sinks):
    nb = SEQ // SWA_BLOCK
    qcol = 4
    kcol = (7 * D_MODEL) // 256
    vcol = kcol + 1
    cur = lambda col: pl.BlockSpec((SWA_BLOCK, 256), lambda b, i, col=col: (b * nb + i, col))
    prev = lambda col: pl.BlockSpec(
        (SWA_BLOCK, 256), lambda b, i, col=col: (b * nb + jnp.maximum(i - 1, 0), col))
    return pl.pallas_call(
        _swa_body,
        grid=(BATCH, nb),
        in_specs=[
            pl.BlockSpec(memory_space=pltpu.SMEM),
            pl.BlockSpec((SWA_BLOCK, D_MODEL), lambda b, i: (b * nb + i, qcol)),
            cur(kcol), prev(kcol), cur(vcol), prev(vcol),
        ],
        out_specs=pl.BlockSpec((SWA_BLOCK, D_MODEL), lambda b, i: (b * nb + i, 0)),
        out_shape=jax.ShapeDtypeStruct((TOKENS, D_MODEL), BF16),
        compiler_params=_cparams(("parallel", "parallel")),
        name="swa",
    )(sinks, proj, proj, proj, proj, proj)


OUT_TM = 512
OUT_SUB = 256


def _outproj_body(x_ref, yg_ref, ys_ref, gg_ref, gs_ref, w_ref, nf_ref, wr_ref, rb_ref,
                  x1_ref, hp_ref, lg_ref):
    def sub_tile(t):
        rows = pl.ds(t * OUT_SUB, OUT_SUB)
        merged = (_sigmoid(gg_ref[rows, :].astype(F32)) * yg_ref[rows, :].astype(F32)
                  + _sigmoid(gs_ref[rows, :].astype(F32)) * ys_ref[rows, :].astype(F32))
        acc = jnp.dot(merged.astype(BF16), w_ref[...], preferred_element_type=F32)
        yield
        x1 = x_ref[rows, :] + acc
        x1_ref[rows, :] = x1
        ms = jnp.mean(x1 * x1, axis=-1, keepdims=True)
        h = x1 * lax.rsqrt(ms + EPS) * nf_ref[...]
        hh = h.astype(BF16)
        r = jnp.dot(hh, wr_ref[...], preferred_element_type=F32)
        lg_ref[rows, :] = r[:, :LANES] + r[:, LANES:] + rb_ref[...]
        half = D_MODEL // 2
        hf = hh.astype(F32)
        wa = pltpu.bitcast(hf[:, :half], U32)
        wb = pltpu.bitcast(hf[:, half:], U32)
        _store_token_tiles(hp_ref, t * OUT_SUB, (wa & jnp.uint32(0xFFFF0000)) | (wb >> 16))

    tiles = [sub_tile(t) for t in range(OUT_TM // OUT_SUB)]
    for _ in range(2):
        for g in tiles:
            next(g, None)


def _outproj(x2d, y_gdn, y_swa, proj, w_out, norm_ffn, wr2, r_bias):
    row = lambda w: pl.BlockSpec((OUT_TM, w), lambda i: (i, 0))
    const = lambda s: pl.BlockSpec(s, lambda i: (0, 0))
    return pl.pallas_call(
        _outproj_body,
        grid=(TOKENS // OUT_TM,),
        in_specs=[
            row(D_MODEL), row(D_MODEL), row(D_MODEL),
            pl.BlockSpec((OUT_TM, D_MODEL), lambda i: (i, 5)),
            pl.BlockSpec((OUT_TM, D_MODEL), lambda i: (i, 6)),
            pl.BlockSpec((D_MODEL, D_MODEL), lambda i: (0, 0), pipeline_mode=pl.Buffered(1)),
            const((1, D_MODEL)),
            const((D_MODEL, 2 * LANES)), const((1, LANES)),
        ],
        out_specs=[row(D_MODEL), pl.BlockSpec((OUT_TM * TILE_ROWS, LANES), lambda i: (i, 0)), row(LANES)],
        out_shape=[
            jax.ShapeDtypeStruct((TOKENS, D_MODEL), F32),
            jax.ShapeDtypeStruct((TOKENS * TILE_ROWS, LANES), U32),
            jax.ShapeDtypeStruct((TOKENS, LANES), F32),
        ],
        compiler_params=_cparams(("parallel",)),
        name="outproj",
    )(x2d, y_gdn, y_swa, proj, proj, w_out, norm_ffn, wr2, r_bias)


ROUTE_TM = 1024


def _route_body(lg_ref, tri_ref, idx_ref, wt_ref, cnt_ref, run_ref):
    @pl.when(pl.program_id(0) == 0)
    def _():
        run_ref[...] = jnp.zeros_like(run_ref)

    lg = lg_ref[...]
    lane = lax.broadcasted_iota(I32, lg.shape, 1)
    ninf = -jnp.inf
    big = jnp.int32(LANES)
    is_g = lane < N_GROUPS
    glog = jnp.where(is_g, lg, ninf)
    gmax = jnp.max(glog, axis=-1, keepdims=True)
    gden = jnp.sum(jnp.where(is_g, jnp.exp(lg - gmax), 0.0), axis=-1, keepdims=True)
    p_sel = 1.0 / gden
    grp = jnp.min(jnp.where(glog == gmax, lane, big), axis=-1, keepdims=True)
    emask = (lane >= N_GROUPS) & (lane < N_GROUPS + N_EXPERTS) & (((lane - N_GROUPS) >> 3) == grp)
    el = jnp.where(emask, lg, ninf)
    v1 = jnp.max(el, axis=-1, keepdims=True)
    i1 = jnp.min(jnp.where(el == v1, lane, big), axis=-1, keepdims=True)
    el2 = jnp.where(lane == i1, ninf, el)
    v2 = jnp.max(el2, axis=-1, keepdims=True)
    i2 = jnp.min(jnp.where(el2 == v2, lane, big), axis=-1, keepdims=True)
    e = jnp.exp(v2 - v1)
    w1 = p_sel / (1.0 + e)
    w2 = p_sel * e / (1.0 + e)
    e0 = i1 - N_GROUPS
    e1 = i2 - N_GROUPS

    oh0 = lane == e0
    oh1 = lane == e1
    onehot = jnp.where(oh0 | oh1, 1.0, 0.0)
    prefix = jnp.dot(tri_ref[...], onehot.astype(BF16), preferred_element_type=F32) + run_ref[0:1, :]
    r0 = jnp.sum(jnp.where(oh0, prefix, 0.0), axis=-1, keepdims=True).astype(I32)
    r1 = jnp.sum(jnp.where(oh1, prefix, 0.0), axis=-1, keepdims=True).astype(I32)
    run = run_ref[0:1, :] + jnp.sum(onehot, axis=0, keepdims=True)
    run_ref[...] = jnp.broadcast_to(run, run_ref.shape)
    cnt_ref[...] = jnp.broadcast_to(run, cnt_ref.shape).astype(I32)

    zi = jnp.zeros(lg.shape, I32)
    idx = jnp.where(lane == 0, e0, zi)
    idx = jnp.where(lane == 1, e1, idx)
    idx = jnp.where(lane == 2, r0, idx)
    idx = jnp.where(lane == 3, r1, idx)
    idx_ref[...] = idx
    wt_ref[...] = jnp.where(lane == 0, w1, jnp.where(lane == 1, w2, 0.0))


def _route(logits, tri):
    row = pl.BlockSpec((ROUTE_TM, LANES), lambda i: (i, 0))
    return pl.pallas_call(
        _route_body,
        grid=(TOKENS // ROUTE_TM,),
        in_specs=[row, pl.BlockSpec((ROUTE_TM, ROUTE_TM), lambda i: (0, 0))],
        out_specs=[row, row, pl.BlockSpec((8, LANES), lambda i: (0, 0))],
        out_shape=[
            jax.ShapeDtypeStruct((TOKENS, LANES), I32),
            jax.ShapeDtypeStruct((TOKENS, LANES), F32),
            jax.ShapeDtypeStruct((8, LANES), I32),
        ],
        scratch_shapes=[pltpu.VMEM((8, LANES), F32)],
        compiler_params=_cparams(("arbitrary",)),
        name="route",
    )(logits, tri)


DISP_TM = 1024
N_DISP = TOKENS // DISP_TM
DISP_RING = 3
DISP_ROWS = DISP_TM * TILE_ROWS


def _dispatch_body(meta_ref, dest_ref, h_hbm, xs_ref, zbuf, hbuf, in_sem, out_sem, zsem):
    @pl.when(pl.program_id(0) == 0)
    def _():
        zbuf[...] = jnp.zeros_like(zbuf)

        def zero_block(row0):
            return pltpu.make_async_copy(
                zbuf, xs_ref.at[pl.ds(row0 * TILE_ROWS, MOE_BLOCK * TILE_ROWS), :], zsem)

        def per_expert(e, n):
            has = meta_ref[N_EXPERTS + e] > 0

            @pl.when(has)
            def _():
                zero_block(pl.multiple_of(meta_ref[e] - MOE_BLOCK, MOE_BLOCK)).start()

            return n + has.astype(I32)

        n_last = lax.fori_loop(0, N_EXPERTS, per_expert, jnp.int32(0))
        used = meta_ref[2 * N_EXPERTS]

        def tail(j, c):
            zero_block(pl.multiple_of(j * MOE_BLOCK, MOE_BLOCK)).start()
            return c

        lax.fori_loop(used, N_BLOCKS, tail, 0)

        def drain(i, c):
            zero_block(0).wait()
            return c

        lax.fori_loop(0, n_last + (N_BLOCKS - used), drain, 0)

    i = pl.program_id(0)
    b = lax.rem(i, DISP_RING)

    def load(step, slot):
        return pltpu.make_async_copy(
            h_hbm.at[pl.ds(pl.multiple_of(step * DISP_ROWS, DISP_ROWS), DISP_ROWS), :], hbuf.at[slot],
            in_sem.at[slot])

    def wait_scatters(slot):
        for k in range(TOP_K):
            pltpu.make_async_copy(hbuf.at[slot], xs_ref.at[pl.ds(0, DISP_ROWS), :], out_sem.at[slot]).wait()

    @pl.when(i == 0)
    def _():
        load(0, 0).start()

    load(i, b).wait()

    @pl.when(i + 1 < N_DISP)
    def _():
        load(i + 1, lax.rem(i + 1, DISP_RING)).start()

    def issue(r, c):
        for k in range(TOP_K):
            d = dest_ref[TOP_K * r + k]
            pltpu.make_async_copy(hbuf.at[b, pl.ds(pl.multiple_of(r * TILE_ROWS, TILE_ROWS), TILE_ROWS), :],
                                  xs_ref.at[pl.ds(pl.multiple_of(d * TILE_ROWS, TILE_ROWS), TILE_ROWS), :],
                                  out_sem.at[b]).start(priority=k)
        return c

    lax.fori_loop(0, DISP_TM, issue, 0, unroll=4)

    @pl.when(i >= 1)
    def _():
        wait_scatters(lax.rem(i + DISP_RING - 1, DISP_RING))

    @pl.when(i == N_DISP - 1)
    def _():
        wait_scatters(b)


def _dispatch(meta, dest_flat, hp):
    grid_spec = pltpu.PrefetchScalarGridSpec(
        num_scalar_prefetch=1,
        grid=(N_DISP,),
        in_specs=[
            pl.BlockSpec((DISP_TM * TOP_K,), lambda i, m: (i,), memory_space=pltpu.SMEM),
            pl.BlockSpec(memory_space=pl.ANY),
        ],
        out_specs=pl.BlockSpec(memory_space=pl.ANY),
        scratch_shapes=[
            pltpu.VMEM((MOE_BLOCK * TILE_ROWS, LANES), U32),
            pltpu.VMEM((DISP_RING, DISP_ROWS, LANES), U32),
            pltpu.SemaphoreType.DMA((DISP_RING,)),
            pltpu.SemaphoreType.DMA((DISP_RING,)),
            pltpu.SemaphoreType.DMA,
        ],
    )
    return pl.pallas_call(
        _dispatch_body,
        grid_spec=grid_spec,
        out_shape=jax.ShapeDtypeStruct((N_SLOTS * TILE_ROWS, LANES), U32),
        compiler_params=_cparams(("arbitrary",)),
        name="dispatch",
    )(meta, dest_flat, hp)


def _slots_body(idx_ref, start_ref, dest_ref):
    idx = idx_ref[...]
    lane = lax.broadcasted_iota(I32, idx.shape, 1)
    start = start_ref[...]
    out = jnp.zeros(idx.shape, I32)
    for k in range(TOP_K):
        e = idx[:, k:k + 1]
        base = jnp.sum(jnp.where(lane == e, start, 0.0), axis=-1, keepdims=True).astype(I32)
        out = jnp.where(lane == k, base + idx[:, TOP_K + k:TOP_K + k + 1], out)
    dest_ref[...] = out


SLOTS_TM = 2048


def _slots(idx, pad_start_row):
    row = pl.BlockSpec((SLOTS_TM, LANES), lambda i: (i, 0))
    return pl.pallas_call(
        _slots_body,
        grid=(TOKENS // SLOTS_TM,),
        in_specs=[row, pl.BlockSpec((1, LANES), lambda i: (0, 0))],
        out_specs=row,
        out_shape=jax.ShapeDtypeStruct((TOKENS, LANES), I32),
        compiler_params=_cparams(("parallel",)),
        name="slots",
    )(idx, pad_start_row)


EXPERT_OUT_COLS = 256


def _experts_body(sched_ref, x_ref, wg_hbm, wu_hbm, wd_hbm, y_ref,
                  wg_st, wu_st, wd_st, wg_bf, wu_bf, wd_bf, sem):
    j = pl.program_id(0)
    e = sched_ref[j]
    slot = sched_ref[N_BLOCKS + j]
    nxt = sched_ref[2 * N_BLOCKS + j]
    used = sched_ref[3 * N_BLOCKS]
    new_expert = (j == 0) | (e != sched_ref[jnp.maximum(j - 1, 0)])

    def weight_copies(expert, s):
        return (pltpu.make_async_copy(wg_hbm.at[expert], wg_st.at[s], sem.at[s, 0]),
                pltpu.make_async_copy(wu_hbm.at[expert], wu_st.at[s], sem.at[s, 1]),
                pltpu.make_async_copy(wd_hbm.at[expert], wd_st.at[s], sem.at[s, 2]))

    @pl.when(j == 0)
    def _():
        for c in weight_copies(e, slot):
            c.start(priority=1)

    @pl.when(new_expert)
    def _():
        for c in weight_copies(e, slot):
            c.wait()

        @pl.when(nxt >= 0)
        def _():
            for c in weight_copies(nxt, 1 - slot):
                c.start(priority=1)

        wg_bf[...] = wg_st[slot].astype(BF16)
        wu_bf[...] = wu_st[slot].astype(BF16)
        wd_bf[...] = wd_st[slot].astype(BF16)

    @pl.when(j < used)
    def _():
        w = _load_token_tiles(x_ref, MOE_BLOCK)
        xa = pltpu.bitcast(w & jnp.uint32(0xFFFF0000), F32).astype(BF16)
        xb = pltpu.bitcast(w << 16, F32).astype(BF16)
        half = D_MODEL // 2
        g = (jnp.dot(xa, wg_bf[:half, :], preferred_element_type=F32)
             + jnp.dot(xb, wg_bf[half:, :], preferred_element_type=F32))
        u = (jnp.dot(xa, wu_bf[:half, :], preferred_element_type=F32)
             + jnp.dot(xb, wu_bf[half:, :], preferred_element_type=F32))
        hb = (_silu(g) * u).astype(BF16)
        for c in range(half // EXPERT_OUT_COLS):
            lo = c * EXPERT_OUT_COLS
            ya = jnp.dot(hb, wd_bf[:, lo:lo + EXPERT_OUT_COLS], preferred_element_type=F32)
            yb = jnp.dot(hb, wd_bf[:, half + lo:half + lo + EXPERT_OUT_COLS], preferred_element_type=F32)
            wa = pltpu.bitcast(ya.astype(BF16).astype(F32), U32)
            wb = pltpu.bitcast(yb.astype(BF16).astype(F32), U32)
            _store_token_tiles(y_ref, 0, wa | (wb >> 16), first_block=lo // LANES)

    @pl.when(j >= used)
    def _():
        y_ref[...] = jnp.zeros_like(y_ref)


def _experts(sched, xs, w_gate, w_up, w_down):
    grid_spec = pltpu.PrefetchScalarGridSpec(
        num_scalar_prefetch=1,
        grid=(N_BLOCKS,),
        in_specs=[
            pl.BlockSpec((MOE_BLOCK * TILE_ROWS, LANES), lambda j, s: (j, 0)),
            pl.BlockSpec(memory_space=pl.ANY),
            pl.BlockSpec(memory_space=pl.ANY),
            pl.BlockSpec(memory_space=pl.ANY),
        ],
        out_specs=pl.BlockSpec((MOE_BLOCK * TILE_ROWS, LANES), lambda j, s: (j, 0)),
        scratch_shapes=[
            pltpu.VMEM((2, D_MODEL, D_EXPERT), F32),
            pltpu.VMEM((2, D_MODEL, D_EXPERT), F32),
            pltpu.VMEM((2, D_EXPERT, D_MODEL), F32),
            pltpu.VMEM((D_MODEL, D_EXPERT), BF16),
            pltpu.VMEM((D_MODEL, D_EXPERT), BF16),
            pltpu.VMEM((D_EXPERT, D_MODEL), BF16),
            pltpu.SemaphoreType.DMA((2, 3)),
        ],
    )
    return pl.pallas_call(
        _experts_body,
        grid_spec=grid_spec,
        out_shape=jax.ShapeDtypeStruct((N_SLOTS * TILE_ROWS, LANES), U32),
        compiler_params=_cparams(("arbitrary",)),
        name="experts",
    )(sched, xs, w_gate, w_up, w_down)


COMB_TM = 256
N_COMB = TOKENS // COMB_TM


def _unpack_pairs(w):
    hi = pltpu.bitcast(w & jnp.uint32(0xFFFF0000), F32)
    lo = pltpu.bitcast(w << 16, F32)
    return jnp.concatenate([hi, lo], axis=1)


def _combine_body(dest_ref, dest_next_ref, wt_ref, x1_ref, gain_ref, ys_ref, o_ref, buf, sem):
    i = pl.program_id(0)
    slot = lax.rem(i, 2)

    def start_gathers(dref, s):
        def issue(r, c):
            for k in range(TOP_K):
                d = dref[TOP_K * r + k]
                pltpu.make_async_copy(
                    ys_ref.at[pl.ds(pl.multiple_of(d * TILE_ROWS, TILE_ROWS), TILE_ROWS), :],
                    buf.at[s, k, pl.ds(pl.multiple_of(r * TILE_ROWS, TILE_ROWS), TILE_ROWS), :],
                    sem.at[s]).start(priority=k)
            return c
        lax.fori_loop(0, COMB_TM, issue, 0, unroll=4)

    @pl.when(i == 0)
    def _():
        start_gathers(dest_ref, slot)

    @pl.when(i + 1 < N_COMB)
    def _():
        start_gathers(dest_next_ref, 1 - slot)

    for k in range(TOP_K):
        pltpu.make_async_copy(ys_ref.at[pl.ds(0, COMB_TM * TILE_ROWS), :], buf.at[slot, k], sem.at[slot]).wait()

    wt = wt_ref[...]
    y = (x1_ref[...] + wt[:, 0:1] * _unpack_pairs(_load_token_tiles(buf.at[slot, 0], COMB_TM))
         + wt[:, 1:2] * _unpack_pairs(_load_token_tiles(buf.at[slot, 1], COMB_TM)))
    ms = jnp.mean(y * y, axis=-1, keepdims=True)
    o_ref[...] = y * lax.rsqrt(ms + EPS) * gain_ref[...]


def _combine(dest_flat, wts, x1, norm_final, ys):
    dblk = lambda f: pl.BlockSpec((COMB_TM * TOP_K,), f, memory_space=pltpu.SMEM)
    return pl.pallas_call(
        _combine_body,
        grid=(N_COMB,),
        in_specs=[
            dblk(lambda i: (i,)),
            dblk(lambda i: (jnp.minimum(i + 1, N_COMB - 1),)),
            pl.BlockSpec((COMB_TM, LANES), lambda i: (i, 0)),
            pl.BlockSpec((COMB_TM, D_MODEL), lambda i: (i, 0)),
            pl.BlockSpec((1, D_MODEL), lambda i: (0, 0)),
            pl.BlockSpec(memory_space=pl.ANY),
        ],
        out_specs=pl.BlockSpec((COMB_TM, D_MODEL), lambda i: (i, 0)),
        out_shape=jax.ShapeDtypeStruct((TOKENS, D_MODEL), F32),
        scratch_shapes=[pltpu.VMEM((2, TOP_K, COMB_TM * TILE_ROWS, LANES), U32), pltpu.SemaphoreType.DMA((2,))],
        compiler_params=_cparams(("arbitrary",)),
        name="combine",
    )(dest_flat, dest_flat, wts, x1, norm_final, ys)


def _pad_lanes(v):
    v = v.reshape(1, -1).astype(F32)
    return jnp.pad(v, ((0, 0), (0, LANES - v.shape[1])))


def kernel(x, norm_mix, w_in, conv_w, gdn_a_log, gdn_dt_bias, gdn_norm, swa_sinks, w_out, norm_ffn,
           w_router_group, b_router_group, w_router_expert, b_router_expert, w_gate, w_up, w_down,
           norm_final):
    l = 0
    x2d = x.reshape(TOKENS, D_MODEL)
    w = w_in[l]
    o_z = 3 * D_MODEL
    o_a = o_z + D_MODEL
    o_sq = o_a + 2 * GDN_HEADS
    o_sk = o_sq + D_MODEL
    o_sv = o_sk + SWA_KV_HEADS * SWA_DH
    o_gg = o_sv + SWA_KV_HEADS * SWA_DH
    o_gs = o_gg + D_MODEL
    w_main = jnp.concatenate(
        [w[:, :o_a], w[:, o_sq:o_sk], w[:, o_gg:o_gs], w[:, o_gs:], w[:, o_sk:o_sv], w[:, o_sv:o_gg]],
        axis=1).astype(BF16)
    w_ab = jnp.pad(w[:, o_a:o_sq], ((0, 0), (0, LANES - 2 * GDN_HEADS))).astype(BF16)

    proj, ab = _inproj(x2d, norm_mix[l].reshape(1, D_MODEL), w_main, w_ab, conv_w[l])

    g1, gt = _gdn_prep(ab, _pad_lanes(gdn_a_log[l]), _pad_lanes(gdn_dt_bias[l]))
    gr = gt[:, :GDN_HEADS, :].reshape(BATCH, GDN_HEADS, N_GROUPS_SEQ, GDN_GROUP_ROWS)
    y_gdn = _gdn(proj, g1, gr, gdn_norm[l].reshape(1, GDN_DV))

    y_swa = _swa(proj, swa_sinks[l].astype(F32))

    w_r = jnp.concatenate([w_router_group[l], w_router_expert[l]], axis=1).astype(F32)
    w_r = jnp.pad(w_r, ((0, 0), (0, LANES - w_r.shape[1])))
    wr_hi = w_r.astype(BF16)
    wr_lo = (w_r - wr_hi.astype(F32)).astype(BF16)
    r_bias = _pad_lanes(jnp.concatenate([b_router_group[l], b_router_expert[l]]))
    x1, hp, logits = _outproj(x2d, y_gdn, y_swa, proj, w_out[l].astype(BF16),
                              norm_ffn[l].reshape(1, D_MODEL), jnp.concatenate([wr_hi, wr_lo], axis=1), r_bias)

    tri = (lax.broadcasted_iota(I32, (ROUTE_TM, ROUTE_TM), 1)
           < lax.broadcasted_iota(I32, (ROUTE_TM, ROUTE_TM), 0)).astype(BF16)
    idx, wts, cnt = _route(logits, tri)

    counts = cnt[0, :N_EXPERTS]
    padded = (counts + MOE_BLOCK - 1) // MOE_BLOCK * MOE_BLOCK
    pad_end = jnp.cumsum(padded)
    pad_start = pad_end - padded
    dest = _slots(idx, _pad_lanes(pad_start))
    dest_flat = dest[:, :TOP_K].reshape(N_ASSIGN)
    blk_pos = jnp.arange(N_BLOCKS, dtype=I32) * MOE_BLOCK
    blk_e = jnp.minimum(jnp.sum((pad_end[None, :] <= blk_pos[:, None]).astype(I32), axis=1), N_EXPERTS - 1)
    used = pad_end[-1:] // MOE_BLOCK
    meta = jnp.concatenate([pad_end, padded, used]).astype(I32)
    is_new = jnp.concatenate([jnp.ones((1,), I32), (blk_e[1:] != blk_e[:-1]).astype(I32)])
    ordinal = jnp.cumsum(is_new) - 1
    nxt_pos = jnp.sum((ordinal[None, :] <= ordinal[:, None]).astype(I32), axis=1)
    nxt = jnp.where(nxt_pos < N_BLOCKS, blk_e[jnp.minimum(nxt_pos, N_BLOCKS - 1)], -1)
    sched = jnp.concatenate([blk_e, ordinal & 1, nxt, used]).astype(I32)

    xs = _dispatch(meta, dest_flat, hp)
    ys = _experts(sched, xs, w_gate[l], w_up[l], w_down[l])
    out = _combine(dest_flat, wts, x1, norm_final.reshape(1, D_MODEL), ys)
    return out.reshape(BATCH, SEQ, D_MODEL)
```
